```python
import jax, jax.numpy as jnp
from jax import lax
import numpy as np

D_MODEL = 1024
BATCH = 32
SEQ = 256
DEPTH = 2
DEC_BATCH = 4
DEC_SEQ = 1024
PAST_LEN = 256

GRID_W = 64
NA_HEADS = 8
NA_HEAD_DIM = 64
NA_WIDTH = NA_HEADS * NA_HEAD_DIM
NA_WIN_ROWS = 8
NA_WIN_COLS = 16
SSD_HEADS = 16
SSD_HEAD_DIM = 64
SSD_D_INNER = SSD_HEADS * SSD_HEAD_DIM
SSD_GROUPS = 2
SSD_STATE = 128
SSD_CONV = 5
SSD_XBC = SSD_D_INNER + 2 * SSD_GROUPS * SSD_STATE
RET_HEADS = 4
RET_QK_DIM = 256
RET_V_DIM = 512
RET_QK_W = RET_HEADS * RET_QK_DIM
RET_V_W = RET_HEADS * RET_V_DIM
SCAN_CHUNK = 64
FFN_HIDDEN = ((8 * D_MODEL + 3 * 256 - 1) // (3 * 256)) * 256
ROPE_BASE = 10000.0
EPS = 1e-6
L0_SPLITS = (NA_WIDTH, 2 * NA_WIDTH, 3 * NA_WIDTH, 3 * NA_WIDTH + SSD_D_INNER,
             3 * NA_WIDTH + SSD_D_INNER + SSD_XBC)
L0_IN = 3 * NA_WIDTH + SSD_D_INNER + SSD_XBC + 2 * SSD_HEADS
L0_MIX = NA_WIDTH + SSD_D_INNER
L1_SPLITS = (RET_QK_W, 2 * RET_QK_W, 2 * RET_QK_W + RET_V_W)
L1_IN = 2 * RET_QK_W + 2 * RET_V_W

kernel_name = "hybrid_na_ssd_retention_diffusion_step"

F32 = jnp.float32


def rmsnorm(x, w):
    x32 = x.astype(F32)
    y = x32 * lax.rsqrt(jnp.mean(x32 * x32, axis=-1, keepdims=True) + EPS)
    return (y * w.astype(F32)).astype(x.dtype)


def adaln(cond, mod_w, mod_b):
    mod = jax.nn.silu(cond) @ mod_w + mod_b
    if mod.ndim == 2:
        mod = mod[:, None, :]
    return jnp.split(mod, 6, axis=-1)


def swiglu(h, w1, w3, w2):
    return (jax.nn.silu(h @ w1) * (h @ w3)) @ w2


def axial_rope(x):
    l, dh = x.shape[1], x.shape[-1]
    t = jnp.arange(l)
    row = (t // GRID_W).astype(F32)
    col = (t % GRID_W).astype(F32)
    half = dh // 2
    freqs = ROPE_BASE ** (-jnp.arange(0, half, 2, dtype=F32) / half)
    ang = jnp.concatenate([row[:, None] * freqs, col[:, None] * freqs], axis=-1)
    cos, sin = jnp.cos(ang)[:, None, :], jnp.sin(ang)[:, None, :]
    xr = x.astype(F32).reshape(x.shape[:-1] + (dh // 2, 2))
    x1, x2 = xr[..., 0], xr[..., 1]
    out = jnp.stack([x1 * cos - x2 * sin, x1 * sin + x2 * cos], axis=-1)
    return out.reshape(x.shape).astype(x.dtype)


def chunked_scan(q, k, v, log_a, s0):
    b, l, h, dk = q.shape
    dv = v.shape[-1]
    nc = l // SCAN_CHUNK

    def to_chunks(t):
        return jnp.moveaxis(t.astype(F32).reshape((b, nc, SCAN_CHUNK) + t.shape[2:]), 1, 0)

    lower = jnp.tril(jnp.ones((SCAN_CHUNK, SCAN_CHUNK), bool))[None, :, :, None]

    def step(s, inp):
        qi, ki, vi, ai = inp
        cs = jnp.cumsum(ai, axis=1)
        diff = cs[:, :, None, :] - cs[:, None, :, :]
        decay = jnp.exp(jnp.where(lower, diff, -jnp.inf))
        scores = jnp.einsum('bihd,bjhd->bijh', qi, ki) * decay
        y = jnp.einsum('bijh,bjhe->bihe', scores, vi)
        y = y + jnp.einsum('bihd,bhde->bihe', qi * jnp.exp(cs)[..., None], s)
        tail = jnp.exp(cs[:, -1:, :] - cs)
        s_new = (jnp.exp(cs[:, -1, :])[:, :, None, None] * s
                 + jnp.einsum('bjhd,bjhe->bhde', ki * tail[..., None], vi))
        return s_new, y

    s_fin, ys = lax.scan(step, s0.astype(F32), (to_chunks(q), to_chunks(k), to_chunks(v), to_chunks(log_a)))
    return jnp.moveaxis(ys, 0, 1).reshape(b, l, h, dv), s_fin


def bidir_scan(q, k, v_f, v_b, a_f, a_b, s0):
    flip = lambda t: jnp.flip(t, axis=1)
    y_f, s_f = chunked_scan(q, k, v_f, a_f, s0[:, 0])
    y_b, s_b = chunked_scan(flip(q), flip(k), flip(v_b), flip(a_b), s0[:, 1])
    return y_f + flip(y_b), jnp.stack([s_f, s_b], axis=1)


def depthwise_conv(x, w, b):
    pad = SSD_CONV // 2
    out = lax.conv_general_dilated(x, w[:, None, :].astype(x.dtype), window_strides=(1,),
                                   padding=[(pad, pad)], dimension_numbers=('NWC', 'WIO', 'NWC'),
                                   feature_group_count=x.shape[-1])
    return out + b


def ssd_mixer(z, xbc, dt_raw, conv_w, conv_b, a_log, dt_bias, d_skip, norm_w, s0):
    b, l, _ = z.shape
    xbc = jax.nn.silu(depthwise_conv(xbc, conv_w, conv_b)).astype(F32)
    xs, bs, cs = jnp.split(xbc, (SSD_D_INNER, SSD_D_INNER + SSD_GROUPS * SSD_STATE), axis=-1)
    rep = SSD_HEADS // SSD_GROUPS
    xh = xs.reshape(b, l, SSD_HEADS, SSD_HEAD_DIM)
    bh = jnp.repeat(bs.reshape(b, l, SSD_GROUPS, SSD_STATE), rep, axis=2)
    ch = jnp.repeat(cs.reshape(b, l, SSD_GROUPS, SSD_STATE), rep, axis=2)
    dt = jax.nn.softplus(dt_raw.astype(F32) + dt_bias.astype(F32))
    log_a = dt * (-jnp.exp(a_log.astype(F32)))
    y, s_fin = bidir_scan(ch, bh, xh * dt[:, :, 0, :, None], xh * dt[:, :, 1, :, None],
                          log_a[:, :, 0], log_a[:, :, 1], s0)
    y = y + d_skip.astype(F32)[:, None] * xh
    y = y.reshape(b, l, SSD_D_INNER) * jax.nn.silu(z.astype(F32))
    yg = y.reshape(b, l, SSD_GROUPS, SSD_D_INNER // SSD_GROUPS)
    yg = yg * lax.rsqrt(jnp.mean(yg * yg, axis=-1, keepdims=True) + EPS)
    return yg.reshape(b, l, SSD_D_INNER) * norm_w.astype(F32), s_fin


def na_context(q, k, v):
    s = jnp.einsum('blhd,bmhd->bhlm', q, k).astype(F32) * NA_HEAD_DIM ** -0.5
    p = jax.nn.softmax(s, axis=-1).astype(v.dtype)
    return jnp.einsum('bhlm,bmhd->blhd', p, v)


def na_latent(q, k, v, k_ctx, v_ctx, rel_bias):
    b, l, h, d = q.shape
    rows = l // GRID_W
    wr = min(NA_WIN_ROWS, rows)
    r = jnp.arange(rows)
    r0 = jnp.clip(r - wr // 2, 0, rows - wr)
    band_rows = r0[:, None] + jnp.arange(wr)
    kg = k.reshape(b, rows, GRID_W, h, d)[:, band_rows]
    vg = v.reshape(b, rows, GRID_W, h, d)[:, band_rows]
    qg = q.reshape(b, rows, GRID_W, h, d)
    scale = NA_HEAD_DIM ** -0.5
    s_loc = jnp.einsum('brchd,brswhd->bhrcsw', qg, kg).astype(F32) * scale
    col = jnp.arange(GRID_W)
    c0 = jnp.clip(col - NA_WIN_COLS // 2, 0, GRID_W - NA_WIN_COLS)
    col_in = (col[None, :] >= c0[:, None]) & (col[None, :] < c0[:, None] + NA_WIN_COLS)
    dr_idx = band_rows - r[:, None] + NA_WIN_ROWS - 1
    dc_idx = jnp.clip(col[None, :] - col[:, None] + NA_WIN_COLS - 1, 0, 2 * NA_WIN_COLS - 2)
    bias = rel_bias.astype(F32)[:, dr_idx[:, None, :, None], dc_idx[None, :, None, :]]
    s_loc = jnp.where(col_in[:, None, :], s_loc + bias[None], -jnp.inf)
    s_ctx = jnp.einsum('brchd,bmhd->bhrcm', qg, k_ctx).astype(F32) * scale
    n_loc = wr * GRID_W
    s_all = jnp.concatenate([s_loc.reshape(b, h, rows, GRID_W, n_loc), s_ctx], axis=-1)
    p = jax.nn.softmax(s_all, axis=-1).astype(v.dtype)
    p_loc = p[..., :n_loc].reshape(b, h, rows, GRID_W, wr, GRID_W)
    out = (jnp.einsum('bhrcsw,brswhd->brchd', p_loc, vg)
           + jnp.einsum('bhrcm,bmhd->brchd', p[..., n_loc:], v_ctx))
    return out.reshape(b, l, h, d)


def even_mixer(h, w_in, w_out, na_bias, conv_w, conv_b, a_log, dt_bias, d_skip, norm_w, cache):
    b, l, _ = h.shape
    q, k, v, z, xbc, dt_raw = jnp.split(h @ w_in, L0_SPLITS, axis=-1)
    heads = lambda t: t.reshape(b, l, NA_HEADS, NA_HEAD_DIM)
    q, k, v = heads(q), heads(k), heads(v)
    if cache is None:
        att = na_context(q, k, v)
        s0 = jnp.zeros((b, 2, SSD_HEADS, SSD_STATE, SSD_HEAD_DIM), F32)
    else:
        k_ctx, v_ctx, s0 = cache
        att = na_latent(q, k, v, k_ctx, v_ctx, na_bias)
    ssd, s_fin = ssd_mixer(z, xbc, dt_raw.reshape(b, l, 2, SSD_HEADS), conv_w, conv_b,
                           a_log, dt_bias, d_skip, norm_w, s0)
    mixed = jnp.concatenate([att.reshape(b, l, NA_WIDTH), ssd.astype(h.dtype)], axis=-1)
    return mixed @ w_out, (k, v, s_fin)


def odd_mixer(h, w_in, w_out, ret_decay, ret_norm_w, cache):
    b, l, _ = h.shape
    q, k, v, g = jnp.split(h @ w_in, L1_SPLITS, axis=-1)
    q = q.reshape(b, l, RET_HEADS, RET_QK_DIM)
    k = k.reshape(b, l, RET_HEADS, RET_QK_DIM) * RET_QK_DIM ** -0.5
    v = v.reshape(b, l, RET_HEADS, RET_V_DIM)
    if cache is None:
        s0 = jnp.zeros((b, 2, RET_HEADS, RET_QK_DIM, RET_V_DIM), F32)
    else:
        q, k, s0 = axial_rope(q), axial_rope(k), cache
    log_g = jax.nn.log_sigmoid(ret_decay.astype(F32))
    a_f = jnp.broadcast_to(log_g[0], (b, l, RET_HEADS))
    a_b = jnp.broadcast_to(log_g[1], (b, l, RET_HEADS))
    y, s_fin = bidir_scan(q, k, v, v, a_f, a_b, s0)
    y = y * lax.rsqrt(jnp.mean(y * y, axis=-1, keepdims=True) + EPS)
    y = y.reshape(b, l, RET_V_W) * ret_norm_w.astype(F32) * jax.nn.silu(g.astype(F32))
    return y.astype(h.dtype) @ w_out, s_fin


def block(x, cond, mixer, norm1_w, norm2_w, mod_w, mod_b, w1, w3, w2):
    sh1, sc1, g1, sh2, sc2, g2 = adaln(cond, mod_w, mod_b)
    out, aux = mixer(rmsnorm(x, norm1_w) * (1 + sc1) + sh1)
    x = x + g1 * out
    x = x + g2 * swiglu(rmsnorm(x, norm2_w) * (1 + sc2) + sh2, w1, w3, w2)
    return x, aux


def setup_inputs(seed: int = 0) -> dict:
    key = jax.random.key(seed)
    ks = iter(jax.random.split(key, 64))
    D = D_MODEL
    nrm = lambda shape, scale: jax.random.normal(next(ks), shape, F32) * scale
    uni = lambda shape, lo, hi: jax.random.uniform(next(ks), shape, F32, lo, hi)
    inp = {}
    inp['x_prompt'] = nrm((BATCH, SEQ, D), 1.0)
    inp['x_sample'] = nrm((DEC_BATCH, DEC_SEQ, D), 1.0)
    inp['cache_l0_na_k'] = nrm((DEC_BATCH, PAST_LEN, NA_HEADS, NA_HEAD_DIM), 1.0)
    inp['cache_l0_na_v'] = nrm((DEC_BATCH, PAST_LEN, NA_HEADS, NA_HEAD_DIM), 1.0)
    inp['state_l0_ssd'] = nrm((DEC_BATCH, 2, SSD_HEADS, SSD_STATE, SSD_HEAD_DIM), 0.1)
    inp['state_l1_ret'] = nrm((DEC_BATCH, 2, RET_HEADS, RET_QK_DIM, RET_V_DIM), 0.1)
    inp['c'] = nrm((DEC_BATCH, D), 1.0)
    inp['c_ctx'] = nrm((D,), 1.0)
    inp['l0_norm1_w'] = 1.0 + nrm((D,), 0.02)
    inp['l0_norm2_w'] = 1.0 + nrm((D,), 0.02)
    inp['l0_mod_w'] = nrm((D, 6 * D), 0.5 * D ** -0.5)
    inp['l0_mod_b'] = nrm((6 * D,), 0.02)
    inp['l0_w_in'] = jnp.concatenate([nrm((D, L0_IN - 2 * SSD_HEADS), D ** -0.5),
                                      nrm((D, 2 * SSD_HEADS), 0.1 * D ** -0.5)], axis=1)
    inp['l0_w_out'] = nrm((L0_MIX, D), L0_MIX ** -0.5)
    inp['l0_na_bias'] = nrm((NA_HEADS, 2 * NA_WIN_ROWS - 1, 2 * NA_WIN_COLS - 1), 0.02)
    inp['l0_conv_w'] = nrm((SSD_CONV, SSD_XBC), SSD_CONV ** -0.5)
    inp['l0_conv_b'] = nrm((SSD_XBC,), 0.02)
    inp['l0_ssd_a_log'] = jnp.log(uni((2, SSD_HEADS), 1.0, 16.0))
    dt0 = jnp.exp(uni((2, SSD_HEADS), float(np.log(1e-3)), float(np.log(1e-1))))
    inp['l0_ssd_dt_bias'] = dt0 + jnp.log(-jnp.expm1(-dt0))
    inp['l0_ssd_d'] = 1.0 + nrm((SSD_HEADS,), 0.1)
    inp['l0_ssd_norm_w'] = 1.0 + nrm((SSD_D_INNER,), 0.02)
    inp['l0_ffn_w1'] = nrm((D, FFN_HIDDEN), D ** -0.5)
    inp['l0_ffn_w3'] = nrm((D, FFN_HIDDEN), D ** -0.5)
    inp['l0_ffn_w2'] = nrm((FFN_HIDDEN, D), FFN_HIDDEN ** -0.5)
    inp['l1_norm1_w'] = 1.0 + nrm((D,), 0.02)
    inp['l1_norm2_w'] = 1.0 + nrm((D,), 0.02)
    inp['l1_mod_w'] = nrm((D, 6 * D), 0.5 * D ** -0.5)
    inp['l1_mod_b'] = nrm((6 * D,), 0.02)
    inp['l1_w_in'] = nrm((D, L1_IN), D ** -0.5)
    inp['l1_w_out'] = nrm((RET_V_W, D), RET_V_W ** -0.5)
    gamma = 1.0 - 2.0 ** (-5.0 - np.arange(RET_HEADS, dtype=np.float32))
    logit = np.log(gamma) - np.log1p(-gamma)
    inp['l1_ret_decay'] = jnp.asarray(logit, F32)[None, :] + nrm((2, RET_HEADS), 0.1)
    inp['l1_ret_norm_w'] = 1.0 + nrm((RET_V_W,), 0.02)
    inp['l1_ffn_w1'] = nrm((D, FFN_HIDDEN), D ** -0.5)
    inp['l1_ffn_w3'] = nrm((D, FFN_HIDDEN), D ** -0.5)
    inp['l1_ffn_w2'] = nrm((FFN_HIDDEN, D), FFN_HIDDEN ** -0.5)
    inp['final_norm_w'] = 1.0 + nrm((D,), 0.02)
    return inp


def reference(x_prompt, x_sample, cache_l0_na_k, cache_l0_na_v, state_l0_ssd, state_l1_ret, c, c_ctx,
              l0_norm1_w, l0_norm2_w, l0_mod_w, l0_mod_b, l0_w_in, l0_w_out, l0_na_bias, l0_conv_w, l0_conv_b,
              l0_ssd_a_log, l0_ssd_dt_bias, l0_ssd_d, l0_ssd_norm_w, l0_ffn_w1, l0_ffn_w3, l0_ffn_w2,
              l1_norm1_w, l1_norm2_w, l1_mod_w, l1_mod_b, l1_w_in, l1_w_out, l1_ret_decay, l1_ret_norm_w,
              l1_ffn_w1, l1_ffn_w3, l1_ffn_w2, final_norm_w):
    mixers = [
        lambda h, cache: even_mixer(h, l0_w_in, l0_w_out, l0_na_bias, l0_conv_w, l0_conv_b, l0_ssd_a_log,
                                    l0_ssd_dt_bias, l0_ssd_d, l0_ssd_norm_w, cache),
        lambda h, cache: odd_mixer(h, l1_w_in, l1_w_out, l1_ret_decay, l1_ret_norm_w, cache),
    ]
    block_w = [
        (l0_norm1_w, l0_norm2_w, l0_mod_w, l0_mod_b, l0_ffn_w1, l0_ffn_w3, l0_ffn_w2),
        (l1_norm1_w, l1_norm2_w, l1_mod_w, l1_mod_b, l1_ffn_w1, l1_ffn_w3, l1_ffn_w2),
    ]
    caches = [(cache_l0_na_k, cache_l0_na_v, state_l0_ssd), state_l1_ret]

    xc = x_prompt
    ctx_out = []
    for i in range(DEPTH):
        xc, aux = block(xc, c_ctx, lambda h: mixers[i](h, None), *block_w[i])
        ctx_out.append(aux)

    xs = x_sample
    for i in range(DEPTH):
        xs, _ = block(xs, c, lambda h: mixers[i](h, caches[i]), *block_w[i])

    y_prompt = rmsnorm(xc, final_norm_w)
    y_sample = rmsnorm(xs, final_norm_w)
    (new_l0_na_k, new_l0_na_v, new_l0_ssd), new_l1_ret = ctx_out[0], ctx_out[1]
    return (y_prompt, y_sample, new_l0_na_k, new_l0_na_v, new_l0_ssd, new_l1_ret)
```

```python
import functools

import numpy as np
import jax
import jax.numpy as jnp
from jax import lax
from jax.experimental import pallas as pl
from jax.experimental.pallas import tpu as pltpu

F32 = jnp.float32
BF16 = jnp.bfloat16

D_MODEL = 1024
GRID_W = 64
NA_HEADS = 8
NA_HEAD_DIM = 64
NA_WIDTH = NA_HEADS * NA_HEAD_DIM
NA_WIN_ROWS = 8
NA_WIN_COLS = 16
SSD_HEADS = 16
SSD_HEAD_DIM = 64
SSD_D_INNER = SSD_HEADS * SSD_HEAD_DIM
SSD_GROUPS = 2
SSD_STATE = 128
SSD_CONV = 5
SSD_XBC = SSD_D_INNER + 2 * SSD_GROUPS * SSD_STATE
RET_HEADS = 4
RET_QK_DIM = 256
RET_V_DIM = 512
RET_QK_W = RET_HEADS * RET_QK_DIM
RET_V_W = RET_HEADS * RET_V_DIM
ROPE_BASE = 10000.0
EPS = 1e-6

LANES = 128
SEQ_BLOCK = 1024
ROW_TILE = 512
SSD_CHUNK = 128
RET_CHUNK = 256
FFN_CHUNK = 256
VMEM_LIMIT = 56 * 1024 * 1024

_NEG_INF = float("-inf")


def _params(**kw):
    return pltpu.CompilerParams(vmem_limit_bytes=VMEM_LIMIT, **kw)


def _silu(x):
    return x * (1.0 / (1.0 + jnp.exp(-x)))


def _softplus(x):
    return jnp.maximum(x, 0.0) + jnp.log(1.0 + jnp.exp(-jnp.abs(x)))


def _mm(a, b):
    return jnp.dot(a.astype(BF16), b.astype(BF16), preferred_element_type=F32)


def _mm_nt(a, b):
    return lax.dot_general(a.astype(BF16), b.astype(BF16), (((1,), (1,)), ((), ())),
                           preferred_element_type=F32)


def _const_spec(shape):
    nd = len(shape)
    return pl.BlockSpec(shape, lambda *_: (0,) * nd)


def _adaln_kernel(c_ref, w_ref, b_ref, o_ref):
    o_ref[...] = _mm(_silu(c_ref[...]), w_ref[...]) + b_ref[...]


def _adaln(cond, mod_w, mod_b):
    r, d = cond.shape
    n = mod_w.shape[1]
    tn = 1536
    return pl.pallas_call(
        _adaln_kernel,
        grid=(n // tn,),
        in_specs=[_const_spec((r, d)),
                  pl.BlockSpec((d, tn), lambda j: (0, j)),
                  pl.BlockSpec((1, tn), lambda j: (0, j))],
        out_specs=pl.BlockSpec((r, tn), lambda j: (0, j)),
        out_shape=jax.ShapeDtypeStruct((r, n), F32),
        compiler_params=_params(dimension_semantics=("arbitrary",)),
        name="adaln_mod",
    )(cond, mod_w, mod_b.reshape(1, n))


def _norm_mod(x, nw, mod_ref, shift_idx, scale_idx):
    ms = jnp.mean(x * x, axis=-1, keepdims=True)
    h = x * lax.rsqrt(ms + EPS) * nw
    return h * (1.0 + mod_ref[0, scale_idx:scale_idx + 1, :]) + mod_ref[0, shift_idx:shift_idx + 1, :]


def _inproj0_kernel(x_ref, nw_ref, mod_ref, w_ref, q_ref, k_ref, v_ref, z_ref, xbc_ref, dt_ref):
    hb = _norm_mod(x_ref[...], nw_ref[...], mod_ref, 0, 1).astype(BF16)
    col = 0
    for o_ref in (q_ref, k_ref, v_ref, z_ref, xbc_ref, dt_ref):
        n = o_ref.shape[1]
        o_ref[...] = jnp.dot(hb, w_ref[:, col:col + n], preferred_element_type=F32)
        col += n


def _inproj0(x, norm_w, mod, w, rows_per_mod):
    t, d = x.shape
    tm = ROW_TILE
    widths = (NA_WIDTH, NA_WIDTH, NA_WIDTH, SSD_D_INNER, SSD_XBC, LANES)
    assert sum(widths) == w.shape[1]
    per = rows_per_mod // tm
    return pl.pallas_call(
        _inproj0_kernel,
        grid=(t // tm,),
        in_specs=[pl.BlockSpec((tm, d), lambda i: (i, 0)),
                  _const_spec((1, d)),
                  pl.BlockSpec((1, 6, d), lambda i: (i // per, 0, 0)),
                  _const_spec(w.shape)],
        out_specs=[pl.BlockSpec((tm, n), lambda i: (i, 0)) for n in widths],
        out_shape=[jax.ShapeDtypeStruct((t, n), F32) for n in widths],
        compiler_params=_params(dimension_semantics=("arbitrary",)),
        name="l0_in_proj",
    )(x, norm_w.reshape(1, d), mod, w)


def _inproj1_kernel(*refs, rope):
    if rope:
        x_ref, nw_ref, mod_ref, w_ref, cos_ref, sin_ref, q_ref, k_ref, v_ref, g_ref = refs
    else:
        x_ref, nw_ref, mod_ref, w_ref, q_ref, k_ref, v_ref, g_ref = refs
    hb = _norm_mod(x_ref[...], nw_ref[...], mod_ref, 0, 1).astype(BF16)
    half = RET_QK_DIM // 2
    for o_ref, col, scale in ((q_ref, 0, 1.0), (k_ref, RET_QK_W, RET_QK_DIM ** -0.5)):
        for h in range(RET_HEADS):
            c0 = col + h * RET_QK_DIM
            y = jnp.dot(hb, w_ref[:, c0:c0 + RET_QK_DIM], preferred_element_type=F32) * scale
            if rope:
                x1, x2 = y[:, :half], y[:, half:]
                cs, sn = cos_ref[...], sin_ref[...]
                o_ref[:, h * RET_QK_DIM:h * RET_QK_DIM + half] = x1 * cs - x2 * sn
                o_ref[:, h * RET_QK_DIM + half:(h + 1) * RET_QK_DIM] = x1 * sn + x2 * cs
            else:
                o_ref[:, h * RET_QK_DIM:(h + 1) * RET_QK_DIM] = y
    col = 2 * RET_QK_W
    for o_ref in (v_ref, g_ref):
        o_ref[...] = jnp.dot(hb, w_ref[:, col:col + RET_V_W], preferred_element_type=F32)
        col += RET_V_W


def _inproj1(x, norm_w, mod, w, rows_per_mod, rope_tables=None):
    t, d = x.shape
    tm = ROW_TILE
    per = rows_per_mod // tm
    widths = (RET_QK_W, RET_QK_W, RET_V_W, RET_V_W)
    in_specs = [pl.BlockSpec((tm, d), lambda i: (i, 0)),
                _const_spec((1, d)),
                pl.BlockSpec((1, 6, d), lambda i: (i // per, 0, 0)),
                _const_spec(w.shape)]
    args = [x, norm_w.reshape(1, d), mod, w]
    if rope_tables is not None:
        cos, sin = rope_tables
        nblk = cos.shape[0] // tm
        in_specs += [pl.BlockSpec((tm, cos.shape[1]), lambda i: (i % nblk, 0))] * 2
        args += [cos, sin]
    return pl.pallas_call(
        functools.partial(_inproj1_kernel, rope=rope_tables is not None),
        grid=(t // tm,),
        in_specs=in_specs,
        out_specs=[pl.BlockSpec((tm, n), lambda i: (i, 0)) for n in widths],
        out_shape=[jax.ShapeDtypeStruct((t, n), F32) for n in widths],
        compiler_params=_params(dimension_semantics=("arbitrary",)),
        name="l1_in_proj",
    )(*args)


def _lane_lo(shape):
    return lax.broadcasted_iota(jnp.int32, shape, 1) < NA_HEAD_DIM


def _head_lanes(x, half):
    lo = _lane_lo(x.shape)
    return jnp.where(lo if half == 0 else jnp.logical_not(lo), x, 0.0)


def _softmax_pv(q2, keys, vals, biases):
    acc = None
    for half in (0, 1):
        qm = _head_lanes(q2, half)
        scores = []
        for kk, bb in zip(keys, biases):
            s = _mm_nt(qm, kk)
            if bb is not None:
                s = s + bb(half)
            scores.append(s)
        mx = functools.reduce(jnp.maximum, [jnp.max(s, axis=-1, keepdims=True) for s in scores])
        es = [jnp.exp(s - mx) for s in scores]
        den = functools.reduce(jnp.add, [jnp.sum(e, axis=-1, keepdims=True) for e in es])
        pv = functools.reduce(jnp.add, [_mm(e, _head_lanes(vv, half)) for e, vv in zip(es, vals)])
        out = pv * (1.0 / den)
        acc = out if acc is None else acc + out
    return acc


def _na_ctx_kernel(q_ref, k_ref, v_ref, o_ref, *, seq):
    scale = NA_HEAD_DIM ** -0.5
    nseq = q_ref.shape[0] // seq
    for s in range(nseq):
        r = slice(s * seq, (s + 1) * seq)
        for hp in range(NA_WIDTH // LANES):
            c = slice(hp * LANES, (hp + 1) * LANES)
            o_ref[r, c] = _softmax_pv(q_ref[r, c] * scale, [k_ref[r, c]], [v_ref[r, c]], [None])


def _na_ctx(q, k, v, seq):
    t, w = q.shape
    spec = pl.BlockSpec((SEQ_BLOCK, w), lambda i: (i, 0))
    return pl.pallas_call(
        functools.partial(_na_ctx_kernel, seq=seq),
        grid=(t // SEQ_BLOCK,),
        in_specs=[spec, spec, spec],
        out_specs=spec,
        out_shape=jax.ShapeDtypeStruct((t, w), F32),
        compiler_params=_params(dimension_semantics=("arbitrary",)),
        name="na_context",
    )(q, k, v)


NA_QBLK = 256
NA_KWIN = 768


def _na_lat_kernel(q_ref, k_ref, v_ref, kc_ref, vc_ref, mb_ref, o_ref):
    scale = NA_HEAD_DIM ** -0.5
    i = pl.program_id(1)
    ws = pl.multiple_of(jnp.where(i < 2, 0, SEQ_BLOCK - NA_KWIN), NA_QBLK)
    kw = k_ref[pl.ds(ws, NA_KWIN), :]
    vw = v_ref[pl.ds(ws, NA_KWIN), :]
    o_ref[...] = _softmax_pv(q_ref[...] * scale, [kw, kc_ref[...]], [vw, vc_ref[...]],
                             [lambda half: mb_ref[half, 0], None])


def _na_mask_bias(rel_bias, seq):
    rows = seq // GRID_W
    nblk = seq // NA_QBLK
    rows_per_blk = NA_QBLK // GRID_W
    ql = np.arange(NA_QBLK)
    kl = np.arange(NA_KWIN)
    blk = np.arange(nblk)
    r = blk[:, None] * rows_per_blk + ql[None, :] // GRID_W
    qc = ql % GRID_W
    ws_rows = np.where(blk < nblk // 2, 0, rows - NA_KWIN // GRID_W)
    kr = ws_rows[:, None] + kl[None, :] // GRID_W
    kc = kl % GRID_W
    r0 = np.clip(r - NA_WIN_ROWS // 2, 0, rows - NA_WIN_ROWS)
    row_ok = (kr[:, None, :] >= r0[:, :, None]) & (kr[:, None, :] < r0[:, :, None] + NA_WIN_ROWS)
    c0 = np.clip(qc - NA_WIN_COLS // 2, 0, GRID_W - NA_WIN_COLS)
    col_ok = (kc[None, :] >= c0[:, None]) & (kc[None, :] < c0[:, None] + NA_WIN_COLS)
    valid = row_ok & col_ok[None]
    dr = np.clip(kr[:, None, :] - r[:, :, None] + NA_WIN_ROWS - 1, 0, 2 * NA_WIN_ROWS - 2)
    dc = np.clip(kc[None, :] - qc[:, None] + NA_WIN_COLS - 1, 0, 2 * NA_WIN_COLS - 2)
    dc = np.broadcast_to(dc[None], dr.shape)
    bias = rel_bias.astype(F32)[:, dr, dc]
    return jnp.where(valid[None], bias, _NEG_INF)


def _na_lat(q, k, v, k_ctx, v_ctx, mask_bias, seq):
    t, w = q.shape
    b = t // seq
    nblk = seq // NA_QBLK
    npair = w // LANES
    past = k_ctx.shape[0] // b
    qspec = pl.BlockSpec((NA_QBLK, LANES), lambda hp, i, bb: (bb * nblk + i, hp))
    kspec = pl.BlockSpec((seq, LANES), lambda hp, i, bb: (bb, hp))
    cspec = pl.BlockSpec((past, LANES), lambda hp, i, bb: (bb, hp))
    mspec = pl.BlockSpec((2, 1, NA_QBLK, NA_KWIN), lambda hp, i, bb: (hp, i, 0, 0))
    return pl.pallas_call(
        _na_lat_kernel,
        grid=(npair, nblk, b),
        in_specs=[qspec, kspec, kspec, cspec, cspec, mspec],
        out_specs=qspec,
        out_shape=jax.ShapeDtypeStruct((t, w), F32),
        compiler_params=_params(dimension_semantics=("arbitrary", "arbitrary", "arbitrary")),
        name="na_latent",
    )(q, k, v, k_ctx, v_ctx, mask_bias)


def _pack3(v, lane):
    vm = jnp.where(lane < 32, v, 0.0)
    hi = vm.astype(BF16).astype(F32)
    r1 = vm - hi
    mid = r1.astype(BF16).astype(F32)
    lo = r1 - mid
    return (hi + pltpu.roll(mid, 32, 1) + pltpu.roll(lo, 64, 1)).astype(BF16)


def _unpack3(res):
    return res + pltpu.roll(res, 96, 1) + pltpu.roll(res, 64, 1)


def _ssd_selectors():
    r = np.arange(LANES)
    out = []
    for width in (SSD_HEAD_DIM, SSD_CHUNK):
        l = np.arange(SSD_HEADS * width)
        for d in (0, 1):
            sel = (r[:, None] < 96) & ((r[:, None] % 32) == d * SSD_HEADS + l[None, :] // width)
            out.append(jnp.asarray(sel, BF16))
    return out


def _ssd_kernel(*refs, seq, has_state):
    if has_state:
        (z_ref, xbc_ref, dtr_ref, cw_ref, cb_ref, dtb_ref, alog_ref, dsk_ref, nw_ref,
         s64f_ref, s64b_ref, s128f_ref, s128b_ref, s0_ref,
         y_ref, xpad, xc, dts, las, st) = refs
        sfin_ref = None
    else:
        (z_ref, xbc_ref, dtr_ref, cw_ref, cb_ref, dtb_ref, alog_ref, dsk_ref, nw_ref,
         s64f_ref, s64b_ref, s128f_ref, s128b_ref,
         y_ref, sfin_ref, xpad, xc, dts, las, st) = refs
        s0_ref = None
    ch = SSD_CHUNK
    nseq = z_ref.shape[0] // seq
    nch = seq // ch
    pad = 8
    halo = SSD_CONV // 2
    lane = lax.broadcasted_iota(jnp.int32, (ch, LANES), 1)
    ri = lax.broadcasted_iota(jnp.int32, (ch, ch), 0)
    ci = lax.broadcasted_iota(jnp.int32, (ch, ch), 1)
    lower = ci <= ri
    upper = ci >= ri
    lane_lo = lane < SSD_HEAD_DIM
    neg_a = -jnp.exp(alog_ref[...])
    gw = SSD_D_INNER // SSD_GROUPS

    for s in range(nseq):
        base = s * seq
        xpad[0:pad, :] = jnp.zeros((pad, SSD_XBC), F32)
        xpad[pad + seq:2 * pad + seq, :] = jnp.zeros((pad, SSD_XBC), F32)
        xpad[pad:pad + seq, :] = xbc_ref[base:base + seq, :]

        def conv_body(c, _):
            r0 = pl.multiple_of(c * ch, ch)
            acc = jnp.broadcast_to(cb_ref[...], (ch, SSD_XBC))
            win = xpad[pl.ds(r0, ch + 2 * pad), :]
            for k in range(SSD_CONV):
                acc = acc + cw_ref[k:k + 1, :] * win[pad - halo + k:pad - halo + k + ch, :]
            xc[pl.ds(r0, ch), :] = _silu(acc)
            dt = _softplus(dtr_ref[pl.ds(base + r0, ch), :] + dtb_ref[...])
            dts[pl.ds(r0, ch), :] = dt
            las[pl.ds(r0, ch), :] = dt * neg_a
            return 0

        lax.fori_loop(0, nch, conv_body, 0)

        for d in (0, 1):
            tri = jnp.where(lower if d == 0 else upper, 1.0, 0.0).astype(BF16)
            keep = lower if d == 0 else upper
            sel64 = (s64f_ref, s64b_ref)[d]
            sel128 = (s128f_ref, s128b_ref)[d]
            if has_state:
                st[d] = s0_ref[0, d]
            else:
                st[d] = jnp.zeros((SSD_STATE, SSD_D_INNER), F32)

            def chunk_body(it, _, d=d, tri=tri, keep=keep, sel64=sel64, sel128=sel128):
                c = it if d == 0 else nch - 1 - it
                r0 = pl.multiple_of(c * ch, ch)
                dt_c = dts[pl.ds(r0, ch), :]
                cs = _unpack3(jnp.dot(tri, _pack3(las[pl.ds(r0, ch), :], lane), preferred_element_type=F32))
                edge = ch - 1 if d == 0 else 0
                tot = cs[edge:edge + 1, :]
                e_c = jnp.exp(cs)
                w_c = dt_c * jnp.exp(tot - cs)
                col = jnp.dot(_pack3(cs, lane), sel128[...], preferred_element_type=F32)
                e64 = jnp.dot(_pack3(e_c, lane), sel64[...], preferred_element_type=F32)
                w64 = jnp.dot(_pack3(w_c, lane), sel64[...], preferred_element_type=F32)
                cs_t = cs.T
                dt_t = dt_c.T
                x_c = xc[pl.ds(r0, ch), 0:SSD_D_INNER]
                state = st[d]
                state_b = state.astype(BF16)
                xw = (x_c * w64).astype(BF16)
                y_intra, y_state, upd = [], [], []
                for g in range(SSD_GROUPS):
                    b_g = xc[pl.ds(r0, ch), SSD_D_INNER + g * SSD_STATE:SSD_D_INNER + (g + 1) * SSD_STATE]
                    cofs = SSD_D_INNER + SSD_GROUPS * SSD_STATE
                    c_g = xc[pl.ds(r0, ch), cofs + g * SSD_STATE:cofs + (g + 1) * SSD_STATE].astype(BF16)
                    gram = _mm_nt(c_g, b_g)
                    for pp in range(gw // LANES):
                        p = g * (gw // LANES) + pp
                        ws = []
                        for half in (0, 1):
                            h = 2 * p + half
                            hd = d * SSD_HEADS + h
                            diff = col[:, h * ch:(h + 1) * ch] - cs_t[hd:hd + 1, :]
                            wm = jnp.exp(jnp.where(keep, diff, _NEG_INF)) * gram * dt_t[hd:hd + 1, :]
                            ws.append(wm.astype(BF16))
                        xp = x_c[:, p * LANES:(p + 1) * LANES]
                        xcat = jnp.concatenate([jnp.where(lane_lo, xp, 0.0), jnp.where(lane_lo, 0.0, xp)],
                                               axis=0).astype(BF16)
                        y_intra.append(jnp.dot(jnp.concatenate(ws, axis=1), xcat, preferred_element_type=F32))
                    y_state.append(jnp.dot(c_g, state_b[:, g * gw:(g + 1) * gw], preferred_element_type=F32))
                    upd.append(jnp.dot(b_g.T.astype(BF16), xw[:, g * gw:(g + 1) * gw],
                                       preferred_element_type=F32))
                y_c = jnp.concatenate(y_intra, axis=1) + jnp.concatenate(y_state, axis=1) * e64
                rows = pl.ds(base + r0, ch)
                if d == 0:
                    y_ref[rows, :] = y_c
                else:
                    y_ref[rows, :] = y_ref[rows, :] + y_c
                st[d] = e64[edge:edge + 1, :] * state + jnp.concatenate(upd, axis=1)
                return 0

            lax.fori_loop(0, nch, chunk_body, 0)
            if sfin_ref is not None:
                sfin_ref[s, d] = st[d]

        def out_body(c, _):
            r0 = pl.multiple_of(c * ch, ch)
            rows = pl.ds(base + r0, ch)
            y = y_ref[rows, :] + dsk_ref[...] * xc[pl.ds(r0, ch), 0:SSD_D_INNER]
            y = y * _silu(z_ref[rows, :])
            outs = []
            for g in range(SSD_GROUPS):
                yg = y[:, g * gw:(g + 1) * gw]
                outs.append(yg * lax.rsqrt(jnp.mean(yg * yg, axis=-1, keepdims=True) + EPS))
            y_ref[rows, :] = jnp.concatenate(outs, axis=1) * nw_ref[...]
            return 0

        lax.fori_loop(0, nch, out_body, 0)


def _ssd(z, xbc, dtr, conv_w, conv_b, a_log, dt_bias, d_skip, norm_w, seq, s0=None):
    t = z.shape[0]
    nseq = SEQ_BLOCK // seq
    nblk = t // SEQ_BLOCK
    pad32 = lambda a: jnp.pad(a.astype(F32).reshape(1, -1), ((0, 0), (0, LANES - 2 * SSD_HEADS)))
    consts = [jnp.pad(conv_w.astype(F32), ((0, 8 - SSD_CONV), (0, 0))),
              conv_b.astype(F32).reshape(1, -1), pad32(dt_bias), pad32(a_log),
              jnp.repeat(d_skip.astype(F32), SSD_HEAD_DIM).reshape(1, -1),
              norm_w.astype(F32).reshape(1, -1)] + _ssd_selectors()
    row = lambda n: pl.BlockSpec((SEQ_BLOCK, n), lambda i: (i, 0))
    in_specs = [row(SSD_D_INNER), row(SSD_XBC), row(LANES)] + [_const_spec(c.shape) for c in consts]
    args = [z, xbc, dtr] + consts
    out_specs = [row(SSD_D_INNER)]
    out_shape = [jax.ShapeDtypeStruct((t, SSD_D_INNER), F32)]
    state_shape = (2, SSD_STATE, SSD_D_INNER)
    if s0 is not None:
        in_specs.append(pl.BlockSpec((1,) + state_shape, lambda i: (i, 0, 0, 0)))
        args.append(s0)
    else:
        out_specs.append(pl.BlockSpec((nseq,) + state_shape, lambda i: (i, 0, 0, 0)))
        out_shape.append(jax.ShapeDtypeStruct((t // seq,) + state_shape, F32))
    pad = 8
    scratch = [pltpu.VMEM((seq + 2 * pad, SSD_XBC), F32), pltpu.VMEM((seq, SSD_XBC), F32),
               pltpu.VMEM((seq, LANES), F32), pltpu.VMEM((seq, LANES), F32),
               pltpu.VMEM(state_shape, F32)]
    return pl.pallas_call(
        functools.partial(_ssd_kernel, seq=seq, has_state=s0 is not None),
        grid=(nblk,),
        in_specs=in_specs,
        out_specs=out_specs,
        out_shape=out_shape,
        scratch_shapes=scratch,
        compiler_params=_params(dimension_semantics=("arbitrary",)),
        name="ssd_mixer",
    )(*args)


def _log_sigmoid(x):
    return -_softplus(-x)


def _ret_kernel(*refs, seq, has_state):
    if has_state:
        q_ref, k_ref, v_ref, g_ref, dec_ref, nw_ref, s0_ref, y_ref, st = refs
        sfin_ref = None
    else:
        q_ref, k_ref, v_ref, g_ref, dec_ref, nw_ref, y_ref, sfin_ref, st = refs
        s0_ref = None
    ch = RET_CHUNK
    nseq = q_ref.shape[0] // seq
    nch = seq // ch
    gf = _log_sigmoid(dec_ref[0, 0:1, :])
    gb = _log_sigmoid(dec_ref[0, 1:2, :])
    ri = lax.broadcasted_iota(jnp.int32, (ch, ch), 0)
    ci = lax.broadcasted_iota(jnp.int32, (ch, ch), 1)
    dist = (ri - ci).astype(F32)
    gfk, gbk = gf[:, :ch], gb[:, :ch]
    decay = (jnp.where(ci <= ri, jnp.exp(jnp.where(ci <= ri, dist, 0.0) * gfk), 0.0)
             + jnp.where(ci >= ri, jnp.exp(jnp.where(ci >= ri, -dist, 0.0) * gbk), 0.0))
    pos = lax.broadcasted_iota(jnp.int32, (ch, RET_QK_DIM), 0).astype(F32)
    gfq, gbq = gf[:, :RET_QK_DIM], gb[:, :RET_QK_DIM]
    e_f = jnp.exp((pos + 1.0) * gfq)
    e_b = jnp.exp((ch - pos) * gbq)
    tail_f = jnp.exp((ch - 1.0 - pos) * gfq)
    tail_b = jnp.exp(pos * gbq)
    dec_f = jnp.exp(ch * gf)
    dec_b = jnp.exp(ch * gb)

    for s in range(nseq):
        for d in (0, 1):
            if has_state:
                st[d] = s0_ref[0, d, 0]
            else:
                st[d] = jnp.zeros((RET_QK_DIM, RET_V_DIM), F32)
        for c in range(nch):
            rows = slice(s * seq + c * ch, s * seq + (c + 1) * ch)
            q, k, v = q_ref[rows, :], k_ref[rows, :], v_ref[rows, :].astype(BF16)
            y = _mm(_mm_nt(q, k) * decay, v)
            if has_state or c > 0:
                y = y + _mm(q * e_f, st[0])
            y_ref[rows, :] = y
            st[0] = dec_f * st[0] + _mm((k * tail_f).T, v)
        for c in reversed(range(nch)):
            rows = slice(s * seq + c * ch, s * seq + (c + 1) * ch)
            q, k, v = q_ref[rows, :], k_ref[rows, :], v_ref[rows, :].astype(BF16)
            if has_state or c < nch - 1:
                y_ref[rows, :] = y_ref[rows, :] + _mm(q * e_b, st[1])
            st[1] = dec_b * st[1] + _mm((k * tail_b).T, v)
        if sfin_ref is not None:
            for d in (0, 1):
                sfin_ref[s, d, 0] = st[d]
        for c in range(nch):
            rows = slice(s * seq + c * ch, s * seq + (c + 1) * ch)
            y = y_ref[rows, :]
            y = y * lax.rsqrt(jnp.mean(y * y, axis=-1, keepdims=True) + EPS)
            y_ref[rows, :] = y * nw_ref[...] * _silu(g_ref[rows, :])


def _retention(q, k, v, g, ret_decay, norm_w, seq, s0=None):
    t = q.shape[0]
    nseq = SEQ_BLOCK // seq
    nblk = t // SEQ_BLOCK
    dec = jnp.broadcast_to(jnp.pad(ret_decay.astype(F32).T, ((0, 0), (0, 6)))[:, :, None],
                           (RET_HEADS, 8, RET_V_DIM))
    qspec = pl.BlockSpec((SEQ_BLOCK, RET_QK_DIM), lambda i, h: (i, h))
    vspec = pl.BlockSpec((SEQ_BLOCK, RET_V_DIM), lambda i, h: (i, h))
    in_specs = [qspec, qspec, vspec, vspec,
                pl.BlockSpec((1, 8, RET_V_DIM), lambda i, h: (h, 0, 0)),
                pl.BlockSpec((1, RET_V_DIM), lambda i, h: (0, h))]
    args = [q, k, v, g, dec, norm_w.astype(F32).reshape(1, -1)]
    out_specs = [vspec]
    out_shape = [jax.ShapeDtypeStruct((t, RET_V_W), F32)]
    if s0 is not None:
        in_specs.append(pl.BlockSpec((1, 2, 1, RET_QK_DIM, RET_V_DIM), lambda i, h: (i, 0, h, 0, 0)))
        args.append(s0)
    else:
        out_specs.append(pl.BlockSpec((nseq, 2, 1, RET_QK_DIM, RET_V_DIM), lambda i, h: (i, 0, h, 0, 0)))
        out_shape.append(jax.ShapeDtypeStruct((t // seq, 2, RET_HEADS, RET_QK_DIM, RET_V_DIM), F32))
    return pl.pallas_call(
        functools.partial(_ret_kernel, seq=seq, has_state=s0 is not None),
        grid=(nblk, RET_HEADS),
        in_specs=in_specs,
        out_specs=out_specs,
        out_shape=out_shape,
        scratch_shapes=[pltpu.VMEM((2, RET_QK_DIM, RET_V_DIM), F32)],
        compiler_params=_params(dimension_semantics=("arbitrary", "arbitrary")),
        name="retention_mixer",
    )(*args)


def _out_ffn_kernel(*refs, n_mix, final_norm):
    x_ref = refs[0]
    mix_refs = refs[1:1 + n_mix]
    wo_refs = refs[1 + n_mix:1 + 2 * n_mix]
    mod_ref, n2_ref, w1_ref, w3_ref, w2_ref = refs[1 + 2 * n_mix:6 + 2 * n_mix]
    rest = refs[6 + 2 * n_mix:]
    if final_norm:
        fn_ref, o_ref = rest
    else:
        (o_ref,) = rest
    mixed = None
    for m_ref, wo_ref in zip(mix_refs, wo_refs):
        term = jnp.dot(m_ref[...].astype(BF16), wo_ref[...], preferred_element_type=F32)
        mixed = term if mixed is None else mixed + term
    x1 = x_ref[...] + mod_ref[0, 2:3, :] * mixed
    hb = _norm_mod(x1, n2_ref[...], mod_ref, 3, 4).astype(BF16)

    def body(c, acc):
        h1 = jnp.dot(hb, w1_ref[c], preferred_element_type=F32)
        h3 = jnp.dot(hb, w3_ref[c], preferred_element_type=F32)
        a = (_silu(h1) * h3).astype(BF16)
        return acc + jnp.dot(a, w2_ref[c], preferred_element_type=F32)

    acc = lax.fori_loop(0, w1_ref.shape[0], body, jnp.zeros(x1.shape, F32))
    x2 = x1 + mod_ref[0, 5:6, :] * acc
    if final_norm:
        ms = jnp.mean(x2 * x2, axis=-1, keepdims=True)
        x2 = x2 * lax.rsqrt(ms + EPS) * fn_ref[...]
    o_ref[...] = x2


def _out_ffn(x, mixes, wos, mod, norm2_w, w1, w3, w2, rows_per_mod, final_norm_w=None):
    t, d = x.shape
    tm = ROW_TILE
    per = rows_per_mod // tm
    single = dict(pipeline_mode=pl.Buffered(1))
    in_specs = [pl.BlockSpec((tm, d), lambda i: (i, 0))]
    in_specs += [pl.BlockSpec((tm, m.shape[1]), lambda i: (i, 0)) for m in mixes]
    in_specs += [pl.BlockSpec(w.shape, lambda i: (0, 0), **single) for w in wos]
    in_specs += [pl.BlockSpec((1, 6, d), lambda i: (i // per, 0, 0)), _const_spec((1, d))]
    in_specs += [pl.BlockSpec(w.shape, lambda i: (0, 0, 0), **single) for w in (w1, w3, w2)]
    args = [x, *mixes, *wos, mod, norm2_w.reshape(1, d), w1, w3, w2]
    if final_norm_w is not None:
        in_specs.append(_const_spec((1, d)))
        args.append(final_norm_w.reshape(1, d))
    return pl.pallas_call(
        functools.partial(_out_ffn_kernel, n_mix=len(mixes), final_norm=final_norm_w is not None),
        grid=(t // tm,),
        in_specs=in_specs,
        out_specs=pl.BlockSpec((tm, d), lambda i: (i, 0)),
        out_shape=jax.ShapeDtypeStruct((t, d), F32),
        compiler_params=_params(dimension_semantics=("arbitrary",)),
        name="out_proj_ffn",
    )(*args)


def _ffn_weights(w1, w3, w2):
    d, hidden = w1.shape
    n = hidden // FFN_CHUNK
    split = lambda w: w.astype(BF16).reshape(d, n, FFN_CHUNK).transpose(1, 0, 2)
    return split(w1), split(w3), w2.astype(BF16).reshape(n, FFN_CHUNK, d)


def _rope_tables(seq):
    half = RET_QK_DIM // 2
    t = jnp.arange(seq)
    row = (t // GRID_W).astype(F32)
    col = (t % GRID_W).astype(F32)
    freqs = ROPE_BASE ** (-jnp.arange(0, half, 2, dtype=F32) / half)
    ang = jnp.concatenate([row[:, None] * freqs, col[:, None] * freqs], axis=-1)
    return jnp.cos(ang), jnp.sin(ang)


def kernel(x_prompt, x_sample, cache_l0_na_k, cache_l0_na_v, state_l0_ssd, state_l1_ret, c, c_ctx,
           l0_norm1_w, l0_norm2_w, l0_mod_w, l0_mod_b, l0_w_in, l0_w_out, l0_na_bias, l0_conv_w, l0_conv_b,
           l0_ssd_a_log, l0_ssd_dt_bias, l0_ssd_d, l0_ssd_norm_w, l0_ffn_w1, l0_ffn_w3, l0_ffn_w2,
           l1_norm1_w, l1_norm2_w, l1_mod_w, l1_mod_b, l1_w_in, l1_w_out, l1_ret_decay, l1_ret_norm_w,
           l1_ffn_w1, l1_ffn_w3, l1_ffn_w2, final_norm_w):
    bc, lc, d = x_prompt.shape
    bl, ll, _ = x_sample.shape
    assert d == D_MODEL and ll == SEQ_BLOCK and SEQ_BLOCK % lc == 0 and bc % (SEQ_BLOCK // lc) == 0
    tc, tl = bc * lc, bl * ll
    xc = x_prompt.reshape(tc, d)
    xl = x_sample.reshape(tl, d)

    nrow = 8 * ((1 + bl + 7) // 8)
    cond = jnp.concatenate([c_ctx[None], c, jnp.zeros((nrow - 1 - bl, d), F32)], axis=0)
    mods = []
    for mod_w, mod_b in ((l0_mod_w, l0_mod_b), (l1_mod_w, l1_mod_b)):
        m = _adaln(cond, mod_w, mod_b).reshape(nrow, 6, d)
        mods.append((m[0:1], m[1:1 + bl]))

    w_in0 = jnp.pad(l0_w_in, ((0, 0), (0, LANES - 2 * SSD_HEADS))).astype(BF16)
    wo0 = l0_w_out.astype(BF16)
    wo0_att, wo0_ssd = wo0[:NA_WIDTH], wo0[NA_WIDTH:]
    ffn0 = _ffn_weights(l0_ffn_w1, l0_ffn_w3, l0_ffn_w2)
    ssd_p = (l0_conv_w, l0_conv_b, l0_ssd_a_log, l0_ssd_dt_bias, l0_ssd_d, l0_ssd_norm_w)

    qc, kc, vc, zc, xbcc, dtc = _inproj0(xc, l0_norm1_w, mods[0][0], w_in0, tc)
    att_c = _na_ctx(qc, kc, vc, lc)
    ssd_c, sfin_c = _ssd(zc, xbcc, dtc, *ssd_p, seq=lc)
    xc = _out_ffn(xc, (att_c, ssd_c), (wo0_att, wo0_ssd), mods[0][0], l0_norm2_w, *ffn0, tc)

    ql, kl, vl, zl, xbcl, dtl = _inproj0(xl, l0_norm1_w, mods[0][1], w_in0, ll)
    mask_bias = _na_mask_bias(l0_na_bias, ll)
    past = cache_l0_na_k.shape[1]
    att_l = _na_lat(ql, kl, vl, cache_l0_na_k.reshape(bl * past, NA_WIDTH),
                    cache_l0_na_v.reshape(bl * past, NA_WIDTH), mask_bias, ll)
    s0_ssd = state_l0_ssd.transpose(0, 1, 3, 2, 4).reshape(bl, 2, SSD_STATE, SSD_D_INNER)
    (ssd_l,) = _ssd(zl, xbcl, dtl, *ssd_p, seq=ll, s0=s0_ssd)
    xl = _out_ffn(xl, (att_l, ssd_l), (wo0_att, wo0_ssd), mods[0][1], l0_norm2_w, *ffn0, ll)

    w_in1 = l1_w_in.astype(BF16)
    perm = np.concatenate([np.arange(0, RET_QK_DIM, 2), np.arange(1, RET_QK_DIM, 2)])
    qk_perm = np.concatenate([h * RET_QK_DIM + perm for h in range(2 * RET_HEADS)])
    w_in1_rope = jnp.concatenate([w_in1[:, qk_perm], w_in1[:, 2 * RET_QK_W:]], axis=1)
    wo1 = l1_w_out.astype(BF16)
    ffn1 = _ffn_weights(l1_ffn_w1, l1_ffn_w3, l1_ffn_w2)

    q1, k1, v1, g1 = _inproj1(xc, l1_norm1_w, mods[1][0], w_in1, tc)
    ret_c, ret_state = _retention(q1, k1, v1, g1, l1_ret_decay, l1_ret_norm_w, lc)
    y_prompt = _out_ffn(xc, (ret_c,), (wo1,), mods[1][0], l1_norm2_w, *ffn1, tc, final_norm_w=final_norm_w)

    q1, k1, v1, g1 = _inproj1(xl, l1_norm1_w, mods[1][1], w_in1_rope, ll, rope_tables=_rope_tables(ll))
    (ret_l,) = _retention(q1, k1, v1, g1, l1_ret_decay, l1_ret_norm_w, ll, s0=state_l1_ret[:, :, :, perm, :])
    y_sample = _out_ffn(xl, (ret_l,), (wo1,), mods[1][1], l1_norm2_w, *ffn1, ll, final_norm_w=final_norm_w)

    new_ssd = sfin_c.reshape(bc, 2, SSD_STATE, SSD_HEADS, SSD_HEAD_DIM).transpose(0, 1, 3, 2, 4)
    return (y_prompt.reshape(bc, lc, d), y_sample.reshape(bl, ll, d),
            kc.reshape(bc, lc, NA_HEADS, NA_HEAD_DIM), vc.reshape(bc, lc, NA_HEADS, NA_HEAD_DIM),
            new_ssd, ret_state)
```

```python
import functools

import numpy as np
import jax
import jax.numpy as jnp
from jax import lax
from jax.experimental import pallas as pl
from jax.experimental.pallas import tpu as pltpu

F32 = jnp.float32
BF16 = jnp.bfloat16

D_MODEL = 1024
GRID_W = 64
NA_HEADS = 8
NA_HEAD_DIM = 64
NA_WIDTH = NA_HEADS * NA_HEAD_DIM
NA_WIN_ROWS = 8
NA_WIN_COLS = 16
SSD_HEADS = 16
SSD_HEAD_DIM = 64
SSD_D_INNER = SSD_HEADS * SSD_HEAD_DIM
SSD_GROUPS = 2
SSD_STATE = 128
SSD_CONV = 5
SSD_XBC = SSD_D_INNER + 2 * SSD_GROUPS * SSD_STATE
RET_HEADS = 4
RET_QK_DIM = 256
RET_V_DIM = 512
RET_QK_W = RET_HEADS * RET_QK_DIM
RET_V_W = RET_HEADS * RET_V_DIM
ROPE_BASE = 10000.0
EPS = 1e-6

LANES = 128
SUBLANES = 8
SEQ_BLOCK = 1024
ROW_TILE = 512
SSD_CHUNK = 128
RET_CHUNK = 256
FFN_CHUNK = 256
VMEM_LIMIT = 56 * 1024 * 1024

_NEG_INF = float("-inf")


def _params(**kw):
    return pltpu.CompilerParams(vmem_limit_bytes=VMEM_LIMIT, **kw)


def _silu(x):
    return x * (1.0 / (1.0 + jnp.exp(-x)))


def _softplus(x):
    return jnp.maximum(x, 0.0) + jnp.log(1.0 + jnp.exp(-jnp.abs(x)))


def _mm(a, b):
    return jnp.dot(a.astype(BF16), b.astype(BF16), preferred_element_type=F32)


def _mm_nt(a, b):
    return lax.dot_general(a.astype(BF16), b.astype(BF16), (((1,), (1,)), ((), ())),
                           preferred_element_type=F32)


def _const_spec(shape, **kw):
    nd = len(shape)
    return pl.BlockSpec(shape, lambda *_: (0,) * nd, **kw)


def _adaln_kernel(c_ref, w_ref, b_ref, o_ref):
    o_ref[...] = _mm(_silu(c_ref[...]), w_ref[...]) + b_ref[...]


def _adaln(cond, mod_w, mod_b):
    r, d = cond.shape
    n = mod_w.shape[1]
    tn = 1536
    return pl.pallas_call(
        _adaln_kernel,
        grid=(n // tn,),
        in_specs=[_const_spec((r, d)),
                  pl.BlockSpec((d, tn), lambda j: (0, j)),
                  pl.BlockSpec((1, tn), lambda j: (0, j))],
        out_specs=pl.BlockSpec((r, tn), lambda j: (0, j)),
        out_shape=jax.ShapeDtypeStruct((r, n), F32),
        compiler_params=_params(dimension_semantics=("arbitrary",)),
        name="adaln_mod",
    )(cond, mod_w, mod_b.reshape(1, n))


def _norm_mod(x, nw, mod_ref, shift_idx, scale_idx):
    ms = jnp.mean(x * x, axis=-1, keepdims=True)
    h = x * lax.rsqrt(ms + EPS) * nw
    return h * (1.0 + mod_ref[0, scale_idx:scale_idx + 1, :]) + mod_ref[0, shift_idx:shift_idx + 1, :]


def _inproj0_kernel(x_ref, nw_ref, mod_ref, w_ref, q_ref, k_ref, v_ref, z_ref, xbc_ref, dt_ref):
    hb = _norm_mod(x_ref[...], nw_ref[...], mod_ref, 0, 1).astype(BF16)
    col = 0
    for o_ref in (q_ref, k_ref, v_ref, z_ref, xbc_ref, dt_ref):
        n = o_ref.shape[1]
        o_ref[...] = jnp.dot(hb, w_ref[:, col:col + n], preferred_element_type=F32).astype(o_ref.dtype)
        col += n


def _inproj0(x, norm_w, mod, w, rows_per_mod, qkv_dtype):
    t, d = x.shape
    tm = ROW_TILE
    widths = (NA_WIDTH, NA_WIDTH, NA_WIDTH, SSD_D_INNER, SSD_XBC, LANES)
    dtypes = (qkv_dtype,) * 3 + (F32,) * 3
    assert sum(widths) == w.shape[1]
    per = rows_per_mod // tm
    return pl.pallas_call(
        _inproj0_kernel,
        grid=(t // tm,),
        in_specs=[pl.BlockSpec((tm, d), lambda i: (i, 0)),
                  _const_spec((1, d)),
                  pl.BlockSpec((1, 6, d), lambda i: (i // per, 0, 0)),
                  _const_spec(w.shape)],
        out_specs=[pl.BlockSpec((tm, n), lambda i: (i, 0)) for n in widths],
        out_shape=[jax.ShapeDtypeStruct((t, n), dt) for n, dt in zip(widths, dtypes)],
        compiler_params=_params(dimension_semantics=("arbitrary",)),
        name="l0_in_proj",
    )(x, norm_w.reshape(1, d), mod, w)


def _swap_lane_pairs(x):
    even = (lax.broadcasted_iota(jnp.int32, x.shape, 1) & 1) == 0
    return jnp.where(even, pltpu.roll(x, LANES - 1, 1), pltpu.roll(x, 1, 1))


def _inproj1_kernel(*refs, rope):
    if rope:
        x_ref, nw_ref, mod_ref, w_ref, cos_ref, sin_ref, q_ref, k_ref, v_ref, g_ref = refs
    else:
        x_ref, nw_ref, mod_ref, w_ref, q_ref, k_ref, v_ref, g_ref = refs
    hb = _norm_mod(x_ref[...], nw_ref[...], mod_ref, 0, 1).astype(BF16)
    for o_ref, col, scale in ((q_ref, 0, 1.0), (k_ref, RET_QK_W, RET_QK_DIM ** -0.5)):
        for h in range(RET_HEADS):
            c0 = h * RET_QK_DIM
            y = jnp.dot(hb, w_ref[:, col + c0:col + c0 + RET_QK_DIM], preferred_element_type=F32) * scale
            if rope:
                for j in range(RET_QK_DIM // LANES):
                    lanes = slice(j * LANES, (j + 1) * LANES)
                    yj = y[:, lanes]
                    rot = yj * cos_ref[:, lanes] + _swap_lane_pairs(yj) * sin_ref[:, lanes]
                    o_ref[:, c0 + j * LANES:c0 + (j + 1) * LANES] = rot.astype(o_ref.dtype)
            else:
                o_ref[:, c0:c0 + RET_QK_DIM] = y.astype(o_ref.dtype)
    col = 2 * RET_QK_W
    for o_ref in (v_ref, g_ref):
        o_ref[...] = jnp.dot(hb, w_ref[:, col:col + RET_V_W], preferred_element_type=F32).astype(o_ref.dtype)
        col += RET_V_W


def _inproj1(x, norm_w, mod, w, rows_per_mod, rope_tables=None):
    t, d = x.shape
    tm = ROW_TILE
    per = rows_per_mod // tm
    widths = (RET_QK_W, RET_QK_W, RET_V_W, RET_V_W)
    dtypes = (BF16, BF16, BF16, F32)
    in_specs = [pl.BlockSpec((tm, d), lambda i: (i, 0)),
                _const_spec((1, d)),
                pl.BlockSpec((1, 6, d), lambda i: (i // per, 0, 0)),
                _const_spec(w.shape)]
    args = [x, norm_w.reshape(1, d), mod, w]
    if rope_tables is not None:
        cos, sin = rope_tables
        nblk = cos.shape[0] // tm
        in_specs += [pl.BlockSpec((tm, cos.shape[1]), lambda i: (i % nblk, 0))] * 2
        args += [cos, sin]
    return pl.pallas_call(
        functools.partial(_inproj1_kernel, rope=rope_tables is not None),
        grid=(t // tm,),
        in_specs=in_specs,
        out_specs=[pl.BlockSpec((tm, n), lambda i: (i, 0)) for n in widths],
        out_shape=[jax.ShapeDtypeStruct((t, n), dt) for n, dt in zip(widths, dtypes)],
        compiler_params=_params(dimension_semantics=("arbitrary",)),
        name="l1_in_proj",
    )(*args)


def _lane_lo(shape):
    return lax.broadcasted_iota(jnp.int32, shape, 1) < NA_HEAD_DIM


def _head_lanes(x, half):
    lo = _lane_lo(x.shape)
    return jnp.where(lo if half == 0 else jnp.logical_not(lo), x, jnp.zeros_like(x))


def _softmax_pv(q2, keys, vals, biases):
    acc = None
    for half in (0, 1):
        qm = _head_lanes(q2, half)
        scores = []
        for kk, bb in zip(keys, biases):
            s = _mm_nt(qm, kk)
            if bb is not None:
                s = s + bb(half)
            scores.append(s)
        mx = functools.reduce(jnp.maximum, [jnp.max(s, axis=-1, keepdims=True) for s in scores])
        es = [jnp.exp(s - mx) for s in scores]
        den = functools.reduce(jnp.add, [jnp.sum(e, axis=-1, keepdims=True) for e in es])
        pv = functools.reduce(jnp.add, [_mm(e, _head_lanes(vv, half)) for e, vv in zip(es, vals)])
        out = pv * (1.0 / den)
        acc = out if acc is None else acc + out
    return acc


def _na_ctx_kernel(q_ref, k_ref, v_ref, o_ref, *, seq):
    scale = NA_HEAD_DIM ** -0.5
    nseq = q_ref.shape[0] // seq
    for s in range(nseq):
        r = slice(s * seq, (s + 1) * seq)
        for hp in range(NA_WIDTH // LANES):
            c = slice(hp * LANES, (hp + 1) * LANES)
            out = _softmax_pv(q_ref[r, c] * scale, [k_ref[r, c]], [v_ref[r, c]], [None])
            o_ref[r, c] = out.astype(o_ref.dtype)


def _na_ctx(q, k, v, seq):
    t, w = q.shape
    spec = pl.BlockSpec((SEQ_BLOCK, w), lambda i: (i, 0))
    return pl.pallas_call(
        functools.partial(_na_ctx_kernel, seq=seq),
        grid=(t // SEQ_BLOCK,),
        in_specs=[spec, spec, spec],
        out_specs=spec,
        out_shape=jax.ShapeDtypeStruct((t, w), BF16),
        compiler_params=_params(dimension_semantics=("arbitrary",)),
        name="na_context",
    )(q, k, v)


NA_QBLK = 256
NA_KWIN = 768


def _na_lat_kernel(q_ref, k_ref, v_ref, kc_ref, vc_ref, mb_ref, o_ref):
    scale = NA_HEAD_DIM ** -0.5
    i = pl.program_id(1)
    ws = pl.multiple_of(jnp.where(i < 2, 0, SEQ_BLOCK - NA_KWIN), NA_QBLK)
    kw = k_ref[pl.ds(ws, NA_KWIN), :]
    vw = v_ref[pl.ds(ws, NA_KWIN), :]
    out = _softmax_pv(q_ref[...] * scale, [kw, kc_ref[...]], [vw, vc_ref[...]],
                      [lambda half: mb_ref[half, 0], None])
    o_ref[...] = out.astype(o_ref.dtype)


def _na_mask_bias(rel_bias, seq):
    rows = seq // GRID_W
    nblk = seq // NA_QBLK
    rows_per_blk = NA_QBLK // GRID_W
    win_rows = NA_KWIN // GRID_W
    h = rel_bias.shape[0]
    col = np.arange(GRID_W)
    c0 = np.clip(col - NA_WIN_COLS // 2, 0, GRID_W - NA_WIN_COLS)
    col_ok = (col[None, :] >= c0[:, None]) & (col[None, :] < c0[:, None] + NA_WIN_COLS)
    dc = np.clip(col[None, :] - col[:, None] + NA_WIN_COLS - 1, 0, 2 * NA_WIN_COLS - 2)
    tb = jnp.take(rel_bias.astype(F32), jnp.asarray(dc.reshape(-1)), axis=2)
    tb = jnp.where(col_ok, tb.reshape(h, 2 * NA_WIN_ROWS - 1, GRID_W, GRID_W), _NEG_INF)
    neg = jnp.full((h, GRID_W, GRID_W), _NEG_INF, F32)
    blocks = []
    for i in range(nblk):
        ws = 0 if i < nblk // 2 else rows - win_rows
        qrows = []
        for rq in range(rows_per_blk):
            r = i * rows_per_blk + rq
            r0 = min(max(r - NA_WIN_ROWS // 2, 0), rows - NA_WIN_ROWS)
            krow = [tb[:, ws + kb - r + NA_WIN_ROWS - 1] if r0 <= ws + kb < r0 + NA_WIN_ROWS else neg
                    for kb in range(win_rows)]
            qrows.append(jnp.concatenate(krow, axis=-1))
        blocks.append(jnp.concatenate(qrows, axis=-2))
    return jnp.stack(blocks, axis=1)


def _na_lat(q, k, v, k_ctx, v_ctx, mask_bias, seq):
    t, w = q.shape
    b = t // seq
    nblk = seq // NA_QBLK
    npair = w // LANES
    past = k_ctx.shape[0] // b
    qspec = pl.BlockSpec((NA_QBLK, LANES), lambda hp, i, bb: (bb * nblk + i, hp))
    kspec = pl.BlockSpec((seq, LANES), lambda hp, i, bb: (bb, hp))
    cspec = pl.BlockSpec((past, LANES), lambda hp, i, bb: (bb, hp))
    mspec = pl.BlockSpec((2, 1, NA_QBLK, NA_KWIN), lambda hp, i, bb: (hp, i, 0, 0))
    return pl.pallas_call(
        _na_lat_kernel,
        grid=(npair, nblk, b),
        in_specs=[qspec, kspec, kspec, cspec, cspec, mspec],
        out_specs=qspec,
        out_shape=jax.ShapeDtypeStruct((t, w), BF16),
        compiler_params=_params(dimension_semantics=("arbitrary", "arbitrary", "arbitrary")),
        name="na_latent",
    )(q, k, v, k_ctx, v_ctx, mask_bias)


def _pack3(v, lane):
    vm = jnp.where(lane < 32, v, 0.0)
    hi = vm.astype(BF16).astype(F32)
    r1 = vm - hi
    mid = r1.astype(BF16).astype(F32)
    lo = r1 - mid
    return (hi + pltpu.roll(mid, 32, 1) + pltpu.roll(lo, 64, 1)).astype(BF16)


def _unpack3(res):
    return res + pltpu.roll(res, 96, 1) + pltpu.roll(res, 64, 1)


def _ssd_selectors():
    r = np.arange(LANES)
    out = []
    for width in (SSD_HEAD_DIM, SSD_CHUNK):
        l = np.arange(SSD_HEADS * width)
        for d in (0, 1):
            sel = (r[:, None] < 96) & ((r[:, None] % 32) == d * SSD_HEADS + l[None, :] // width)
            out.append(jnp.asarray(sel, BF16))
    return out


def _ssd_kernel(*refs, seq, has_state):
    if has_state:
        (z_ref, xbc_ref, dtr_ref, cw_ref, cb_ref, dtb_ref, alog_ref, dsk_ref, nw_ref,
         s64f_ref, s64b_ref, s128f_ref, s128b_ref, s0_ref,
         y_ref, xpad, xc, yacc, csp, ep, wp, cst, dtt, st) = refs
        sfin_ref = None
    else:
        (z_ref, xbc_ref, dtr_ref, cw_ref, cb_ref, dtb_ref, alog_ref, dsk_ref, nw_ref,
         s64f_ref, s64b_ref, s128f_ref, s128b_ref,
         y_ref, sfin_ref, xpad, xc, yacc, csp, ep, wp, cst, dtt, st) = refs
        s0_ref = None
    ch = SSD_CHUNK
    nseq = z_ref.shape[0] // seq
    nch = seq // ch
    assert nch % 2 == 0
    pad = SUBLANES
    lane = lax.broadcasted_iota(jnp.int32, (ch, LANES), 1)
    ri = lax.broadcasted_iota(jnp.int32, (ch, ch), 0)
    ci = lax.broadcasted_iota(jnp.int32, (ch, ch), 1)
    keeps = (ci <= ri, ci >= ri)
    tril = jnp.where(keeps[0], 1.0, 0.0).astype(BF16)
    triu = jnp.where(keeps[1], 1.0, 0.0).astype(BF16)
    lane_lo = lane < SSD_HEAD_DIM
    fwd_lane = lane < SSD_HEADS
    neg_a = -jnp.exp(alog_ref[...])
    gw = SSD_D_INNER // SSD_GROUPS
    sel64s = (s64f_ref, s64b_ref)
    sel128s = (s128f_ref, s128b_ref)
    edges = (ch - 1, 0)

    def seq_body(s, _):
        base = pl.multiple_of(s * seq, seq)
        xpad[0:pad, :] = jnp.zeros((pad, SSD_XBC), F32)
        xpad[pad + seq:2 * pad + seq, :] = jnp.zeros((pad, SSD_XBC), F32)
        xpad[pad:pad + seq, :] = xbc_ref[pl.ds(base, seq), :]

        def prep_body(c, _):
            r0 = pl.multiple_of(c * ch, ch)
            win = xpad[pl.ds(r0, ch + 2 * pad), :]
            conv = jnp.broadcast_to(cb_ref[...], (ch, SSD_XBC))
            for k in range(SSD_CONV):
                ofs = pad - SSD_CONV // 2 + k
                conv = conv + cw_ref[k:k + 1, :] * win[ofs:ofs + ch, :]
            act = _silu(conv)
            xc[pl.ds(r0, ch), :] = act
            yacc[pl.ds(r0, ch), :] = dsk_ref[...] * act[:, 0:SSD_D_INNER]
            return 0

        lax.fori_loop(0, nch, prep_body, 0)

        def decay_body(c, _):
            r0 = pl.multiple_of(c * ch, ch)
            dt = _softplus(dtr_ref[pl.ds(base + r0, ch), :] + dtb_ref[...])
            la = _pack3(dt * neg_a, lane)
            cs = jnp.where(fwd_lane,
                           _unpack3(jnp.dot(tril, la, preferred_element_type=F32)),
                           _unpack3(jnp.dot(triu, la, preferred_element_type=F32)))
            tot = jnp.where(fwd_lane, cs[ch - 1:ch, :], cs[0:1, :])
            rows = pl.ds(r0, ch)
            csp[rows, :] = _pack3(cs, lane)
            ep[rows, :] = _pack3(jnp.exp(cs), lane)
            wp[rows, :] = _pack3(dt * jnp.exp(tot - cs), lane)
            cst[c] = cs.T
            dtt[c] = dt.T
            return 0

        lax.fori_loop(0, nch, decay_body, 0)

        for d in (0, 1):
            if has_state:
                st[d] = s0_ref[0, d]
            else:
                st[d] = jnp.zeros((SSD_STATE, SSD_D_INNER), F32)

        def chunk(d, c):
            r0 = pl.multiple_of(c * ch, ch)
            rows = pl.ds(r0, ch)
            col = jnp.dot(csp[rows, :], sel128s[d][...], preferred_element_type=F32)
            e64 = jnp.dot(ep[rows, :], sel64s[d][...], preferred_element_type=F32)
            w64 = jnp.dot(wp[rows, :], sel64s[d][...], preferred_element_type=F32)
            cs_t = cst[c]
            dt_t = dtt[c]
            x_c = xc[rows, 0:SSD_D_INNER]
            state = st[d]
            state_b = state.astype(BF16)
            xw = (x_c * w64).astype(BF16)
            y_intra, y_state, upd = [], [], []
            for g in range(SSD_GROUPS):
                b_g = xc[rows, SSD_D_INNER + g * SSD_STATE:SSD_D_INNER + (g + 1) * SSD_STATE]
                cofs = SSD_D_INNER + SSD_GROUPS * SSD_STATE
                c_g = xc[rows, cofs + g * SSD_STATE:cofs + (g + 1) * SSD_STATE].astype(BF16)
                gram = _mm_nt(c_g, b_g)
                for pp in range(gw // LANES):
                    p = g * (gw // LANES) + pp
                    ws = []
                    for half in (0, 1):
                        h = 2 * p + half
                        hd = d * SSD_HEADS + h
                        diff = col[:, h * ch:(h + 1) * ch] - cs_t[hd:hd + 1, :]
                        wm = jnp.exp(jnp.where(keeps[d], diff, _NEG_INF)) * gram * dt_t[hd:hd + 1, :]
                        ws.append(wm.astype(BF16))
                    xp = x_c[:, p * LANES:(p + 1) * LANES]
                    xcat = jnp.concatenate([jnp.where(lane_lo, xp, 0.0), jnp.where(lane_lo, 0.0, xp)],
                                           axis=0).astype(BF16)
                    y_intra.append(jnp.dot(jnp.concatenate(ws, axis=1), xcat, preferred_element_type=F32))
                y_state.append(jnp.dot(c_g, state_b[:, g * gw:(g + 1) * gw], preferred_element_type=F32))
                upd.append(jnp.dot(b_g.T.astype(BF16), xw[:, g * gw:(g + 1) * gw],
                                   preferred_element_type=F32))
            yacc[rows, :] = (yacc[rows, :] + jnp.concatenate(y_intra, axis=1)
                             + jnp.concatenate(y_state, axis=1) * e64)
            st[d] = e64[edges[d]:edges[d] + 1, :] * state + jnp.concatenate(upd, axis=1)

        def pair_body(j, _):
            for step in (0, 1):
                chunk(0, 2 * j + step)
                chunk(1, nch - 1 - 2 * j - step)
            return 0

        lax.fori_loop(0, nch // 2, pair_body, 0)
        if sfin_ref is not None:
            for d in (0, 1):
                sfin_ref[s, d] = st[d]

        def out_body(c, _):
            r0 = pl.multiple_of(c * ch, ch)
            y = yacc[pl.ds(r0, ch), :] * _silu(z_ref[pl.ds(base + r0, ch), :])
            outs = []
            for g in range(SSD_GROUPS):
                yg = y[:, g * gw:(g + 1) * gw]
                outs.append(yg * lax.rsqrt(jnp.mean(yg * yg, axis=-1, keepdims=True) + EPS))
            y_ref[pl.ds(base + r0, ch), :] = (jnp.concatenate(outs, axis=1) * nw_ref[...]).astype(y_ref.dtype)
            return 0

        lax.fori_loop(0, nch, out_body, 0)
        return 0

    lax.fori_loop(0, nseq, seq_body, 0)


def _ssd(z, xbc, dtr, conv_w, conv_b, a_log, dt_bias, d_skip, norm_w, seq, s0=None):
    t = z.shape[0]
    nseq = SEQ_BLOCK // seq
    nblk = t // SEQ_BLOCK
    nch = seq // SSD_CHUNK
    pad32 = lambda a: jnp.pad(a.astype(F32).reshape(1, -1), ((0, 0), (0, LANES - 2 * SSD_HEADS)))
    consts = [jnp.pad(conv_w.astype(F32), ((0, SUBLANES - SSD_CONV), (0, 0))),
              conv_b.astype(F32).reshape(1, -1), pad32(dt_bias), pad32(a_log),
              jnp.repeat(d_skip.astype(F32), SSD_HEAD_DIM).reshape(1, -1),
              norm_w.astype(F32).reshape(1, -1)] + _ssd_selectors()
    row = lambda n: pl.BlockSpec((SEQ_BLOCK, n), lambda i: (i, 0))
    in_specs = [row(SSD_D_INNER), row(SSD_XBC), row(LANES)] + [_const_spec(c.shape) for c in consts]
    args = [z, xbc, dtr] + consts
    out_specs = [row(SSD_D_INNER)]
    out_shape = [jax.ShapeDtypeStruct((t, SSD_D_INNER), BF16)]
    state_shape = (2, SSD_STATE, SSD_D_INNER)
    if s0 is not None:
        in_specs.append(pl.BlockSpec((1,) + state_shape, lambda i: (i, 0, 0, 0)))
        args.append(s0)
    else:
        out_specs.append(pl.BlockSpec((nseq,) + state_shape, lambda i: (i, 0, 0, 0)))
        out_shape.append(jax.ShapeDtypeStruct((t // seq,) + state_shape, F32))
    scratch = [pltpu.VMEM((seq + 2 * SUBLANES, SSD_XBC), F32),
               pltpu.VMEM((seq, SSD_XBC), F32),
               pltpu.VMEM((seq, SSD_D_INNER), F32),
               pltpu.VMEM((seq, LANES), BF16), pltpu.VMEM((seq, LANES), BF16), pltpu.VMEM((seq, LANES), BF16),
               pltpu.VMEM((nch, LANES, SSD_CHUNK), F32), pltpu.VMEM((nch, LANES, SSD_CHUNK), F32),
               pltpu.VMEM(state_shape, F32)]
    return pl.pallas_call(
        functools.partial(_ssd_kernel, seq=seq, has_state=s0 is not None),
        grid=(nblk,),
        in_specs=in_specs,
        out_specs=out_specs,
        out_shape=out_shape,
        scratch_shapes=scratch,
        compiler_params=_params(dimension_semantics=("arbitrary",)),
        name="ssd_mixer",
    )(*args)


def _log_sigmoid(x):
    return -_softplus(-x)


def _ret_kernel(*refs, seq, has_state):
    if has_state:
        q_ref, k_ref, v_ref, g_ref, dec_ref, nw_ref, s0_ref, y_ref, yacc, st = refs
        sfin_ref = None
    else:
        q_ref, k_ref, v_ref, g_ref, dec_ref, nw_ref, y_ref, sfin_ref, yacc, st = refs
        s0_ref = None
    ch = RET_CHUNK
    nseq = q_ref.shape[0] // seq
    nch = seq // ch
    gf = _log_sigmoid(dec_ref[0, 0:1, :])
    gb = _log_sigmoid(dec_ref[0, 1:2, :])
    ri = lax.broadcasted_iota(jnp.int32, (ch, ch), 0)
    ci = lax.broadcasted_iota(jnp.int32, (ch, ch), 1)
    dist = (ri - ci).astype(F32)
    gfk, gbk = gf[:, :ch], gb[:, :ch]
    decay = (jnp.where(ci <= ri, jnp.exp(jnp.where(ci <= ri, dist, 0.0) * gfk), 0.0)
             + jnp.where(ci >= ri, jnp.exp(jnp.where(ci >= ri, -dist, 0.0) * gbk), 0.0))
    pos = lax.broadcasted_iota(jnp.int32, (ch, RET_QK_DIM), 0).astype(F32)
    gfq, gbq = gf[:, :RET_QK_DIM], gb[:, :RET_QK_DIM]
    e_f = jnp.exp((pos + 1.0) * gfq)
    e_b = jnp.exp((ch - pos) * gbq)
    tail_f = jnp.exp((ch - 1.0 - pos) * gfq)
    tail_b = jnp.exp(pos * gbq)
    dec_f = jnp.exp(ch * gf)
    dec_b = jnp.exp(ch * gb)

    for s in range(nseq):
        for d in (0, 1):
            if has_state:
                st[d] = s0_ref[0, d, 0]
            else:
                st[d] = jnp.zeros((RET_QK_DIM, RET_V_DIM), F32)
        for c in range(nch):
            rows = slice(s * seq + c * ch, s * seq + (c + 1) * ch)
            q, k, v = q_ref[rows, :], k_ref[rows, :].astype(F32), v_ref[rows, :]
            y = _mm(_mm_nt(q, k) * decay, v)
            if has_state or c > 0:
                y = y + _mm(q.astype(F32) * e_f, st[0])
            yacc[rows, :] = y
            st[0] = dec_f * st[0] + _mm((k * tail_f).T, v)
        for c in reversed(range(nch)):
            rows = slice(s * seq + c * ch, s * seq + (c + 1) * ch)
            q, k, v = q_ref[rows, :], k_ref[rows, :].astype(F32), v_ref[rows, :]
            if has_state or c < nch - 1:
                yacc[rows, :] = yacc[rows, :] + _mm(q.astype(F32) * e_b, st[1])
            st[1] = dec_b * st[1] + _mm((k * tail_b).T, v)
        if sfin_ref is not None:
            for d in (0, 1):
                sfin_ref[s, d, 0] = st[d]
        for c in range(nch):
            rows = slice(s * seq + c * ch, s * seq + (c + 1) * ch)
            y = yacc[rows, :]
            y = y * lax.rsqrt(jnp.mean(y * y, axis=-1, keepdims=True) + EPS)
            y_ref[rows, :] = (y * nw_ref[...] * _silu(g_ref[rows, :])).astype(y_ref.dtype)


def _retention(q, k, v, g, ret_decay, norm_w, seq, s0=None):
    t = q.shape[0]
    nseq = SEQ_BLOCK // seq
    nblk = t // SEQ_BLOCK
    dec = jnp.broadcast_to(jnp.pad(ret_decay.astype(F32).T, ((0, 0), (0, SUBLANES - 2)))[:, :, None],
                           (RET_HEADS, SUBLANES, RET_V_DIM))
    qspec = pl.BlockSpec((SEQ_BLOCK, RET_QK_DIM), lambda i, h: (i, h))
    vspec = pl.BlockSpec((SEQ_BLOCK, RET_V_DIM), lambda i, h: (i, h))
    in_specs = [qspec, qspec, vspec, vspec,
                pl.BlockSpec((1, SUBLANES, RET_V_DIM), lambda i, h: (h, 0, 0)),
                pl.BlockSpec((1, RET_V_DIM), lambda i, h: (0, h))]
    args = [q, k, v, g, dec, norm_w.astype(F32).reshape(1, -1)]
    out_specs = [vspec]
    out_shape = [jax.ShapeDtypeStruct((t, RET_V_W), BF16)]
    if s0 is not None:
        in_specs.append(pl.BlockSpec((1, 2, 1, RET_QK_DIM, RET_V_DIM), lambda i, h: (i, 0, h, 0, 0)))
        args.append(s0)
    else:
        out_specs.append(pl.BlockSpec((nseq, 2, 1, RET_QK_DIM, RET_V_DIM), lambda i, h: (i, 0, h, 0, 0)))
        out_shape.append(jax.ShapeDtypeStruct((t // seq, 2, RET_HEADS, RET_QK_DIM, RET_V_DIM), F32))
    return pl.pallas_call(
        functools.partial(_ret_kernel, seq=seq, has_state=s0 is not None),
        grid=(nblk, RET_HEADS),
        in_specs=in_specs,
        out_specs=out_specs,
        out_shape=out_shape,
        scratch_shapes=[pltpu.VMEM((SEQ_BLOCK, RET_V_DIM), F32),
                        pltpu.VMEM((2, RET_QK_DIM, RET_V_DIM), F32)],
        compiler_params=_params(dimension_semantics=("arbitrary", "arbitrary")),
        name="retention_mixer",
    )(*args)


def _out_ffn_kernel(*refs, n_mix, final_norm):
    x_ref = refs[0]
    mix_refs = refs[1:1 + n_mix]
    wo_ref, mod_ref, n2_ref, w1_ref, w3_ref, w2_ref = refs[1 + n_mix:7 + n_mix]
    if final_norm:
        fn_ref, o_ref, act = refs[7 + n_mix:]
    else:
        o_ref, act = refs[7 + n_mix:]
    mixed, row = None, 0
    for m_ref in mix_refs:
        k = m_ref.shape[1]
        term = jnp.dot(m_ref[...], wo_ref[row:row + k, :], preferred_element_type=F32)
        mixed = term if mixed is None else mixed + term
        row += k
    x1 = x_ref[...] + mod_ref[0, 2:3, :] * mixed
    hb = _norm_mod(x1, n2_ref[...], mod_ref, 3, 4).astype(BF16)
    for c in range(w1_ref.shape[1] // FFN_CHUNK):
        cols = slice(c * FFN_CHUNK, (c + 1) * FFN_CHUNK)
        h1 = jnp.dot(hb, w1_ref[:, cols], preferred_element_type=F32)
        h3 = jnp.dot(hb, w3_ref[:, cols], preferred_element_type=F32)
        act[:, cols] = (_silu(h1) * h3).astype(BF16)
    x2 = x1 + mod_ref[0, 5:6, :] * jnp.dot(act[...], w2_ref[...], preferred_element_type=F32)
    if final_norm:
        ms = jnp.mean(x2 * x2, axis=-1, keepdims=True)
        x2 = x2 * lax.rsqrt(ms + EPS) * fn_ref[...]
    o_ref[...] = x2


def _out_ffn(x, mixes, wo, mod, norm2_w, w1, w3, w2, rows_per_mod, final_norm_w=None):
    t, d = x.shape
    tm = ROW_TILE
    per = rows_per_mod // tm
    single = dict(pipeline_mode=pl.Buffered(1))
    in_specs = [pl.BlockSpec((tm, d), lambda i: (i, 0))]
    in_specs += [pl.BlockSpec((tm, m.shape[1]), lambda i: (i, 0)) for m in mixes]
    in_specs += [_const_spec(wo.shape, **single),
                 pl.BlockSpec((1, 6, d), lambda i: (i // per, 0, 0)), _const_spec((1, d))]
    in_specs += [_const_spec(w.shape, **single) for w in (w1, w3, w2)]
    args = [x, *mixes, wo, mod, norm2_w.reshape(1, d), w1, w3, w2]
    if final_norm_w is not None:
        in_specs.append(_const_spec((1, d)))
        args.append(final_norm_w.reshape(1, d))
    return pl.pallas_call(
        functools.partial(_out_ffn_kernel, n_mix=len(mixes), final_norm=final_norm_w is not None),
        grid=(t // tm,),
        in_specs=in_specs,
        out_specs=pl.BlockSpec((tm, d), lambda i: (i, 0)),
        out_shape=jax.ShapeDtypeStruct((t, d), F32),
        scratch_shapes=[pltpu.VMEM((tm, w1.shape[1]), BF16)],
        compiler_params=_params(dimension_semantics=("arbitrary",)),
        name="out_proj_ffn",
    )(*args)


def _rope_tables(seq):
    half = RET_QK_DIM // 2
    t = jnp.arange(seq)
    row = (t // GRID_W).astype(F32)
    col = (t % GRID_W).astype(F32)
    freqs = ROPE_BASE ** (-jnp.arange(0, half, 2, dtype=F32) / half)
    ang = jnp.concatenate([row[:, None] * freqs, col[:, None] * freqs], axis=-1)
    cos = jnp.repeat(jnp.cos(ang), 2, axis=1)
    sin = jnp.stack([-jnp.sin(ang), jnp.sin(ang)], axis=-1).reshape(seq, RET_QK_DIM)
    return cos, sin


def kernel(x_prompt, x_sample, cache_l0_na_k, cache_l0_na_v, state_l0_ssd, state_l1_ret, c, c_ctx,
           l0_norm1_w, l0_norm2_w, l0_mod_w, l0_mod_b, l0_w_in, l0_w_out, l0_na_bias, l0_conv_w, l0_conv_b,
           l0_ssd_a_log, l0_ssd_dt_bias, l0_ssd_d, l0_ssd_norm_w, l0_ffn_w1, l0_ffn_w3, l0_ffn_w2,
           l1_norm1_w, l1_norm2_w, l1_mod_w, l1_mod_b, l1_w_in, l1_w_out, l1_ret_decay, l1_ret_norm_w,
           l1_ffn_w1, l1_ffn_w3, l1_ffn_w2, final_norm_w):
    bc, lc, d = x_prompt.shape
    bl, ll, _ = x_sample.shape
    assert d == D_MODEL and ll == SEQ_BLOCK and SEQ_BLOCK % lc == 0 and bc % (SEQ_BLOCK // lc) == 0
    tc, tl = bc * lc, bl * ll
    xc = x_prompt.reshape(tc, d)
    xl = x_sample.reshape(tl, d)

    nrow = SUBLANES * ((1 + bl + SUBLANES - 1) // SUBLANES)
    cond = jnp.concatenate([c_ctx[None], c, jnp.zeros((nrow - 1 - bl, d), F32)], axis=0)
    mods = []
    for mod_w, mod_b in ((l0_mod_w, l0_mod_b), (l1_mod_w, l1_mod_b)):
        m = _adaln(cond, mod_w, mod_b).reshape(nrow, 6, d)
        mods.append((m[0:1], m[1:1 + bl]))

    w_in0 = jnp.pad(l0_w_in.astype(BF16), ((0, 0), (0, LANES - 2 * SSD_HEADS)))
    wo0 = l0_w_out.astype(BF16)
    ffn0 = (l0_ffn_w1.astype(BF16), l0_ffn_w3.astype(BF16), l0_ffn_w2.astype(BF16))
    ssd_p = (l0_conv_w, l0_conv_b, l0_ssd_a_log, l0_ssd_dt_bias, l0_ssd_d, l0_ssd_norm_w)

    qc, kc, vc, zc, xbcc, dtc = _inproj0(xc, l0_norm1_w, mods[0][0], w_in0, tc, F32)
    att_c = _na_ctx(qc, kc, vc, lc)
    ssd_c, sfin_c = _ssd(zc, xbcc, dtc, *ssd_p, seq=lc)
    xc = _out_ffn(xc, (att_c, ssd_c), wo0, mods[0][0], l0_norm2_w, *ffn0, tc)

    ql, kl, vl, zl, xbcl, dtl = _inproj0(xl, l0_norm1_w, mods[0][1], w_in0, ll, BF16)
    mask_bias = _na_mask_bias(l0_na_bias, ll)
    past = cache_l0_na_k.shape[1]
    att_l = _na_lat(ql, kl, vl, cache_l0_na_k.reshape(bl * past, NA_WIDTH),
                    cache_l0_na_v.reshape(bl * past, NA_WIDTH), mask_bias, ll)
    s0_ssd = state_l0_ssd.transpose(0, 1, 3, 2, 4).reshape(bl, 2, SSD_STATE, SSD_D_INNER)
    (ssd_l,) = _ssd(zl, xbcl, dtl, *ssd_p, seq=ll, s0=s0_ssd)
    xl = _out_ffn(xl, (att_l, ssd_l), wo0, mods[0][1], l0_norm2_w, *ffn0, ll)

    w_in1 = l1_w_in.astype(BF16)
    wo1 = l1_w_out.astype(BF16)
    ffn1 = (l1_ffn_w1.astype(BF16), l1_ffn_w3.astype(BF16), l1_ffn_w2.astype(BF16))

    q1, k1, v1, g1 = _inproj1(xc, l1_norm1_w, mods[1][0], w_in1, tc)
    ret_c, ret_state = _retention(q1, k1, v1, g1, l1_ret_decay, l1_ret_norm_w, lc)
    y_prompt = _out_ffn(xc, (ret_c,), wo1, mods[1][0], l1_norm2_w, *ffn1, tc, final_norm_w=final_norm_w)

    q1, k1, v1, g1 = _inproj1(xl, l1_norm1_w, mods[1][1], w_in1, ll, rope_tables=_rope_tables(ll))
    (ret_l,) = _retention(q1, k1, v1, g1, l1_ret_decay, l1_ret_norm_w, ll, s0=state_l1_ret)
    y_sample = _out_ffn(xl, (ret_l,), wo1, mods[1][1], l1_norm2_w, *ffn1, ll, final_norm_w=final_norm_w)

    new_ssd = sfin_c.reshape(bc, 2, SSD_STATE, SSD_HEADS, SSD_HEAD_DIM).transpose(0, 1, 3, 2, 4)
    return (y_prompt.reshape(bc, lc, d), y_sample.reshape(bl, ll, d),
            kc.reshape(bc, lc, NA_HEADS, NA_HEAD_DIM), vc.reshape(bc, lc, NA_HEADS, NA_HEAD_DIM),
            new_ssd, ret_state)
```

```python
import functools

import numpy as np
import jax
import jax.numpy as jnp
from jax import lax
from jax.experimental import pallas as pl
from jax.experimental.pallas import tpu as pltpu

F32 = jnp.float32
BF16 = jnp.bfloat16

D_MODEL = 1024
GRID_W = 64
NA_HEADS = 8
NA_HEAD_DIM = 64
NA_WIDTH = NA_HEADS * NA_HEAD_DIM
NA_WIN_ROWS = 8
NA_WIN_COLS = 16
SSD_HEADS = 16
SSD_HEAD_DIM = 64
SSD_D_INNER = SSD_HEADS * SSD_HEAD_DIM
SSD_GROUPS = 2
SSD_STATE = 128
SSD_CONV = 5
SSD_XBC = SSD_D_INNER + 2 * SSD_GROUPS * SSD_STATE
RET_HEADS = 4
RET_QK_DIM = 256
RET_V_DIM = 512
RET_QK_W = RET_HEADS * RET_QK_DIM
RET_V_W = RET_HEADS * RET_V_DIM
ROPE_BASE = 10000.0
EPS = 1e-6

LANES = 128
SUBLANES = 8
SEQ_BLOCK = 1024
ROW_TILE = 512
SSD_CHUNK = 128
RET_CHUNK = 256
FFN_CHUNK = 256
VMEM_LIMIT = 56 * 1024 * 1024

_NEG_INF = float("-inf")


def _params(**kw):
    return pltpu.CompilerParams(vmem_limit_bytes=VMEM_LIMIT, **kw)


def _silu(x):
    return x * (1.0 / (1.0 + jnp.exp(-x)))


def _softplus(x):
    return jnp.maximum(x, 0.0) + jnp.log(1.0 + jnp.exp(-jnp.abs(x)))


def _mm(a, b):
    return jnp.dot(a.astype(BF16), b.astype(BF16), preferred_element_type=F32)


def _mm_nt(a, b):
    return lax.dot_general(a.astype(BF16), b.astype(BF16), (((1,), (1,)), ((), ())),
                           preferred_element_type=F32)


def _const_spec(shape, **kw):
    nd = len(shape)
    return pl.BlockSpec(shape, lambda *_: (0,) * nd, **kw)


def _adaln_kernel(c_ref, w_ref, b_ref, o_ref):
    o_ref[...] = _mm(_silu(c_ref[...]), w_ref[...]) + b_ref[...]


def _adaln(cond, mod_w, mod_b):
    r, d = cond.shape
    n = mod_w.shape[1]
    tn = 1536
    return pl.pallas_call(
        _adaln_kernel,
        grid=(n // tn,),
        in_specs=[_const_spec((r, d)),
                  pl.BlockSpec((d, tn), lambda j: (0, j)),
                  pl.BlockSpec((1, tn), lambda j: (0, j))],
        out_specs=pl.BlockSpec((r, tn), lambda j: (0, j)),
        out_shape=jax.ShapeDtypeStruct((r, n), F32),
        compiler_params=_params(dimension_semantics=("arbitrary",)),
        name="adaln_mod",
    )(cond, mod_w, mod_b.reshape(1, n))


def _norm_mod(x, nw, mod_ref, shift_idx, scale_idx):
    ms = jnp.mean(x * x, axis=-1, keepdims=True)
    h = x * lax.rsqrt(ms + EPS) * nw
    return h * (1.0 + mod_ref[0, scale_idx:scale_idx + 1, :]) + mod_ref[0, shift_idx:shift_idx + 1, :]


def _inproj0_kernel(x_ref, nw_ref, mod_ref, w_ref, q_ref, k_ref, v_ref, z_ref, xbc_ref, dt_ref):
    hb = _norm_mod(x_ref[...], nw_ref[...], mod_ref, 0, 1).astype(BF16)
    col = 0
    for o_ref in (q_ref, k_ref, v_ref, z_ref, xbc_ref, dt_ref):
        n = o_ref.shape[1]
        o_ref[...] = jnp.dot(hb, w_ref[:, col:col + n], preferred_element_type=F32).astype(o_ref.dtype)
        col += n


def _inproj0(x, norm_w, mod, w, rows_per_mod, qkv_dtype):
    t, d = x.shape
    tm = ROW_TILE
    widths = (NA_WIDTH, NA_WIDTH, NA_WIDTH, SSD_D_INNER, SSD_XBC, LANES)
    dtypes = (qkv_dtype,) * 3 + (F32,) * 3
    assert sum(widths) == w.shape[1]
    per = rows_per_mod // tm
    return pl.pallas_call(
        _inproj0_kernel,
        grid=(t // tm,),
        in_specs=[pl.BlockSpec((tm, d), lambda i: (i, 0)),
                  _const_spec((1, d)),
                  pl.BlockSpec((1, 6, d), lambda i: (i // per, 0, 0)),
                  _const_spec(w.shape)],
        out_specs=[pl.BlockSpec((tm, n), lambda i: (i, 0)) for n in widths],
        out_shape=[jax.ShapeDtypeStruct((t, n), dt) for n, dt in zip(widths, dtypes)],
        compiler_params=_params(dimension_semantics=("arbitrary",)),
        name="l0_in_proj",
    )(x, norm_w.reshape(1, d), mod, w)


def _swap_lane_pairs(x):
    even = (lax.broadcasted_iota(jnp.int32, x.shape, 1) & 1) == 0
    return jnp.where(even, pltpu.roll(x, LANES - 1, 1), pltpu.roll(x, 1, 1))


def _inproj1_kernel(*refs, rope):
    if rope:
        x_ref, nw_ref, mod_ref, w_ref, cos_ref, sin_ref, q_ref, k_ref, v_ref, g_ref = refs
    else:
        x_ref, nw_ref, mod_ref, w_ref, q_ref, k_ref, v_ref, g_ref = refs
    hb = _norm_mod(x_ref[...], nw_ref[...], mod_ref, 0, 1).astype(BF16)
    for o_ref, col, scale in ((q_ref, 0, 1.0), (k_ref, RET_QK_W, RET_QK_DIM ** -0.5)):
        for h in range(RET_HEADS):
            c0 = h * RET_QK_DIM
            y = jnp.dot(hb, w_ref[:, col + c0:col + c0 + RET_QK_DIM], preferred_element_type=F32) * scale
            if rope:
                for j in range(RET_QK_DIM // LANES):
                    lanes = slice(j * LANES, (j + 1) * LANES)
                    yj = y[:, lanes]
                    rot = yj * cos_ref[:, lanes] + _swap_lane_pairs(yj) * sin_ref[:, lanes]
                    o_ref[:, c0 + j * LANES:c0 + (j + 1) * LANES] = rot.astype(o_ref.dtype)
            else:
                o_ref[:, c0:c0 + RET_QK_DIM] = y.astype(o_ref.dtype)
    col = 2 * RET_QK_W
    for o_ref in (v_ref, g_ref):
        o_ref[...] = jnp.dot(hb, w_ref[:, col:col + RET_V_W], preferred_element_type=F32).astype(o_ref.dtype)
        col += RET_V_W


def _inproj1(x, norm_w, mod, w, rows_per_mod, rope_tables=None):
    t, d = x.shape
    tm = ROW_TILE
    per = rows_per_mod // tm
    widths = (RET_QK_W, RET_QK_W, RET_V_W, RET_V_W)
    dtypes = (BF16, BF16, BF16, BF16)
    in_specs = [pl.BlockSpec((tm, d), lambda i: (i, 0)),
                _const_spec((1, d)),
                pl.BlockSpec((1, 6, d), lambda i: (i // per, 0, 0)),
                _const_spec(w.shape)]
    args = [x, norm_w.reshape(1, d), mod, w]
    if rope_tables is not None:
        cos, sin = rope_tables
        nblk = cos.shape[0] // tm
        in_specs += [pl.BlockSpec((tm, cos.shape[1]), lambda i: (i % nblk, 0))] * 2
        args += [cos, sin]
    return pl.pallas_call(
        functools.partial(_inproj1_kernel, rope=rope_tables is not None),
        grid=(t // tm,),
        in_specs=in_specs,
        out_specs=[pl.BlockSpec((tm, n), lambda i: (i, 0)) for n in widths],
        out_shape=[jax.ShapeDtypeStruct((t, n), dt) for n, dt in zip(widths, dtypes)],
        compiler_params=_params(dimension_semantics=("arbitrary",)),
        name="l1_in_proj",
    )(*args)


def _lane_lo(shape):
    return lax.broadcasted_iota(jnp.int32, shape, 1) < NA_HEAD_DIM


def _head_lanes(x, half):
    lo = _lane_lo(x.shape)
    return jnp.where(lo if half == 0 else jnp.logical_not(lo), x, jnp.zeros_like(x))


def _softmax_pv(q2, keys, vals, biases):
    acc = None
    for half in (0, 1):
        qm = _head_lanes(q2, half)
        scores = []
        for kk, bb in zip(keys, biases):
            s = _mm_nt(qm, kk)
            if bb is not None:
                s = s + bb(half)
            scores.append(s)
        mx = functools.reduce(jnp.maximum, [jnp.max(s, axis=-1, keepdims=True) for s in scores])
        es = [jnp.exp(s - mx) for s in scores]
        den = functools.reduce(jnp.add, [jnp.sum(e, axis=-1, keepdims=True) for e in es])
        pv = functools.reduce(jnp.add, [_mm(e, _head_lanes(vv, half)) for e, vv in zip(es, vals)])
        out = pv * (1.0 / den)
        acc = out if acc is None else acc + out
    return acc


def _na_ctx_kernel(q_ref, k_ref, v_ref, o_ref, *, seq):
    scale = NA_HEAD_DIM ** -0.5
    nseq = q_ref.shape[0] // seq
    for s in range(nseq):
        r = slice(s * seq, (s + 1) * seq)
        for hp in range(NA_WIDTH // LANES):
            c = slice(hp * LANES, (hp + 1) * LANES)
            out = _softmax_pv(q_ref[r, c] * scale, [k_ref[r, c]], [v_ref[r, c]], [None])
            o_ref[r, c] = out.astype(o_ref.dtype)


def _na_ctx(q, k, v, seq):
    t, w = q.shape
    spec = pl.BlockSpec((SEQ_BLOCK, w), lambda i: (i, 0))
    return pl.pallas_call(
        functools.partial(_na_ctx_kernel, seq=seq),
        grid=(t // SEQ_BLOCK,),
        in_specs=[spec, spec, spec],
        out_specs=spec,
        out_shape=jax.ShapeDtypeStruct((t, w), BF16),
        compiler_params=_params(dimension_semantics=("arbitrary",)),
        name="na_context",
    )(q, k, v)


NA_QBLK = 256
NA_KWIN = 768


def _na_lat_kernel(q_ref, k_ref, v_ref, kc_ref, vc_ref, mb_ref, o_ref):
    scale = NA_HEAD_DIM ** -0.5
    seq = q_ref.shape[0]
    nblk = seq // NA_QBLK
    for i in range(nblk):
        ws = 0 if i < nblk // 2 else seq - NA_KWIN
        rows = slice(i * NA_QBLK, (i + 1) * NA_QBLK)
        kw = k_ref[ws:ws + NA_KWIN, :]
        vw = v_ref[ws:ws + NA_KWIN, :]
        out = _softmax_pv(q_ref[rows, :] * scale, [kw, kc_ref[...]], [vw, vc_ref[...]],
                          [lambda half, i=i: mb_ref[half, i], None])
        o_ref[rows, :] = out.astype(o_ref.dtype)


def _na_mask_bias(rel_bias, seq):
    rows = seq // GRID_W
    nblk = seq // NA_QBLK
    rows_per_blk = NA_QBLK // GRID_W
    win_rows = NA_KWIN // GRID_W
    h = rel_bias.shape[0]
    col = np.arange(GRID_W)
    c0 = np.clip(col - NA_WIN_COLS // 2, 0, GRID_W - NA_WIN_COLS)
    col_ok = (col[None, :] >= c0[:, None]) & (col[None, :] < c0[:, None] + NA_WIN_COLS)
    dc = np.clip(col[None, :] - col[:, None] + NA_WIN_COLS - 1, 0, 2 * NA_WIN_COLS - 2)
    tb = jnp.take(rel_bias.astype(F32), jnp.asarray(dc.reshape(-1)), axis=2)
    tb = jnp.where(col_ok, tb.reshape(h, 2 * NA_WIN_ROWS - 1, GRID_W, GRID_W), _NEG_INF)
    tb = lax.optimization_barrier(tb)
    neg = jnp.full((h, GRID_W, GRID_W), _NEG_INF, F32)
    blocks = []
    for i in range(nblk):
        ws = 0 if i < nblk // 2 else rows - win_rows
        qrows = []
        for rq in range(rows_per_blk):
            r = i * rows_per_blk + rq
            r0 = min(max(r - NA_WIN_ROWS // 2, 0), rows - NA_WIN_ROWS)
            krow = [tb[:, ws + kb - r + NA_WIN_ROWS - 1] if r0 <= ws + kb < r0 + NA_WIN_ROWS else neg
                    for kb in range(win_rows)]
            qrows.append(jnp.concatenate(krow, axis=-1))
        blocks.append(jnp.concatenate(qrows, axis=-2))
    return jnp.stack(blocks, axis=1)


def _na_lat(q, k, v, k_ctx, v_ctx, mask_bias, seq):
    t, w = q.shape
    b = t // seq
    nblk = seq // NA_QBLK
    npair = w // LANES
    past = k_ctx.shape[0] // b
    kspec = pl.BlockSpec((seq, LANES), lambda hp, bb: (bb, hp))
    cspec = pl.BlockSpec((past, LANES), lambda hp, bb: (bb, hp))
    mspec = pl.BlockSpec((2, nblk, NA_QBLK, NA_KWIN), lambda hp, bb: (hp, 0, 0, 0))
    return pl.pallas_call(
        _na_lat_kernel,
        grid=(npair, b),
        in_specs=[kspec, kspec, kspec, cspec, cspec, mspec],
        out_specs=kspec,
        out_shape=jax.ShapeDtypeStruct((t, w), BF16),
        compiler_params=_params(dimension_semantics=("arbitrary", "arbitrary")),
        name="na_latent",
    )(q, k, v, k_ctx, v_ctx, mask_bias)


def _pack3(v, lane):
    vm = jnp.where(lane < 32, v, 0.0)
    hi = vm.astype(BF16).astype(F32)
    r1 = vm - hi
    mid = r1.astype(BF16).astype(F32)
    lo = r1 - mid
    return (hi + pltpu.roll(mid, 32, 1) + pltpu.roll(lo, 64, 1)).astype(BF16)


def _unpack3(res):
    return res + pltpu.roll(res, 96, 1) + pltpu.roll(res, 64, 1)


def _ssd_selectors():
    r = np.arange(LANES)
    out = []
    for width in (SSD_HEAD_DIM, SSD_CHUNK):
        l = np.arange(SSD_HEADS * width)
        for d in (0, 1):
            sel = (r[:, None] < 96) & ((r[:, None] % 32) == d * SSD_HEADS + l[None, :] // width)
            out.append(jnp.asarray(sel, BF16))
    return out


def _ssd_kernel(*refs, seq, has_state):
    if has_state:
        (z_ref, xbc_ref, dtr_ref, cw_ref, cb_ref, dtb_ref, alog_ref, dsk_ref, nw_ref,
         s64f_ref, s64b_ref, s128f_ref, s128b_ref, s0_ref,
         y_ref, xpad, xc, yacc, csp, ep, wp, cst, dtt, st) = refs
        sfin_ref = None
    else:
        (z_ref, xbc_ref, dtr_ref, cw_ref, cb_ref, dtb_ref, alog_ref, dsk_ref, nw_ref,
         s64f_ref, s64b_ref, s128f_ref, s128b_ref,
         y_ref, sfin_ref, xpad, xc, yacc, csp, ep, wp, cst, dtt, st) = refs
        s0_ref = None
    ch = SSD_CHUNK
    nseq = z_ref.shape[0] // seq
    nch = seq // ch
    assert nch % 2 == 0
    pad = SUBLANES
    lane = lax.broadcasted_iota(jnp.int32, (ch, LANES), 1)
    ri = lax.broadcasted_iota(jnp.int32, (ch, ch), 0)
    ci = lax.broadcasted_iota(jnp.int32, (ch, ch), 1)
    keeps = (ci <= ri, ci >= ri)
    tril = jnp.where(keeps[0], 1.0, 0.0).astype(BF16)
    triu = jnp.where(keeps[1], 1.0, 0.0).astype(BF16)
    lane_lo = lane < SSD_HEAD_DIM
    fwd_lane = lane < SSD_HEADS
    neg_a = -jnp.exp(alog_ref[...])
    gw = SSD_D_INNER // SSD_GROUPS
    sel64s = (s64f_ref, s64b_ref)
    sel128s = (s128f_ref, s128b_ref)
    edges = (ch - 1, 0)

    def seq_body(s, _):
        base = pl.multiple_of(s * seq, seq)
        xpad[0:pad, :] = jnp.zeros((pad, SSD_XBC), F32)
        xpad[pad + seq:2 * pad + seq, :] = jnp.zeros((pad, SSD_XBC), F32)
        xpad[pad:pad + seq, :] = xbc_ref[pl.ds(base, seq), :]

        def prep_body(c, _):
            r0 = pl.multiple_of(c * ch, ch)
            win = xpad[pl.ds(r0, ch + 2 * pad), :]
            conv = jnp.broadcast_to(cb_ref[...], (ch, SSD_XBC))
            for k in range(SSD_CONV):
                ofs = pad - SSD_CONV // 2 + k
                conv = conv + cw_ref[k:k + 1, :] * win[ofs:ofs + ch, :]
            act = _silu(conv)
            xc[pl.ds(r0, ch), :] = act
            yacc[pl.ds(r0, ch), :] = dsk_ref[...] * act[:, 0:SSD_D_INNER]
            return 0

        lax.fori_loop(0, nch, prep_body, 0)

        def decay_body(c, _):
            r0 = pl.multiple_of(c * ch, ch)
            dt = _softplus(dtr_ref[pl.ds(base + r0, ch), :] + dtb_ref[...])
            la = _pack3(dt * neg_a, lane)
            cs = jnp.where(fwd_lane,
                           _unpack3(jnp.dot(tril, la, preferred_element_type=F32)),
                           _unpack3(jnp.dot(triu, la, preferred_element_type=F32)))
            tot = jnp.where(fwd_lane, cs[ch - 1:ch, :], cs[0:1, :])
            rows = pl.ds(r0, ch)
            csp[rows, :] = _pack3(cs, lane)
            ep[rows, :] = _pack3(jnp.exp(cs), lane)
            wp[rows, :] = _pack3(dt * jnp.exp(tot - cs), lane)
            cst[c] = cs.T
            dtt[c] = dt.T
            return 0

        lax.fori_loop(0, nch, decay_body, 0, unroll=2)

        for d in (0, 1):
            if has_state:
                st[d] = s0_ref[0, d]
            else:
                st[d] = jnp.zeros((SSD_STATE, SSD_D_INNER), F32)

        def chunk(d, c):
            r0 = pl.multiple_of(c * ch, ch)
            rows = pl.ds(r0, ch)
            col = jnp.dot(csp[rows, :], sel128s[d][...], preferred_element_type=F32)
            e64 = jnp.dot(ep[rows, :], sel64s[d][...], preferred_element_type=F32)
            w64 = jnp.dot(wp[rows, :], sel64s[d][...], preferred_element_type=F32)
            cs_t = cst[c]
            dt_t = dtt[c]
            x_c = xc[rows, 0:SSD_D_INNER]
            state = st[d]
            state_b = state.astype(BF16)
            xw = (x_c * w64).astype(BF16)
            y_intra, y_state, upd = [], [], []
            for g in range(SSD_GROUPS):
                b_g = xc[rows, SSD_D_INNER + g * SSD_STATE:SSD_D_INNER + (g + 1) * SSD_STATE]
                cofs = SSD_D_INNER + SSD_GROUPS * SSD_STATE
                c_g = xc[rows, cofs + g * SSD_STATE:cofs + (g + 1) * SSD_STATE].astype(BF16)
                gram = _mm_nt(c_g, b_g)
                for pp in range(gw // LANES):
                    p = g * (gw // LANES) + pp
                    ws = []
                    for half in (0, 1):
                        h = 2 * p + half
                        hd = d * SSD_HEADS + h
                        diff = col[:, h * ch:(h + 1) * ch] - cs_t[hd:hd + 1, :]
                        wm = jnp.exp(jnp.where(keeps[d], diff, _NEG_INF)) * gram * dt_t[hd:hd + 1, :]
                        ws.append(wm.astype(BF16))
                    xp = x_c[:, p * LANES:(p + 1) * LANES]
                    xcat = jnp.concatenate([jnp.where(lane_lo, xp, 0.0), jnp.where(lane_lo, 0.0, xp)],
                                           axis=0).astype(BF16)
                    y_intra.append(jnp.dot(jnp.concatenate(ws, axis=1), xcat, preferred_element_type=F32))
                y_state.append(jnp.dot(c_g, state_b[:, g * gw:(g + 1) * gw], preferred_element_type=F32))
                upd.append(jnp.dot(b_g.T.astype(BF16), xw[:, g * gw:(g + 1) * gw],
                                   preferred_element_type=F32))
            yacc[rows, :] = (yacc[rows, :] + jnp.concatenate(y_intra, axis=1)
                             + jnp.concatenate(y_state, axis=1) * e64)
            st[d] = e64[edges[d]:edges[d] + 1, :] * state + jnp.concatenate(upd, axis=1)

        def pair_body(j, _):
            for step in (0, 1):
                chunk(0, 2 * j + step)
                chunk(1, nch - 1 - 2 * j - step)
            return 0

        lax.fori_loop(0, nch // 2, pair_body, 0)
        if sfin_ref is not None:
            for d in (0, 1):
                sfin_ref[s, d] = st[d]

        def out_body(c, _):
            r0 = pl.multiple_of(c * ch, ch)
            y = yacc[pl.ds(r0, ch), :] * _silu(z_ref[pl.ds(base + r0, ch), :])
            outs = []
            for g in range(SSD_GROUPS):
                yg = y[:, g * gw:(g + 1) * gw]
                outs.append(yg * lax.rsqrt(jnp.mean(yg * yg, axis=-1, keepdims=True) + EPS))
            y_ref[pl.ds(base + r0, ch), :] = (jnp.concatenate(outs, axis=1) * nw_ref[...]).astype(y_ref.dtype)
            return 0

        lax.fori_loop(0, nch, out_body, 0)
        return 0

    lax.fori_loop(0, nseq, seq_body, 0)


def _ssd(z, xbc, dtr, conv_w, conv_b, a_log, dt_bias, d_skip, norm_w, seq, s0=None):
    t = z.shape[0]
    nseq = SEQ_BLOCK // seq
    nblk = t // SEQ_BLOCK
    nch = seq // SSD_CHUNK
    pad32 = lambda a: jnp.pad(a.astype(F32).reshape(1, -1), ((0, 0), (0, LANES - 2 * SSD_HEADS)))
    consts = [jnp.pad(conv_w.astype(F32), ((0, SUBLANES - SSD_CONV), (0, 0))),
              conv_b.astype(F32).reshape(1, -1), pad32(dt_bias), pad32(a_log),
              jnp.repeat(d_skip.astype(F32), SSD_HEAD_DIM).reshape(1, -1),
              norm_w.astype(F32).reshape(1, -1)] + _ssd_selectors()
    row = lambda n: pl.BlockSpec((SEQ_BLOCK, n), lambda i: (i, 0))
    in_specs = [row(SSD_D_INNER), row(SSD_XBC), row(LANES)] + [_const_spec(c.shape) for c in consts]
    args = [z, xbc, dtr] + consts
    out_specs = [row(SSD_D_INNER)]
    out_shape = [jax.ShapeDtypeStruct((t, SSD_D_INNER), BF16)]
    state_shape = (2, SSD_STATE, SSD_D_INNER)
    if s0 is not None:
        in_specs.append(pl.BlockSpec((1,) + state_shape, lambda i: (i, 0, 0, 0)))
        args.append(s0)
    else:
        out_specs.append(pl.BlockSpec((nseq,) + state_shape, lambda i: (i, 0, 0, 0)))
        out_shape.append(jax.ShapeDtypeStruct((t // seq,) + state_shape, F32))
    scratch = [pltpu.VMEM((seq + 2 * SUBLANES, SSD_XBC), F32),
               pltpu.VMEM((seq, SSD_XBC), F32),
               pltpu.VMEM((seq, SSD_D_INNER), F32),
               pltpu.VMEM((seq, LANES), BF16), pltpu.VMEM((seq, LANES), BF16), pltpu.VMEM((seq, LANES), BF16),
               pltpu.VMEM((nch, LANES, SSD_CHUNK), F32), pltpu.VMEM((nch, LANES, SSD_CHUNK), F32),
               pltpu.VMEM(state_shape, F32)]
    return pl.pallas_call(
        functools.partial(_ssd_kernel, seq=seq, has_state=s0 is not None),
        grid=(nblk,),
        in_specs=in_specs,
        out_specs=out_specs,
        out_shape=out_shape,
        scratch_shapes=scratch,
        compiler_params=_params(dimension_semantics=("arbitrary",)),
        name="ssd_mixer",
    )(*args)


def _log_sigmoid(x):
    return -_softplus(-x)


def _ret_kernel(*refs, seq, has_state):
    if has_state:
        q_ref, k_ref, v_ref, g_ref, dec_ref, nw_ref, s0_ref, y_ref, yacc, st = refs
        sfin_ref = None
    else:
        q_ref, k_ref, v_ref, g_ref, dec_ref, nw_ref, y_ref, sfin_ref, yacc, st = refs
        s0_ref = None
    ch = RET_CHUNK
    nseq = q_ref.shape[0] // seq
    nch = seq // ch
    gf = _log_sigmoid(dec_ref[0, 0:1, :])
    gb = _log_sigmoid(dec_ref[0, 1:2, :])
    ri = lax.broadcasted_iota(jnp.int32, (ch, ch), 0)
    ci = lax.broadcasted_iota(jnp.int32, (ch, ch), 1)
    dist = (ri - ci).astype(F32)
    gfk, gbk = gf[:, :ch], gb[:, :ch]
    decay = (jnp.where(ci <= ri, jnp.exp(jnp.where(ci <= ri, dist, 0.0) * gfk), 0.0)
             + jnp.where(ci >= ri, jnp.exp(jnp.where(ci >= ri, -dist, 0.0) * gbk), 0.0))
    pos = lax.broadcasted_iota(jnp.int32, (ch, RET_QK_DIM), 0).astype(F32)
    gfq, gbq = gf[:, :RET_QK_DIM], gb[:, :RET_QK_DIM]
    e_f = jnp.exp((pos + 1.0) * gfq)
    e_b = jnp.exp((ch - pos) * gbq)
    tail_f = jnp.exp((ch - 1.0 - pos) * gfq)
    tail_b = jnp.exp(pos * gbq)
    dec_f = jnp.exp(ch * gf)
    dec_b = jnp.exp(ch * gb)

    for s in range(nseq):
        for d in (0, 1):
            if has_state:
                st[d] = s0_ref[0, d, 0]
            else:
                st[d] = jnp.zeros((RET_QK_DIM, RET_V_DIM), F32)
        for c in range(nch):
            rows = slice(s * seq + c * ch, s * seq + (c + 1) * ch)
            q, k, v = q_ref[rows, :], k_ref[rows, :].astype(F32), v_ref[rows, :]
            y = _mm(_mm_nt(q, k) * decay, v)
            if has_state or c > 0:
                y = y + _mm(q.astype(F32) * e_f, st[0])
            yacc[rows, :] = y
            st[0] = dec_f * st[0] + _mm((k * tail_f).T, v)
        for c in reversed(range(nch)):
            rows = slice(s * seq + c * ch, s * seq + (c + 1) * ch)
            q, k, v = q_ref[rows, :], k_ref[rows, :].astype(F32), v_ref[rows, :]
            if has_state or c < nch - 1:
                yacc[rows, :] = yacc[rows, :] + _mm(q.astype(F32) * e_b, st[1])
            st[1] = dec_b * st[1] + _mm((k * tail_b).T, v)
        if sfin_ref is not None:
            for d in (0, 1):
                sfin_ref[s, d, 0] = st[d]
        for c in range(nch):
            rows = slice(s * seq + c * ch, s * seq + (c + 1) * ch)
            y = yacc[rows, :]
            y = y * lax.rsqrt(jnp.mean(y * y, axis=-1, keepdims=True) + EPS)
            y_ref[rows, :] = (y * nw_ref[...] * _silu(g_ref[rows, :].astype(F32))).astype(y_ref.dtype)


def _retention(q, k, v, g, ret_decay, norm_w, seq, s0=None):
    t = q.shape[0]
    nseq = SEQ_BLOCK // seq
    nblk = t // SEQ_BLOCK
    dec = jnp.broadcast_to(jnp.pad(ret_decay.astype(F32).T, ((0, 0), (0, SUBLANES - 2)))[:, :, None],
                           (RET_HEADS, SUBLANES, RET_V_DIM))
    qspec = pl.BlockSpec((SEQ_BLOCK, RET_QK_DIM), lambda i, h: (i, h))
    vspec = pl.BlockSpec((SEQ_BLOCK, RET_V_DIM), lambda i, h: (i, h))
    in_specs = [qspec, qspec, vspec, vspec,
                pl.BlockSpec((1, SUBLANES, RET_V_DIM), lambda i, h: (h, 0, 0)),
                pl.BlockSpec((1, RET_V_DIM), lambda i, h: (0, h))]
    args = [q, k, v, g, dec, norm_w.astype(F32).reshape(1, -1)]
    out_specs = [vspec]
    out_shape = [jax.ShapeDtypeStruct((t, RET_V_W), BF16)]
    if s0 is not None:
        in_specs.append(pl.BlockSpec((1, 2, 1, RET_QK_DIM, RET_V_DIM), lambda i, h: (i, 0, h, 0, 0)))
        args.append(s0)
    else:
        out_specs.append(pl.BlockSpec((nseq, 2, 1, RET_QK_DIM, RET_V_DIM), lambda i, h: (i, 0, h, 0, 0)))
        out_shape.append(jax.ShapeDtypeStruct((t // seq, 2, RET_HEADS, RET_QK_DIM, RET_V_DIM), F32))
    return pl.pallas_call(
        functools.partial(_ret_kernel, seq=seq, has_state=s0 is not None),
        grid=(nblk, RET_HEADS),
        in_specs=in_specs,
        out_specs=out_specs,
        out_shape=out_shape,
        scratch_shapes=[pltpu.VMEM((SEQ_BLOCK, RET_V_DIM), F32),
                        pltpu.VMEM((2, RET_QK_DIM, RET_V_DIM), F32)],
        compiler_params=_params(dimension_semantics=("arbitrary", "arbitrary")),
        name="retention_mixer",
    )(*args)


def _out_ffn_kernel(*refs, n_mix, final_norm):
    x_ref = refs[0]
    mix_refs = refs[1:1 + n_mix]
    wo_ref, mod_ref, n2_ref, w1_ref, w3_ref, w2_ref = refs[1 + n_mix:7 + n_mix]
    if final_norm:
        fn_ref, o_ref, act = refs[7 + n_mix:]
    else:
        o_ref, act = refs[7 + n_mix:]
    mixed, row = None, 0
    for m_ref in mix_refs:
        k = m_ref.shape[1]
        term = jnp.dot(m_ref[...], wo_ref[row:row + k, :], preferred_element_type=F32)
        mixed = term if mixed is None else mixed + term
        row += k
    x1 = x_ref[...] + mod_ref[0, 2:3, :] * mixed
    hb = _norm_mod(x1, n2_ref[...], mod_ref, 3, 4).astype(BF16)
    for c in range(w1_ref.shape[1] // FFN_CHUNK):
        cols = slice(c * FFN_CHUNK, (c + 1) * FFN_CHUNK)
        h1 = jnp.dot(hb, w1_ref[:, cols], preferred_element_type=F32)
        h3 = jnp.dot(hb, w3_ref[:, cols], preferred_element_type=F32)
        act[:, cols] = (_silu(h1) * h3).astype(BF16)
    x2 = x1 + mod_ref[0, 5:6, :] * jnp.dot(act[...], w2_ref[...], preferred_element_type=F32)
    if final_norm:
        ms = jnp.mean(x2 * x2, axis=-1, keepdims=True)
        x2 = x2 * lax.rsqrt(ms + EPS) * fn_ref[...]
    o_ref[...] = x2


def _out_ffn(x, mixes, wo, mod, norm2_w, w1, w3, w2, rows_per_mod, final_norm_w=None):
    t, d = x.shape
    tm = ROW_TILE
    per = rows_per_mod // tm
    single = dict(pipeline_mode=pl.Buffered(1))
    in_specs = [pl.BlockSpec((tm, d), lambda i: (i, 0))]
    in_specs += [pl.BlockSpec((tm, m.shape[1]), lambda i: (i, 0)) for m in mixes]
    in_specs += [_const_spec(wo.shape, **single),
                 pl.BlockSpec((1, 6, d), lambda i: (i // per, 0, 0)), _const_spec((1, d))]
    in_specs += [_const_spec(w.shape, **single) for w in (w1, w3, w2)]
    args = [x, *mixes, wo, mod, norm2_w.reshape(1, d), w1, w3, w2]
    if final_norm_w is not None:
        in_specs.append(_const_spec((1, d)))
        args.append(final_norm_w.reshape(1, d))
    return pl.pallas_call(
        functools.partial(_out_ffn_kernel, n_mix=len(mixes), final_norm=final_norm_w is not None),
        grid=(t // tm,),
        in_specs=in_specs,
        out_specs=pl.BlockSpec((tm, d), lambda i: (i, 0)),
        out_shape=jax.ShapeDtypeStruct((t, d), F32),
        scratch_shapes=[pltpu.VMEM((tm, w1.shape[1]), BF16)],
        compiler_params=_params(dimension_semantics=("arbitrary",)),
        name="out_proj_ffn",
    )(*args)


def _rope_tables(seq):
    half = RET_QK_DIM // 2
    t = jnp.arange(seq)
    row = (t // GRID_W).astype(F32)
    col = (t % GRID_W).astype(F32)
    freqs = ROPE_BASE ** (-jnp.arange(0, half, 2, dtype=F32) / half)
    ang = jnp.concatenate([row[:, None] * freqs, col[:, None] * freqs], axis=-1)
    cos = jnp.repeat(jnp.cos(ang), 2, axis=1)
    sin = jnp.stack([-jnp.sin(ang), jnp.sin(ang)], axis=-1).reshape(seq, RET_QK_DIM)
    return cos, sin


def kernel(x_prompt, x_sample, cache_l0_na_k, cache_l0_na_v, state_l0_ssd, state_l1_ret, c, c_ctx,
           l0_norm1_w, l0_norm2_w, l0_mod_w, l0_mod_b, l0_w_in, l0_w_out, l0_na_bias, l0_conv_w, l0_conv_b,
           l0_ssd_a_log, l0_ssd_dt_bias, l0_ssd_d, l0_ssd_norm_w, l0_ffn_w1, l0_ffn_w3, l0_ffn_w2,
           l1_norm1_w, l1_norm2_w, l1_mod_w, l1_mod_b, l1_w_in, l1_w_out, l1_ret_decay, l1_ret_norm_w,
           l1_ffn_w1, l1_ffn_w3, l1_ffn_w2, final_norm_w):
    bc, lc, d = x_prompt.shape
    bl, ll, _ = x_sample.shape
    assert d == D_MODEL and ll == SEQ_BLOCK and SEQ_BLOCK % lc == 0 and bc % (SEQ_BLOCK // lc) == 0
    tc, tl = bc * lc, bl * ll
    xc = x_prompt.reshape(tc, d)
    xl = x_sample.reshape(tl, d)

    nrow = SUBLANES * ((1 + bl + SUBLANES - 1) // SUBLANES)
    cond = jnp.concatenate([c_ctx[None], c, jnp.zeros((nrow - 1 - bl, d), F32)], axis=0)
    mods = []
    for mod_w, mod_b in ((l0_mod_w, l0_mod_b), (l1_mod_w, l1_mod_b)):
        m = _adaln(cond, mod_w, mod_b).reshape(nrow, 6, d)
        mods.append((m[0:1], m[1:1 + bl]))

    w_in0 = jnp.pad(l0_w_in.astype(BF16), ((0, 0), (0, LANES - 2 * SSD_HEADS)))
    wo0 = l0_w_out.astype(BF16)
    ffn0 = (l0_ffn_w1.astype(BF16), l0_ffn_w3.astype(BF16), l0_ffn_w2.astype(BF16))
    ssd_p = (l0_conv_w, l0_conv_b, l0_ssd_a_log, l0_ssd_dt_bias, l0_ssd_d, l0_ssd_norm_w)

    qc, kc, vc, zc, xbcc, dtc = _inproj0(xc, l0_norm1_w, mods[0][0], w_in0, tc, F32)
    att_c = _na_ctx(qc, kc, vc, lc)
    ssd_c, sfin_c = _ssd(zc, xbcc, dtc, *ssd_p, seq=lc)
    xc = _out_ffn(xc, (att_c, ssd_c), wo0, mods[0][0], l0_norm2_w, *ffn0, tc)

    ql, kl, vl, zl, xbcl, dtl = _inproj0(xl, l0_norm1_w, mods[0][1], w_in0, ll, BF16)
    mask_bias = _na_mask_bias(l0_na_bias, ll)
    past = cache_l0_na_k.shape[1]
    att_l = _na_lat(ql, kl, vl, cache_l0_na_k.reshape(bl * past, NA_WIDTH),
                    cache_l0_na_v.reshape(bl * past, NA_WIDTH), mask_bias, ll)
    s0_ssd = state_l0_ssd.transpose(0, 1, 3, 2, 4).reshape(bl, 2, SSD_STATE, SSD_D_INNER)
    (ssd_l,) = _ssd(zl, xbcl, dtl, *ssd_p, seq=ll, s0=s0_ssd)
    xl = _out_ffn(xl, (att_l, ssd_l), wo0, mods[0][1], l0_norm2_w, *ffn0, ll)

    w_in1 = l1_w_in.astype(BF16)
    wo1 = l1_w_out.astype(BF16)
    ffn1 = (l1_ffn_w1.astype(BF16), l1_ffn_w3.astype(BF16), l1_ffn_w2.astype(BF16))

    q1, k1, v1, g1 = _inproj1(xc, l1_norm1_w, mods[1][0], w_in1, tc)
    ret_c, ret_state = _retention(q1, k1, v1, g1, l1_ret_decay, l1_ret_norm_w, lc)
    y_prompt = _out_ffn(xc, (ret_c,), wo1, mods[1][0], l1_norm2_w, *ffn1, tc, final_norm_w=final_norm_w)

    q1, k1, v1, g1 = _inproj1(xl, l1_norm1_w, mods[1][1], w_in1, ll, rope_tables=_rope_tables(ll))
    (ret_l,) = _retention(q1, k1, v1, g1, l1_ret_decay, l1_ret_norm_w, ll, s0=state_l1_ret)
    y_sample = _out_ffn(xl, (ret_l,), wo1, mods[1][1], l1_norm2_w, *ffn1, ll, final_norm_w=final_norm_w)

    new_ssd = sfin_c.reshape(bc, 2, SSD_STATE, SSD_HEADS, SSD_HEAD_DIM).transpose(0, 1, 3, 2, 4)
    return (y_prompt.reshape(bc, lc, d), y_sample.reshape(bl, ll, d),
            kc.reshape(bc, lc, NA_HEADS, NA_HEAD_DIM), vc.reshape(bc, lc, NA_HEADS, NA_HEAD_DIM),
            new_ssd, ret_state)
```

```python
import functools

import numpy as np
import jax
import jax.numpy as jnp
from jax import lax
from jax.experimental import pallas as pl
from jax.experimental.pallas import tpu as pltpu

F32 = jnp.float32
BF16 = jnp.bfloat16

D_MODEL = 1024
GRID_W = 64
NA_HEADS = 8
NA_HEAD_DIM = 64
NA_WIDTH = NA_HEADS * NA_HEAD_DIM
NA_WIN_ROWS = 8
NA_WIN_COLS = 16
SSD_HEADS = 16
SSD_HEAD_DIM = 64
SSD_D_INNER = SSD_HEADS * SSD_HEAD_DIM
SSD_GROUPS = 2
SSD_STATE = 128
SSD_CONV = 5
SSD_XBC = SSD_D_INNER + 2 * SSD_GROUPS * SSD_STATE
RET_HEADS = 4
RET_QK_DIM = 256
RET_V_DIM = 512
RET_QK_W = RET_HEADS * RET_QK_DIM
RET_V_W = RET_HEADS * RET_V_DIM
ROPE_BASE = 10000.0
EPS = 1e-6

LANES = 128
SUBLANES = 8
SEQ_BLOCK = 1024
ROW_TILE = 512
SSD_CHUNK = 128
RET_CHUNK = 256
FFN_CHUNK = 256
VMEM_LIMIT = 56 * 1024 * 1024

_NEG_INF = float("-inf")


def _params(**kw):
    return pltpu.CompilerParams(vmem_limit_bytes=VMEM_LIMIT, **kw)


def _silu(x):
    return x * (1.0 / (1.0 + jnp.exp(-x)))


def _softplus(x):
    return jnp.maximum(x, 0.0) + jnp.log(1.0 + jnp.exp(-jnp.abs(x)))


def _mm(a, b):
    return jnp.dot(a.astype(BF16), b.astype(BF16), preferred_element_type=F32)


def _mm_nt(a, b):
    return lax.dot_general(a.astype(BF16), b.astype(BF16), (((1,), (1,)), ((), ())),
                           preferred_element_type=F32)


def _const_spec(shape, **kw):
    nd = len(shape)
    return pl.BlockSpec(shape, lambda *_: (0,) * nd, **kw)


def _adaln_kernel(c_ref, w_ref, b_ref, o_ref):
    o_ref[...] = _mm(_silu(c_ref[...]), w_ref[...]) + b_ref[...]


def _adaln(cond, mod_w, mod_b):
    r, d = cond.shape
    n = mod_w.shape[1]
    tn = 1536
    return pl.pallas_call(
        _adaln_kernel,
        grid=(n // tn,),
        in_specs=[_const_spec((r, d)),
                  pl.BlockSpec((d, tn), lambda j: (0, j)),
                  pl.BlockSpec((1, tn), lambda j: (0, j))],
        out_specs=pl.BlockSpec((r, tn), lambda j: (0, j)),
        out_shape=jax.ShapeDtypeStruct((r, n), F32),
        compiler_params=_params(dimension_semantics=("arbitrary",)),
        name="adaln_mod",
    )(cond, mod_w, mod_b.reshape(1, n))


def _norm_mod(x, nw, mod_ref, shift_idx, scale_idx):
    ms = jnp.mean(x * x, axis=-1, keepdims=True)
    h = x * lax.rsqrt(ms + EPS) * nw
    return h * (1.0 + mod_ref[0, scale_idx:scale_idx + 1, :]) + mod_ref[0, shift_idx:shift_idx + 1, :]


def _inproj0_kernel(*refs, seq, halo):
    if halo:
        (x_ref, xp_ref, xn_ref, nw_ref, mod_ref, w_ref, cw_ref, cb_ref,
         q_ref, k_ref, v_ref, z_ref, xs_ref, bc_ref, dt_ref, xpad) = refs
    else:
        (x_ref, nw_ref, mod_ref, w_ref, cw_ref, cb_ref,
         q_ref, k_ref, v_ref, z_ref, xs_ref, bc_ref, dt_ref, xpad) = refs
    tm = x_ref.shape[0]
    pad = SUBLANES
    sub = xpad.shape[1] - 2 * pad
    h = _norm_mod(x_ref[...], nw_ref[...], mod_ref, 0, 1)
    hb = h.astype(BF16)
    xbc_col = 3 * NA_WIDTH + SSD_D_INNER
    w_xbc = w_ref[:, xbc_col:xbc_col + SSD_XBC]
    if halo:
        tiles_per_seq = seq // tm
        p = pl.program_id(0) % tiles_per_seq
        h_prev = jnp.where(p > 0, _norm_mod(xp_ref[...], nw_ref[...], mod_ref, 0, 1), 0.0)
        h_next = jnp.where(p < tiles_per_seq - 1, _norm_mod(xn_ref[...], nw_ref[...], mod_ref, 0, 1), 0.0)
        ext = jnp.concatenate([h_prev, h, h_next], axis=0).astype(BF16)
        xpad[0] = jnp.dot(ext, w_xbc, preferred_element_type=F32)
    else:
        xbc = jnp.dot(hb, w_xbc, preferred_element_type=F32)
        for s in range(tm // sub):
            xpad[s, 0:pad, :] = jnp.zeros((pad, SSD_XBC), F32)
            xpad[s, pad:pad + sub, :] = xbc[s * sub:(s + 1) * sub]
            xpad[s, pad + sub:2 * pad + sub, :] = jnp.zeros((pad, SSD_XBC), F32)

    def conv_chunk(r0):
        s, rs = divmod(r0, sub)
        win = xpad[s, rs:rs + SSD_CHUNK + 2 * pad, :]
        conv = jnp.broadcast_to(cb_ref[...], (SSD_CHUNK, SSD_XBC))
        for k in range(SSD_CONV):
            ofs = pad - SSD_CONV // 2 + k
            conv = conv + cw_ref[k:k + 1, :] * win[ofs:ofs + SSD_CHUNK, :]
        act = _silu(conv)
        xs_ref[r0:r0 + SSD_CHUNK, :] = act[:, :SSD_D_INNER]
        bc_ref[r0:r0 + SSD_CHUNK, :] = act[:, SSD_D_INNER:].astype(BF16)

    chunks = list(range(0, tm, SSD_CHUNK))
    col = 0
    for o_ref in (q_ref, k_ref, v_ref, z_ref):
        if chunks:
            conv_chunk(chunks.pop(0))
        n = o_ref.shape[1]
        o_ref[...] = jnp.dot(hb, w_ref[:, col:col + n], preferred_element_type=F32).astype(o_ref.dtype)
        col += n
    col += SSD_XBC
    dt_ref[...] = jnp.dot(hb, w_ref[:, col:col + LANES], preferred_element_type=F32)
    for r0 in chunks:
        conv_chunk(r0)


def _inproj0(x, norm_w, mod, w, conv_w, conv_b, seq, qkv_dtype):
    t, d = x.shape
    tm = ROW_TILE
    assert seq % tm == 0 or tm % seq == 0
    halo = seq > tm
    sub = min(seq, tm)
    widths = (NA_WIDTH, NA_WIDTH, NA_WIDTH, SSD_D_INNER, SSD_D_INNER, SSD_XBC - SSD_D_INNER, LANES)
    dtypes = (qkv_dtype,) * 3 + (F32, F32, BF16, F32)
    assert sum(widths) == w.shape[1]
    per = max(seq // tm, 1) if mod.shape[0] > 1 else t // tm
    in_specs = [pl.BlockSpec((tm, d), lambda i: (i, 0))]
    args = [x]
    if halo:
        rb = tm // SUBLANES
        last = t // SUBLANES - 1
        in_specs += [pl.BlockSpec((SUBLANES, d), lambda i: (jnp.maximum(i * rb - 1, 0), 0)),
                     pl.BlockSpec((SUBLANES, d), lambda i: (jnp.minimum((i + 1) * rb, last), 0))]
        args += [x, x]
    in_specs += [_const_spec((1, d)),
                 pl.BlockSpec((1, 6, d), lambda i: (i // per, 0, 0)),
                 _const_spec(w.shape), _const_spec((SUBLANES, SSD_XBC)), _const_spec((1, SSD_XBC))]
    args += [norm_w.reshape(1, d), mod, w,
             jnp.pad(conv_w.astype(F32), ((0, SUBLANES - SSD_CONV), (0, 0))), conv_b.astype(F32).reshape(1, -1)]
    return pl.pallas_call(
        functools.partial(_inproj0_kernel, seq=seq, halo=halo),
        grid=(t // tm,),
        in_specs=in_specs,
        out_specs=[pl.BlockSpec((tm, n), lambda i: (i, 0)) for n in widths],
        out_shape=[jax.ShapeDtypeStruct((t, n), dt) for n, dt in zip(widths, dtypes)],
        scratch_shapes=[pltpu.VMEM((tm // sub, sub + 2 * SUBLANES, SSD_XBC), F32)],
        compiler_params=_params(dimension_semantics=("arbitrary",)),
        name="l0_in_proj",
    )(*args)


def _swap_lane_pairs(x):
    even = (lax.broadcasted_iota(jnp.int32, x.shape, 1) & 1) == 0
    return jnp.where(even, pltpu.roll(x, LANES - 1, 1), pltpu.roll(x, 1, 1))


def _inproj1_kernel(*refs, rope):
    if rope:
        x_ref, nw_ref, mod_ref, w_ref, cos_ref, sin_ref, q_ref, k_ref, v_ref, g_ref = refs
    else:
        x_ref, nw_ref, mod_ref, w_ref, q_ref, k_ref, v_ref, g_ref = refs
    hb = _norm_mod(x_ref[...], nw_ref[...], mod_ref, 0, 1).astype(BF16)
    for o_ref, col, scale in ((q_ref, 0, 1.0), (k_ref, RET_QK_W, RET_QK_DIM ** -0.5)):
        for h in range(RET_HEADS):
            c0 = h * RET_QK_DIM
            y = jnp.dot(hb, w_ref[:, col + c0:col + c0 + RET_QK_DIM], preferred_element_type=F32) * scale
            if rope:
                for j in range(RET_QK_DIM // LANES):
                    lanes = slice(j * LANES, (j + 1) * LANES)
                    yj = y[:, lanes]
                    rot = yj * cos_ref[:, lanes] + _swap_lane_pairs(yj) * sin_ref[:, lanes]
                    o_ref[:, c0 + j * LANES:c0 + (j + 1) * LANES] = rot.astype(o_ref.dtype)
            else:
                o_ref[:, c0:c0 + RET_QK_DIM] = y.astype(o_ref.dtype)
    col = 2 * RET_QK_W
    for o_ref in (v_ref, g_ref):
        o_ref[...] = jnp.dot(hb, w_ref[:, col:col + RET_V_W], preferred_element_type=F32).astype(o_ref.dtype)
        col += RET_V_W


def _inproj1(x, norm_w, mod, w, rows_per_mod, rope_tables=None):
    t, d = x.shape
    tm = ROW_TILE
    per = rows_per_mod // tm
    widths = (RET_QK_W, RET_QK_W, RET_V_W, RET_V_W)
    dtypes = (BF16, BF16, BF16, BF16)
    in_specs = [pl.BlockSpec((tm, d), lambda i: (i, 0)),
                _const_spec((1, d)),
                pl.BlockSpec((1, 6, d), lambda i: (i // per, 0, 0)),
                _const_spec(w.shape)]
    args = [x, norm_w.reshape(1, d), mod, w]
    if rope_tables is not None:
        cos, sin = rope_tables
        nblk = cos.shape[0] // tm
        in_specs += [pl.BlockSpec((tm, cos.shape[1]), lambda i: (i % nblk, 0))] * 2
        args += [cos, sin]
    return pl.pallas_call(
        functools.partial(_inproj1_kernel, rope=rope_tables is not None),
        grid=(t // tm,),
        in_specs=in_specs,
        out_specs=[pl.BlockSpec((tm, n), lambda i: (i, 0)) for n in widths],
        out_shape=[jax.ShapeDtypeStruct((t, n), dt) for n, dt in zip(widths, dtypes)],
        compiler_params=_params(dimension_semantics=("arbitrary",)),
        name="l1_in_proj",
    )(*args)


def _lane_lo(shape):
    return lax.broadcasted_iota(jnp.int32, shape, 1) < NA_HEAD_DIM


def _head_lanes(x, half):
    lo = _lane_lo(x.shape)
    return jnp.where(lo if half == 0 else jnp.logical_not(lo), x, jnp.zeros_like(x))


def _softmax_pv(q2, keys, vals, biases):
    acc = None
    for half in (0, 1):
        qm = _head_lanes(q2, half)
        scores = []
        for kk, bb in zip(keys, biases):
            s = _mm_nt(qm, kk)
            if bb is not None:
                s = bb(half, s)
            scores.append(s)
        mx = functools.reduce(jnp.maximum, [jnp.max(s, axis=-1, keepdims=True) for s in scores])
        es = [jnp.exp(s - mx) for s in scores]
        den = functools.reduce(jnp.add, [jnp.sum(e, axis=-1, keepdims=True) for e in es])
        pv = functools.reduce(jnp.add, [_mm(e, _head_lanes(vv, half)) for e, vv in zip(es, vals)])
        out = pv * (1.0 / den)
        acc = out if acc is None else acc + out
    return acc


def _na_ctx_kernel(q_ref, k_ref, v_ref, o_ref, *, seq):
    scale = NA_HEAD_DIM ** -0.5
    nseq = q_ref.shape[0] // seq
    for s in range(nseq):
        r = slice(s * seq, (s + 1) * seq)
        for hp in range(NA_WIDTH // LANES):
            c = slice(hp * LANES, (hp + 1) * LANES)
            out = _softmax_pv(q_ref[r, c] * scale, [k_ref[r, c]], [v_ref[r, c]], [None])
            o_ref[r, c] = out.astype(o_ref.dtype)


def _na_ctx(q, k, v, seq):
    t, w = q.shape
    spec = pl.BlockSpec((SEQ_BLOCK, w), lambda i: (i, 0))
    return pl.pallas_call(
        functools.partial(_na_ctx_kernel, seq=seq),
        grid=(t // SEQ_BLOCK,),
        in_specs=[spec, spec, spec],
        out_specs=spec,
        out_shape=jax.ShapeDtypeStruct((t, w), BF16),
        compiler_params=_params(dimension_semantics=("arbitrary",)),
        name="na_context",
    )(q, k, v)


NA_QBLK = 256
NA_KWIN = 768


def _na_lat_kernel(q_ref, k_ref, v_ref, kc_ref, vc_ref, tab_ref, o_ref):
    scale = NA_HEAD_DIM ** -0.5
    seq = q_ref.shape[0]
    nblk = seq // NA_QBLK
    rows_per_blk = NA_QBLK // GRID_W
    grid_rows = seq // GRID_W
    n_off = 2 * NA_WIN_ROWS

    qc = lax.broadcasted_iota(jnp.int32, (GRID_W, LANES), 0)
    lane = lax.broadcasted_iota(jnp.int32, (GRID_W, LANES), 1)
    kc = lane & (GRID_W - 1)
    c0 = jnp.clip(qc - NA_WIN_COLS // 2, 0, GRID_W - NA_WIN_COLS)
    col_ok = (kc >= c0) & (kc < c0 + NA_WIN_COLS)
    first = lane < GRID_W
    pair_bias = [[jnp.where(col_ok,
                            pltpu.roll(jnp.broadcast_to(tab_ref[half, e:e + 1, :], (GRID_W, LANES)), 0, 1,
                                       stride=1, stride_axis=0),
                            _NEG_INF)
                  for e in range(n_off)] for half in (0, 1)]
    neg = jnp.full((GRID_W, LANES), _NEG_INF, F32)

    for i in range(nblk):
        ws = 0 if i < nblk // 2 else seq - NA_KWIN
        ws_row = ws // GRID_W

        def add_bias(half, s, i=i, ws_row=ws_row):
            out_rows = []
            for rq in range(rows_per_blk):
                r = i * rows_per_blk + rq
                r0 = min(max(r - NA_WIN_ROWS // 2, 0), grid_rows - NA_WIN_ROWS)
                tiles = []
                for kp in range(NA_KWIN // LANES):
                    kr = ws_row + 2 * kp
                    ok0 = r0 <= kr < r0 + NA_WIN_ROWS
                    ok1 = r0 <= kr + 1 < r0 + NA_WIN_ROWS
                    if ok0 or ok1:
                        tile = pair_bias[half][kr - r + NA_WIN_ROWS]
                        if not ok1:
                            tile = jnp.where(first, tile, _NEG_INF)
                        elif not ok0:
                            tile = jnp.where(first, _NEG_INF, tile)
                    else:
                        tile = neg
                    tiles.append(s[rq * GRID_W:(rq + 1) * GRID_W, kp * LANES:(kp + 1) * LANES] + tile)
                out_rows.append(jnp.concatenate(tiles, axis=1))
            return jnp.concatenate(out_rows, axis=0)

        rows = slice(i * NA_QBLK, (i + 1) * NA_QBLK)
        kw = k_ref[ws:ws + NA_KWIN, :]
        vw = v_ref[ws:ws + NA_KWIN, :]
        out = _softmax_pv(q_ref[rows, :] * scale, [kw, kc_ref[...]], [vw, vc_ref[...]], [add_bias, None])
        o_ref[rows, :] = out.astype(o_ref.dtype)


def _na_bias_table(rel_bias):
    h, n_dr, n_dc = rel_bias.shape
    half_dc = n_dc // 2
    tz = jnp.pad(rel_bias.astype(F32), ((0, 0), (1, 1), (0, 0)))
    lo, hi = tz[:, 0:n_dr + 1], tz[:, 1:n_dr + 2]
    gap = jnp.zeros((h, n_dr + 1, GRID_W - half_dc - 1 - half_dc), F32)
    return jnp.concatenate([lo[..., half_dc:], gap, hi, gap, lo[..., :half_dc]], axis=-1)


def _na_lat(q, k, v, k_ctx, v_ctx, bias_table, seq):
    t, w = q.shape
    b = t // seq
    npair = w // LANES
    past = k_ctx.shape[0] // b
    kspec = pl.BlockSpec((seq, LANES), lambda hp, bb: (bb, hp))
    cspec = pl.BlockSpec((past, LANES), lambda hp, bb: (bb, hp))
    tspec = pl.BlockSpec((2,) + bias_table.shape[1:], lambda hp, bb: (hp, 0, 0))
    return pl.pallas_call(
        _na_lat_kernel,
        grid=(npair, b),
        in_specs=[kspec, kspec, kspec, cspec, cspec, tspec],
        out_specs=kspec,
        out_shape=jax.ShapeDtypeStruct((t, w), BF16),
        compiler_params=_params(dimension_semantics=("arbitrary", "arbitrary")),
        name="na_latent",
    )(q, k, v, k_ctx, v_ctx, bias_table)


def _pack3(v, lane):
    vm = jnp.where(lane < 32, v, 0.0)
    hi = vm.astype(BF16).astype(F32)
    r1 = vm - hi
    mid = r1.astype(BF16).astype(F32)
    lo = r1 - mid
    return (hi + pltpu.roll(mid, 32, 1) + pltpu.roll(lo, 64, 1)).astype(BF16)


def _unpack3(res):
    return res + pltpu.roll(res, 96, 1) + pltpu.roll(res, 64, 1)


def _ssd_selectors():
    r = np.arange(LANES)
    out = []
    for width in (SSD_HEAD_DIM, SSD_CHUNK):
        l = np.arange(SSD_HEADS * width)
        for d in (0, 1):
            sel = (r[:, None] < 96) & ((r[:, None] % 32) == d * SSD_HEADS + l[None, :] // width)
            out.append(jnp.asarray(sel, BF16))
    return out


def _ssd_kernel(*refs, seq, has_state):
    if has_state:
        (z_ref, xs_ref, bc_ref, dtr_ref, dtb_ref, alog_ref, dsk_ref, nw_ref,
         s64f_ref, s64b_ref, s128f_ref, s128b_ref, s0_ref,
         y_ref, yacc, csp, ep, wp, cst, dtt, st) = refs
        sfin_ref = None
    else:
        (z_ref, xs_ref, bc_ref, dtr_ref, dtb_ref, alog_ref, dsk_ref, nw_ref,
         s64f_ref, s64b_ref, s128f_ref, s128b_ref,
         y_ref, sfin_ref, yacc, csp, ep, wp, cst, dtt, st) = refs
        s0_ref = None
    ch = SSD_CHUNK
    nseq = z_ref.shape[0] // seq
    nch = seq // ch
    assert nch % 2 == 0
    lane = lax.broadcasted_iota(jnp.int32, (ch, LANES), 1)
    ri = lax.broadcasted_iota(jnp.int32, (ch, ch), 0)
    ci = lax.broadcasted_iota(jnp.int32, (ch, ch), 1)
    keeps = (ci <= ri, ci >= ri)
    tril = jnp.where(keeps[0], 1.0, 0.0).astype(BF16)
    triu = jnp.where(keeps[1], 1.0, 0.0).astype(BF16)
    lane_lo = lane < SSD_HEAD_DIM
    fwd_lane = lane < SSD_HEADS
    neg_a = -jnp.exp(alog_ref[...])
    gw = SSD_D_INNER // SSD_GROUPS
    sel64s = (s64f_ref, s64b_ref)
    sel128s = (s128f_ref, s128b_ref)
    edges = (ch - 1, 0)

    def seq_body(s, _):
        base = pl.multiple_of(s * seq, seq)

        def decay_body(c, _):
            r0 = pl.multiple_of(c * ch, ch)
            yacc[pl.ds(r0, ch), :] = dsk_ref[...] * xs_ref[pl.ds(base + r0, ch), :]
            dt = _softplus(dtr_ref[pl.ds(base + r0, ch), :] + dtb_ref[...])
            la = _pack3(dt * neg_a, lane)
            cs = jnp.where(fwd_lane,
                           _unpack3(jnp.dot(tril, la, preferred_element_type=F32)),
                           _unpack3(jnp.dot(triu, la, preferred_element_type=F32)))
            tot = jnp.where(fwd_lane, cs[ch - 1:ch, :], cs[0:1, :])
            rows = pl.ds(r0, ch)
            csp[rows, :] = _pack3(cs, lane)
            ep[rows, :] = _pack3(jnp.exp(cs), lane)
            wp[rows, :] = _pack3(dt * jnp.exp(tot - cs), lane)
            cst[c] = cs.T
            dtt[c] = dt.T
            return 0

        lax.fori_loop(0, nch, decay_body, 0, unroll=2)

        for d in (0, 1):
            if has_state:
                st[d] = s0_ref[0, d]
            else:
                st[d] = jnp.zeros((SSD_STATE, SSD_D_INNER), F32)

        def chunk(d, c):
            r0 = pl.multiple_of(c * ch, ch)
            rows = pl.ds(r0, ch)
            col = jnp.dot(csp[rows, :], sel128s[d][...], preferred_element_type=F32)
            e64 = jnp.dot(ep[rows, :], sel64s[d][...], preferred_element_type=F32)
            w64 = jnp.dot(wp[rows, :], sel64s[d][...], preferred_element_type=F32)
            cs_t = cst[c]
            dt_t = dtt[c]
            in_rows = pl.ds(base + r0, ch)
            x_c = xs_ref[in_rows, :]
            state = st[d]
            state_b = state.astype(BF16)
            xw = (x_c * w64).astype(BF16)
            y_intra, y_state, upd = [], [], []
            for g in range(SSD_GROUPS):
                b_g = bc_ref[in_rows, g * SSD_STATE:(g + 1) * SSD_STATE]
                cofs = SSD_GROUPS * SSD_STATE
                c_g = bc_ref[in_rows, cofs + g * SSD_STATE:cofs + (g + 1) * SSD_STATE]
                gram = _mm_nt(c_g, b_g)
                for pp in range(gw // LANES):
                    p = g * (gw // LANES) + pp
                    ws = []
                    for half in (0, 1):
                        h = 2 * p + half
                        hd = d * SSD_HEADS + h
                        diff = col[:, h * ch:(h + 1) * ch] - cs_t[hd:hd + 1, :]
                        wm = jnp.exp(jnp.where(keeps[d], diff, _NEG_INF)) * gram * dt_t[hd:hd + 1, :]
                        ws.append(wm.astype(BF16))
                    xp = x_c[:, p * LANES:(p + 1) * LANES]
                    xcat = jnp.concatenate([jnp.where(lane_lo, xp, 0.0), jnp.where(lane_lo, 0.0, xp)],
                                           axis=0).astype(BF16)
                    y_intra.append(jnp.dot(jnp.concatenate(ws, axis=1), xcat, preferred_element_type=F32))
                y_state.append(jnp.dot(c_g, state_b[:, g * gw:(g + 1) * gw], preferred_element_type=F32))
                upd.append(jnp.dot(b_g.T, xw[:, g * gw:(g + 1) * gw], preferred_element_type=F32))
            yacc[rows, :] = (yacc[rows, :] + jnp.concatenate(y_intra, axis=1)
                             + jnp.concatenate(y_state, axis=1) * e64)
            st[d] = e64[edges[d]:edges[d] + 1, :] * state + jnp.concatenate(upd, axis=1)

        def pair_body(j, _):
            for step in (0, 1):
                chunk(0, 2 * j + step)
                chunk(1, nch - 1 - 2 * j - step)
            return 0

        lax.fori_loop(0, nch // 2, pair_body, 0)
        if sfin_ref is not None:
            for d in (0, 1):
                sfin_ref[s, d] = st[d]

        def out_body(c, _):
            r0 = pl.multiple_of(c * ch, ch)
            y = yacc[pl.ds(r0, ch), :] * _silu(z_ref[pl.ds(base + r0, ch), :])
            outs = []
            for g in range(SSD_GROUPS):
                yg = y[:, g * gw:(g + 1) * gw]
                outs.append(yg * lax.rsqrt(jnp.mean(yg * yg, axis=-1, keepdims=True) + EPS))
            y_ref[pl.ds(base + r0, ch), :] = (jnp.concatenate(outs, axis=1) * nw_ref[...]).astype(y_ref.dtype)
            return 0

        lax.fori_loop(0, nch, out_body, 0)
        return 0

    lax.fori_loop(0, nseq, seq_body, 0)


def _ssd(z, xs, bc, dtr, a_log, dt_bias, d_skip, norm_w, seq, s0=None):
    t = z.shape[0]
    nseq = SEQ_BLOCK // seq
    nblk = t // SEQ_BLOCK
    nch = seq // SSD_CHUNK
    pad32 = lambda a: jnp.pad(a.astype(F32).reshape(1, -1), ((0, 0), (0, LANES - 2 * SSD_HEADS)))
    consts = [pad32(dt_bias), pad32(a_log),
              jnp.repeat(d_skip.astype(F32), SSD_HEAD_DIM).reshape(1, -1),
              norm_w.astype(F32).reshape(1, -1)] + _ssd_selectors()
    row = lambda n: pl.BlockSpec((SEQ_BLOCK, n), lambda i: (i, 0))
    in_specs = ([row(SSD_D_INNER), row(SSD_D_INNER), row(bc.shape[1]), row(LANES)]
                + [_const_spec(c.shape) for c in consts])
    args = [z, xs, bc, dtr] + consts
    out_specs = [row(SSD_D_INNER)]
    out_shape = [jax.ShapeDtypeStruct((t, SSD_D_INNER), BF16)]
    state_shape = (2, SSD_STATE, SSD_D_INNER)
    if s0 is not None:
        in_specs.append(pl.BlockSpec((1,) + state_shape, lambda i: (i, 0, 0, 0)))
        args.append(s0)
    else:
        out_specs.append(pl.BlockSpec((nseq,) + state_shape, lambda i: (i, 0, 0, 0)))
        out_shape.append(jax.ShapeDtypeStruct((t // seq,) + state_shape, F32))
    scratch = [pltpu.VMEM((seq, SSD_D_INNER), F32),
               pltpu.VMEM((seq, LANES), BF16), pltpu.VMEM((seq, LANES), BF16), pltpu.VMEM((seq, LANES), BF16),
               pltpu.VMEM((nch, LANES, SSD_CHUNK), F32), pltpu.VMEM((nch, LANES, SSD_CHUNK), F32),
               pltpu.VMEM(state_shape, F32)]
    return pl.pallas_call(
        functools.partial(_ssd_kernel, seq=seq, has_state=s0 is not None),
        grid=(nblk,),
        in_specs=in_specs,
        out_specs=out_specs,
        out_shape=out_shape,
        scratch_shapes=scratch,
        compiler_params=_params(dimension_semantics=("arbitrary",)),
        name="ssd_mixer",
    )(*args)


def _log_sigmoid(x):
    return -_softplus(-x)


def _ret_kernel(*refs, seq, has_state):
    if has_state:
        q_ref, k_ref, v_ref, g_ref, dec_ref, nw_ref, s0_ref, y_ref, yacc, st = refs
        sfin_ref = None
    else:
        q_ref, k_ref, v_ref, g_ref, dec_ref, nw_ref, y_ref, sfin_ref, yacc, st = refs
        s0_ref = None
    ch = RET_CHUNK
    nseq = q_ref.shape[0] // seq
    nch = seq // ch
    gf = _log_sigmoid(dec_ref[0, 0:1, :])
    gb = _log_sigmoid(dec_ref[0, 1:2, :])
    ri = lax.broadcasted_iota(jnp.int32, (ch, ch), 0)
    ci = lax.broadcasted_iota(jnp.int32, (ch, ch), 1)
    dist = (ri - ci).astype(F32)
    gfk, gbk = gf[:, :ch], gb[:, :ch]
    decay = (jnp.where(ci <= ri, jnp.exp(jnp.where(ci <= ri, dist, 0.0) * gfk), 0.0)
             + jnp.where(ci >= ri, jnp.exp(jnp.where(ci >= ri, -dist, 0.0) * gbk), 0.0))
    pos = lax.broadcasted_iota(jnp.int32, (ch, RET_QK_DIM), 0).astype(F32)
    gfq, gbq = gf[:, :RET_QK_DIM], gb[:, :RET_QK_DIM]
    e_f = jnp.exp((pos + 1.0) * gfq)
    e_b = jnp.exp((ch - pos) * gbq)
    tail_f = jnp.exp((ch - 1.0 - pos) * gfq)
    tail_b = jnp.exp(pos * gbq)
    dec_f = jnp.exp(ch * gf)
    dec_b = jnp.exp(ch * gb)

    for s in range(nseq):
        for d in (0, 1):
            if has_state:
                st[d] = s0_ref[0, d, 0]
            else:
                st[d] = jnp.zeros((RET_QK_DIM, RET_V_DIM), F32)
        for c in range(nch):
            rows = slice(s * seq + c * ch, s * seq + (c + 1) * ch)
            q, k, v = q_ref[rows, :], k_ref[rows, :].astype(F32), v_ref[rows, :]
            y = _mm(_mm_nt(q, k) * decay, v)
            if has_state or c > 0:
                y = y + _mm(q.astype(F32) * e_f, st[0])
            yacc[rows, :] = y
            st[0] = dec_f * st[0] + _mm((k * tail_f).T, v)
        for c in reversed(range(nch)):
            rows = slice(s * seq + c * ch, s * seq + (c + 1) * ch)
            q, k, v = q_ref[rows, :], k_ref[rows, :].astype(F32), v_ref[rows, :]
            if has_state or c < nch - 1:
                yacc[rows, :] = yacc[rows, :] + _mm(q.astype(F32) * e_b, st[1])
            st[1] = dec_b * st[1] + _mm((k * tail_b).T, v)
        if sfin_ref is not None:
            for d in (0, 1):
                sfin_ref[s, d, 0] = st[d]
        for c in range(nch):
            rows = slice(s * seq + c * ch, s * seq + (c + 1) * ch)
            y = yacc[rows, :]
            y = y * lax.rsqrt(jnp.mean(y * y, axis=-1, keepdims=True) + EPS)
            y_ref[rows, :] = (y * nw_ref[...] * _silu(g_ref[rows, :].astype(F32))).astype(y_ref.dtype)


def _retention(q, k, v, g, ret_decay, norm_w, seq, s0=None):
    t = q.shape[0]
    nseq = SEQ_BLOCK // seq
    nblk = t // SEQ_BLOCK
    dec = jnp.broadcast_to(jnp.pad(ret_decay.astype(F32).T, ((0, 0), (0, SUBLANES - 2)))[:, :, None],
                           (RET_HEADS, SUBLANES, RET_V_DIM))
    qspec = pl.BlockSpec((SEQ_BLOCK, RET_QK_DIM), lambda i, h: (i, h))
    vspec = pl.BlockSpec((SEQ_BLOCK, RET_V_DIM), lambda i, h: (i, h))
    in_specs = [qspec, qspec, vspec, vspec,
                pl.BlockSpec((1, SUBLANES, RET_V_DIM), lambda i, h: (h, 0, 0)),
                pl.BlockSpec((1, RET_V_DIM), lambda i, h: (0, h))]
    args = [q, k, v, g, dec, norm_w.astype(F32).reshape(1, -1)]
    out_specs = [vspec]
    out_shape = [jax.ShapeDtypeStruct((t, RET_V_W), BF16)]
    if s0 is not None:
        in_specs.append(pl.BlockSpec((1, 2, 1, RET_QK_DIM, RET_V_DIM), lambda i, h: (i, 0, h, 0, 0)))
        args.append(s0)
    else:
        out_specs.append(pl.BlockSpec((nseq, 2, 1, RET_QK_DIM, RET_V_DIM), lambda i, h: (i, 0, h, 0, 0)))
        out_shape.append(jax.ShapeDtypeStruct((t // seq, 2, RET_HEADS, RET_QK_DIM, RET_V_DIM), F32))
    return pl.pallas_call(
        functools.partial(_ret_kernel, seq=seq, has_state=s0 is not None),
        grid=(nblk, RET_HEADS),
        in_specs=in_specs,
        out_specs=out_specs,
        out_shape=out_shape,
        scratch_shapes=[pltpu.VMEM((SEQ_BLOCK, RET_V_DIM), F32),
                        pltpu.VMEM((2, RET_QK_DIM, RET_V_DIM), F32)],
        compiler_params=_params(dimension_semantics=("arbitrary", "arbitrary")),
        name="retention_mixer",
    )(*args)


def _out_ffn_kernel(*refs, n_mix, final_norm):
    x_ref = refs[0]
    mix_refs = refs[1:1 + n_mix]
    wo_ref, mod_ref, n2_ref, w1_ref, w3_ref, w2_ref = refs[1 + n_mix:7 + n_mix]
    if final_norm:
        fn_ref, o_ref, act = refs[7 + n_mix:]
    else:
        o_ref, act = refs[7 + n_mix:]
    mixed, row = None, 0
    for m_ref in mix_refs:
        k = m_ref.shape[1]
        term = jnp.dot(m_ref[...], wo_ref[row:row + k, :], preferred_element_type=F32)
        mixed = term if mixed is None else mixed + term
        row += k
    x1 = x_ref[...] + mod_ref[0, 2:3, :] * mixed
    hb = _norm_mod(x1, n2_ref[...], mod_ref, 3, 4).astype(BF16)
    for c in range(w1_ref.shape[1] // FFN_CHUNK):
        cols = slice(c * FFN_CHUNK, (c + 1) * FFN_CHUNK)
        h1 = jnp.dot(hb, w1_ref[:, cols], preferred_element_type=F32)
        h3 = jnp.dot(hb, w3_ref[:, cols], preferred_element_type=F32)
        act[:, cols] = (_silu(h1) * h3).astype(BF16)
    x2 = x1 + mod_ref[0, 5:6, :] * jnp.dot(act[...], w2_ref[...], preferred_element_type=F32)
    if final_norm:
        ms = jnp.mean(x2 * x2, axis=-1, keepdims=True)
        x2 = x2 * lax.rsqrt(ms + EPS) * fn_ref[...]
    o_ref[...] = x2


def _out_ffn(x, mixes, wo, mod, norm2_w, w1, w3, w2, rows_per_mod, final_norm_w=None):
    t, d = x.shape
    tm = ROW_TILE
    per = rows_per_mod // tm
    single = dict(pipeline_mode=pl.Buffered(1))
    in_specs = [pl.BlockSpec((tm, d), lambda i: (i, 0))]
    in_specs += [pl.BlockSpec((tm, m.shape[1]), lambda i: (i, 0)) for m in mixes]
    in_specs += [_const_spec(wo.shape, **single),
                 pl.BlockSpec((1, 6, d), lambda i: (i // per, 0, 0)), _const_spec((1, d))]
    in_specs += [_const_spec(w.shape, **single) for w in (w1, w3, w2)]
    args = [x, *mixes, wo, mod, norm2_w.reshape(1, d), w1, w3, w2]
    if final_norm_w is not None:
        in_specs.append(_const_spec((1, d)))
        args.append(final_norm_w.reshape(1, d))
    return pl.pallas_call(
        functools.partial(_out_ffn_kernel, n_mix=len(mixes), final_norm=final_norm_w is not None),
        grid=(t // tm,),
        in_specs=in_specs,
        out_specs=pl.BlockSpec((tm, d), lambda i: (i, 0)),
        out_shape=jax.ShapeDtypeStruct((t, d), F32),
        scratch_shapes=[pltpu.VMEM((tm, w1.shape[1]), BF16)],
        compiler_params=_params(dimension_semantics=("arbitrary",)),
        name="out_proj_ffn",
    )(*args)


def _rope_tables(seq):
    half = RET_QK_DIM // 2
    t = jnp.arange(seq)
    row = (t // GRID_W).astype(F32)
    col = (t % GRID_W).astype(F32)
    freqs = ROPE_BASE ** (-jnp.arange(0, half, 2, dtype=F32) / half)
    ang = jnp.concatenate([row[:, None] * freqs, col[:, None] * freqs], axis=-1)
    cos = jnp.repeat(jnp.cos(ang), 2, axis=1)
    sin = jnp.stack([-jnp.sin(ang), jnp.sin(ang)], axis=-1).reshape(seq, RET_QK_DIM)
    return cos, sin


def kernel(x_prompt, x_sample, cache_l0_na_k, cache_l0_na_v, state_l0_ssd, state_l1_ret, c, c_ctx,
           l0_norm1_w, l0_norm2_w, l0_mod_w, l0_mod_b, l0_w_in, l0_w_out, l0_na_bias, l0_conv_w, l0_conv_b,
           l0_ssd_a_log, l0_ssd_dt_bias, l0_ssd_d, l0_ssd_norm_w, l0_ffn_w1, l0_ffn_w3, l0_ffn_w2,
           l1_norm1_w, l1_norm2_w, l1_mod_w, l1_mod_b, l1_w_in, l1_w_out, l1_ret_decay, l1_ret_norm_w,
           l1_ffn_w1, l1_ffn_w3, l1_ffn_w2, final_norm_w):
    bc, lc, d = x_prompt.shape
    bl, ll, _ = x_sample.shape
    assert d == D_MODEL and ll == SEQ_BLOCK and SEQ_BLOCK % lc == 0 and bc % (SEQ_BLOCK // lc) == 0
    tc, tl = bc * lc, bl * ll
    xc = x_prompt.reshape(tc, d)
    xl = x_sample.reshape(tl, d)

    nrow = SUBLANES * ((1 + bl + SUBLANES - 1) // SUBLANES)
    cond = jnp.concatenate([c_ctx[None], c, jnp.zeros((nrow - 1 - bl, d), F32)], axis=0)
    mods = []
    for mod_w, mod_b in ((l0_mod_w, l0_mod_b), (l1_mod_w, l1_mod_b)):
        m = _adaln(cond, mod_w, mod_b).reshape(nrow, 6, d)
        mods.append((m[0:1], m[1:1 + bl]))

    w_in0 = jnp.pad(l0_w_in.astype(BF16), ((0, 0), (0, LANES - 2 * SSD_HEADS)))
    wo0 = l0_w_out.astype(BF16)
    ffn0 = (l0_ffn_w1.astype(BF16), l0_ffn_w3.astype(BF16), l0_ffn_w2.astype(BF16))
    ssd_p = (l0_ssd_a_log, l0_ssd_dt_bias, l0_ssd_d, l0_ssd_norm_w)

    qc, kc, vc, zc, xsc, bcc, dtc = _inproj0(xc, l0_norm1_w, mods[0][0], w_in0, l0_conv_w, l0_conv_b, lc, F32)
    att_c = _na_ctx(qc, kc, vc, lc)
    ssd_c, sfin_c = _ssd(zc, xsc, bcc, dtc, *ssd_p, seq=lc)
    xc = _out_ffn(xc, (att_c, ssd_c), wo0, mods[0][0], l0_norm2_w, *ffn0, tc)

    ql, kl, vl, zl, xsl, bcl, dtl = _inproj0(xl, l0_norm1_w, mods[0][1], w_in0, l0_conv_w, l0_conv_b, ll, BF16)
    past = cache_l0_na_k.shape[1]
    att_l = _na_lat(ql, kl, vl, cache_l0_na_k.reshape(bl * past, NA_WIDTH),
                    cache_l0_na_v.reshape(bl * past, NA_WIDTH), _na_bias_table(l0_na_bias), ll)
    s0_ssd = state_l0_ssd.transpose(0, 1, 3, 2, 4).reshape(bl, 2, SSD_STATE, SSD_D_INNER)
    (ssd_l,) = _ssd(zl, xsl, bcl, dtl, *ssd_p, seq=ll, s0=s0_ssd)
    xl = _out_ffn(xl, (att_l, ssd_l), wo0, mods[0][1], l0_norm2_w, *ffn0, ll)

    w_in1 = l1_w_in.astype(BF16)
    wo1 = l1_w_out.astype(BF16)
    ffn1 = (l1_ffn_w1.astype(BF16), l1_ffn_w3.astype(BF16), l1_ffn_w2.astype(BF16))

    q1, k1, v1, g1 = _inproj1(xc, l1_norm1_w, mods[1][0], w_in1, tc)
    ret_c, ret_state = _retention(q1, k1, v1, g1, l1_ret_decay, l1_ret_norm_w, lc)
    y_prompt = _out_ffn(xc, (ret_c,), wo1, mods[1][0], l1_norm2_w, *ffn1, tc, final_norm_w=final_norm_w)

    q1, k1, v1, g1 = _inproj1(xl, l1_norm1_w, mods[1][1], w_in1, ll, rope_tables=_rope_tables(ll))
    (ret_l,) = _retention(q1, k1, v1, g1, l1_ret_decay, l1_ret_norm_w, ll, s0=state_l1_ret)
    y_sample = _out_ffn(xl, (ret_l,), wo1, mods[1][1], l1_norm2_w, *ffn1, ll, final_norm_w=final_norm_w)

    new_ssd = sfin_c.reshape(bc, 2, SSD_STATE, SSD_HEADS, SSD_HEAD_DIM).transpose(0, 1, 3, 2, 4)
    return (y_prompt.reshape(bc, lc, d), y_sample.reshape(bl, ll, d),
            kc.reshape(bc, lc, NA_HEADS, NA_HEAD_DIM), vc.reshape(bc, lc, NA_HEADS, NA_HEAD_DIM),
            new_ssd, ret_state)
```

```python
import functools

import numpy as np
import jax
import jax.numpy as jnp
from jax import lax
from jax.experimental import pallas as pl
from jax.experimental.pallas import tpu as pltpu

F32 = jnp.float32
BF16 = jnp.bfloat16

D_MODEL = 1024
GRID_W = 64
NA_HEADS = 8
NA_HEAD_DIM = 64
NA_WIDTH = NA_HEADS * NA_HEAD_DIM
NA_WIN_ROWS = 8
NA_WIN_COLS = 16
SSD_HEADS = 16
SSD_HEAD_DIM = 64
SSD_D_INNER = SSD_HEADS * SSD_HEAD_DIM
SSD_GROUPS = 2
SSD_STATE = 128
SSD_CONV = 5
SSD_XBC = SSD_D_INNER + 2 * SSD_GROUPS * SSD_STATE
RET_HEADS = 4
RET_QK_DIM = 256
RET_V_DIM = 512
RET_QK_W = RET_HEADS * RET_QK_DIM
RET_V_W = RET_HEADS * RET_V_DIM
ROPE_BASE = 10000.0
EPS = 1e-6

LANES = 128
SUBLANES = 8
SEQ_BLOCK = 1024
ROW_TILE = 512
SSD_CHUNK = 128
CONV_ROWS = 256
RET_CHUNK = 256
FFN_CHUNK = 256
VMEM_LIMIT = 56 * 1024 * 1024

_NEG_INF = float("-inf")


def _params(**kw):
    return pltpu.CompilerParams(vmem_limit_bytes=VMEM_LIMIT, **kw)


def _silu(x):
    return x * (1.0 / (1.0 + jnp.exp(-x)))


def _softplus(x):
    return jnp.maximum(x, 0.0) + jnp.log(1.0 + jnp.exp(-jnp.abs(x)))


def _mm(a, b):
    return jnp.dot(a.astype(BF16), b.astype(BF16), preferred_element_type=F32)


def _mm_nt(a, b):
    return lax.dot_general(a.astype(BF16), b.astype(BF16), (((1,), (1,)), ((), ())),
                           preferred_element_type=F32)


def _const_spec(shape, **kw):
    nd = len(shape)
    return pl.BlockSpec(shape, lambda *_: (0,) * nd, **kw)


def _cast_kernel(*refs):
    n = len(refs) // 2
    for i_ref, o_ref in zip(refs[:n], refs[n:]):
        o_ref[...] = i_ref[...].astype(o_ref.dtype)


def _to_bf16(*ws):
    steps = 8
    specs = [pl.BlockSpec((w.shape[0] // steps, w.shape[1]), lambda i: (i, 0)) for w in ws]
    assert all(w.shape[0] % (steps * 2 * SUBLANES) == 0 for w in ws)
    return pl.pallas_call(
        _cast_kernel,
        grid=(steps,),
        in_specs=specs,
        out_specs=specs,
        out_shape=[jax.ShapeDtypeStruct(w.shape, BF16) for w in ws],
        compiler_params=_params(dimension_semantics=("arbitrary",)),
        name="weights_to_bf16",
    )(*ws)


def _adaln_kernel(c_ref, w_ref, b_ref, o_ref):
    o_ref[...] = _mm(_silu(c_ref[...]), w_ref[...]) + b_ref[...]


def _adaln(cond, mod_w, mod_b):
    r, d = cond.shape
    n = mod_w.shape[1]
    tn = 1536
    return pl.pallas_call(
        _adaln_kernel,
        grid=(n // tn,),
        in_specs=[_const_spec((r, d)),
                  pl.BlockSpec((d, tn), lambda j: (0, j)),
                  pl.BlockSpec((1, tn), lambda j: (0, j))],
        out_specs=pl.BlockSpec((r, tn), lambda j: (0, j)),
        out_shape=jax.ShapeDtypeStruct((r, n), F32),
        compiler_params=_params(dimension_semantics=("arbitrary",)),
        name="adaln_mod",
    )(cond, mod_w, mod_b.reshape(1, n))


def _norm_mod(x, nw, mod_ref, shift_idx, scale_idx):
    ms = jnp.mean(x * x, axis=-1, keepdims=True)
    h = x * lax.rsqrt(ms + EPS) * nw
    return h * (1.0 + mod_ref[0, scale_idx:scale_idx + 1, :]) + mod_ref[0, shift_idx:shift_idx + 1, :]


def _inproj0_kernel(*refs, seq, halo):
    if halo:
        (x_ref, xp_ref, xn_ref, nw_ref, mod_ref, w_ref, cw_ref, cb_ref,
         q_ref, k_ref, v_ref, z_ref, xs_ref, bc_ref, dt_ref, xpad) = refs
    else:
        (x_ref, nw_ref, mod_ref, w_ref, cw_ref, cb_ref,
         q_ref, k_ref, v_ref, z_ref, xs_ref, bc_ref, dt_ref, xpad) = refs
    tm = x_ref.shape[0]
    pad = SUBLANES
    sub = xpad.shape[1] - 2 * pad
    h = _norm_mod(x_ref[...], nw_ref[...], mod_ref, 0, 1)
    hb = h.astype(BF16)
    xbc_col = 3 * NA_WIDTH + SSD_D_INNER
    w_xbc = w_ref[:, xbc_col:xbc_col + SSD_XBC]
    if halo:
        tiles_per_seq = seq // tm
        p = pl.program_id(0) % tiles_per_seq
        h_prev = jnp.where(p > 0, _norm_mod(xp_ref[...], nw_ref[...], mod_ref, 0, 1), 0.0)
        h_next = jnp.where(p < tiles_per_seq - 1, _norm_mod(xn_ref[...], nw_ref[...], mod_ref, 0, 1), 0.0)
        ext = jnp.concatenate([h_prev, h, h_next], axis=0).astype(BF16)
        xpad[0] = jnp.dot(ext, w_xbc, preferred_element_type=F32)
    else:
        xbc = jnp.dot(hb, w_xbc, preferred_element_type=F32)
        for s in range(tm // sub):
            xpad[s, 0:pad, :] = jnp.zeros((pad, SSD_XBC), F32)
            xpad[s, pad:pad + sub, :] = xbc[s * sub:(s + 1) * sub]
            xpad[s, pad + sub:2 * pad + sub, :] = jnp.zeros((pad, SSD_XBC), F32)

    cr = CONV_ROWS

    def conv_chunk(r0):
        s, rs = divmod(r0, sub)
        nwin = cr + 2 * pad
        for lt in range(SSD_XBC // LANES):
            cols = slice(lt * LANES, (lt + 1) * LANES)
            win = xpad[s, rs:rs + nwin, cols]
            taps = [cw_ref[k:k + 1, cols] * win for k in range(SSD_CONV)]
            before = taps[1] + pltpu.roll(taps[0], 1, 0)
            after = taps[3] + pltpu.roll(taps[4], nwin - 1, 0)
            conv = taps[2] + pltpu.roll(before, 1, 0) + pltpu.roll(after, nwin - 1, 0)
            act = _silu(conv[pad:pad + cr, :] + cb_ref[:, cols])
            if lt < SSD_D_INNER // LANES:
                xs_ref[r0:r0 + cr, cols] = act
            else:
                bc_ref[r0:r0 + cr, lt * LANES - SSD_D_INNER:(lt + 1) * LANES - SSD_D_INNER] = act.astype(BF16)

    chunks = list(range(0, tm, cr))
    col = 0
    for o_ref in (q_ref, k_ref, v_ref, z_ref):
        if chunks:
            conv_chunk(chunks.pop(0))
        n = o_ref.shape[1]
        o_ref[...] = jnp.dot(hb, w_ref[:, col:col + n], preferred_element_type=F32).astype(o_ref.dtype)
        col += n
    col += SSD_XBC
    dt_ref[...] = jnp.dot(hb, w_ref[:, col:col + LANES], preferred_element_type=F32)
    for r0 in chunks:
        conv_chunk(r0)


def _inproj0(x, norm_w, mod, w, conv_w, conv_b, seq, qkv_dtype):
    t, d = x.shape
    tm = ROW_TILE
    assert seq % tm == 0 or tm % seq == 0
    halo = seq > tm
    sub = min(seq, tm)
    widths = (NA_WIDTH, NA_WIDTH, NA_WIDTH, SSD_D_INNER, SSD_D_INNER, SSD_XBC - SSD_D_INNER, LANES)
    dtypes = (qkv_dtype,) * 3 + (F32, F32, BF16, F32)
    assert sum(widths) == w.shape[1]
    per = max(seq // tm, 1) if mod.shape[0] > 1 else t // tm
    in_specs = [pl.BlockSpec((tm, d), lambda i: (i, 0))]
    args = [x]
    if halo:
        rb = tm // SUBLANES
        last = t // SUBLANES - 1
        in_specs += [pl.BlockSpec((SUBLANES, d), lambda i: (jnp.maximum(i * rb - 1, 0), 0)),
                     pl.BlockSpec((SUBLANES, d), lambda i: (jnp.minimum((i + 1) * rb, last), 0))]
        args += [x, x]
    in_specs += [_const_spec((1, d)),
                 pl.BlockSpec((1, 6, d), lambda i: (i // per, 0, 0)),
                 _const_spec(w.shape), _const_spec((SUBLANES, SSD_XBC)), _const_spec((1, SSD_XBC))]
    args += [norm_w.reshape(1, d), mod, w,
             jnp.pad(conv_w.astype(F32), ((0, SUBLANES - SSD_CONV), (0, 0))), conv_b.astype(F32).reshape(1, -1)]
    return pl.pallas_call(
        functools.partial(_inproj0_kernel, seq=seq, halo=halo),
        grid=(t // tm,),
        in_specs=in_specs,
        out_specs=[pl.BlockSpec((tm, n), lambda i: (i, 0)) for n in widths],
        out_shape=[jax.ShapeDtypeStruct((t, n), dt) for n, dt in zip(widths, dtypes)],
        scratch_shapes=[pltpu.VMEM((tm // sub, sub + 2 * SUBLANES, SSD_XBC), F32)],
        compiler_params=_params(dimension_semantics=("arbitrary",)),
        name="l0_in_proj",
    )(*args)


def _swap_lane_pairs(x):
    even = (lax.broadcasted_iota(jnp.int32, x.shape, 1) & 1) == 0
    return jnp.where(even, pltpu.roll(x, LANES - 1, 1), pltpu.roll(x, 1, 1))


def _inproj1_kernel(*refs, rope):
    if rope:
        x_ref, nw_ref, mod_ref, w_ref, cos_ref, sin_ref, q_ref, k_ref, v_ref, g_ref = refs
    else:
        x_ref, nw_ref, mod_ref, w_ref, q_ref, k_ref, v_ref, g_ref = refs
    hb = _norm_mod(x_ref[...], nw_ref[...], mod_ref, 0, 1).astype(BF16)
    for o_ref, col, scale in ((q_ref, 0, 1.0), (k_ref, RET_QK_W, RET_QK_DIM ** -0.5)):
        for h in range(RET_HEADS):
            c0 = h * RET_QK_DIM
            y = jnp.dot(hb, w_ref[:, col + c0:col + c0 + RET_QK_DIM], preferred_element_type=F32) * scale
            if rope:
                for j in range(RET_QK_DIM // LANES):
                    lanes = slice(j * LANES, (j + 1) * LANES)
                    yj = y[:, lanes]
                    rot = yj * cos_ref[:, lanes] + _swap_lane_pairs(yj) * sin_ref[:, lanes]
                    o_ref[:, c0 + j * LANES:c0 + (j + 1) * LANES] = rot.astype(o_ref.dtype)
            else:
                o_ref[:, c0:c0 + RET_QK_DIM] = y.astype(o_ref.dtype)
    col = 2 * RET_QK_W
    for o_ref in (v_ref, g_ref):
        o_ref[...] = jnp.dot(hb, w_ref[:, col:col + RET_V_W], preferred_element_type=F32).astype(o_ref.dtype)
        col += RET_V_W


def _inproj1(x, norm_w, mod, w, rows_per_mod, rope_tables=None):
    t, d = x.shape
    tm = ROW_TILE
    per = rows_per_mod // tm
    widths = (RET_QK_W, RET_QK_W, RET_V_W, RET_V_W)
    dtypes = (BF16, BF16, BF16, BF16)
    in_specs = [pl.BlockSpec((tm, d), lambda i: (i, 0)),
                _const_spec((1, d)),
                pl.BlockSpec((1, 6, d), lambda i: (i // per, 0, 0)),
                _const_spec(w.shape)]
    args = [x, norm_w.reshape(1, d), mod, w]
    if rope_tables is not None:
        cos, sin = rope_tables
        nblk = cos.shape[0] // tm
        in_specs += [pl.BlockSpec((tm, cos.shape[1]), lambda i: (i % nblk, 0))] * 2
        args += [cos, sin]
    return pl.pallas_call(
        functools.partial(_inproj1_kernel, rope=rope_tables is not None),
        grid=(t // tm,),
        in_specs=in_specs,
        out_specs=[pl.BlockSpec((tm, n), lambda i: (i, 0)) for n in widths],
        out_shape=[jax.ShapeDtypeStruct((t, n), dt) for n, dt in zip(widths, dtypes)],
        compiler_params=_params(dimension_semantics=("arbitrary",)),
        name="l1_in_proj",
    )(*args)


def _lane_lo(shape):
    return lax.broadcasted_iota(jnp.int32, shape, 1) < NA_HEAD_DIM


def _head_lanes(x, half):
    lo = _lane_lo(x.shape)
    return jnp.where(lo if half == 0 else jnp.logical_not(lo), x, jnp.zeros_like(x))


def _softmax_pv(q2, keys, vals, biases):
    acc = None
    for half in (0, 1):
        qm = _head_lanes(q2, half)
        scores = []
        for kk, bb in zip(keys, biases):
            s = _mm_nt(qm, kk)
            if bb is not None:
                s = bb(half, s)
            scores.append(s)
        mx = functools.reduce(jnp.maximum, [jnp.max(s, axis=-1, keepdims=True) for s in scores])
        es = [jnp.exp(s - mx) for s in scores]
        den = functools.reduce(jnp.add, [jnp.sum(e, axis=-1, keepdims=True) for e in es])
        pv = functools.reduce(jnp.add, [_mm(e, _head_lanes(vv, half)) for e, vv in zip(es, vals)])
        out = pv * (1.0 / den)
        acc = out if acc is None else acc + out
    return acc


def _na_ctx_kernel(q_ref, k_ref, v_ref, o_ref, *, seq):
    scale = NA_HEAD_DIM ** -0.5
    nseq = q_ref.shape[0] // seq
    for s in range(nseq):
        r = slice(s * seq, (s + 1) * seq)
        for hp in range(NA_WIDTH // LANES):
            c = slice(hp * LANES, (hp + 1) * LANES)
            out = _softmax_pv(q_ref[r, c] * scale, [k_ref[r, c]], [v_ref[r, c]], [None])
            o_ref[r, c] = out.astype(o_ref.dtype)


def _na_ctx(q, k, v, seq):
    t, w = q.shape
    spec = pl.BlockSpec((SEQ_BLOCK, w), lambda i: (i, 0))
    return pl.pallas_call(
        functools.partial(_na_ctx_kernel, seq=seq),
        grid=(t // SEQ_BLOCK,),
        in_specs=[spec, spec, spec],
        out_specs=spec,
        out_shape=jax.ShapeDtypeStruct((t, w), BF16),
        compiler_params=_params(dimension_semantics=("arbitrary",)),
        name="na_context",
    )(q, k, v)


NA_QBLK = 256


def _na_lat_kernel(q_ref, k_ref, v_ref, kc_ref, vc_ref, tab_ref, o_ref):
    scale = NA_HEAD_DIM ** -0.5
    seq = q_ref.shape[0]
    nblk = seq // NA_QBLK
    rows_per_blk = NA_QBLK // GRID_W
    grid_rows = seq // GRID_W
    n_off = 2 * NA_WIN_ROWS

    qc = lax.broadcasted_iota(jnp.int32, (GRID_W, LANES), 0)
    lane = lax.broadcasted_iota(jnp.int32, (GRID_W, LANES), 1)
    kc = lane & (GRID_W - 1)
    c0 = jnp.clip(qc - NA_WIN_COLS // 2, 0, GRID_W - NA_WIN_COLS)
    col_ok = (kc >= c0) & (kc < c0 + NA_WIN_COLS)
    first = lane < GRID_W
    pair_bias = [[jnp.where(col_ok,
                            pltpu.roll(jnp.broadcast_to(tab_ref[half, e:e + 1, :], (GRID_W, LANES)), 0, 1,
                                       stride=1, stride_axis=0),
                            _NEG_INF)
                  for e in range(n_off)] for half in (0, 1)]
    neg = jnp.full((GRID_W, LANES), _NEG_INF, F32)

    def band_start(r):
        return min(max(r - NA_WIN_ROWS // 2, 0), grid_rows - NA_WIN_ROWS)

    for i in range(nblk):
        ws_row = band_start(i * rows_per_blk) // 2 * 2
        we_row = min((band_start((i + 1) * rows_per_blk - 1) + NA_WIN_ROWS + 1) // 2 * 2, grid_rows)
        ws, kwin = ws_row * GRID_W, (we_row - ws_row) * GRID_W

        def add_bias(half, s, i=i, ws_row=ws_row, kwin=kwin):
            out_rows = []
            for rq in range(rows_per_blk):
                r = i * rows_per_blk + rq
                r0 = band_start(r)
                tiles = []
                for kp in range(kwin // LANES):
                    kr = ws_row + 2 * kp
                    ok0 = r0 <= kr < r0 + NA_WIN_ROWS
                    ok1 = r0 <= kr + 1 < r0 + NA_WIN_ROWS
                    if ok0 or ok1:
                        tile = pair_bias[half][kr - r + NA_WIN_ROWS]
                        if not ok1:
                            tile = jnp.where(first, tile, _NEG_INF)
                        elif not ok0:
                            tile = jnp.where(first, _NEG_INF, tile)
                    else:
                        tile = neg
                    tiles.append(s[rq * GRID_W:(rq + 1) * GRID_W, kp * LANES:(kp + 1) * LANES] + tile)
                out_rows.append(jnp.concatenate(tiles, axis=1))
            return jnp.concatenate(out_rows, axis=0)

        rows = slice(i * NA_QBLK, (i + 1) * NA_QBLK)
        kw = k_ref[ws:ws + kwin, :]
        vw = v_ref[ws:ws + kwin, :]
        out = _softmax_pv(q_ref[rows, :] * scale, [kw, kc_ref[...]], [vw, vc_ref[...]], [add_bias, None])
        o_ref[rows, :] = out.astype(o_ref.dtype)


def _na_bias_table(rel_bias):
    h, n_dr, n_dc = rel_bias.shape
    half_dc = n_dc // 2
    tz = jnp.pad(rel_bias.astype(F32), ((0, 0), (1, 1), (0, 0)))
    lo, hi = tz[:, 0:n_dr + 1], tz[:, 1:n_dr + 2]
    gap = jnp.zeros((h, n_dr + 1, GRID_W - half_dc - 1 - half_dc), F32)
    return jnp.concatenate([lo[..., half_dc:], gap, hi, gap, lo[..., :half_dc]], axis=-1)


def _na_lat(q, k, v, k_ctx, v_ctx, bias_table, seq):
    t, w = q.shape
    b = t // seq
    npair = w // LANES
    past = k_ctx.shape[0] // b
    kspec = pl.BlockSpec((seq, LANES), lambda hp, bb: (bb, hp))
    cspec = pl.BlockSpec((past, LANES), lambda hp, bb: (bb, hp))
    tspec = pl.BlockSpec((2,) + bias_table.shape[1:], lambda hp, bb: (hp, 0, 0))
    return pl.pallas_call(
        _na_lat_kernel,
        grid=(npair, b),
        in_specs=[kspec, kspec, kspec, cspec, cspec, tspec],
        out_specs=kspec,
        out_shape=jax.ShapeDtypeStruct((t, w), BF16),
        compiler_params=_params(dimension_semantics=("arbitrary", "arbitrary")),
        name="na_latent",
    )(q, k, v, k_ctx, v_ctx, bias_table)


def _pack3(v, lane):
    vm = jnp.where(lane < 32, v, 0.0)
    hi = vm.astype(BF16).astype(F32)
    r1 = vm - hi
    mid = r1.astype(BF16).astype(F32)
    lo = r1 - mid
    return (hi + pltpu.roll(mid, 32, 1) + pltpu.roll(lo, 64, 1)).astype(BF16)


def _unpack3(res):
    return res + pltpu.roll(res, 96, 1) + pltpu.roll(res, 64, 1)


def _ssd_selectors():
    r = np.arange(LANES)
    out = []
    for width in (SSD_HEAD_DIM, SSD_CHUNK):
        l = np.arange(SSD_HEADS * width)
        for d in (0, 1):
            sel = (r[:, None] < 96) & ((r[:, None] % 32) == d * SSD_HEADS + l[None, :] // width)
            out.append(jnp.asarray(sel, BF16))
    return out


def _ssd_kernel(*refs, seq, has_state):
    if has_state:
        (z_ref, xs_ref, bc_ref, dtr_ref, dtb_ref, alog_ref, dsk_ref, nw_ref,
         s64f_ref, s64b_ref, s128f_ref, s128b_ref, s0_ref,
         y_ref, yacc, csp, ep, wp, cst, dtt, st) = refs
        sfin_ref = None
    else:
        (z_ref, xs_ref, bc_ref, dtr_ref, dtb_ref, alog_ref, dsk_ref, nw_ref,
         s64f_ref, s64b_ref, s128f_ref, s128b_ref,
         y_ref, sfin_ref, yacc, csp, ep, wp, cst, dtt, st) = refs
        s0_ref = None
    ch = SSD_CHUNK
    nseq = z_ref.shape[0] // seq
    nch = seq // ch
    assert nch % 2 == 0
    lane = lax.broadcasted_iota(jnp.int32, (ch, LANES), 1)
    ri = lax.broadcasted_iota(jnp.int32, (ch, ch), 0)
    ci = lax.broadcasted_iota(jnp.int32, (ch, ch), 1)
    keeps = (ci <= ri, ci >= ri)
    tril = jnp.where(keeps[0], 1.0, 0.0).astype(BF16)
    triu = jnp.where(keeps[1], 1.0, 0.0).astype(BF16)
    lane_lo = lane < SSD_HEAD_DIM
    fwd_lane = lane < SSD_HEADS
    neg_a = -jnp.exp(alog_ref[...])
    gw = SSD_D_INNER // SSD_GROUPS
    sel64s = (s64f_ref, s64b_ref)
    sel128s = (s128f_ref, s128b_ref)
    edges = (ch - 1, 0)

    def seq_body(s, _):
        base = pl.multiple_of(s * seq, seq)

        def decay_body(c, _):
            r0 = pl.multiple_of(c * ch, ch)
            yacc[pl.ds(r0, ch), :] = dsk_ref[...] * xs_ref[pl.ds(base + r0, ch), :]
            dt = _softplus(dtr_ref[pl.ds(base + r0, ch), :] + dtb_ref[...])
            la = _pack3(dt * neg_a, lane)
            cs = jnp.where(fwd_lane,
                           _unpack3(jnp.dot(tril, la, preferred_element_type=F32)),
                           _unpack3(jnp.dot(triu, la, preferred_element_type=F32)))
            tot = jnp.where(fwd_lane, cs[ch - 1:ch, :], cs[0:1, :])
            rows = pl.ds(r0, ch)
            csp[rows, :] = _pack3(cs, lane)
            ep[rows, :] = _pack3(jnp.exp(cs), lane)
            wp[rows, :] = _pack3(dt * jnp.exp(tot - cs), lane)
            cst[c] = cs.T
            dtt[c] = dt.T
            return 0

        lax.fori_loop(0, nch, decay_body, 0, unroll=2)

        for d in (0, 1):
            if has_state:
                st[d] = s0_ref[0, d]
            else:
                st[d] = jnp.zeros((SSD_STATE, SSD_D_INNER), F32)

        def chunk(d, c):
            r0 = pl.multiple_of(c * ch, ch)
            rows = pl.ds(r0, ch)
            col = jnp.dot(csp[rows, :], sel128s[d][...], preferred_element_type=F32)
            e64 = jnp.dot(ep[rows, :], sel64s[d][...], preferred_element_type=F32)
            w64 = jnp.dot(wp[rows, :], sel64s[d][...], preferred_element_type=F32)
            cs_t = cst[c]
            dt_t = dtt[c]
            in_rows = pl.ds(base + r0, ch)
            x_c = xs_ref[in_rows, :]
            state = st[d]
            state_b = state.astype(BF16)
            xw = (x_c * w64).astype(BF16)
            y_intra, y_state, upd = [], [], []
            for g in range(SSD_GROUPS):
                b_g = bc_ref[in_rows, g * SSD_STATE:(g + 1) * SSD_STATE]
                cofs = SSD_GROUPS * SSD_STATE
                c_g = bc_ref[in_rows, cofs + g * SSD_STATE:cofs + (g + 1) * SSD_STATE]
                gram = _mm_nt(c_g, b_g)
                for pp in range(gw // LANES):
                    p = g * (gw // LANES) + pp
                    ws = []
                    for half in (0, 1):
                        h = 2 * p + half
                        hd = d * SSD_HEADS + h
                        diff = col[:, h * ch:(h + 1) * ch] - cs_t[hd:hd + 1, :]
                        wm = jnp.exp(jnp.where(keeps[d], diff, _NEG_INF)) * gram * dt_t[hd:hd + 1, :]
                        ws.append(wm.astype(BF16))
                    xp = x_c[:, p * LANES:(p + 1) * LANES]
                    xcat = jnp.concatenate([jnp.where(lane_lo, xp, 0.0), jnp.where(lane_lo, 0.0, xp)],
                                           axis=0).astype(BF16)
                    y_intra.append(jnp.dot(jnp.concatenate(ws, axis=1), xcat, preferred_element_type=F32))
                y_state.append(jnp.dot(c_g, state_b[:, g * gw:(g + 1) * gw], preferred_element_type=F32))
                upd.append(jnp.dot(b_g.T, xw[:, g * gw:(g + 1) * gw], preferred_element_type=F32))
            yacc[rows, :] = (yacc[rows, :] + jnp.concatenate(y_intra, axis=1)
                             + jnp.concatenate(y_state, axis=1) * e64)
            st[d] = e64[edges[d]:edges[d] + 1, :] * state + jnp.concatenate(upd, axis=1)

        def pair_body(j, _):
            for step in (0, 1):
                chunk(0, 2 * j + step)
                chunk(1, nch - 1 - 2 * j - step)
            return 0

        lax.fori_loop(0, nch // 2, pair_body, 0)
        if sfin_ref is not None:
            for d in (0, 1):
                sfin_ref[s, d] = st[d]

        def out_body(c, _):
            r0 = pl.multiple_of(c * ch, ch)
            y = yacc[pl.ds(r0, ch), :] * _silu(z_ref[pl.ds(base + r0, ch), :])
            outs = []
            for g in range(SSD_GROUPS):
                yg = y[:, g * gw:(g + 1) * gw]
                outs.append(yg * lax.rsqrt(jnp.mean(yg * yg, axis=-1, keepdims=True) + EPS))
            y_ref[pl.ds(base + r0, ch), :] = (jnp.concatenate(outs, axis=1) * nw_ref[...]).astype(y_ref.dtype)
            return 0

        lax.fori_loop(0, nch, out_body, 0)
        return 0

    lax.fori_loop(0, nseq, seq_body, 0)


def _ssd(z, xs, bc, dtr, a_log, dt_bias, d_skip, norm_w, seq, s0=None):
    t = z.shape[0]
    nseq = SEQ_BLOCK // seq
    nblk = t // SEQ_BLOCK
    nch = seq // SSD_CHUNK
    pad32 = lambda a: jnp.pad(a.astype(F32).reshape(1, -1), ((0, 0), (0, LANES - 2 * SSD_HEADS)))
    consts = [pad32(dt_bias), pad32(a_log),
              jnp.repeat(d_skip.astype(F32), SSD_HEAD_DIM).reshape(1, -1),
              norm_w.astype(F32).reshape(1, -1)] + _ssd_selectors()
    row = lambda n: pl.BlockSpec((SEQ_BLOCK, n), lambda i: (i, 0))
    in_specs = ([row(SSD_D_INNER), row(SSD_D_INNER), row(bc.shape[1]), row(LANES)]
                + [_const_spec(c.shape) for c in consts])
    args = [z, xs, bc, dtr] + consts
    out_specs = [row(SSD_D_INNER)]
    out_shape = [jax.ShapeDtypeStruct((t, SSD_D_INNER), BF16)]
    state_shape = (2, SSD_STATE, SSD_D_INNER)
    if s0 is not None:
        in_specs.append(pl.BlockSpec((1,) + state_shape, lambda i: (i, 0, 0, 0)))
        args.append(s0)
    else:
        out_specs.append(pl.BlockSpec((nseq,) + state_shape, lambda i: (i, 0, 0, 0)))
        out_shape.append(jax.ShapeDtypeStruct((t // seq,) + state_shape, F32))
    scratch = [pltpu.VMEM((seq, SSD_D_INNER), F32),
               pltpu.VMEM((seq, LANES), BF16), pltpu.VMEM((seq, LANES), BF16), pltpu.VMEM((seq, LANES), BF16),
               pltpu.VMEM((nch, LANES, SSD_CHUNK), F32), pltpu.VMEM((nch, LANES, SSD_CHUNK), F32),
               pltpu.VMEM(state_shape, F32)]
    return pl.pallas_call(
        functools.partial(_ssd_kernel, seq=seq, has_state=s0 is not None),
        grid=(nblk,),
        in_specs=in_specs,
        out_specs=out_specs,
        out_shape=out_shape,
        scratch_shapes=scratch,
        compiler_params=_params(dimension_semantics=("arbitrary",)),
        name="ssd_mixer",
    )(*args)


def _log_sigmoid(x):
    return -_softplus(-x)


def _ret_kernel(*refs, seq, has_state):
    if has_state:
        q_ref, k_ref, v_ref, g_ref, dec_ref, nw_ref, s0_ref, y_ref, yacc, st = refs
        sfin_ref = None
    else:
        q_ref, k_ref, v_ref, g_ref, dec_ref, nw_ref, y_ref, sfin_ref, yacc, st = refs
        s0_ref = None
    ch = RET_CHUNK
    nseq = q_ref.shape[0] // seq
    nch = seq // ch
    gf = _log_sigmoid(dec_ref[0, 0:1, :])
    gb = _log_sigmoid(dec_ref[0, 1:2, :])
    ri = lax.broadcasted_iota(jnp.int32, (ch, ch), 0)
    ci = lax.broadcasted_iota(jnp.int32, (ch, ch), 1)
    dist = (ri - ci).astype(F32)
    gfk, gbk = gf[:, :ch], gb[:, :ch]
    decay = (jnp.where(ci <= ri, jnp.exp(jnp.where(ci <= ri, dist, 0.0) * gfk), 0.0)
             + jnp.where(ci >= ri, jnp.exp(jnp.where(ci >= ri, -dist, 0.0) * gbk), 0.0))
    pos = lax.broadcasted_iota(jnp.int32, (ch, RET_QK_DIM), 0).astype(F32)
    gfq, gbq = gf[:, :RET_QK_DIM], gb[:, :RET_QK_DIM]
    e_f = jnp.exp((pos + 1.0) * gfq)
    e_b = jnp.exp((ch - pos) * gbq)
    tail_f = jnp.exp((ch - 1.0 - pos) * gfq)
    tail_b = jnp.exp(pos * gbq)
    dec_f = jnp.exp(ch * gf)
    dec_b = jnp.exp(ch * gb)

    for s in range(nseq):
        for d in (0, 1):
            if has_state:
                st[d] = s0_ref[0, d, 0]
            else:
                st[d] = jnp.zeros((RET_QK_DIM, RET_V_DIM), F32)
        for c in range(nch):
            rows = slice(s * seq + c * ch, s * seq + (c + 1) * ch)
            q, k, v = q_ref[rows, :], k_ref[rows, :].astype(F32), v_ref[rows, :]
            y = _mm(_mm_nt(q, k) * decay, v)
            if has_state or c > 0:
                y = y + _mm(q.astype(F32) * e_f, st[0])
            yacc[rows, :] = y
            st[0] = dec_f * st[0] + _mm((k * tail_f).T, v)
        for c in reversed(range(nch)):
            rows = slice(s * seq + c * ch, s * seq + (c + 1) * ch)
            q, k, v = q_ref[rows, :], k_ref[rows, :].astype(F32), v_ref[rows, :]
            if has_state or c < nch - 1:
                yacc[rows, :] = yacc[rows, :] + _mm(q.astype(F32) * e_b, st[1])
            st[1] = dec_b * st[1] + _mm((k * tail_b).T, v)
        if sfin_ref is not None:
            for d in (0, 1):
                sfin_ref[s, d, 0] = st[d]
        for c in range(nch):
            rows = slice(s * seq + c * ch, s * seq + (c + 1) * ch)
            y = yacc[rows, :]
            y = y * lax.rsqrt(jnp.mean(y * y, axis=-1, keepdims=True) + EPS)
            y_ref[rows, :] = (y * nw_ref[...] * _silu(g_ref[rows, :].astype(F32))).astype(y_ref.dtype)


def _retention(q, k, v, g, ret_decay, norm_w, seq, s0=None):
    t = q.shape[0]
    nseq = SEQ_BLOCK // seq
    nblk = t // SEQ_BLOCK
    dec = jnp.broadcast_to(jnp.pad(ret_decay.astype(F32).T, ((0, 0), (0, SUBLANES - 2)))[:, :, None],
                           (RET_HEADS, SUBLANES, RET_V_DIM))
    qspec = pl.BlockSpec((SEQ_BLOCK, RET_QK_DIM), lambda i, h: (i, h))
    vspec = pl.BlockSpec((SEQ_BLOCK, RET_V_DIM), lambda i, h: (i, h))
    in_specs = [qspec, qspec, vspec, vspec,
                pl.BlockSpec((1, SUBLANES, RET_V_DIM), lambda i, h: (h, 0, 0)),
                pl.BlockSpec((1, RET_V_DIM), lambda i, h: (0, h))]
    args = [q, k, v, g, dec, norm_w.astype(F32).reshape(1, -1)]
    out_specs = [vspec]
    out_shape = [jax.ShapeDtypeStruct((t, RET_V_W), BF16)]
    if s0 is not None:
        in_specs.append(pl.BlockSpec((1, 2, 1, RET_QK_DIM, RET_V_DIM), lambda i, h: (i, 0, h, 0, 0)))
        args.append(s0)
    else:
        out_specs.append(pl.BlockSpec((nseq, 2, 1, RET_QK_DIM, RET_V_DIM), lambda i, h: (i, 0, h, 0, 0)))
        out_shape.append(jax.ShapeDtypeStruct((t // seq, 2, RET_HEADS, RET_QK_DIM, RET_V_DIM), F32))
    return pl.pallas_call(
        functools.partial(_ret_kernel, seq=seq, has_state=s0 is not None),
        grid=(nblk, RET_HEADS),
        in_specs=in_specs,
        out_specs=out_specs,
        out_shape=out_shape,
        scratch_shapes=[pltpu.VMEM((SEQ_BLOCK, RET_V_DIM), F32),
                        pltpu.VMEM((2, RET_QK_DIM, RET_V_DIM), F32)],
        compiler_params=_params(dimension_semantics=("arbitrary", "arbitrary")),
        name="retention_mixer",
    )(*args)


def _out_ffn_kernel(*refs, n_mix, final_norm):
    x_ref = refs[0]
    mix_refs = refs[1:1 + n_mix]
    wo_ref, mod_ref, n2_ref, w1_ref, w3_ref, w2_ref = refs[1 + n_mix:7 + n_mix]
    if final_norm:
        fn_ref, o_ref, act = refs[7 + n_mix:]
    else:
        o_ref, act = refs[7 + n_mix:]
    mixed, row = None, 0
    for m_ref in mix_refs:
        k = m_ref.shape[1]
        term = jnp.dot(m_ref[...], wo_ref[row:row + k, :], preferred_element_type=F32)
        mixed = term if mixed is None else mixed + term
        row += k
    x1 = x_ref[...] + mod_ref[0, 2:3, :] * mixed
    hb = _norm_mod(x1, n2_ref[...], mod_ref, 3, 4).astype(BF16)
    for c in range(w1_ref.shape[1] // FFN_CHUNK):
        cols = slice(c * FFN_CHUNK, (c + 1) * FFN_CHUNK)
        h1 = jnp.dot(hb, w1_ref[:, cols], preferred_element_type=F32)
        h3 = jnp.dot(hb, w3_ref[:, cols], preferred_element_type=F32)
        act[:, cols] = (_silu(h1) * h3).astype(BF16)
    x2 = x1 + mod_ref[0, 5:6, :] * jnp.dot(act[...], w2_ref[...], preferred_element_type=F32)
    if final_norm:
        ms = jnp.mean(x2 * x2, axis=-1, keepdims=True)
        x2 = x2 * lax.rsqrt(ms + EPS) * fn_ref[...]
    o_ref[...] = x2


def _out_ffn(x, mixes, wo, mod, norm2_w, w1, w3, w2, rows_per_mod, final_norm_w=None):
    t, d = x.shape
    tm = ROW_TILE
    per = rows_per_mod // tm
    single = dict(pipeline_mode=pl.Buffered(1))
    in_specs = [pl.BlockSpec((tm, d), lambda i: (i, 0))]
    in_specs += [pl.BlockSpec((tm, m.shape[1]), lambda i: (i, 0)) for m in mixes]
    in_specs += [_const_spec(wo.shape, **single),
                 pl.BlockSpec((1, 6, d), lambda i: (i // per, 0, 0)), _const_spec((1, d))]
    in_specs += [_const_spec(w.shape, **single) for w in (w1, w3, w2)]
    args = [x, *mixes, wo, mod, norm2_w.reshape(1, d), w1, w3, w2]
    if final_norm_w is not None:
        in_specs.append(_const_spec((1, d)))
        args.append(final_norm_w.reshape(1, d))
    return pl.pallas_call(
        functools.partial(_out_ffn_kernel, n_mix=len(mixes), final_norm=final_norm_w is not None),
        grid=(t // tm,),
        in_specs=in_specs,
        out_specs=pl.BlockSpec((tm, d), lambda i: (i, 0)),
        out_shape=jax.ShapeDtypeStruct((t, d), F32),
        scratch_shapes=[pltpu.VMEM((tm, w1.shape[1]), BF16)],
        compiler_params=_params(dimension_semantics=("arbitrary",)),
        name="out_proj_ffn",
    )(*args)


def _rope_tables(seq):
    half = RET_QK_DIM // 2
    t = jnp.arange(seq)
    row = (t // GRID_W).astype(F32)
    col = (t % GRID_W).astype(F32)
    freqs = ROPE_BASE ** (-jnp.arange(0, half, 2, dtype=F32) / half)
    ang = jnp.concatenate([row[:, None] * freqs, col[:, None] * freqs], axis=-1)
    cos = jnp.repeat(jnp.cos(ang), 2, axis=1)
    sin = jnp.stack([-jnp.sin(ang), jnp.sin(ang)], axis=-1).reshape(seq, RET_QK_DIM)
    return cos, sin


def kernel(x_prompt, x_sample, cache_l0_na_k, cache_l0_na_v, state_l0_ssd, state_l1_ret, c, c_ctx,
           l0_norm1_w, l0_norm2_w, l0_mod_w, l0_mod_b, l0_w_in, l0_w_out, l0_na_bias, l0_conv_w, l0_conv_b,
           l0_ssd_a_log, l0_ssd_dt_bias, l0_ssd_d, l0_ssd_norm_w, l0_ffn_w1, l0_ffn_w3, l0_ffn_w2,
           l1_norm1_w, l1_norm2_w, l1_mod_w, l1_mod_b, l1_w_in, l1_w_out, l1_ret_decay, l1_ret_norm_w,
           l1_ffn_w1, l1_ffn_w3, l1_ffn_w2, final_norm_w):
    bc, lc, d = x_prompt.shape
    bl, ll, _ = x_sample.shape
    assert d == D_MODEL and ll == SEQ_BLOCK and SEQ_BLOCK % lc == 0 and bc % (SEQ_BLOCK // lc) == 0
    tc, tl = bc * lc, bl * ll
    xc = x_prompt.reshape(tc, d)
    xl = x_sample.reshape(tl, d)

    nrow = SUBLANES * ((1 + bl + SUBLANES - 1) // SUBLANES)
    cond = jnp.concatenate([c_ctx[None], c, jnp.zeros((nrow - 1 - bl, d), F32)], axis=0)
    mods = []
    for mod_w, mod_b in ((l0_mod_w, l0_mod_b), (l1_mod_w, l1_mod_b)):
        m = _adaln(cond, mod_w, mod_b).reshape(nrow, 6, d)
        mods.append((m[0:1], m[1:1 + bl]))

    w_in0 = jnp.pad(l0_w_in.astype(BF16), ((0, 0), (0, LANES - 2 * SSD_HEADS)))
    (wo0, ffn0_w1, ffn0_w3, ffn0_w2, w_in1, wo1, ffn1_w1, ffn1_w3, ffn1_w2) = _to_bf16(
        l0_w_out, l0_ffn_w1, l0_ffn_w3, l0_ffn_w2, l1_w_in, l1_w_out, l1_ffn_w1, l1_ffn_w3, l1_ffn_w2)
    ffn0 = (ffn0_w1, ffn0_w3, ffn0_w2)
    ffn1 = (ffn1_w1, ffn1_w3, ffn1_w2)
    ssd_p = (l0_ssd_a_log, l0_ssd_dt_bias, l0_ssd_d, l0_ssd_norm_w)

    qc, kc, vc, zc, xsc, bcc, dtc = _inproj0(xc, l0_norm1_w, mods[0][0], w_in0, l0_conv_w, l0_conv_b, lc, F32)
    att_c = _na_ctx(qc, kc, vc, lc)
    ssd_c, sfin_c = _ssd(zc, xsc, bcc, dtc, *ssd_p, seq=lc)
    xc = _out_ffn(xc, (att_c, ssd_c), wo0, mods[0][0], l0_norm2_w, *ffn0, tc)

    ql, kl, vl, zl, xsl, bcl, dtl = _inproj0(xl, l0_norm1_w, mods[0][1], w_in0, l0_conv_w, l0_conv_b, ll, BF16)
    past = cache_l0_na_k.shape[1]
    att_l = _na_lat(ql, kl, vl, cache_l0_na_k.reshape(bl * past, NA_WIDTH),
                    cache_l0_na_v.reshape(bl * past, NA_WIDTH), _na_bias_table(l0_na_bias), ll)
    s0_ssd = state_l0_ssd.transpose(0, 1, 3, 2, 4).reshape(bl, 2, SSD_STATE, SSD_D_INNER)
    (ssd_l,) = _ssd(zl, xsl, bcl, dtl, *ssd_p, seq=ll, s0=s0_ssd)
    xl = _out_ffn(xl, (att_l, ssd_l), wo0, mods[0][1], l0_norm2_w, *ffn0, ll)

    q1, k1, v1, g1 = _inproj1(xc, l1_norm1_w, mods[1][0], w_in1, tc)
    ret_c, ret_state = _retention(q1, k1, v1, g1, l1_ret_decay, l1_ret_norm_w, lc)
    y_prompt = _out_ffn(xc, (ret_c,), wo1, mods[1][0], l1_norm2_w, *ffn1, tc, final_norm_w=final_norm_w)

    q1, k1, v1, g1 = _inproj1(xl, l1_norm1_w, mods[1][1], w_in1, ll, rope_tables=_rope_tables(ll))
    (ret_l,) = _retention(q1, k1, v1, g1, l1_ret_decay, l1_ret_norm_w, ll, s0=state_l1_ret)
    y_sample = _out_ffn(xl, (ret_l,), wo1, mods[1][1], l1_norm2_w, *ffn1, ll, final_norm_w=final_norm_w)

    new_ssd = sfin_c.reshape(bc, 2, SSD_STATE, SSD_HEADS, SSD_HEAD_DIM).transpose(0, 1, 3, 2, 4)
    return (y_prompt.reshape(bc, lc, d), y_sample.reshape(bl, ll, d),
            kc.reshape(bc, lc, NA_HEADS, NA_HEAD_DIM), vc.reshape(bc, lc, NA_HEADS, NA_HEAD_DIM),
            new_ssd, ret_state)
```

```python
import functools

import numpy as np
import jax
import jax.numpy as jnp
from jax import lax
from jax.experimental import pallas as pl
from jax.experimental.pallas import tpu as pltpu

F32 = jnp.float32
BF16 = jnp.bfloat16

D_MODEL = 1024
GRID_W = 64
NA_HEADS = 8
NA_HEAD_DIM = 64
NA_WIDTH = NA_HEADS * NA_HEAD_DIM
NA_WIN_ROWS = 8
NA_WIN_COLS = 16
SSD_HEADS = 16
SSD_HEAD_DIM = 64
SSD_D_INNER = SSD_HEADS * SSD_HEAD_DIM
SSD_GROUPS = 2
SSD_STATE = 128
SSD_CONV = 5
SSD_XBC = SSD_D_INNER + 2 * SSD_GROUPS * SSD_STATE
RET_HEADS = 4
RET_QK_DIM = 256
RET_V_DIM = 512
RET_QK_W = RET_HEADS * RET_QK_DIM
RET_V_W = RET_HEADS * RET_V_DIM
ROPE_BASE = 10000.0
EPS = 1e-6

LANES = 128
SUBLANES = 8
SEQ_BLOCK = 1024
ROW_TILE = 512
SSD_CHUNK = 128
CONV_ROWS = 256
RET_CHUNK = 256
FFN_CHUNK = 256
VMEM_LIMIT = 56 * 1024 * 1024

_NEG_INF = float("-inf")


def _params(**kw):
    return pltpu.CompilerParams(vmem_limit_bytes=VMEM_LIMIT, **kw)


def _silu(x):
    return x * (1.0 / (1.0 + jnp.exp(-x)))


def _softplus(x):
    return jnp.maximum(x, 0.0) + jnp.log(1.0 + jnp.exp(-jnp.abs(x)))


def _mm(a, b):
    return jnp.dot(a.astype(BF16), b.astype(BF16), preferred_element_type=F32)


def _mm_nt(a, b):
    return lax.dot_general(a.astype(BF16), b.astype(BF16), (((1,), (1,)), ((), ())),
                           preferred_element_type=F32)


def _const_spec(shape, **kw):
    nd = len(shape)
    return pl.BlockSpec(shape, lambda *_: (0,) * nd, **kw)


def _cast_kernel(*refs):
    n = len(refs) // 2
    for i_ref, o_ref in zip(refs[:n], refs[n:]):
        o_ref[...] = i_ref[...].astype(o_ref.dtype)


def _to_bf16(*ws):
    steps = 8
    specs = [pl.BlockSpec((w.shape[0] // steps, w.shape[1]), lambda i: (i, 0)) for w in ws]
    assert all(w.shape[0] % (steps * 2 * SUBLANES) == 0 for w in ws)
    return pl.pallas_call(
        _cast_kernel,
        grid=(steps,),
        in_specs=specs,
        out_specs=specs,
        out_shape=[jax.ShapeDtypeStruct(w.shape, BF16) for w in ws],
        compiler_params=_params(dimension_semantics=("arbitrary",)),
        name="weights_to_bf16",
    )(*ws)


def _adaln_kernel(c_ref, w_ref, b_ref, o_ref):
    o_ref[...] = _mm(_silu(c_ref[...]), w_ref[...]) + b_ref[...]


def _adaln(cond, mod_w, mod_b):
    r, d = cond.shape
    n = mod_w.shape[1]
    tn = 1536
    return pl.pallas_call(
        _adaln_kernel,
        grid=(n // tn,),
        in_specs=[_const_spec((r, d)),
                  pl.BlockSpec((d, tn), lambda j: (0, j)),
                  pl.BlockSpec((1, tn), lambda j: (0, j))],
        out_specs=pl.BlockSpec((r, tn), lambda j: (0, j)),
        out_shape=jax.ShapeDtypeStruct((r, n), F32),
        compiler_params=_params(dimension_semantics=("arbitrary",)),
        name="adaln_mod",
    )(cond, mod_w, mod_b.reshape(1, n))


def _norm_mod(x, nw, mod_ref, shift_idx, scale_idx):
    ms = jnp.mean(x * x, axis=-1, keepdims=True)
    h = x * lax.rsqrt(ms + EPS) * nw
    return h * (1.0 + mod_ref[0, scale_idx:scale_idx + 1, :]) + mod_ref[0, shift_idx:shift_idx + 1, :]


def _inproj0_kernel(*refs, seq, halo):
    if halo:
        (x_ref, xp_ref, xn_ref, nw_ref, mod_ref, w_ref, cw_ref, cb_ref,
         q_ref, k_ref, v_ref, z_ref, xs_ref, bc_ref, dt_ref, xpad) = refs
    else:
        (x_ref, nw_ref, mod_ref, w_ref, cw_ref, cb_ref,
         q_ref, k_ref, v_ref, z_ref, xs_ref, bc_ref, dt_ref, xpad) = refs
    tm = x_ref.shape[0]
    pad = SUBLANES
    sub = xpad.shape[1] - 2 * pad
    h = _norm_mod(x_ref[...], nw_ref[...], mod_ref, 0, 1)
    hb = h.astype(BF16)
    xbc_col = 3 * NA_WIDTH + SSD_D_INNER
    w_xbc = w_ref[:, xbc_col:xbc_col + SSD_XBC]
    if halo:
        tiles_per_seq = seq // tm
        p = pl.program_id(0) % tiles_per_seq
        h_prev = jnp.where(p > 0, _norm_mod(xp_ref[...], nw_ref[...], mod_ref, 0, 1), 0.0)
        h_next = jnp.where(p < tiles_per_seq - 1, _norm_mod(xn_ref[...], nw_ref[...], mod_ref, 0, 1), 0.0)
        ext = jnp.concatenate([h_prev, h, h_next], axis=0).astype(BF16)
        xpad[0] = jnp.dot(ext, w_xbc, preferred_element_type=F32)
    else:
        xbc = jnp.dot(hb, w_xbc, preferred_element_type=F32)
        for s in range(tm // sub):
            xpad[s, 0:pad, :] = jnp.zeros((pad, SSD_XBC), F32)
            xpad[s, pad:pad + sub, :] = xbc[s * sub:(s + 1) * sub]
            xpad[s, pad + sub:2 * pad + sub, :] = jnp.zeros((pad, SSD_XBC), F32)

    cr = CONV_ROWS

    def conv_chunk(r0):
        s, rs = divmod(r0, sub)
        nwin = cr + 2 * pad
        for lt in range(SSD_XBC // LANES):
            cols = slice(lt * LANES, (lt + 1) * LANES)
            win = xpad[s, rs:rs + nwin, cols]
            taps = [cw_ref[k:k + 1, cols] * win for k in range(SSD_CONV)]
            before = taps[1] + pltpu.roll(taps[0], 1, 0)
            after = taps[3] + pltpu.roll(taps[4], nwin - 1, 0)
            conv = taps[2] + pltpu.roll(before, 1, 0) + pltpu.roll(after, nwin - 1, 0)
            act = _silu(conv[pad:pad + cr, :] + cb_ref[:, cols])
            if lt < SSD_D_INNER // LANES:
                xs_ref[r0:r0 + cr, cols] = act
            else:
                bc_ref[r0:r0 + cr, lt * LANES - SSD_D_INNER:(lt + 1) * LANES - SSD_D_INNER] = act.astype(BF16)

    chunks = list(range(0, tm, cr))
    col = 0
    for o_ref in (q_ref, k_ref, v_ref, z_ref):
        if chunks:
            conv_chunk(chunks.pop(0))
        n = o_ref.shape[1]
        o_ref[...] = jnp.dot(hb, w_ref[:, col:col + n], preferred_element_type=F32).astype(o_ref.dtype)
        col += n
    col += SSD_XBC
    dt_ref[...] = jnp.dot(hb, w_ref[:, col:col + LANES], preferred_element_type=F32)
    for r0 in chunks:
        conv_chunk(r0)


def _inproj0(x, norm_w, mod, w, conv_w, conv_b, seq, qkv_dtype):
    t, d = x.shape
    tm = ROW_TILE
    assert seq % tm == 0 or tm % seq == 0
    halo = seq > tm
    sub = min(seq, tm)
    widths = (NA_WIDTH, NA_WIDTH, NA_WIDTH, SSD_D_INNER, SSD_D_INNER, SSD_XBC - SSD_D_INNER, LANES)
    dtypes = (qkv_dtype,) * 3 + (F32, F32, BF16, F32)
    assert sum(widths) == w.shape[1]
    per = max(seq // tm, 1) if mod.shape[0] > 1 else t // tm
    in_specs = [pl.BlockSpec((tm, d), lambda i: (i, 0))]
    args = [x]
    if halo:
        rb = tm // SUBLANES
        last = t // SUBLANES - 1
        in_specs += [pl.BlockSpec((SUBLANES, d), lambda i: (jnp.maximum(i * rb - 1, 0), 0)),
                     pl.BlockSpec((SUBLANES, d), lambda i: (jnp.minimum((i + 1) * rb, last), 0))]
        args += [x, x]
    in_specs += [_const_spec((1, d)),
                 pl.BlockSpec((1, 6, d), lambda i: (i // per, 0, 0)),
                 _const_spec(w.shape), _const_spec((SUBLANES, SSD_XBC)), _const_spec((1, SSD_XBC))]
    args += [norm_w.reshape(1, d), mod, w,
             jnp.pad(conv_w.astype(F32), ((0, SUBLANES - SSD_CONV), (0, 0))), conv_b.astype(F32).reshape(1, -1)]
    return pl.pallas_call(
        functools.partial(_inproj0_kernel, seq=seq, halo=halo),
        grid=(t // tm,),
        in_specs=in_specs,
        out_specs=[pl.BlockSpec((tm, n), lambda i: (i, 0)) for n in widths],
        out_shape=[jax.ShapeDtypeStruct((t, n), dt) for n, dt in zip(widths, dtypes)],
        scratch_shapes=[pltpu.VMEM((tm // sub, sub + 2 * SUBLANES, SSD_XBC), F32)],
        compiler_params=_params(dimension_semantics=("arbitrary",)),
        name="l0_in_proj",
    )(*args)


def _swap_lane_pairs(x):
    even = (lax.broadcasted_iota(jnp.int32, x.shape, 1) & 1) == 0
    return jnp.where(even, pltpu.roll(x, LANES - 1, 1), pltpu.roll(x, 1, 1))


def _lane_lo(shape):
    return lax.broadcasted_iota(jnp.int32, shape, 1) < NA_HEAD_DIM


def _head_lanes(x, half):
    lo = _lane_lo(x.shape)
    return jnp.where(lo if half == 0 else jnp.logical_not(lo), x, jnp.zeros_like(x))


def _softmax_pv(q2, keys, vals, biases):
    acc = None
    for half in (0, 1):
        qm = _head_lanes(q2, half)
        scores = []
        for kk, bb in zip(keys, biases):
            s = _mm_nt(qm, kk)
            if bb is not None:
                s = bb(half, s)
            scores.append(s)
        mx = functools.reduce(jnp.maximum, [jnp.max(s, axis=-1, keepdims=True) for s in scores])
        es = [jnp.exp(s - mx) for s in scores]
        den = functools.reduce(jnp.add, [jnp.sum(e, axis=-1, keepdims=True) for e in es])
        pv = functools.reduce(jnp.add, [_mm(e, _head_lanes(vv, half)) for e, vv in zip(es, vals)])
        out = pv * (1.0 / den)
        acc = out if acc is None else acc + out
    return acc


def _na_ctx_kernel(q_ref, k_ref, v_ref, o_ref, *, seq):
    scale = NA_HEAD_DIM ** -0.5
    nseq = q_ref.shape[0] // seq
    for s in range(nseq):
        r = slice(s * seq, (s + 1) * seq)
        for hp in range(NA_WIDTH // LANES):
            c = slice(hp * LANES, (hp + 1) * LANES)
            out = _softmax_pv(q_ref[r, c] * scale, [k_ref[r, c]], [v_ref[r, c]], [None])
            o_ref[r, c] = out.astype(o_ref.dtype)


def _na_ctx(q, k, v, seq):
    t, w = q.shape
    spec = pl.BlockSpec((SEQ_BLOCK, w), lambda i: (i, 0))
    return pl.pallas_call(
        functools.partial(_na_ctx_kernel, seq=seq),
        grid=(t // SEQ_BLOCK,),
        in_specs=[spec, spec, spec],
        out_specs=spec,
        out_shape=jax.ShapeDtypeStruct((t, w), BF16),
        compiler_params=_params(dimension_semantics=("arbitrary",)),
        name="na_context",
    )(q, k, v)


NA_QBLK = 256


def _na_lat_kernel(q_ref, k_ref, v_ref, kc_ref, vc_ref, tab_ref, o_ref):
    scale = NA_HEAD_DIM ** -0.5
    seq = q_ref.shape[0]
    nblk = seq // NA_QBLK
    rows_per_blk = NA_QBLK // GRID_W
    grid_rows = seq // GRID_W
    n_off = 2 * NA_WIN_ROWS

    qc = lax.broadcasted_iota(jnp.int32, (GRID_W, LANES), 0)
    lane = lax.broadcasted_iota(jnp.int32, (GRID_W, LANES), 1)
    kc = lane & (GRID_W - 1)
    c0 = jnp.clip(qc - NA_WIN_COLS // 2, 0, GRID_W - NA_WIN_COLS)
    col_ok = (kc >= c0) & (kc < c0 + NA_WIN_COLS)
    first = lane < GRID_W
    pair_bias = [[jnp.where(col_ok,
                            pltpu.roll(jnp.broadcast_to(tab_ref[half, e:e + 1, :], (GRID_W, LANES)), 0, 1,
                                       stride=1, stride_axis=0),
                            _NEG_INF)
                  for e in range(n_off)] for half in (0, 1)]
    neg = jnp.full((GRID_W, LANES), _NEG_INF, F32)

    def band_start(r):
        return min(max(r - NA_WIN_ROWS // 2, 0), grid_rows - NA_WIN_ROWS)

    for i in range(nblk):
        ws_row = band_start(i * rows_per_blk) // 2 * 2
        we_row = min((band_start((i + 1) * rows_per_blk - 1) + NA_WIN_ROWS + 1) // 2 * 2, grid_rows)
        ws, kwin = ws_row * GRID_W, (we_row - ws_row) * GRID_W

        def add_bias(half, s, i=i, ws_row=ws_row, kwin=kwin):
            out_rows = []
            for rq in range(rows_per_blk):
                r = i * rows_per_blk + rq
                r0 = band_start(r)
                tiles = []
                for kp in range(kwin // LANES):
                    kr = ws_row + 2 * kp
                    ok0 = r0 <= kr < r0 + NA_WIN_ROWS
                    ok1 = r0 <= kr + 1 < r0 + NA_WIN_ROWS
                    if ok0 or ok1:
                        tile = pair_bias[half][kr - r + NA_WIN_ROWS]
                        if not ok1:
                            tile = jnp.where(first, tile, _NEG_INF)
                        elif not ok0:
                            tile = jnp.where(first, _NEG_INF, tile)
                    else:
                        tile = neg
                    tiles.append(s[rq * GRID_W:(rq + 1) * GRID_W, kp * LANES:(kp + 1) * LANES] + tile)
                out_rows.append(jnp.concatenate(tiles, axis=1))
            return jnp.concatenate(out_rows, axis=0)

        rows = slice(i * NA_QBLK, (i + 1) * NA_QBLK)
        kw = k_ref[ws:ws + kwin, :]
        vw = v_ref[ws:ws + kwin, :]
        out = _softmax_pv(q_ref[rows, :] * scale, [kw, kc_ref[...]], [vw, vc_ref[...]], [add_bias, None])
        o_ref[rows, :] = out.astype(o_ref.dtype)


def _na_bias_table(rel_bias):
    h, n_dr, n_dc = rel_bias.shape
    half_dc = n_dc // 2
    tz = jnp.pad(rel_bias.astype(F32), ((0, 0), (1, 1), (0, 0)))
    lo, hi = tz[:, 0:n_dr + 1], tz[:, 1:n_dr + 2]
    gap = jnp.zeros((h, n_dr + 1, GRID_W - half_dc - 1 - half_dc), F32)
    return jnp.concatenate([lo[..., half_dc:], gap, hi, gap, lo[..., :half_dc]], axis=-1)


def _na_lat(q, k, v, k_ctx, v_ctx, bias_table, seq):
    t, w = q.shape
    b = t // seq
    npair = w // LANES
    past = k_ctx.shape[0] // b
    kspec = pl.BlockSpec((seq, LANES), lambda hp, bb: (bb, hp))
    cspec = pl.BlockSpec((past, LANES), lambda hp, bb: (bb, hp))
    tspec = pl.BlockSpec((2,) + bias_table.shape[1:], lambda hp, bb: (hp, 0, 0))
    return pl.pallas_call(
        _na_lat_kernel,
        grid=(npair, b),
        in_specs=[kspec, kspec, kspec, cspec, cspec, tspec],
        out_specs=kspec,
        out_shape=jax.ShapeDtypeStruct((t, w), BF16),
        compiler_params=_params(dimension_semantics=("arbitrary", "arbitrary")),
        name="na_latent",
    )(q, k, v, k_ctx, v_ctx, bias_table)


def _pack3(v, lane):
    vm = jnp.where(lane < 32, v, 0.0)
    hi = vm.astype(BF16).astype(F32)
    r1 = vm - hi
    mid = r1.astype(BF16).astype(F32)
    lo = r1 - mid
    return (hi + pltpu.roll(mid, 32, 1) + pltpu.roll(lo, 64, 1)).astype(BF16)


def _unpack3(res):
    return res + pltpu.roll(res, 96, 1) + pltpu.roll(res, 64, 1)


def _ssd_selectors():
    r = np.arange(LANES)
    out = []
    for width in (SSD_HEAD_DIM, SSD_CHUNK):
        l = np.arange(SSD_HEADS * width)
        for d in (0, 1):
            sel = (r[:, None] < 96) & ((r[:, None] % 32) == d * SSD_HEADS + l[None, :] // width)
            out.append(jnp.asarray(sel, BF16))
    return out


def _ssd_kernel(*refs, seq, has_state):
    if has_state:
        (z_ref, xs_ref, bc_ref, dtr_ref, dtb_ref, alog_ref, dsk_ref, nw_ref,
         s64f_ref, s64b_ref, s128f_ref, s128b_ref, s0_ref,
         y_ref, yacc, csp, ep, wp, cst, dtt, st) = refs
        sfin_ref = None
    else:
        (z_ref, xs_ref, bc_ref, dtr_ref, dtb_ref, alog_ref, dsk_ref, nw_ref,
         s64f_ref, s64b_ref, s128f_ref, s128b_ref,
         y_ref, sfin_ref, yacc, csp, ep, wp, cst, dtt, st) = refs
        s0_ref = None
    ch = SSD_CHUNK
    nseq = z_ref.shape[0] // seq
    nch = seq // ch
    assert nch % 2 == 0
    lane = lax.broadcasted_iota(jnp.int32, (ch, LANES), 1)
    ri = lax.broadcasted_iota(jnp.int32, (ch, ch), 0)
    ci = lax.broadcasted_iota(jnp.int32, (ch, ch), 1)
    keeps = (ci <= ri, ci >= ri)
    tril = jnp.where(keeps[0], 1.0, 0.0).astype(BF16)
    triu = jnp.where(keeps[1], 1.0, 0.0).astype(BF16)
    lane_lo = lane < SSD_HEAD_DIM
    fwd_lane = lane < SSD_HEADS
    neg_a = -jnp.exp(alog_ref[...])
    gw = SSD_D_INNER // SSD_GROUPS
    sel64s = (s64f_ref, s64b_ref)
    sel128s = (s128f_ref, s128b_ref)
    edges = (ch - 1, 0)

    def seq_body(s, _):
        base = pl.multiple_of(s * seq, seq)

        def decay_body(c, _):
            r0 = pl.multiple_of(c * ch, ch)
            yacc[pl.ds(r0, ch), :] = dsk_ref[...] * xs_ref[pl.ds(base + r0, ch), :]
            dt = _softplus(dtr_ref[pl.ds(base + r0, ch), :] + dtb_ref[...])
            la = _pack3(dt * neg_a, lane)
            cs = jnp.where(fwd_lane,
                           _unpack3(jnp.dot(tril, la, preferred_element_type=F32)),
                           _unpack3(jnp.dot(triu, la, preferred_element_type=F32)))
            tot = jnp.where(fwd_lane, cs[ch - 1:ch, :], cs[0:1, :])
            rows = pl.ds(r0, ch)
            csp[rows, :] = _pack3(cs, lane)
            ep[rows, :] = _pack3(jnp.exp(cs), lane)
            wp[rows, :] = _pack3(dt * jnp.exp(tot - cs), lane)
            cst[c] = cs.T
            dtt[c] = dt.T
            return 0

        lax.fori_loop(0, nch, decay_body, 0, unroll=2)

        for d in (0, 1):
            if has_state:
                st[d] = s0_ref[0, d]
            else:
                st[d] = jnp.zeros((SSD_STATE, SSD_D_INNER), F32)

        def chunk(d, c):
            r0 = pl.multiple_of(c * ch, ch)
            rows = pl.ds(r0, ch)
            col = jnp.dot(csp[rows, :], sel128s[d][...], preferred_element_type=F32)
            e64 = jnp.dot(ep[rows, :], sel64s[d][...], preferred_element_type=F32)
            w64 = jnp.dot(wp[rows, :], sel64s[d][...], preferred_element_type=F32)
            cs_t = cst[c]
            dt_t = dtt[c]
            in_rows = pl.ds(base + r0, ch)
            x_c = xs_ref[in_rows, :]
            state = st[d]
            state_b = state.astype(BF16)
            xw = (x_c * w64).astype(BF16)
            y_intra, y_state, upd = [], [], []
            for g in range(SSD_GROUPS):
                b_g = bc_ref[in_rows, g * SSD_STATE:(g + 1) * SSD_STATE]
                cofs = SSD_GROUPS * SSD_STATE
                c_g = bc_ref[in_rows, cofs + g * SSD_STATE:cofs + (g + 1) * SSD_STATE]
                gram = _mm_nt(c_g, b_g)
                for pp in range(gw // LANES):
                    p = g * (gw // LANES) + pp
                    ws = []
                    for half in (0, 1):
                        h = 2 * p + half
                        hd = d * SSD_HEADS + h
                        diff = col[:, h * ch:(h + 1) * ch] - cs_t[hd:hd + 1, :]
                        wm = jnp.exp(jnp.where(keeps[d], diff, _NEG_INF)) * gram * dt_t[hd:hd + 1, :]
                        ws.append(wm.astype(BF16))
                    xp = x_c[:, p * LANES:(p + 1) * LANES]
                    xcat = jnp.concatenate([jnp.where(lane_lo, xp, 0.0), jnp.where(lane_lo, 0.0, xp)],
                                           axis=0).astype(BF16)
                    y_intra.append(jnp.dot(jnp.concatenate(ws, axis=1), xcat, preferred_element_type=F32))
                y_state.append(jnp.dot(c_g, state_b[:, g * gw:(g + 1) * gw], preferred_element_type=F32))
                upd.append(jnp.dot(b_g.T, xw[:, g * gw:(g + 1) * gw], preferred_element_type=F32))
            yacc[rows, :] = (yacc[rows, :] + jnp.concatenate(y_intra, axis=1)
                             + jnp.concatenate(y_state, axis=1) * e64)
            st[d] = e64[edges[d]:edges[d] + 1, :] * state + jnp.concatenate(upd, axis=1)

        def pair_body(j, _):
            for step in (0, 1):
                chunk(0, 2 * j + step)
                chunk(1, nch - 1 - 2 * j - step)
            return 0

        lax.fori_loop(0, nch // 2, pair_body, 0)
        if sfin_ref is not None:
            for d in (0, 1):
                sfin_ref[s, d] = st[d]

        def out_body(c, _):
            r0 = pl.multiple_of(c * ch, ch)
            y = yacc[pl.ds(r0, ch), :] * _silu(z_ref[pl.ds(base + r0, ch), :])
            outs = []
            for g in range(SSD_GROUPS):
                yg = y[:, g * gw:(g + 1) * gw]
                outs.append(yg * lax.rsqrt(jnp.mean(yg * yg, axis=-1, keepdims=True) + EPS))
            y_ref[pl.ds(base + r0, ch), :] = (jnp.concatenate(outs, axis=1) * nw_ref[...]).astype(y_ref.dtype)
            return 0

        lax.fori_loop(0, nch, out_body, 0)
        return 0

    lax.fori_loop(0, nseq, seq_body, 0)


def _ssd(z, xs, bc, dtr, a_log, dt_bias, d_skip, norm_w, seq, s0=None):
    t = z.shape[0]
    nseq = SEQ_BLOCK // seq
    nblk = t // SEQ_BLOCK
    nch = seq // SSD_CHUNK
    pad32 = lambda a: jnp.pad(a.astype(F32).reshape(1, -1), ((0, 0), (0, LANES - 2 * SSD_HEADS)))
    consts = [pad32(dt_bias), pad32(a_log),
              jnp.repeat(d_skip.astype(F32), SSD_HEAD_DIM).reshape(1, -1),
              norm_w.astype(F32).reshape(1, -1)] + _ssd_selectors()
    row = lambda n: pl.BlockSpec((SEQ_BLOCK, n), lambda i: (i, 0))
    in_specs = ([row(SSD_D_INNER), row(SSD_D_INNER), row(bc.shape[1]), row(LANES)]
                + [_const_spec(c.shape) for c in consts])
    args = [z, xs, bc, dtr] + consts
    out_specs = [row(SSD_D_INNER)]
    out_shape = [jax.ShapeDtypeStruct((t, SSD_D_INNER), BF16)]
    state_shape = (2, SSD_STATE, SSD_D_INNER)
    if s0 is not None:
        in_specs.append(pl.BlockSpec((1,) + state_shape, lambda i: (i, 0, 0, 0)))
        args.append(s0)
    else:
        out_specs.append(pl.BlockSpec((nseq,) + state_shape, lambda i: (i, 0, 0, 0)))
        out_shape.append(jax.ShapeDtypeStruct((t // seq,) + state_shape, F32))
    scratch = [pltpu.VMEM((seq, SSD_D_INNER), F32),
               pltpu.VMEM((seq, LANES), BF16), pltpu.VMEM((seq, LANES), BF16), pltpu.VMEM((seq, LANES), BF16),
               pltpu.VMEM((nch, LANES, SSD_CHUNK), F32), pltpu.VMEM((nch, LANES, SSD_CHUNK), F32),
               pltpu.VMEM(state_shape, F32)]
    return pl.pallas_call(
        functools.partial(_ssd_kernel, seq=seq, has_state=s0 is not None),
        grid=(nblk,),
        in_specs=in_specs,
        out_specs=out_specs,
        out_shape=out_shape,
        scratch_shapes=scratch,
        compiler_params=_params(dimension_semantics=("arbitrary",)),
        name="ssd_mixer",
    )(*args)


def _log_sigmoid(x):
    return -_softplus(-x)


def _ret_kernel(*refs, seq, has_state, rope):
    refs = list(refs)
    x_ref, n1_ref, mod_ref, wq_ref, wk_ref, wv_ref, wg_ref = refs[:7]
    del refs[:7]
    cos_ref, sin_ref = (refs.pop(0), refs.pop(0)) if rope else (None, None)
    dec_ref, nw_ref = refs.pop(0), refs.pop(0)
    s0_ref = refs.pop(0) if has_state else None
    y_ref = refs.pop(0)
    sfin_ref = None if has_state else refs.pop(0)
    hb_s, q_ref, k_ref, v_ref, g_ref, yacc, st = refs

    @pl.when(pl.program_id(1) == 0)
    def _():
        hb_s[...] = _norm_mod(x_ref[...], n1_ref[...], mod_ref, 0, 1).astype(BF16)

    hb = hb_s[...]
    for o_ref, w_ref, scale in ((q_ref, wq_ref, 1.0), (k_ref, wk_ref, RET_QK_DIM ** -0.5)):
        y = jnp.dot(hb, w_ref[...], preferred_element_type=F32) * scale
        if rope:
            for j in range(RET_QK_DIM // LANES):
                lanes = slice(j * LANES, (j + 1) * LANES)
                yj = y[:, lanes]
                rot = yj * cos_ref[:, lanes] + _swap_lane_pairs(yj) * sin_ref[:, lanes]
                o_ref[:, lanes] = rot.astype(o_ref.dtype)
        else:
            o_ref[...] = y.astype(o_ref.dtype)
    v_ref[...] = jnp.dot(hb, wv_ref[...], preferred_element_type=F32).astype(v_ref.dtype)
    g_ref[...] = jnp.dot(hb, wg_ref[...], preferred_element_type=F32).astype(g_ref.dtype)

    ch = RET_CHUNK
    nseq = q_ref.shape[0] // seq
    nch = seq // ch
    gf = _log_sigmoid(dec_ref[0, 0:1, :])
    gb = _log_sigmoid(dec_ref[0, 1:2, :])
    ri = lax.broadcasted_iota(jnp.int32, (ch, ch), 0)
    ci = lax.broadcasted_iota(jnp.int32, (ch, ch), 1)
    dist = (ri - ci).astype(F32)
    gfk, gbk = gf[:, :ch], gb[:, :ch]
    decay = (jnp.where(ci <= ri, jnp.exp(jnp.where(ci <= ri, dist, 0.0) * gfk), 0.0)
             + jnp.where(ci >= ri, jnp.exp(jnp.where(ci >= ri, -dist, 0.0) * gbk), 0.0))
    pos = lax.broadcasted_iota(jnp.int32, (ch, RET_QK_DIM), 0).astype(F32)
    gfq, gbq = gf[:, :RET_QK_DIM], gb[:, :RET_QK_DIM]
    e_f = jnp.exp((pos + 1.0) * gfq)
    e_b = jnp.exp((ch - pos) * gbq)
    tail_f = jnp.exp((ch - 1.0 - pos) * gfq)
    tail_b = jnp.exp(pos * gbq)
    dec_f = jnp.exp(ch * gf)
    dec_b = jnp.exp(ch * gb)

    for s in range(nseq):
        for d in (0, 1):
            if has_state:
                st[d] = s0_ref[0, d, 0]
            else:
                st[d] = jnp.zeros((RET_QK_DIM, RET_V_DIM), F32)
        for c in range(nch):
            rows = slice(s * seq + c * ch, s * seq + (c + 1) * ch)
            q, k, v = q_ref[rows, :], k_ref[rows, :].astype(F32), v_ref[rows, :]
            y = _mm(_mm_nt(q, k) * decay, v)
            if has_state or c > 0:
                y = y + _mm(q.astype(F32) * e_f, st[0])
            yacc[rows, :] = y
            st[0] = dec_f * st[0] + _mm((k * tail_f).T, v)
        for c in reversed(range(nch)):
            rows = slice(s * seq + c * ch, s * seq + (c + 1) * ch)
            q, k, v = q_ref[rows, :], k_ref[rows, :].astype(F32), v_ref[rows, :]
            if has_state or c < nch - 1:
                yacc[rows, :] = yacc[rows, :] + _mm(q.astype(F32) * e_b, st[1])
            st[1] = dec_b * st[1] + _mm((k * tail_b).T, v)
        if sfin_ref is not None:
            for d in (0, 1):
                sfin_ref[s, d, 0] = st[d]
        for c in range(nch):
            rows = slice(s * seq + c * ch, s * seq + (c + 1) * ch)
            y = yacc[rows, :]
            y = y * lax.rsqrt(jnp.mean(y * y, axis=-1, keepdims=True) + EPS)
            y_ref[rows, :] = (y * nw_ref[...] * _silu(g_ref[rows, :].astype(F32))).astype(y_ref.dtype)


def _retention(x, norm1_w, mod, w_in, ret_decay, norm_w, seq, s0=None, rope_tables=None):
    t, d = x.shape
    nseq = SEQ_BLOCK // seq
    nblk = t // SEQ_BLOCK
    per = 1 if mod.shape[0] > 1 else nblk
    dec = jnp.broadcast_to(jnp.pad(ret_decay.astype(F32).T, ((0, 0), (0, SUBLANES - 2)))[:, :, None],
                           (RET_HEADS, SUBLANES, RET_V_DIM))
    vspec = pl.BlockSpec((SEQ_BLOCK, RET_V_DIM), lambda i, h: (i, h))
    qk_blocks, v_blocks = RET_QK_W // RET_QK_DIM, RET_V_W // RET_V_DIM
    wq_spec = pl.BlockSpec((d, RET_QK_DIM), lambda i, h: (0, h))
    wk_spec = pl.BlockSpec((d, RET_QK_DIM), lambda i, h: (0, qk_blocks + h))
    wv_spec = pl.BlockSpec((d, RET_V_DIM), lambda i, h: (0, 2 * RET_QK_W // RET_V_DIM + h))
    wg_spec = pl.BlockSpec((d, RET_V_DIM), lambda i, h: (0, 2 * RET_QK_W // RET_V_DIM + v_blocks + h))
    in_specs = [pl.BlockSpec((SEQ_BLOCK, d), lambda i, h: (i, 0)), _const_spec((1, d)),
                pl.BlockSpec((1, 6, d), lambda i, h: (i // per, 0, 0)),
                wq_spec, wk_spec, wv_spec, wg_spec]
    args = [x, norm1_w.reshape(1, d), mod, w_in, w_in, w_in, w_in]
    if rope_tables is not None:
        in_specs += [_const_spec(rope_tables[0].shape)] * 2
        args += list(rope_tables)
    in_specs += [pl.BlockSpec((1, SUBLANES, RET_V_DIM), lambda i, h: (h, 0, 0)),
                 pl.BlockSpec((1, RET_V_DIM), lambda i, h: (0, h))]
    args += [dec, norm_w.astype(F32).reshape(1, -1)]
    out_specs = [vspec]
    out_shape = [jax.ShapeDtypeStruct((t, RET_V_W), BF16)]
    if s0 is not None:
        in_specs.append(pl.BlockSpec((1, 2, 1, RET_QK_DIM, RET_V_DIM), lambda i, h: (i, 0, h, 0, 0)))
        args.append(s0)
    else:
        out_specs.append(pl.BlockSpec((nseq, 2, 1, RET_QK_DIM, RET_V_DIM), lambda i, h: (i, 0, h, 0, 0)))
        out_shape.append(jax.ShapeDtypeStruct((t // seq, 2, RET_HEADS, RET_QK_DIM, RET_V_DIM), F32))
    return pl.pallas_call(
        functools.partial(_ret_kernel, seq=seq, has_state=s0 is not None, rope=rope_tables is not None),
        grid=(nblk, RET_HEADS),
        in_specs=in_specs,
        out_specs=out_specs,
        out_shape=out_shape,
        scratch_shapes=[pltpu.VMEM((SEQ_BLOCK, d), BF16),
                        pltpu.VMEM((SEQ_BLOCK, RET_QK_DIM), BF16), pltpu.VMEM((SEQ_BLOCK, RET_QK_DIM), BF16),
                        pltpu.VMEM((SEQ_BLOCK, RET_V_DIM), BF16), pltpu.VMEM((SEQ_BLOCK, RET_V_DIM), BF16),
                        pltpu.VMEM((SEQ_BLOCK, RET_V_DIM), F32),
                        pltpu.VMEM((2, RET_QK_DIM, RET_V_DIM), F32)],
        compiler_params=_params(dimension_semantics=("arbitrary", "arbitrary")),
        name="retention_mixer",
    )(*args)


def _out_ffn_kernel(*refs, n_mix, final_norm):
    x_ref = refs[0]
    mix_refs = refs[1:1 + n_mix]
    wo_ref, mod_ref, n2_ref, w1_ref, w3_ref, w2_ref = refs[1 + n_mix:7 + n_mix]
    if final_norm:
        fn_ref, o_ref, act = refs[7 + n_mix:]
    else:
        o_ref, act = refs[7 + n_mix:]
    mixed, row = None, 0
    for m_ref in mix_refs:
        k = m_ref.shape[1]
        term = jnp.dot(m_ref[...], wo_ref[row:row + k, :], preferred_element_type=F32)
        mixed = term if mixed is None else mixed + term
        row += k
    x1 = x_ref[...] + mod_ref[0, 2:3, :] * mixed
    hb = _norm_mod(x1, n2_ref[...], mod_ref, 3, 4).astype(BF16)
    for c in range(w1_ref.shape[1] // FFN_CHUNK):
        cols = slice(c * FFN_CHUNK, (c + 1) * FFN_CHUNK)
        h1 = jnp.dot(hb, w1_ref[:, cols], preferred_element_type=F32)
        h3 = jnp.dot(hb, w3_ref[:, cols], preferred_element_type=F32)
        act[:, cols] = (_silu(h1) * h3).astype(BF16)
    x2 = x1 + mod_ref[0, 5:6, :] * jnp.dot(act[...], w2_ref[...], preferred_element_type=F32)
    if final_norm:
        ms = jnp.mean(x2 * x2, axis=-1, keepdims=True)
        x2 = x2 * lax.rsqrt(ms + EPS) * fn_ref[...]
    o_ref[...] = x2


def _out_ffn(x, mixes, wo, mod, norm2_w, w1, w3, w2, rows_per_mod, final_norm_w=None):
    t, d = x.shape
    tm = ROW_TILE
    per = rows_per_mod // tm
    single = dict(pipeline_mode=pl.Buffered(1))
    in_specs = [pl.BlockSpec((tm, d), lambda i: (i, 0))]
    in_specs += [pl.BlockSpec((tm, m.shape[1]), lambda i: (i, 0)) for m in mixes]
    in_specs += [_const_spec(wo.shape, **single),
                 pl.BlockSpec((1, 6, d), lambda i: (i // per, 0, 0)), _const_spec((1, d))]
    in_specs += [_const_spec(w.shape, **single) for w in (w1, w3, w2)]
    args = [x, *mixes, wo, mod, norm2_w.reshape(1, d), w1, w3, w2]
    if final_norm_w is not None:
        in_specs.append(_const_spec((1, d)))
        args.append(final_norm_w.reshape(1, d))
    return pl.pallas_call(
        functools.partial(_out_ffn_kernel, n_mix=len(mixes), final_norm=final_norm_w is not None),
        grid=(t // tm,),
        in_specs=in_specs,
        out_specs=pl.BlockSpec((tm, d), lambda i: (i, 0)),
        out_shape=jax.ShapeDtypeStruct((t, d), F32),
        scratch_shapes=[pltpu.VMEM((tm, w1.shape[1]), BF16)],
        compiler_params=_params(dimension_semantics=("arbitrary",)),
        name="out_proj_ffn",
    )(*args)


def _rope_tables(seq):
    half = RET_QK_DIM // 2
    t = jnp.arange(seq)
    row = (t // GRID_W).astype(F32)
    col = (t % GRID_W).astype(F32)
    freqs = ROPE_BASE ** (-jnp.arange(0, half, 2, dtype=F32) / half)
    ang = jnp.concatenate([row[:, None] * freqs, col[:, None] * freqs], axis=-1)
    cos = jnp.repeat(jnp.cos(ang), 2, axis=1)
    sin = jnp.stack([-jnp.sin(ang), jnp.sin(ang)], axis=-1).reshape(seq, RET_QK_DIM)
    return cos, sin


def kernel(x_prompt, x_sample, cache_l0_na_k, cache_l0_na_v, state_l0_ssd, state_l1_ret, c, c_ctx,
           l0_norm1_w, l0_norm2_w, l0_mod_w, l0_mod_b, l0_w_in, l0_w_out, l0_na_bias, l0_conv_w, l0_conv_b,
           l0_ssd_a_log, l0_ssd_dt_bias, l0_ssd_d, l0_ssd_norm_w, l0_ffn_w1, l0_ffn_w3, l0_ffn_w2,
           l1_norm1_w, l1_norm2_w, l1_mod_w, l1_mod_b, l1_w_in, l1_w_out, l1_ret_decay, l1_ret_norm_w,
           l1_ffn_w1, l1_ffn_w3, l1_ffn_w2, final_norm_w):
    bc, lc, d = x_prompt.shape
    bl, ll, _ = x_sample.shape
    assert d == D_MODEL and ll == SEQ_BLOCK and SEQ_BLOCK % lc == 0 and bc % (SEQ_BLOCK // lc) == 0
    tc, tl = bc * lc, bl * ll
    xc = x_prompt.reshape(tc, d)
    xl = x_sample.reshape(tl, d)

    nrow = SUBLANES * ((1 + bl + SUBLANES - 1) // SUBLANES)
    cond = jnp.concatenate([c_ctx[None], c, jnp.zeros((nrow - 1 - bl, d), F32)], axis=0)
    mods = []
    for mod_w, mod_b in ((l0_mod_w, l0_mod_b), (l1_mod_w, l1_mod_b)):
        m = _adaln(cond, mod_w, mod_b).reshape(nrow, 6, d)
        mods.append((m[0:1], m[1:1 + bl]))

    w_in0 = jnp.pad(l0_w_in.astype(BF16), ((0, 0), (0, LANES - 2 * SSD_HEADS)))
    (wo0, ffn0_w1, ffn0_w3, ffn0_w2, w_in1, wo1, ffn1_w1, ffn1_w3, ffn1_w2) = _to_bf16(
        l0_w_out, l0_ffn_w1, l0_ffn_w3, l0_ffn_w2, l1_w_in, l1_w_out, l1_ffn_w1, l1_ffn_w3, l1_ffn_w2)
    ffn0 = (ffn0_w1, ffn0_w3, ffn0_w2)
    ffn1 = (ffn1_w1, ffn1_w3, ffn1_w2)
    ssd_p = (l0_ssd_a_log, l0_ssd_dt_bias, l0_ssd_d, l0_ssd_norm_w)

    qc, kc, vc, zc, xsc, bcc, dtc = _inproj0(xc, l0_norm1_w, mods[0][0], w_in0, l0_conv_w, l0_conv_b, lc, F32)
    att_c = _na_ctx(qc, kc, vc, lc)
    ssd_c, sfin_c = _ssd(zc, xsc, bcc, dtc, *ssd_p, seq=lc)
    xc = _out_ffn(xc, (att_c, ssd_c), wo0, mods[0][0], l0_norm2_w, *ffn0, tc)

    ql, kl, vl, zl, xsl, bcl, dtl = _inproj0(xl, l0_norm1_w, mods[0][1], w_in0, l0_conv_w, l0_conv_b, ll, BF16)
    past = cache_l0_na_k.shape[1]
    att_l = _na_lat(ql, kl, vl, cache_l0_na_k.reshape(bl * past, NA_WIDTH),
                    cache_l0_na_v.reshape(bl * past, NA_WIDTH), _na_bias_table(l0_na_bias), ll)
    s0_ssd = state_l0_ssd.transpose(0, 1, 3, 2, 4).reshape(bl, 2, SSD_STATE, SSD_D_INNER)
    (ssd_l,) = _ssd(zl, xsl, bcl, dtl, *ssd_p, seq=ll, s0=s0_ssd)
    xl = _out_ffn(xl, (att_l, ssd_l), wo0, mods[0][1], l0_norm2_w, *ffn0, ll)

    ret_c, ret_state = _retention(xc, l1_norm1_w, mods[1][0], w_in1, l1_ret_decay, l1_ret_norm_w, lc)
    y_prompt = _out_ffn(xc, (ret_c,), wo1, mods[1][0], l1_norm2_w, *ffn1, tc, final_norm_w=final_norm_w)

    (ret_l,) = _retention(xl, l1_norm1_w, mods[1][1], w_in1, l1_ret_decay, l1_ret_norm_w, ll,
                          s0=state_l1_ret, rope_tables=_rope_tables(ll))
    y_sample = _out_ffn(xl, (ret_l,), wo1, mods[1][1], l1_norm2_w, *ffn1, ll, final_norm_w=final_norm_w)

    new_ssd = sfin_c.reshape(bc, 2, SSD_STATE, SSD_HEADS, SSD_HEAD_DIM).transpose(0, 1, 3, 2, 4)
    return (y_prompt.reshape(bc, lc, d), y_sample.reshape(bl, ll, d),
            kc.reshape(bc, lc, NA_HEADS, NA_HEAD_DIM), vc.reshape(bc, lc, NA_HEADS, NA_HEAD_DIM),
            new_ssd, ret_state)
```

```python
import functools

import numpy as np
import jax
import jax.numpy as jnp
from jax import lax
from jax.experimental import pallas as pl
from jax.experimental.pallas import tpu as pltpu

F32 = jnp.float32
BF16 = jnp.bfloat16

D_MODEL = 1024
GRID_W = 64
NA_HEADS = 8
NA_HEAD_DIM = 64
NA_WIDTH = NA_HEADS * NA_HEAD_DIM
NA_WIN_ROWS = 8
NA_WIN_COLS = 16
SSD_HEADS = 16
SSD_HEAD_DIM = 64
SSD_D_INNER = SSD_HEADS * SSD_HEAD_DIM
SSD_GROUPS = 2
SSD_STATE = 128
SSD_CONV = 5
SSD_XBC = SSD_D_INNER + 2 * SSD_GROUPS * SSD_STATE
RET_HEADS = 4
RET_QK_DIM = 256
RET_V_DIM = 512
RET_QK_W = RET_HEADS * RET_QK_DIM
RET_V_W = RET_HEADS * RET_V_DIM
ROPE_BASE = 10000.0
EPS = 1e-6

LANES = 128
SUBLANES = 8
SEQ_BLOCK = 1024
ROW_TILE = 512
SSD_CHUNK = 128
CONV_ROWS = 256
RET_CHUNK = 256
FFN_CHUNK = 256
VMEM_LIMIT = 56 * 1024 * 1024

_NEG_INF = float("-inf")


def _params(**kw):
    return pltpu.CompilerParams(vmem_limit_bytes=VMEM_LIMIT, **kw)


def _silu(x):
    return x * (1.0 / (1.0 + jnp.exp(-x)))


def _softplus(x):
    return jnp.maximum(x, 0.0) + jnp.log(1.0 + jnp.exp(-jnp.abs(x)))


def _mm(a, b):
    return jnp.dot(a.astype(BF16), b.astype(BF16), preferred_element_type=F32)


def _mm_nt(a, b):
    return lax.dot_general(a.astype(BF16), b.astype(BF16), (((1,), (1,)), ((), ())),
                           preferred_element_type=F32)


def _const_spec(shape, **kw):
    nd = len(shape)
    return pl.BlockSpec(shape, lambda *_: (0,) * nd, **kw)


def _cast_kernel(*refs):
    n = len(refs) // 2
    for i_ref, o_ref in zip(refs[:n], refs[n:]):
        o_ref[...] = i_ref[...].astype(o_ref.dtype)


def _to_bf16(*ws):
    steps = 8
    specs = [pl.BlockSpec((w.shape[0] // steps, w.shape[1]), lambda i: (i, 0)) for w in ws]
    assert all(w.shape[0] % (steps * 2 * SUBLANES) == 0 for w in ws)
    return pl.pallas_call(
        _cast_kernel,
        grid=(steps,),
        in_specs=specs,
        out_specs=specs,
        out_shape=[jax.ShapeDtypeStruct(w.shape, BF16) for w in ws],
        compiler_params=_params(dimension_semantics=("arbitrary",)),
        name="weights_to_bf16",
    )(*ws)


def _adaln_kernel(c_ref, w_ref, b_ref, o_ref):
    o_ref[...] = _mm(_silu(c_ref[...]), w_ref[...]) + b_ref[...]


def _adaln(cond, mod_w, mod_b):
    r, d = cond.shape
    n = mod_w.shape[1]
    tn = 1536
    return pl.pallas_call(
        _adaln_kernel,
        grid=(n // tn,),
        in_specs=[_const_spec((r, d)),
                  pl.BlockSpec((d, tn), lambda j: (0, j)),
                  pl.BlockSpec((1, tn), lambda j: (0, j))],
        out_specs=pl.BlockSpec((r, tn), lambda j: (0, j)),
        out_shape=jax.ShapeDtypeStruct((r, n), F32),
        compiler_params=_params(dimension_semantics=("arbitrary",)),
        name="adaln_mod",
    )(cond, mod_w, mod_b.reshape(1, n))


def _norm_mod(x, nw, mod_ref, shift_idx, scale_idx):
    ms = jnp.mean(x * x, axis=-1, keepdims=True)
    h = x * lax.rsqrt(ms + EPS) * nw
    return h * (1.0 + mod_ref[0, scale_idx:scale_idx + 1, :]) + mod_ref[0, shift_idx:shift_idx + 1, :]


def _inproj0_kernel(*refs, seq, halo):
    if halo:
        (x_ref, xp_ref, xn_ref, nw_ref, mod_ref, w_ref, cw_ref, cb_ref,
         q_ref, k_ref, v_ref, z_ref, xs_ref, bc_ref, dt_ref, xpad) = refs
    else:
        (x_ref, nw_ref, mod_ref, w_ref, cw_ref, cb_ref,
         q_ref, k_ref, v_ref, z_ref, xs_ref, bc_ref, dt_ref, xpad) = refs
    tm = x_ref.shape[0]
    pad = SUBLANES
    sub = xpad.shape[1] - 2 * pad
    h = _norm_mod(x_ref[...], nw_ref[...], mod_ref, 0, 1)
    hb = h.astype(BF16)
    xbc_col = 3 * NA_WIDTH + SSD_D_INNER
    w_xbc = w_ref[:, xbc_col:xbc_col + SSD_XBC]
    if halo:
        tiles_per_seq = seq // tm
        p = pl.program_id(0) % tiles_per_seq
        h_prev = jnp.where(p > 0, _norm_mod(xp_ref[...], nw_ref[...], mod_ref, 0, 1), 0.0)
        h_next = jnp.where(p < tiles_per_seq - 1, _norm_mod(xn_ref[...], nw_ref[...], mod_ref, 0, 1), 0.0)
        ext = jnp.concatenate([h_prev, h, h_next], axis=0).astype(BF16)
        xpad[0] = jnp.dot(ext, w_xbc, preferred_element_type=F32)
    else:
        xbc = jnp.dot(hb, w_xbc, preferred_element_type=F32)
        for s in range(tm // sub):
            xpad[s, 0:pad, :] = jnp.zeros((pad, SSD_XBC), F32)
            xpad[s, pad:pad + sub, :] = xbc[s * sub:(s + 1) * sub]
            xpad[s, pad + sub:2 * pad + sub, :] = jnp.zeros((pad, SSD_XBC), F32)

    cr = CONV_ROWS

    def conv_chunk(r0):
        s, rs = divmod(r0, sub)
        nwin = cr + 2 * pad
        for lt in range(SSD_XBC // LANES):
            cols = slice(lt * LANES, (lt + 1) * LANES)
            win = xpad[s, rs:rs + nwin, cols]
            taps = [cw_ref[k:k + 1, cols] * win for k in range(SSD_CONV)]
            before = taps[1] + pltpu.roll(taps[0], 1, 0)
            after = taps[3] + pltpu.roll(taps[4], nwin - 1, 0)
            conv = taps[2] + pltpu.roll(before, 1, 0) + pltpu.roll(after, nwin - 1, 0)
            act = _silu(conv[pad:pad + cr, :] + cb_ref[:, cols])
            if lt < SSD_D_INNER // LANES:
                xs_ref[r0:r0 + cr, cols] = act
            else:
                bc_ref[r0:r0 + cr, lt * LANES - SSD_D_INNER:(lt + 1) * LANES - SSD_D_INNER] = act.astype(BF16)

    chunks = list(range(0, tm, cr))
    col = 0
    for o_ref in (q_ref, k_ref, v_ref, z_ref):
        if chunks:
            conv_chunk(chunks.pop(0))
        n = o_ref.shape[1]
        o_ref[...] = jnp.dot(hb, w_ref[:, col:col + n], preferred_element_type=F32).astype(o_ref.dtype)
        col += n
    col += SSD_XBC
    n_dt = w_ref.shape[1] - col
    dt_ref[...] = jnp.zeros(dt_ref.shape, F32)
    dt_ref[:, 0:n_dt] = jnp.dot(hb, w_ref[:, col:col + n_dt], preferred_element_type=F32)
    for r0 in chunks:
        conv_chunk(r0)


def _inproj0(x, norm_w, mod, w, conv_w, conv_b, seq, qkv_dtype):
    t, d = x.shape
    tm = ROW_TILE
    assert seq % tm == 0 or tm % seq == 0
    halo = seq > tm
    sub = min(seq, tm)
    widths = (NA_WIDTH, NA_WIDTH, NA_WIDTH, SSD_D_INNER, SSD_D_INNER, SSD_XBC - SSD_D_INNER, LANES)
    dtypes = (qkv_dtype,) * 3 + (F32, F32, BF16, F32)
    assert sum(widths) - LANES + 2 * SSD_HEADS == w.shape[1]
    per = max(seq // tm, 1) if mod.shape[0] > 1 else t // tm
    in_specs = [pl.BlockSpec((tm, d), lambda i: (i, 0))]
    args = [x]
    if halo:
        rb = tm // SUBLANES
        last = t // SUBLANES - 1
        in_specs += [pl.BlockSpec((SUBLANES, d), lambda i: (jnp.maximum(i * rb - 1, 0), 0)),
                     pl.BlockSpec((SUBLANES, d), lambda i: (jnp.minimum((i + 1) * rb, last), 0))]
        args += [x, x]
    in_specs += [_const_spec((1, d)),
                 pl.BlockSpec((1, 6, d), lambda i: (i // per, 0, 0)),
                 _const_spec(w.shape), _const_spec((SUBLANES, SSD_XBC)), _const_spec((1, SSD_XBC))]
    args += [norm_w.reshape(1, d), mod, w,
             jnp.pad(conv_w.astype(F32), ((0, SUBLANES - SSD_CONV), (0, 0))), conv_b.astype(F32).reshape(1, -1)]
    return pl.pallas_call(
        functools.partial(_inproj0_kernel, seq=seq, halo=halo),
        grid=(t // tm,),
        in_specs=in_specs,
        out_specs=[pl.BlockSpec((tm, n), lambda i: (i, 0)) for n in widths],
        out_shape=[jax.ShapeDtypeStruct((t, n), dt) for n, dt in zip(widths, dtypes)],
        scratch_shapes=[pltpu.VMEM((tm // sub, sub + 2 * SUBLANES, SSD_XBC), F32)],
        compiler_params=_params(dimension_semantics=("arbitrary",)),
        name="l0_in_proj",
    )(*args)


def _swap_lane_pairs(x):
    even = (lax.broadcasted_iota(jnp.int32, x.shape, 1) & 1) == 0
    return jnp.where(even, pltpu.roll(x, LANES - 1, 1), pltpu.roll(x, 1, 1))


def _lane_lo(shape):
    return lax.broadcasted_iota(jnp.int32, shape, 1) < NA_HEAD_DIM


def _head_lanes(x, half):
    lo = _lane_lo(x.shape)
    return jnp.where(lo if half == 0 else jnp.logical_not(lo), x, jnp.zeros_like(x))


def _softmax_pv(q2, keys, vals, biases):
    acc = None
    for half in (0, 1):
        qm = _head_lanes(q2, half)
        scores = []
        for kk, bb in zip(keys, biases):
            s = _mm_nt(qm, kk)
            if bb is not None:
                s = bb(half, s)
            scores.append(s)
        mx = functools.reduce(jnp.maximum, [jnp.max(s, axis=-1, keepdims=True) for s in scores])
        es = [jnp.exp(s - mx) for s in scores]
        den = functools.reduce(jnp.add, [jnp.sum(e, axis=-1, keepdims=True) for e in es])
        pv = functools.reduce(jnp.add, [_mm(e, _head_lanes(vv, half)) for e, vv in zip(es, vals)])
        out = pv * (1.0 / den)
        acc = out if acc is None else acc + out
    return acc


def _na_ctx_kernel(q_ref, k_ref, v_ref, o_ref, kh_ref, vh_ref, *, seq):
    scale = NA_HEAD_DIM ** -0.5
    nseq = q_ref.shape[0] // seq
    for s in range(nseq):
        r = slice(s * seq, (s + 1) * seq)
        for hp in range(NA_WIDTH // LANES):
            c = slice(hp * LANES, (hp + 1) * LANES)
            out = _softmax_pv(q_ref[r, c] * scale, [k_ref[r, c]], [v_ref[r, c]], [None])
            o_ref[r, c] = out.astype(o_ref.dtype)
    kh_ref[...] = k_ref[...].reshape(kh_ref.shape)
    vh_ref[...] = v_ref[...].reshape(vh_ref.shape)


def _na_ctx(q, k, v, seq):
    t, w = q.shape
    spec = pl.BlockSpec((SEQ_BLOCK, w), lambda i: (i, 0))
    hspec = pl.BlockSpec((SEQ_BLOCK, NA_HEADS, NA_HEAD_DIM), lambda i: (i, 0, 0))
    hshape = jax.ShapeDtypeStruct((t, NA_HEADS, NA_HEAD_DIM), k.dtype)
    return pl.pallas_call(
        functools.partial(_na_ctx_kernel, seq=seq),
        grid=(t // SEQ_BLOCK,),
        in_specs=[spec, spec, spec],
        out_specs=[spec, hspec, hspec],
        out_shape=[jax.ShapeDtypeStruct((t, w), BF16), hshape, hshape],
        compiler_params=_params(dimension_semantics=("arbitrary",)),
        name="na_context",
    )(q, k, v)


NA_QBLK = 256


def _na_lat_kernel(q_ref, k_ref, v_ref, kc_ref, vc_ref, tab_ref, o_ref):
    scale = NA_HEAD_DIM ** -0.5
    seq = q_ref.shape[0]
    nblk = seq // NA_QBLK
    rows_per_blk = NA_QBLK // GRID_W
    grid_rows = seq // GRID_W
    n_off = 2 * NA_WIN_ROWS

    qc = lax.broadcasted_iota(jnp.int32, (GRID_W, LANES), 0)
    lane = lax.broadcasted_iota(jnp.int32, (GRID_W, LANES), 1)
    kc = lane & (GRID_W - 1)
    c0 = jnp.clip(qc - NA_WIN_COLS // 2, 0, GRID_W - NA_WIN_COLS)
    col_ok = (kc >= c0) & (kc < c0 + NA_WIN_COLS)
    first = lane < GRID_W
    pair_bias = [[jnp.where(col_ok,
                            pltpu.roll(jnp.broadcast_to(tab_ref[half, e:e + 1, :], (GRID_W, LANES)), 0, 1,
                                       stride=1, stride_axis=0),
                            _NEG_INF)
                  for e in range(n_off)] for half in (0, 1)]
    neg = jnp.full((GRID_W, LANES), _NEG_INF, F32)

    def band_start(r):
        return min(max(r - NA_WIN_ROWS // 2, 0), grid_rows - NA_WIN_ROWS)

    for i in range(nblk):
        ws_row = band_start(i * rows_per_blk) // 2 * 2
        we_row = min((band_start((i + 1) * rows_per_blk - 1) + NA_WIN_ROWS + 1) // 2 * 2, grid_rows)
        ws, kwin = ws_row * GRID_W, (we_row - ws_row) * GRID_W

        def add_bias(half, s, i=i, ws_row=ws_row, kwin=kwin):
            out_rows = []
            for rq in range(rows_per_blk):
                r = i * rows_per_blk + rq
                r0 = band_start(r)
                tiles = []
                for kp in range(kwin // LANES):
                    kr = ws_row + 2 * kp
                    ok0 = r0 <= kr < r0 + NA_WIN_ROWS
                    ok1 = r0 <= kr + 1 < r0 + NA_WIN_ROWS
                    if ok0 or ok1:
                        tile = pair_bias[half][kr - r + NA_WIN_ROWS]
                        if not ok1:
                            tile = jnp.where(first, tile, _NEG_INF)
                        elif not ok0:
                            tile = jnp.where(first, _NEG_INF, tile)
                    else:
                        tile = neg
                    tiles.append(s[rq * GRID_W:(rq + 1) * GRID_W, kp * LANES:(kp + 1) * LANES] + tile)
                out_rows.append(jnp.concatenate(tiles, axis=1))
            return jnp.concatenate(out_rows, axis=0)

        rows = slice(i * NA_QBLK, (i + 1) * NA_QBLK)
        kw = k_ref[ws:ws + kwin, :]
        vw = v_ref[ws:ws + kwin, :]
        out = _softmax_pv(q_ref[rows, :] * scale, [kw, kc_ref[...]], [vw, vc_ref[...]], [add_bias, None])
        o_ref[rows, :] = out.astype(o_ref.dtype)


def _na_bias_table(rel_bias):
    h, n_dr, n_dc = rel_bias.shape
    half_dc = n_dc // 2
    tz = jnp.pad(rel_bias.astype(F32), ((0, 0), (1, 1), (0, 0)))
    lo, hi = tz[:, 0:n_dr + 1], tz[:, 1:n_dr + 2]
    gap = jnp.zeros((h, n_dr + 1, GRID_W - half_dc - 1 - half_dc), F32)
    return jnp.concatenate([lo[..., half_dc:], gap, hi, gap, lo[..., :half_dc]], axis=-1)


def _na_lat(q, k, v, k_ctx, v_ctx, bias_table, seq):
    t, w = q.shape
    b = t // seq
    npair = w // LANES
    past = k_ctx.shape[0] // b
    kspec = pl.BlockSpec((seq, LANES), lambda hp, bb: (bb, hp))
    cspec = pl.BlockSpec((past, LANES), lambda hp, bb: (bb, hp))
    tspec = pl.BlockSpec((2,) + bias_table.shape[1:], lambda hp, bb: (hp, 0, 0))
    return pl.pallas_call(
        _na_lat_kernel,
        grid=(npair, b),
        in_specs=[kspec, kspec, kspec, cspec, cspec, tspec],
        out_specs=kspec,
        out_shape=jax.ShapeDtypeStruct((t, w), BF16),
        compiler_params=_params(dimension_semantics=("arbitrary", "arbitrary")),
        name="na_latent",
    )(q, k, v, k_ctx, v_ctx, bias_table)


def _pack3(v, lane):
    vm = jnp.where(lane < 32, v, 0.0)
    hi = vm.astype(BF16).astype(F32)
    r1 = vm - hi
    mid = r1.astype(BF16).astype(F32)
    lo = r1 - mid
    return (hi + pltpu.roll(mid, 32, 1) + pltpu.roll(lo, 64, 1)).astype(BF16)


def _unpack3(res):
    return res + pltpu.roll(res, 96, 1) + pltpu.roll(res, 64, 1)


def _ssd_selectors():
    r = np.arange(LANES)
    out = []
    for width in (SSD_HEAD_DIM, SSD_CHUNK):
        l = np.arange(SSD_HEADS * width)
        for d in (0, 1):
            sel = (r[:, None] < 96) & ((r[:, None] % 32) == d * SSD_HEADS + l[None, :] // width)
            out.append(jnp.asarray(sel, BF16))
    return out


def _ssd_kernel(*refs, seq, has_state):
    if has_state:
        (z_ref, xs_ref, bc_ref, dtr_ref, dtb_ref, alog_ref, dsk_ref, nw_ref,
         s64f_ref, s64b_ref, s128f_ref, s128b_ref, s0_ref,
         y_ref, yacc, csp, ep, wp, cst, dtt, st) = refs
        sfin_ref = None
    else:
        (z_ref, xs_ref, bc_ref, dtr_ref, dtb_ref, alog_ref, dsk_ref, nw_ref,
         s64f_ref, s64b_ref, s128f_ref, s128b_ref,
         y_ref, sfin_ref, yacc, csp, ep, wp, cst, dtt, st) = refs
        s0_ref = None
    ch = SSD_CHUNK
    nseq = z_ref.shape[0] // seq
    nch = seq // ch
    assert nch % 2 == 0
    lane = lax.broadcasted_iota(jnp.int32, (ch, LANES), 1)
    ri = lax.broadcasted_iota(jnp.int32, (ch, ch), 0)
    ci = lax.broadcasted_iota(jnp.int32, (ch, ch), 1)
    keeps = (ci <= ri, ci >= ri)
    tril = jnp.where(keeps[0], 1.0, 0.0).astype(BF16)
    triu = jnp.where(keeps[1], 1.0, 0.0).astype(BF16)
    lane_lo = lane < SSD_HEAD_DIM
    fwd_lane = lane < SSD_HEADS
    neg_a = -jnp.exp(alog_ref[...])
    gw = SSD_D_INNER // SSD_GROUPS
    sel64s = (s64f_ref, s64b_ref)
    sel128s = (s128f_ref, s128b_ref)
    edges = (ch - 1, 0)

    def seq_body(s, _):
        base = pl.multiple_of(s * seq, seq)

        def decay_body(c, _):
            r0 = pl.multiple_of(c * ch, ch)
            yacc[pl.ds(r0, ch), :] = dsk_ref[...] * xs_ref[pl.ds(base + r0, ch), :]
            dt = _softplus(dtr_ref[pl.ds(base + r0, ch), :] + dtb_ref[...])
            la = _pack3(dt * neg_a, lane)
            cs = jnp.where(fwd_lane,
                           _unpack3(jnp.dot(tril, la, preferred_element_type=F32)),
                           _unpack3(jnp.dot(triu, la, preferred_element_type=F32)))
            tot = jnp.where(fwd_lane, cs[ch - 1:ch, :], cs[0:1, :])
            rows = pl.ds(r0, ch)
            csp[rows, :] = _pack3(cs, lane)
            ep[rows, :] = _pack3(jnp.exp(cs), lane)
            wp[rows, :] = _pack3(dt * jnp.exp(tot - cs), lane)
            cst[c] = cs.T
            dtt[c] = dt.T
            return 0

        lax.fori_loop(0, nch, decay_body, 0, unroll=2)

        for d in (0, 1):
            if has_state:
                for h in range(SSD_HEADS):
                    st[d, :, h * SSD_HEAD_DIM:(h + 1) * SSD_HEAD_DIM] = s0_ref[0, d, h]
            else:
                st[d] = jnp.zeros((SSD_STATE, SSD_D_INNER), F32)

        def chunk(d, c):
            r0 = pl.multiple_of(c * ch, ch)
            rows = pl.ds(r0, ch)
            col = jnp.dot(csp[rows, :], sel128s[d][...], preferred_element_type=F32)
            e64 = jnp.dot(ep[rows, :], sel64s[d][...], preferred_element_type=F32)
            w64 = jnp.dot(wp[rows, :], sel64s[d][...], preferred_element_type=F32)
            cs_t = cst[c]
            dt_t = dtt[c]
            in_rows = pl.ds(base + r0, ch)
            x_c = xs_ref[in_rows, :]
            state = st[d]
            state_b = state.astype(BF16)
            xw = (x_c * w64).astype(BF16)
            y_intra, y_state, upd = [], [], []
            for g in range(SSD_GROUPS):
                b_g = bc_ref[in_rows, g * SSD_STATE:(g + 1) * SSD_STATE]
                cofs = SSD_GROUPS * SSD_STATE
                c_g = bc_ref[in_rows, cofs + g * SSD_STATE:cofs + (g + 1) * SSD_STATE]
                gram = _mm_nt(c_g, b_g)
                for pp in range(gw // LANES):
                    p = g * (gw // LANES) + pp
                    ws = []
                    for half in (0, 1):
                        h = 2 * p + half
                        hd = d * SSD_HEADS + h
                        diff = col[:, h * ch:(h + 1) * ch] - cs_t[hd:hd + 1, :]
                        wm = jnp.exp(jnp.where(keeps[d], diff, _NEG_INF)) * gram * dt_t[hd:hd + 1, :]
                        ws.append(wm.astype(BF16))
                    xp = x_c[:, p * LANES:(p + 1) * LANES]
                    xcat = jnp.concatenate([jnp.where(lane_lo, xp, 0.0), jnp.where(lane_lo, 0.0, xp)],
                                           axis=0).astype(BF16)
                    y_intra.append(jnp.dot(jnp.concatenate(ws, axis=1), xcat, preferred_element_type=F32))
                y_state.append(jnp.dot(c_g, state_b[:, g * gw:(g + 1) * gw], preferred_element_type=F32))
                upd.append(jnp.dot(b_g.T, xw[:, g * gw:(g + 1) * gw], preferred_element_type=F32))
            yacc[rows, :] = (yacc[rows, :] + jnp.concatenate(y_intra, axis=1)
                             + jnp.concatenate(y_state, axis=1) * e64)
            st[d] = e64[edges[d]:edges[d] + 1, :] * state + jnp.concatenate(upd, axis=1)

        def pair_body(j, _):
            for step in (0, 1):
                chunk(0, 2 * j + step)
                chunk(1, nch - 1 - 2 * j - step)
            return 0

        lax.fori_loop(0, nch // 2, pair_body, 0)
        if sfin_ref is not None:
            for d in (0, 1):
                for h in range(SSD_HEADS):
                    sfin_ref[s, d, h] = st[d, :, h * SSD_HEAD_DIM:(h + 1) * SSD_HEAD_DIM]

        def out_body(c, _):
            r0 = pl.multiple_of(c * ch, ch)
            y = yacc[pl.ds(r0, ch), :] * _silu(z_ref[pl.ds(base + r0, ch), :])
            outs = []
            for g in range(SSD_GROUPS):
                yg = y[:, g * gw:(g + 1) * gw]
                outs.append(yg * lax.rsqrt(jnp.mean(yg * yg, axis=-1, keepdims=True) + EPS))
            y_ref[pl.ds(base + r0, ch), :] = (jnp.concatenate(outs, axis=1) * nw_ref[...]).astype(y_ref.dtype)
            return 0

        lax.fori_loop(0, nch, out_body, 0)
        return 0

    lax.fori_loop(0, nseq, seq_body, 0)


def _ssd(z, xs, bc, dtr, a_log, dt_bias, d_skip, norm_w, seq, s0=None):
    t = z.shape[0]
    nseq = SEQ_BLOCK // seq
    nblk = t // SEQ_BLOCK
    nch = seq // SSD_CHUNK
    pad32 = lambda a: jnp.pad(a.astype(F32).reshape(1, -1), ((0, 0), (0, LANES - 2 * SSD_HEADS)))
    consts = [pad32(dt_bias), pad32(a_log),
              jnp.repeat(d_skip.astype(F32), SSD_HEAD_DIM).reshape(1, -1),
              norm_w.astype(F32).reshape(1, -1)] + _ssd_selectors()
    row = lambda n: pl.BlockSpec((SEQ_BLOCK, n), lambda i: (i, 0))
    in_specs = ([row(SSD_D_INNER), row(SSD_D_INNER), row(bc.shape[1]), row(LANES)]
                + [_const_spec(c.shape) for c in consts])
    args = [z, xs, bc, dtr] + consts
    out_specs = [row(SSD_D_INNER)]
    out_shape = [jax.ShapeDtypeStruct((t, SSD_D_INNER), BF16)]
    state_shape = (2, SSD_STATE, SSD_D_INNER)
    io_state = (2, SSD_HEADS, SSD_STATE, SSD_HEAD_DIM)
    if s0 is not None:
        in_specs.append(pl.BlockSpec((1,) + io_state, lambda i: (i, 0, 0, 0, 0)))
        args.append(s0)
    else:
        out_specs.append(pl.BlockSpec((nseq,) + io_state, lambda i: (i, 0, 0, 0, 0)))
        out_shape.append(jax.ShapeDtypeStruct((t // seq,) + io_state, F32))
    scratch = [pltpu.VMEM((seq, SSD_D_INNER), F32),
               pltpu.VMEM((seq, LANES), BF16), pltpu.VMEM((seq, LANES), BF16), pltpu.VMEM((seq, LANES), BF16),
               pltpu.VMEM((nch, LANES, SSD_CHUNK), F32), pltpu.VMEM((nch, LANES, SSD_CHUNK), F32),
               pltpu.VMEM(state_shape, F32)]
    return pl.pallas_call(
        functools.partial(_ssd_kernel, seq=seq, has_state=s0 is not None),
        grid=(nblk,),
        in_specs=in_specs,
        out_specs=out_specs,
        out_shape=out_shape,
        scratch_shapes=scratch,
        compiler_params=_params(dimension_semantics=("arbitrary",)),
        name="ssd_mixer",
    )(*args)


def _log_sigmoid(x):
    return -_softplus(-x)


def _ret_kernel(*refs, seq, has_state, rope):
    refs = list(refs)
    x_ref, n1_ref, mod_ref, wq_ref, wk_ref, wv_ref, wg_ref = refs[:7]
    del refs[:7]
    cos_ref, sin_ref = (refs.pop(0), refs.pop(0)) if rope else (None, None)
    dec_ref, nw_ref = refs.pop(0), refs.pop(0)
    s0_ref = refs.pop(0) if has_state else None
    y_ref = refs.pop(0)
    sfin_ref = None if has_state else refs.pop(0)
    hb_s, q_ref, k_ref, v_ref, g_ref, yacc, st = refs

    @pl.when(pl.program_id(1) == 0)
    def _():
        hb_s[...] = _norm_mod(x_ref[...], n1_ref[...], mod_ref, 0, 1).astype(BF16)

    hb = hb_s[...]
    for o_ref, w_ref, scale in ((q_ref, wq_ref, 1.0), (k_ref, wk_ref, RET_QK_DIM ** -0.5)):
        y = jnp.dot(hb, w_ref[...], preferred_element_type=F32) * scale
        if rope:
            for j in range(RET_QK_DIM // LANES):
                lanes = slice(j * LANES, (j + 1) * LANES)
                yj = y[:, lanes]
                rot = yj * cos_ref[:, lanes] + _swap_lane_pairs(yj) * sin_ref[:, lanes]
                o_ref[:, lanes] = rot.astype(o_ref.dtype)
        else:
            o_ref[...] = y.astype(o_ref.dtype)
    v_ref[...] = jnp.dot(hb, wv_ref[...], preferred_element_type=F32).astype(v_ref.dtype)
    g_ref[...] = jnp.dot(hb, wg_ref[...], preferred_element_type=F32).astype(g_ref.dtype)

    ch = RET_CHUNK
    nseq = q_ref.shape[0] // seq
    nch = seq // ch
    gf = _log_sigmoid(dec_ref[0, 0:1, :])
    gb = _log_sigmoid(dec_ref[0, 1:2, :])
    ri = lax.broadcasted_iota(jnp.int32, (ch, ch), 0)
    ci = lax.broadcasted_iota(jnp.int32, (ch, ch), 1)
    dist = (ri - ci).astype(F32)
    gfk, gbk = gf[:, :ch], gb[:, :ch]
    decay = (jnp.where(ci <= ri, jnp.exp(jnp.where(ci <= ri, dist, 0.0) * gfk), 0.0)
             + jnp.where(ci >= ri, jnp.exp(jnp.where(ci >= ri, -dist, 0.0) * gbk), 0.0))
    pos = lax.broadcasted_iota(jnp.int32, (ch, RET_QK_DIM), 0).astype(F32)
    gfq, gbq = gf[:, :RET_QK_DIM], gb[:, :RET_QK_DIM]
    e_f = jnp.exp((pos + 1.0) * gfq)
    e_b = jnp.exp((ch - pos) * gbq)
    tail_f = jnp.exp((ch - 1.0 - pos) * gfq)
    tail_b = jnp.exp(pos * gbq)
    dec_f = jnp.exp(ch * gf)
    dec_b = jnp.exp(ch * gb)

    for s in range(nseq):
        for d in (0, 1):
            if has_state:
                st[d] = s0_ref[0, d, 0]
            else:
                st[d] = jnp.zeros((RET_QK_DIM, RET_V_DIM), F32)
        for c in range(nch):
            rows = slice(s * seq + c * ch, s * seq + (c + 1) * ch)
            q, k, v = q_ref[rows, :], k_ref[rows, :].astype(F32), v_ref[rows, :]
            y = _mm(_mm_nt(q, k) * decay, v)
            if has_state or c > 0:
                y = y + _mm(q.astype(F32) * e_f, st[0])
            yacc[rows, :] = y
            st[0] = dec_f * st[0] + _mm((k * tail_f).T, v)
        for c in reversed(range(nch)):
            rows = slice(s * seq + c * ch, s * seq + (c + 1) * ch)
            q, k, v = q_ref[rows, :], k_ref[rows, :].astype(F32), v_ref[rows, :]
            if has_state or c < nch - 1:
                yacc[rows, :] = yacc[rows, :] + _mm(q.astype(F32) * e_b, st[1])
            st[1] = dec_b * st[1] + _mm((k * tail_b).T, v)
        if sfin_ref is not None:
            for d in (0, 1):
                sfin_ref[s, d, 0] = st[d]
        for c in range(nch):
            rows = slice(s * seq + c * ch, s * seq + (c + 1) * ch)
            y = yacc[rows, :]
            y = y * lax.rsqrt(jnp.mean(y * y, axis=-1, keepdims=True) + EPS)
            y_ref[rows, :] = (y * nw_ref[...] * _silu(g_ref[rows, :].astype(F32))).astype(y_ref.dtype)


def _retention(x, norm1_w, mod, w_in, ret_decay, norm_w, seq, s0=None, rope_tables=None):
    t, d = x.shape
    nseq = SEQ_BLOCK // seq
    nblk = t // SEQ_BLOCK
    per = 1 if mod.shape[0] > 1 else nblk
    dec = jnp.broadcast_to(jnp.pad(ret_decay.astype(F32).T, ((0, 0), (0, SUBLANES - 2)))[:, :, None],
                           (RET_HEADS, SUBLANES, RET_V_DIM))
    vspec = pl.BlockSpec((SEQ_BLOCK, RET_V_DIM), lambda i, h: (i, h))
    qk_blocks, v_blocks = RET_QK_W // RET_QK_DIM, RET_V_W // RET_V_DIM
    wq_spec = pl.BlockSpec((d, RET_QK_DIM), lambda i, h: (0, h))
    wk_spec = pl.BlockSpec((d, RET_QK_DIM), lambda i, h: (0, qk_blocks + h))
    wv_spec = pl.BlockSpec((d, RET_V_DIM), lambda i, h: (0, 2 * RET_QK_W // RET_V_DIM + h))
    wg_spec = pl.BlockSpec((d, RET_V_DIM), lambda i, h: (0, 2 * RET_QK_W // RET_V_DIM + v_blocks + h))
    in_specs = [pl.BlockSpec((SEQ_BLOCK, d), lambda i, h: (i, 0)), _const_spec((1, d)),
                pl.BlockSpec((1, 6, d), lambda i, h: (i // per, 0, 0)),
                wq_spec, wk_spec, wv_spec, wg_spec]
    args = [x, norm1_w.reshape(1, d), mod, w_in, w_in, w_in, w_in]
    if rope_tables is not None:
        in_specs += [_const_spec(rope_tables[0].shape)] * 2
        args += list(rope_tables)
    in_specs += [pl.BlockSpec((1, SUBLANES, RET_V_DIM), lambda i, h: (h, 0, 0)),
                 pl.BlockSpec((1, RET_V_DIM), lambda i, h: (0, h))]
    args += [dec, norm_w.astype(F32).reshape(1, -1)]
    out_specs = [vspec]
    out_shape = [jax.ShapeDtypeStruct((t, RET_V_W), BF16)]
    if s0 is not None:
        in_specs.append(pl.BlockSpec((1, 2, 1, RET_QK_DIM, RET_V_DIM), lambda i, h: (i, 0, h, 0, 0)))
        args.append(s0)
    else:
        out_specs.append(pl.BlockSpec((nseq, 2, 1, RET_QK_DIM, RET_V_DIM), lambda i, h: (i, 0, h, 0, 0)))
        out_shape.append(jax.ShapeDtypeStruct((t // seq, 2, RET_HEADS, RET_QK_DIM, RET_V_DIM), F32))
    return pl.pallas_call(
        functools.partial(_ret_kernel, seq=seq, has_state=s0 is not None, rope=rope_tables is not None),
        grid=(nblk, RET_HEADS),
        in_specs=in_specs,
        out_specs=out_specs,
        out_shape=out_shape,
        scratch_shapes=[pltpu.VMEM((SEQ_BLOCK, d), BF16),
                        pltpu.VMEM((SEQ_BLOCK, RET_QK_DIM), BF16), pltpu.VMEM((SEQ_BLOCK, RET_QK_DIM), BF16),
                        pltpu.VMEM((SEQ_BLOCK, RET_V_DIM), BF16), pltpu.VMEM((SEQ_BLOCK, RET_V_DIM), BF16),
                        pltpu.VMEM((SEQ_BLOCK, RET_V_DIM), F32),
                        pltpu.VMEM((2, RET_QK_DIM, RET_V_DIM), F32)],
        compiler_params=_params(dimension_semantics=("arbitrary", "arbitrary")),
        name="retention_mixer",
    )(*args)


def _out_ffn_kernel(*refs, n_mix, final_norm):
    x_ref = refs[0]
    mix_refs = refs[1:1 + n_mix]
    wo_ref, mod_ref, n2_ref, w1_ref, w3_ref, w2_ref = refs[1 + n_mix:7 + n_mix]
    if final_norm:
        fn_ref, o_ref, act = refs[7 + n_mix:]
    else:
        o_ref, act = refs[7 + n_mix:]
    mixed, row = None, 0
    for m_ref in mix_refs:
        k = m_ref.shape[1]
        term = jnp.dot(m_ref[...], wo_ref[row:row + k, :], preferred_element_type=F32)
        mixed = term if mixed is None else mixed + term
        row += k
    x1 = x_ref[...] + mod_ref[0, 2:3, :] * mixed
    hb = _norm_mod(x1, n2_ref[...], mod_ref, 3, 4).astype(BF16)
    for c in range(w1_ref.shape[1] // FFN_CHUNK):
        cols = slice(c * FFN_CHUNK, (c + 1) * FFN_CHUNK)
        h1 = jnp.dot(hb, w1_ref[:, cols], preferred_element_type=F32)
        h3 = jnp.dot(hb, w3_ref[:, cols], preferred_element_type=F32)
        act[:, cols] = (_silu(h1) * h3).astype(BF16)
    x2 = x1 + mod_ref[0, 5:6, :] * jnp.dot(act[...], w2_ref[...], preferred_element_type=F32)
    if final_norm:
        ms = jnp.mean(x2 * x2, axis=-1, keepdims=True)
        x2 = x2 * lax.rsqrt(ms + EPS) * fn_ref[...]
    o_ref[...] = x2


def _out_ffn(x, mixes, wo, mod, norm2_w, w1, w3, w2, rows_per_mod, final_norm_w=None):
    t, d = x.shape
    tm = ROW_TILE
    per = rows_per_mod // tm
    single = dict(pipeline_mode=pl.Buffered(1))
    in_specs = [pl.BlockSpec((tm, d), lambda i: (i, 0))]
    in_specs += [pl.BlockSpec((tm, m.shape[1]), lambda i: (i, 0)) for m in mixes]
    in_specs += [_const_spec(wo.shape, **single),
                 pl.BlockSpec((1, 6, d), lambda i: (i // per, 0, 0)), _const_spec((1, d))]
    in_specs += [_const_spec(w.shape, **single) for w in (w1, w3, w2)]
    args = [x, *mixes, wo, mod, norm2_w.reshape(1, d), w1, w3, w2]
    if final_norm_w is not None:
        in_specs.append(_const_spec((1, d)))
        args.append(final_norm_w.reshape(1, d))
    return pl.pallas_call(
        functools.partial(_out_ffn_kernel, n_mix=len(mixes), final_norm=final_norm_w is not None),
        grid=(t // tm,),
        in_specs=in_specs,
        out_specs=pl.BlockSpec((tm, d), lambda i: (i, 0)),
        out_shape=jax.ShapeDtypeStruct((t, d), F32),
        scratch_shapes=[pltpu.VMEM((tm, w1.shape[1]), BF16)],
        compiler_params=_params(dimension_semantics=("arbitrary",)),
        name="out_proj_ffn",
    )(*args)


def _rope_tables(seq):
    half = RET_QK_DIM // 2
    t = jnp.arange(seq)
    row = (t // GRID_W).astype(F32)
    col = (t % GRID_W).astype(F32)
    freqs = ROPE_BASE ** (-jnp.arange(0, half, 2, dtype=F32) / half)
    ang = jnp.concatenate([row[:, None] * freqs, col[:, None] * freqs], axis=-1)
    cos = jnp.repeat(jnp.cos(ang), 2, axis=1)
    sin = jnp.stack([-jnp.sin(ang), jnp.sin(ang)], axis=-1).reshape(seq, RET_QK_DIM)
    return cos, sin


def kernel(x_prompt, x_sample, cache_l0_na_k, cache_l0_na_v, state_l0_ssd, state_l1_ret, c, c_ctx,
           l0_norm1_w, l0_norm2_w, l0_mod_w, l0_mod_b, l0_w_in, l0_w_out, l0_na_bias, l0_conv_w, l0_conv_b,
           l0_ssd_a_log, l0_ssd_dt_bias, l0_ssd_d, l0_ssd_norm_w, l0_ffn_w1, l0_ffn_w3, l0_ffn_w2,
           l1_norm1_w, l1_norm2_w, l1_mod_w, l1_mod_b, l1_w_in, l1_w_out, l1_ret_decay, l1_ret_norm_w,
           l1_ffn_w1, l1_ffn_w3, l1_ffn_w2, final_norm_w):
    bc, lc, d = x_prompt.shape
    bl, ll, _ = x_sample.shape
    assert d == D_MODEL and ll == SEQ_BLOCK and SEQ_BLOCK % lc == 0 and bc % (SEQ_BLOCK // lc) == 0
    tc, tl = bc * lc, bl * ll
    xc = x_prompt.reshape(tc, d)
    xl = x_sample.reshape(tl, d)

    nrow = SUBLANES * ((1 + bl + SUBLANES - 1) // SUBLANES)
    cond = jnp.concatenate([c_ctx[None], c, jnp.zeros((nrow - 1 - bl, d), F32)], axis=0)
    mods = []
    for mod_w, mod_b in ((l0_mod_w, l0_mod_b), (l1_mod_w, l1_mod_b)):
        m = _adaln(cond, mod_w, mod_b).reshape(nrow, 6, d)
        mods.append((m[0:1], m[1:1 + bl]))

    (w_in0, wo0, ffn0_w1, ffn0_w3, ffn0_w2, w_in1, wo1, ffn1_w1, ffn1_w3, ffn1_w2) = _to_bf16(
        l0_w_in, l0_w_out, l0_ffn_w1, l0_ffn_w3, l0_ffn_w2, l1_w_in, l1_w_out, l1_ffn_w1, l1_ffn_w3, l1_ffn_w2)
    ffn0 = (ffn0_w1, ffn0_w3, ffn0_w2)
    ffn1 = (ffn1_w1, ffn1_w3, ffn1_w2)
    ssd_p = (l0_ssd_a_log, l0_ssd_dt_bias, l0_ssd_d, l0_ssd_norm_w)

    qc, kc, vc, zc, xsc, bcc, dtc = _inproj0(xc, l0_norm1_w, mods[0][0], w_in0, l0_conv_w, l0_conv_b, lc, F32)
    att_c, kc_heads, vc_heads = _na_ctx(qc, kc, vc, lc)
    ssd_c, sfin_c = _ssd(zc, xsc, bcc, dtc, *ssd_p, seq=lc)
    xc = _out_ffn(xc, (att_c, ssd_c), wo0, mods[0][0], l0_norm2_w, *ffn0, tc)

    ql, kl, vl, zl, xsl, bcl, dtl = _inproj0(xl, l0_norm1_w, mods[0][1], w_in0, l0_conv_w, l0_conv_b, ll, BF16)
    past = cache_l0_na_k.shape[1]
    att_l = _na_lat(ql, kl, vl, cache_l0_na_k.reshape(bl * past, NA_WIDTH),
                    cache_l0_na_v.reshape(bl * past, NA_WIDTH), _na_bias_table(l0_na_bias), ll)
    (ssd_l,) = _ssd(zl, xsl, bcl, dtl, *ssd_p, seq=ll, s0=state_l0_ssd)
    xl = _out_ffn(xl, (att_l, ssd_l), wo0, mods[0][1], l0_norm2_w, *ffn0, ll)

    ret_c, ret_state = _retention(xc, l1_norm1_w, mods[1][0], w_in1, l1_ret_decay, l1_ret_norm_w, lc)
    y_prompt = _out_ffn(xc, (ret_c,), wo1, mods[1][0], l1_norm2_w, *ffn1, tc, final_norm_w=final_norm_w)

    (ret_l,) = _retention(xl, l1_norm1_w, mods[1][1], w_in1, l1_ret_decay, l1_ret_norm_w, ll,
                          s0=state_l1_ret, rope_tables=_rope_tables(ll))
    y_sample = _out_ffn(xl, (ret_l,), wo1, mods[1][1], l1_norm2_w, *ffn1, ll, final_norm_w=final_norm_w)

    return (y_prompt.reshape(bc, lc, d), y_sample.reshape(bl, ll, d),
            kc_heads.reshape(bc, lc, NA_HEADS, NA_HEAD_DIM), vc_heads.reshape(bc, lc, NA_HEADS, NA_HEAD_DIM),
            sfin_c, ret_state)
```

```python
import functools

import numpy as np
import jax
import jax.numpy as jnp
from jax import lax
from jax.experimental import pallas as pl
from jax.experimental.pallas import tpu as pltpu

F32 = jnp.float32
BF16 = jnp.bfloat16

D_MODEL = 1024
GRID_W = 64
NA_HEADS = 8
NA_HEAD_DIM = 64
NA_WIDTH = NA_HEADS * NA_HEAD_DIM
NA_WIN_ROWS = 8
NA_WIN_COLS = 16
SSD_HEADS = 16
SSD_HEAD_DIM = 64
SSD_D_INNER = SSD_HEADS * SSD_HEAD_DIM
SSD_GROUPS = 2
SSD_STATE = 128
SSD_CONV = 5
SSD_XBC = SSD_D_INNER + 2 * SSD_GROUPS * SSD_STATE
RET_HEADS = 4
RET_QK_DIM = 256
RET_V_DIM = 512
RET_QK_W = RET_HEADS * RET_QK_DIM
RET_V_W = RET_HEADS * RET_V_DIM
ROPE_BASE = 10000.0
EPS = 1e-6

LANES = 128
SUBLANES = 8
SEQ_BLOCK = 1024
ROW_TILE = 512
SSD_CHUNK = 128
CONV_ROWS = 256
RET_CHUNK = 256
FFN_CHUNK = 256
VMEM_LIMIT = 56 * 1024 * 1024

_NEG_INF = float("-inf")


def _params(**kw):
    return pltpu.CompilerParams(vmem_limit_bytes=VMEM_LIMIT, **kw)


def _silu(x):
    return x * (1.0 / (1.0 + jnp.exp(-x)))


def _softplus(x):
    return jnp.maximum(x, 0.0) + jnp.log(1.0 + jnp.exp(-jnp.abs(x)))


def _mm(a, b):
    return jnp.dot(a.astype(BF16), b.astype(BF16), preferred_element_type=F32)


def _mm_nt(a, b):
    return lax.dot_general(a.astype(BF16), b.astype(BF16), (((1,), (1,)), ((), ())),
                           preferred_element_type=F32)


def _const_spec(shape, **kw):
    nd = len(shape)
    return pl.BlockSpec(shape, lambda *_: (0,) * nd, **kw)


def _cast_kernel(*refs):
    n = len(refs) // 2
    for i_ref, o_ref in zip(refs[:n], refs[n:]):
        o_ref[...] = i_ref[...].astype(o_ref.dtype)


def _to_bf16(*ws):
    steps = 8
    specs = [pl.BlockSpec((w.shape[0] // steps, w.shape[1]), lambda i: (i, 0)) for w in ws]
    assert all(w.shape[0] % (steps * 2 * SUBLANES) == 0 for w in ws)
    return pl.pallas_call(
        _cast_kernel,
        grid=(steps,),
        in_specs=specs,
        out_specs=specs,
        out_shape=[jax.ShapeDtypeStruct(w.shape, BF16) for w in ws],
        compiler_params=_params(dimension_semantics=("arbitrary",)),
        name="weights_to_bf16",
    )(*ws)


def _adaln_kernel(c_ref, w_ref, b_ref, o_ref):
    o_ref[...] = _mm(_silu(c_ref[...]), w_ref[...]) + b_ref[...]


def _adaln(cond, mod_w, mod_b):
    r, d = cond.shape
    n = mod_w.shape[1]
    tn = 1536
    return pl.pallas_call(
        _adaln_kernel,
        grid=(n // tn,),
        in_specs=[_const_spec((r, d)),
                  pl.BlockSpec((d, tn), lambda j: (0, j)),
                  pl.BlockSpec((1, tn), lambda j: (0, j))],
        out_specs=pl.BlockSpec((r, tn), lambda j: (0, j)),
        out_shape=jax.ShapeDtypeStruct((r, n), F32),
        compiler_params=_params(dimension_semantics=("arbitrary",)),
        name="adaln_mod",
    )(cond, mod_w, mod_b.reshape(1, n))


def _norm_mod(x, nw, mod_ref, shift_idx, scale_idx):
    ms = jnp.mean(x * x, axis=-1, keepdims=True)
    h = x * lax.rsqrt(ms + EPS) * nw
    return h * (1.0 + mod_ref[0, scale_idx:scale_idx + 1, :]) + mod_ref[0, shift_idx:shift_idx + 1, :]


def _inproj0_kernel(*refs, seq, halo):
    if halo:
        (x_ref, xp_ref, xn_ref, nw_ref, mod_ref, w_ref, cw_ref, cb_ref,
         q_ref, k_ref, v_ref, z_ref, xs_ref, bc_ref, dt_ref, xpad) = refs
    else:
        (x_ref, nw_ref, mod_ref, w_ref, cw_ref, cb_ref,
         q_ref, k_ref, v_ref, z_ref, xs_ref, bc_ref, dt_ref, xpad) = refs
    tm = x_ref.shape[0]
    pad = SUBLANES
    sub = xpad.shape[1] - 2 * pad
    h = _norm_mod(x_ref[...], nw_ref[...], mod_ref, 0, 1)
    hb = h.astype(BF16)
    xbc_col = 3 * NA_WIDTH + SSD_D_INNER
    w_xbc = w_ref[:, xbc_col:xbc_col + SSD_XBC]
    if halo:
        tiles_per_seq = seq // tm
        p = pl.program_id(0) % tiles_per_seq
        h_prev = jnp.where(p > 0, _norm_mod(xp_ref[...], nw_ref[...], mod_ref, 0, 1), 0.0)
        h_next = jnp.where(p < tiles_per_seq - 1, _norm_mod(xn_ref[...], nw_ref[...], mod_ref, 0, 1), 0.0)
        ext = jnp.concatenate([h_prev, h, h_next], axis=0).astype(BF16)
        xpad[0] = jnp.dot(ext, w_xbc, preferred_element_type=F32)
    else:
        xbc = jnp.dot(hb, w_xbc, preferred_element_type=F32)
        for s in range(tm // sub):
            xpad[s, 0:pad, :] = jnp.zeros((pad, SSD_XBC), F32)
            xpad[s, pad:pad + sub, :] = xbc[s * sub:(s + 1) * sub]
            xpad[s, pad + sub:2 * pad + sub, :] = jnp.zeros((pad, SSD_XBC), F32)

    cr = CONV_ROWS

    def conv_chunk(r0):
        s, rs = divmod(r0, sub)
        nwin = cr + 2 * pad
        for lt in range(SSD_XBC // LANES):
            cols = slice(lt * LANES, (lt + 1) * LANES)
            win = xpad[s, rs:rs + nwin, cols]
            taps = [cw_ref[k:k + 1, cols] * win for k in range(SSD_CONV)]
            before = taps[1] + pltpu.roll(taps[0], 1, 0)
            after = taps[3] + pltpu.roll(taps[4], nwin - 1, 0)
            conv = taps[2] + pltpu.roll(before, 1, 0) + pltpu.roll(after, nwin - 1, 0)
            act = _silu(conv[pad:pad + cr, :] + cb_ref[:, cols])
            if lt < SSD_D_INNER // LANES:
                xs_ref[r0:r0 + cr, cols] = act
            else:
                bc_ref[r0:r0 + cr, lt * LANES - SSD_D_INNER:(lt + 1) * LANES - SSD_D_INNER] = act.astype(BF16)

    chunks = list(range(0, tm, cr))
    col = 0
    for o_ref in (q_ref, k_ref, v_ref, z_ref):
        if chunks:
            conv_chunk(chunks.pop(0))
        n = o_ref.shape[1]
        o_ref[...] = jnp.dot(hb, w_ref[:, col:col + n], preferred_element_type=F32).astype(o_ref.dtype)
        col += n
    col += SSD_XBC
    n_dt = w_ref.shape[1] - col
    dt_ref[...] = jnp.zeros(dt_ref.shape, F32)
    dt_ref[:, 0:n_dt] = jnp.dot(hb, w_ref[:, col:col + n_dt], preferred_element_type=F32)
    for r0 in chunks:
        conv_chunk(r0)


def _inproj0(x, norm_w, mod, w, conv_w, conv_b, seq, qkv_dtype):
    t, d = x.shape
    tm = ROW_TILE
    assert seq % tm == 0 or tm % seq == 0
    halo = seq > tm
    sub = min(seq, tm)
    widths = (NA_WIDTH, NA_WIDTH, NA_WIDTH, SSD_D_INNER, SSD_D_INNER, SSD_XBC - SSD_D_INNER, LANES)
    dtypes = (qkv_dtype,) * 3 + (F32, F32, BF16, F32)
    assert sum(widths) - LANES + 2 * SSD_HEADS == w.shape[1]
    per = max(seq // tm, 1) if mod.shape[0] > 1 else t // tm
    in_specs = [pl.BlockSpec((tm, d), lambda i: (i, 0))]
    args = [x]
    if halo:
        rb = tm // SUBLANES
        last = t // SUBLANES - 1
        in_specs += [pl.BlockSpec((SUBLANES, d), lambda i: (jnp.maximum(i * rb - 1, 0), 0)),
                     pl.BlockSpec((SUBLANES, d), lambda i: (jnp.minimum((i + 1) * rb, last), 0))]
        args += [x, x]
    in_specs += [_const_spec((1, d)),
                 pl.BlockSpec((1, 6, d), lambda i: (i // per, 0, 0)),
                 _const_spec(w.shape), _const_spec((SUBLANES, SSD_XBC)), _const_spec((1, SSD_XBC))]
    args += [norm_w.reshape(1, d), mod, w,
             jnp.pad(conv_w.astype(F32), ((0, SUBLANES - SSD_CONV), (0, 0))), conv_b.astype(F32).reshape(1, -1)]
    return pl.pallas_call(
        functools.partial(_inproj0_kernel, seq=seq, halo=halo),
        grid=(t // tm,),
        in_specs=in_specs,
        out_specs=[pl.BlockSpec((tm, n), lambda i: (i, 0)) for n in widths],
        out_shape=[jax.ShapeDtypeStruct((t, n), dt) for n, dt in zip(widths, dtypes)],
        scratch_shapes=[pltpu.VMEM((tm // sub, sub + 2 * SUBLANES, SSD_XBC), F32)],
        compiler_params=_params(dimension_semantics=("arbitrary",)),
        name="l0_in_proj",
    )(*args)


def _swap_lane_pairs(x):
    even = (lax.broadcasted_iota(jnp.int32, x.shape, 1) & 1) == 0
    return jnp.where(even, pltpu.roll(x, LANES - 1, 1), pltpu.roll(x, 1, 1))


def _lane_lo(shape):
    return lax.broadcasted_iota(jnp.int32, shape, 1) < NA_HEAD_DIM


def _head_lanes(x, half):
    lo = _lane_lo(x.shape)
    return jnp.where(lo if half == 0 else jnp.logical_not(lo), x, jnp.zeros_like(x))


def _softmax_pv(q2, keys, vals, biases):
    acc = None
    for half in (0, 1):
        qm = _head_lanes(q2, half)
        scores = []
        for kk, bb in zip(keys, biases):
            s = _mm_nt(qm, kk)
            if bb is not None:
                s = bb(half, s)
            scores.append(s)
        mx = functools.reduce(jnp.maximum, [jnp.max(s, axis=-1, keepdims=True) for s in scores])
        es = [jnp.exp(s - mx) for s in scores]
        den = functools.reduce(jnp.add, [jnp.sum(e, axis=-1, keepdims=True) for e in es])
        pv = functools.reduce(jnp.add, [_mm(e, _head_lanes(vv, half)) for e, vv in zip(es, vals)])
        out = pv * (1.0 / den)
        acc = out if acc is None else acc + out
    return acc


def _na_ctx_kernel(q_ref, k_ref, v_ref, o_ref, kh_ref, vh_ref, *, seq):
    scale = NA_HEAD_DIM ** -0.5
    nseq = q_ref.shape[0] // seq
    for s in range(nseq):
        r = slice(s * seq, (s + 1) * seq)
        for hp in range(NA_WIDTH // LANES):
            c = slice(hp * LANES, (hp + 1) * LANES)
            out = _softmax_pv(q_ref[r, c] * scale, [k_ref[r, c]], [v_ref[r, c]], [None])
            o_ref[r, c] = out.astype(o_ref.dtype)
    kh_ref[...] = k_ref[...].reshape(kh_ref.shape)
    vh_ref[...] = v_ref[...].reshape(vh_ref.shape)


def _na_ctx(q, k, v, seq):
    t, w = q.shape
    spec = pl.BlockSpec((SEQ_BLOCK, w), lambda i: (i, 0))
    hspec = pl.BlockSpec((SEQ_BLOCK, NA_HEADS, NA_HEAD_DIM), lambda i: (i, 0, 0))
    hshape = jax.ShapeDtypeStruct((t, NA_HEADS, NA_HEAD_DIM), k.dtype)
    return pl.pallas_call(
        functools.partial(_na_ctx_kernel, seq=seq),
        grid=(t // SEQ_BLOCK,),
        in_specs=[spec, spec, spec],
        out_specs=[spec, hspec, hspec],
        out_shape=[jax.ShapeDtypeStruct((t, w), BF16), hshape, hshape],
        compiler_params=_params(dimension_semantics=("arbitrary",)),
        name="na_context",
    )(q, k, v)


NA_QBLK = 256


def _na_lat_kernel(q_ref, k_ref, v_ref, kc_ref, vc_ref, tab_ref, o_ref):
    scale = NA_HEAD_DIM ** -0.5
    seq = q_ref.shape[0]
    nblk = seq // NA_QBLK
    rows_per_blk = NA_QBLK // GRID_W
    grid_rows = seq // GRID_W
    n_off = 2 * NA_WIN_ROWS

    qc = lax.broadcasted_iota(jnp.int32, (GRID_W, LANES), 0)
    lane = lax.broadcasted_iota(jnp.int32, (GRID_W, LANES), 1)
    kc = lane & (GRID_W - 1)
    c0 = jnp.clip(qc - NA_WIN_COLS // 2, 0, GRID_W - NA_WIN_COLS)
    col_ok = (kc >= c0) & (kc < c0 + NA_WIN_COLS)
    first = lane < GRID_W
    pair_bias = [[jnp.where(col_ok,
                            pltpu.roll(jnp.broadcast_to(tab_ref[half, e:e + 1, :], (GRID_W, LANES)), 0, 1,
                                       stride=1, stride_axis=0),
                            _NEG_INF)
                  for e in range(n_off)] for half in (0, 1)]
    neg = jnp.full((GRID_W, LANES), _NEG_INF, F32)

    def band_start(r):
        return min(max(r - NA_WIN_ROWS // 2, 0), grid_rows - NA_WIN_ROWS)

    for i in range(nblk):
        ws_row = band_start(i * rows_per_blk) // 2 * 2
        we_row = min((band_start((i + 1) * rows_per_blk - 1) + NA_WIN_ROWS + 1) // 2 * 2, grid_rows)
        ws, kwin = ws_row * GRID_W, (we_row - ws_row) * GRID_W

        def add_bias(half, s, i=i, ws_row=ws_row, kwin=kwin):
            out_rows = []
            for rq in range(rows_per_blk):
                r = i * rows_per_blk + rq
                r0 = band_start(r)
                tiles = []
                for kp in range(kwin // LANES):
                    kr = ws_row + 2 * kp
                    ok0 = r0 <= kr < r0 + NA_WIN_ROWS
                    ok1 = r0 <= kr + 1 < r0 + NA_WIN_ROWS
                    if ok0 or ok1:
                        tile = pair_bias[half][kr - r + NA_WIN_ROWS]
                        if not ok1:
                            tile = jnp.where(first, tile, _NEG_INF)
                        elif not ok0:
                            tile = jnp.where(first, _NEG_INF, tile)
                    else:
                        tile = neg
                    tiles.append(s[rq * GRID_W:(rq + 1) * GRID_W, kp * LANES:(kp + 1) * LANES] + tile)
                out_rows.append(jnp.concatenate(tiles, axis=1))
            return jnp.concatenate(out_rows, axis=0)

        rows = slice(i * NA_QBLK, (i + 1) * NA_QBLK)
        kw = k_ref[ws:ws + kwin, :]
        vw = v_ref[ws:ws + kwin, :]
        out = _softmax_pv(q_ref[rows, :] * scale, [kw, kc_ref[...]], [vw, vc_ref[...]], [add_bias, None])
        o_ref[rows, :] = out.astype(o_ref.dtype)


def _na_bias_table(rel_bias):
    h, n_dr, n_dc = rel_bias.shape
    half_dc = n_dc // 2
    tz = jnp.pad(rel_bias.astype(F32), ((0, 0), (1, 1), (0, 0)))
    lo, hi = tz[:, 0:n_dr + 1], tz[:, 1:n_dr + 2]
    gap = jnp.zeros((h, n_dr + 1, GRID_W - half_dc - 1 - half_dc), F32)
    return jnp.concatenate([lo[..., half_dc:], gap, hi, gap, lo[..., :half_dc]], axis=-1)


def _na_lat(q, k, v, k_ctx, v_ctx, bias_table, seq):
    t, w = q.shape
    b = t // seq
    npair = w // LANES
    past = k_ctx.shape[0] // b
    kspec = pl.BlockSpec((seq, LANES), lambda hp, bb: (bb, hp))
    cspec = pl.BlockSpec((past, LANES), lambda hp, bb: (bb, hp))
    tspec = pl.BlockSpec((2,) + bias_table.shape[1:], lambda hp, bb: (hp, 0, 0))
    return pl.pallas_call(
        _na_lat_kernel,
        grid=(npair, b),
        in_specs=[kspec, kspec, kspec, cspec, cspec, tspec],
        out_specs=kspec,
        out_shape=jax.ShapeDtypeStruct((t, w), BF16),
        compiler_params=_params(dimension_semantics=("arbitrary", "arbitrary")),
        name="na_latent",
    )(q, k, v, k_ctx, v_ctx, bias_table)


def _pack3(v, lane):
    vm = jnp.where(lane < 32, v, 0.0)
    hi = vm.astype(BF16).astype(F32)
    r1 = vm - hi
    mid = r1.astype(BF16).astype(F32)
    lo = r1 - mid
    return (hi + pltpu.roll(mid, 32, 1) + pltpu.roll(lo, 64, 1)).astype(BF16)


def _unpack3(res):
    return res + pltpu.roll(res, 96, 1) + pltpu.roll(res, 64, 1)


def _ssd_selectors():
    r = np.arange(LANES)
    out = []
    for width in (SSD_HEAD_DIM, SSD_CHUNK):
        l = np.arange(SSD_HEADS * width)
        for d in (0, 1):
            sel = (r[:, None] < 96) & ((r[:, None] % 32) == d * SSD_HEADS + l[None, :] // width)
            out.append(jnp.asarray(sel, BF16))
    return out


def _ssd_kernel(*refs, seq, has_state):
    if has_state:
        (z_ref, xs_ref, bc_ref, dtr_ref, dtb_ref, alog_ref, dsk_ref, nw_ref,
         s64f_ref, s64b_ref, s128f_ref, s128b_ref, s0_ref,
         y_ref, yacc, csp, ep, wp, cst, dtt, st) = refs
        sfin_ref = None
    else:
        (z_ref, xs_ref, bc_ref, dtr_ref, dtb_ref, alog_ref, dsk_ref, nw_ref,
         s64f_ref, s64b_ref, s128f_ref, s128b_ref,
         y_ref, sfin_ref, yacc, csp, ep, wp, cst, dtt, st) = refs
        s0_ref = None
    ch = SSD_CHUNK
    nseq = z_ref.shape[0] // seq
    nch = seq // ch
    assert nch % 2 == 0
    lane = lax.broadcasted_iota(jnp.int32, (ch, LANES), 1)
    ri = lax.broadcasted_iota(jnp.int32, (ch, ch), 0)
    ci = lax.broadcasted_iota(jnp.int32, (ch, ch), 1)
    keeps = (ci <= ri, ci >= ri)
    tril = jnp.where(keeps[0], 1.0, 0.0).astype(BF16)
    triu = jnp.where(keeps[1], 1.0, 0.0).astype(BF16)
    lane_lo = lane < SSD_HEAD_DIM
    fwd_lane = lane < SSD_HEADS
    neg_a = -jnp.exp(alog_ref[...])
    gw = SSD_D_INNER // SSD_GROUPS
    sel64s = (s64f_ref, s64b_ref)
    sel128s = (s128f_ref, s128b_ref)
    edges = (ch - 1, 0)

    def seq_body(s, _):
        base = pl.multiple_of(s * seq, seq)

        def decay_body(c, _):
            r0 = pl.multiple_of(c * ch, ch)
            yacc[pl.ds(r0, ch), :] = dsk_ref[...] * xs_ref[pl.ds(base + r0, ch), :]
            dt = _softplus(dtr_ref[pl.ds(base + r0, ch), :] + dtb_ref[...])
            la = _pack3(dt * neg_a, lane)
            cs = jnp.where(fwd_lane,
                           _unpack3(jnp.dot(tril, la, preferred_element_type=F32)),
                           _unpack3(jnp.dot(triu, la, preferred_element_type=F32)))
            tot = jnp.where(fwd_lane, cs[ch - 1:ch, :], cs[0:1, :])
            rows = pl.ds(r0, ch)
            csp[rows, :] = _pack3(cs, lane)
            ep[rows, :] = _pack3(jnp.exp(cs), lane)
            wp[rows, :] = _pack3(dt * jnp.exp(tot - cs), lane)
            cst[c] = cs.T
            dtt[c] = dt.T
            return 0

        lax.fori_loop(0, nch, decay_body, 0, unroll=2)

        for d in (0, 1):
            if has_state:
                for h in range(SSD_HEADS):
                    st[d, :, h * SSD_HEAD_DIM:(h + 1) * SSD_HEAD_DIM] = s0_ref[0, d, h]
            else:
                st[d] = jnp.zeros((SSD_STATE, SSD_D_INNER), F32)

        def chunk(d, c):
            r0 = pl.multiple_of(c * ch, ch)
            rows = pl.ds(r0, ch)
            col = jnp.dot(csp[rows, :], sel128s[d][...], preferred_element_type=F32)
            e64 = jnp.dot(ep[rows, :], sel64s[d][...], preferred_element_type=F32)
            w64 = jnp.dot(wp[rows, :], sel64s[d][...], preferred_element_type=F32)
            cs_t = cst[c]
            dt_t = dtt[c]
            in_rows = pl.ds(base + r0, ch)
            x_c = xs_ref[in_rows, :]
            state = st[d]
            state_b = state.astype(BF16)
            xw = (x_c * w64).astype(BF16)
            y_intra, y_state, upd = [], [], []
            for g in range(SSD_GROUPS):
                b_g = bc_ref[in_rows, g * SSD_STATE:(g + 1) * SSD_STATE]
                cofs = SSD_GROUPS * SSD_STATE
                c_g = bc_ref[in_rows, cofs + g * SSD_STATE:cofs + (g + 1) * SSD_STATE]
                gram = _mm_nt(c_g, b_g)
                for pp in range(gw // LANES):
                    p = g * (gw // LANES) + pp
                    ws = []
                    for half in (0, 1):
                        h = 2 * p + half
                        hd = d * SSD_HEADS + h
                        diff = col[:, h * ch:(h + 1) * ch] - cs_t[hd:hd + 1, :]
                        wm = jnp.exp(jnp.where(keeps[d], diff, _NEG_INF)) * gram * dt_t[hd:hd + 1, :]
                        ws.append(wm.astype(BF16))
                    xp = x_c[:, p * LANES:(p + 1) * LANES]
                    xcat = jnp.concatenate([jnp.where(lane_lo, xp, 0.0), jnp.where(lane_lo, 0.0, xp)],
                                           axis=0).astype(BF16)
                    y_intra.append(jnp.dot(jnp.concatenate(ws, axis=1), xcat, preferred_element_type=F32))
                y_state.append(jnp.dot(c_g, state_b[:, g * gw:(g + 1) * gw], preferred_element_type=F32))
                upd.append(jnp.dot(b_g.T, xw[:, g * gw:(g + 1) * gw], preferred_element_type=F32))
            yacc[rows, :] = (yacc[rows, :] + jnp.concatenate(y_intra, axis=1)
                             + jnp.concatenate(y_state, axis=1) * e64)
            st[d] = e64[edges[d]:edges[d] + 1, :] * state + jnp.concatenate(upd, axis=1)

        def pair_body(j, _):
            for step in (0, 1):
                chunk(0, 2 * j + step)
                chunk(1, nch - 1 - 2 * j - step)
            return 0

        lax.fori_loop(0, nch // 2, pair_body, 0)
        if sfin_ref is not None:
            for d in (0, 1):
                for h in range(SSD_HEADS):
                    sfin_ref[s, d, h] = st[d, :, h * SSD_HEAD_DIM:(h + 1) * SSD_HEAD_DIM]

        def out_body(c, _):
            r0 = pl.multiple_of(c * ch, ch)
            y = yacc[pl.ds(r0, ch), :] * _silu(z_ref[pl.ds(base + r0, ch), :])
            outs = []
            for g in range(SSD_GROUPS):
                yg = y[:, g * gw:(g + 1) * gw]
                outs.append(yg * lax.rsqrt(jnp.mean(yg * yg, axis=-1, keepdims=True) + EPS))
            y_ref[pl.ds(base + r0, ch), :] = (jnp.concatenate(outs, axis=1) * nw_ref[...]).astype(y_ref.dtype)
            return 0

        lax.fori_loop(0, nch, out_body, 0)
        return 0

    lax.fori_loop(0, nseq, seq_body, 0)


def _ssd(z, xs, bc, dtr, a_log, dt_bias, d_skip, norm_w, seq, s0=None):
    t = z.shape[0]
    nseq = SEQ_BLOCK // seq
    nblk = t // SEQ_BLOCK
    nch = seq // SSD_CHUNK
    pad32 = lambda a: jnp.pad(a.astype(F32).reshape(1, -1), ((0, 0), (0, LANES - 2 * SSD_HEADS)))
    consts = [pad32(dt_bias), pad32(a_log),
              jnp.repeat(d_skip.astype(F32), SSD_HEAD_DIM).reshape(1, -1),
              norm_w.astype(F32).reshape(1, -1)] + _ssd_selectors()
    row = lambda n: pl.BlockSpec((SEQ_BLOCK, n), lambda i: (i, 0))
    in_specs = ([row(SSD_D_INNER), row(SSD_D_INNER), row(bc.shape[1]), row(LANES)]
                + [_const_spec(c.shape) for c in consts])
    args = [z, xs, bc, dtr] + consts
    out_specs = [row(SSD_D_INNER)]
    out_shape = [jax.ShapeDtypeStruct((t, SSD_D_INNER), BF16)]
    state_shape = (2, SSD_STATE, SSD_D_INNER)
    io_state = (2, SSD_HEADS, SSD_STATE, SSD_HEAD_DIM)
    if s0 is not None:
        in_specs.append(pl.BlockSpec((1,) + io_state, lambda i: (i, 0, 0, 0, 0)))
        args.append(s0)
    else:
        out_specs.append(pl.BlockSpec((nseq,) + io_state, lambda i: (i, 0, 0, 0, 0)))
        out_shape.append(jax.ShapeDtypeStruct((t // seq,) + io_state, F32))
    scratch = [pltpu.VMEM((seq, SSD_D_INNER), F32),
               pltpu.VMEM((seq, LANES), BF16), pltpu.VMEM((seq, LANES), BF16), pltpu.VMEM((seq, LANES), BF16),
               pltpu.VMEM((nch, LANES, SSD_CHUNK), F32), pltpu.VMEM((nch, LANES, SSD_CHUNK), F32),
               pltpu.VMEM(state_shape, F32)]
    return pl.pallas_call(
        functools.partial(_ssd_kernel, seq=seq, has_state=s0 is not None),
        grid=(nblk,),
        in_specs=in_specs,
        out_specs=out_specs,
        out_shape=out_shape,
        scratch_shapes=scratch,
        compiler_params=_params(dimension_semantics=("arbitrary",)),
        name="ssd_mixer",
    )(*args)


def _log_sigmoid(x):
    return -_softplus(-x)


def _ret_kernel(*refs, seq, has_state, rope):
    refs = list(refs)
    x_ref, n1_ref, mod_ref, wq_ref, wk_ref, wv_ref, wg_ref = refs[:7]
    del refs[:7]
    cos_ref, sin_ref = (refs.pop(0), refs.pop(0)) if rope else (None, None)
    dec_ref, nw_ref = refs.pop(0), refs.pop(0)
    s0_ref = refs.pop(0) if has_state else None
    y_ref = refs.pop(0)
    sfin_ref = None if has_state else refs.pop(0)
    hb_s, q_ref, k_ref, v_ref, g_ref, yacc, st = refs

    @pl.when(pl.program_id(1) == 0)
    def _():
        hb_s[...] = _norm_mod(x_ref[...], n1_ref[...], mod_ref, 0, 1).astype(BF16)

    hb = hb_s[...]
    for o_ref, w_ref, scale in ((q_ref, wq_ref, 1.0), (k_ref, wk_ref, RET_QK_DIM ** -0.5)):
        y = jnp.dot(hb, w_ref[...], preferred_element_type=F32) * scale
        if rope:
            for j in range(RET_QK_DIM // LANES):
                lanes = slice(j * LANES, (j + 1) * LANES)
                yj = y[:, lanes]
                rot = yj * cos_ref[:, lanes] + _swap_lane_pairs(yj) * sin_ref[:, lanes]
                o_ref[:, lanes] = rot.astype(o_ref.dtype)
        else:
            o_ref[...] = y.astype(o_ref.dtype)
    v_ref[...] = jnp.dot(hb, wv_ref[...], preferred_element_type=F32).astype(v_ref.dtype)
    g_ref[...] = jnp.dot(hb, wg_ref[...], preferred_element_type=F32).astype(g_ref.dtype)

    ch = RET_CHUNK
    nseq = q_ref.shape[0] // seq
    nch = seq // ch
    gf = _log_sigmoid(dec_ref[0, 0:1, :])
    gb = _log_sigmoid(dec_ref[0, 1:2, :])
    ri = lax.broadcasted_iota(jnp.int32, (ch, ch), 0)
    ci = lax.broadcasted_iota(jnp.int32, (ch, ch), 1)
    dist = (ri - ci).astype(F32)
    gfk, gbk = gf[:, :ch], gb[:, :ch]
    decay = (jnp.where(ci <= ri, jnp.exp(jnp.where(ci <= ri, dist, 0.0) * gfk), 0.0)
             + jnp.where(ci >= ri, jnp.exp(jnp.where(ci >= ri, -dist, 0.0) * gbk), 0.0))
    pos = lax.broadcasted_iota(jnp.int32, (ch, RET_QK_DIM), 0).astype(F32)
    gfq, gbq = gf[:, :RET_QK_DIM], gb[:, :RET_QK_DIM]
    e_f = jnp.exp((pos + 1.0) * gfq)
    e_b = jnp.exp((ch - pos) * gbq)
    tail_f = jnp.exp((ch - 1.0 - pos) * gfq)
    tail_b = jnp.exp(pos * gbq)
    dec_f = jnp.exp(ch * gf)
    dec_b = jnp.exp(ch * gb)

    for s in range(nseq):
        for d in (0, 1):
            if has_state:
                st[d] = s0_ref[0, d, 0]
            else:
                st[d] = jnp.zeros((RET_QK_DIM, RET_V_DIM), F32)
        for c in range(nch):
            rows = slice(s * seq + c * ch, s * seq + (c + 1) * ch)
            q, k, v = q_ref[rows, :], k_ref[rows, :].astype(F32), v_ref[rows, :]
            y = _mm(_mm_nt(q, k) * decay, v)
            if has_state or c > 0:
                y = y + _mm(q.astype(F32) * e_f, st[0])
            yacc[rows, :] = y
            st[0] = dec_f * st[0] + _mm((k * tail_f).T, v)
        for c in reversed(range(nch)):
            rows = slice(s * seq + c * ch, s * seq + (c + 1) * ch)
            q, k, v = q_ref[rows, :], k_ref[rows, :].astype(F32), v_ref[rows, :]
            if has_state or c < nch - 1:
                yacc[rows, :] = yacc[rows, :] + _mm(q.astype(F32) * e_b, st[1])
            st[1] = dec_b * st[1] + _mm((k * tail_b).T, v)
        if sfin_ref is not None:
            for d in (0, 1):
                sfin_ref[s, d, 0] = st[d]
        for c in range(nch):
            rows = slice(s * seq + c * ch, s * seq + (c + 1) * ch)
            y = yacc[rows, :]
            y = y * lax.rsqrt(jnp.mean(y * y, axis=-1, keepdims=True) + EPS)
            y_ref[rows, :] = (y * nw_ref[...] * _silu(g_ref[rows, :].astype(F32))).astype(y_ref.dtype)


def _retention(x, row0, t, norm1_w, mod, w_in, ret_decay, norm_w, seq, s0=None, rope_tables=None):
    d = x.shape[1]
    nseq = SEQ_BLOCK // seq
    nblk = t // SEQ_BLOCK
    blk0 = row0 // SEQ_BLOCK
    per = 1 if mod.shape[0] > 1 else nblk
    dec = jnp.broadcast_to(jnp.pad(ret_decay.astype(F32).T, ((0, 0), (0, SUBLANES - 2)))[:, :, None],
                           (RET_HEADS, SUBLANES, RET_V_DIM))
    vspec = pl.BlockSpec((SEQ_BLOCK, RET_V_DIM), lambda i, h: (i, h))
    qk_blocks, v_blocks = RET_QK_W // RET_QK_DIM, RET_V_W // RET_V_DIM
    wq_spec = pl.BlockSpec((d, RET_QK_DIM), lambda i, h: (0, h))
    wk_spec = pl.BlockSpec((d, RET_QK_DIM), lambda i, h: (0, qk_blocks + h))
    wv_spec = pl.BlockSpec((d, RET_V_DIM), lambda i, h: (0, 2 * RET_QK_W // RET_V_DIM + h))
    wg_spec = pl.BlockSpec((d, RET_V_DIM), lambda i, h: (0, 2 * RET_QK_W // RET_V_DIM + v_blocks + h))
    in_specs = [pl.BlockSpec((SEQ_BLOCK, d), lambda i, h: (blk0 + i, 0)), _const_spec((1, d)),
                pl.BlockSpec((1, 6, d), lambda i, h: (i // per, 0, 0)),
                wq_spec, wk_spec, wv_spec, wg_spec]
    args = [x, norm1_w.reshape(1, d), mod, w_in, w_in, w_in, w_in]
    if rope_tables is not None:
        in_specs += [_const_spec(rope_tables[0].shape)] * 2
        args += list(rope_tables)
    in_specs += [pl.BlockSpec((1, SUBLANES, RET_V_DIM), lambda i, h: (h, 0, 0)),
                 pl.BlockSpec((1, RET_V_DIM), lambda i, h: (0, h))]
    args += [dec, norm_w.astype(F32).reshape(1, -1)]
    out_specs = [vspec]
    out_shape = [jax.ShapeDtypeStruct((t, RET_V_W), BF16)]
    if s0 is not None:
        in_specs.append(pl.BlockSpec((1, 2, 1, RET_QK_DIM, RET_V_DIM), lambda i, h: (i, 0, h, 0, 0)))
        args.append(s0)
    else:
        out_specs.append(pl.BlockSpec((nseq, 2, 1, RET_QK_DIM, RET_V_DIM), lambda i, h: (i, 0, h, 0, 0)))
        out_shape.append(jax.ShapeDtypeStruct((t // seq, 2, RET_HEADS, RET_QK_DIM, RET_V_DIM), F32))
    return pl.pallas_call(
        functools.partial(_ret_kernel, seq=seq, has_state=s0 is not None, rope=rope_tables is not None),
        grid=(nblk, RET_HEADS),
        in_specs=in_specs,
        out_specs=out_specs,
        out_shape=out_shape,
        scratch_shapes=[pltpu.VMEM((SEQ_BLOCK, d), BF16),
                        pltpu.VMEM((SEQ_BLOCK, RET_QK_DIM), BF16), pltpu.VMEM((SEQ_BLOCK, RET_QK_DIM), BF16),
                        pltpu.VMEM((SEQ_BLOCK, RET_V_DIM), BF16), pltpu.VMEM((SEQ_BLOCK, RET_V_DIM), BF16),
                        pltpu.VMEM((SEQ_BLOCK, RET_V_DIM), F32),
                        pltpu.VMEM((2, RET_QK_DIM, RET_V_DIM), F32)],
        compiler_params=_params(dimension_semantics=("arbitrary", "arbitrary")),
        name="retention_mixer",
    )(*args)


def _out_ffn_kernel(*refs, n_x, n_mix, n_out, n_ctx_steps, final_norm):
    refs = list(refs)
    x_refs = [refs.pop(0) for _ in range(n_x)]
    mix_refs = [(refs.pop(0), refs.pop(0)) for _ in range(n_mix)]
    wo_ref, mod_ref, n2_ref, w1_ref, w3_ref, w2_ref = (refs.pop(0) for _ in range(6))
    fn_ref = refs.pop(0) if final_norm else None
    o_refs = [refs.pop(0) for _ in range(n_out)]
    (act,) = refs
    is_ctx = pl.program_id(0) < n_ctx_steps

    def pick(pair):
        return pair[0][...] if len(pair) == 1 else jnp.where(is_ctx, pair[0][...], pair[1][...])

    mixed, row = None, 0
    for pair in mix_refs:
        k = pair[0].shape[1]
        term = jnp.dot(pick(pair), wo_ref[row:row + k, :], preferred_element_type=F32)
        mixed = term if mixed is None else mixed + term
        row += k
    x1 = pick(x_refs) + mod_ref[0, 2:3, :] * mixed
    hb = _norm_mod(x1, n2_ref[...], mod_ref, 3, 4).astype(BF16)
    for c in range(w1_ref.shape[1] // FFN_CHUNK):
        cols = slice(c * FFN_CHUNK, (c + 1) * FFN_CHUNK)
        h1 = jnp.dot(hb, w1_ref[:, cols], preferred_element_type=F32)
        h3 = jnp.dot(hb, w3_ref[:, cols], preferred_element_type=F32)
        act[:, cols] = (_silu(h1) * h3).astype(BF16)
    x2 = x1 + mod_ref[0, 5:6, :] * jnp.dot(act[...], w2_ref[...], preferred_element_type=F32)
    if final_norm:
        ms = jnp.mean(x2 * x2, axis=-1, keepdims=True)
        x2 = x2 * lax.rsqrt(ms + EPS) * fn_ref[...]
    if n_out == 1:
        o_refs[0][...] = x2
    else:
        @pl.when(is_ctx)
        def _():
            o_refs[0][...] = x2

        @pl.when(jnp.logical_not(is_ctx))
        def _():
            o_refs[1][...] = x2


def _out_ffn(x, mixes, wo, mod, norm2_w, w1, w3, w2, t_ctx, t_lat, lat_seq, split_out, final_norm_w=None):
    tm = ROW_TILE
    d = wo.shape[1]
    n0, n1 = t_ctx // tm, t_lat // tm
    per = lat_seq // tm
    ctx_rows = lambda n: pl.BlockSpec((tm, n), lambda i: (jnp.minimum(i, n0 - 1), 0))
    lat_rows = lambda n: pl.BlockSpec((tm, n), lambda i: (jnp.maximum(i - n0, 0), 0))
    all_rows = lambda n: pl.BlockSpec((tm, n), lambda i: (i, 0))
    single = dict(pipeline_mode=pl.Buffered(1))
    x = tuple(x) if isinstance(x, (tuple, list)) else (x,)
    in_specs = [all_rows(d)] if len(x) == 1 else [ctx_rows(d), lat_rows(d)]
    args = list(x)
    for m_ctx, m_lat in mixes:
        in_specs += [ctx_rows(m_ctx.shape[1]), lat_rows(m_lat.shape[1])]
        args += [m_ctx, m_lat]
    in_specs += [_const_spec(wo.shape, **single),
                 pl.BlockSpec((1, 6, d), lambda i: (jnp.where(i < n0, 0, 1 + (i - n0) // per), 0, 0)),
                 _const_spec((1, d))]
    in_specs += [_const_spec(w.shape, **single) for w in (w1, w3, w2)]
    args += [wo, mod, norm2_w.reshape(1, d), w1, w3, w2]
    if final_norm_w is not None:
        in_specs.append(_const_spec((1, d)))
        args.append(final_norm_w.reshape(1, d))
    if split_out:
        out_specs = [ctx_rows(d), lat_rows(d)]
        out_shape = [jax.ShapeDtypeStruct((t_ctx, d), F32), jax.ShapeDtypeStruct((t_lat, d), F32)]
    else:
        out_specs = [all_rows(d)]
        out_shape = [jax.ShapeDtypeStruct((t_ctx + t_lat, d), F32)]
    out = pl.pallas_call(
        functools.partial(_out_ffn_kernel, n_x=len(x), n_mix=len(mixes), n_out=len(out_specs), n_ctx_steps=n0,
                          final_norm=final_norm_w is not None),
        grid=(n0 + n1,),
        in_specs=in_specs,
        out_specs=out_specs,
        out_shape=out_shape,
        scratch_shapes=[pltpu.VMEM((tm, w1.shape[1]), BF16)],
        compiler_params=_params(dimension_semantics=("arbitrary",)),
        name="out_proj_ffn",
    )(*args)
    return out if split_out else out[0]


def _rope_tables(seq):
    half = RET_QK_DIM // 2
    t = jnp.arange(seq)
    row = (t // GRID_W).astype(F32)
    col = (t % GRID_W).astype(F32)
    freqs = ROPE_BASE ** (-jnp.arange(0, half, 2, dtype=F32) / half)
    ang = jnp.concatenate([row[:, None] * freqs, col[:, None] * freqs], axis=-1)
    cos = jnp.repeat(jnp.cos(ang), 2, axis=1)
    sin = jnp.stack([-jnp.sin(ang), jnp.sin(ang)], axis=-1).reshape(seq, RET_QK_DIM)
    return cos, sin


def kernel(x_prompt, x_sample, cache_l0_na_k, cache_l0_na_v, state_l0_ssd, state_l1_ret, c, c_ctx,
           l0_norm1_w, l0_norm2_w, l0_mod_w, l0_mod_b, l0_w_in, l0_w_out, l0_na_bias, l0_conv_w, l0_conv_b,
           l0_ssd_a_log, l0_ssd_dt_bias, l0_ssd_d, l0_ssd_norm_w, l0_ffn_w1, l0_ffn_w3, l0_ffn_w2,
           l1_norm1_w, l1_norm2_w, l1_mod_w, l1_mod_b, l1_w_in, l1_w_out, l1_ret_decay, l1_ret_norm_w,
           l1_ffn_w1, l1_ffn_w3, l1_ffn_w2, final_norm_w):
    bc, lc, d = x_prompt.shape
    bl, ll, _ = x_sample.shape
    assert d == D_MODEL and ll == SEQ_BLOCK and SEQ_BLOCK % lc == 0 and bc % (SEQ_BLOCK // lc) == 0
    tc, tl = bc * lc, bl * ll
    xc = x_prompt.reshape(tc, d)
    xl = x_sample.reshape(tl, d)

    nrow = SUBLANES * ((1 + bl + SUBLANES - 1) // SUBLANES)
    cond = jnp.concatenate([c_ctx[None], c, jnp.zeros((nrow - 1 - bl, d), F32)], axis=0)
    mods = [_adaln(cond, mod_w, mod_b).reshape(nrow, 6, d)
            for mod_w, mod_b in ((l0_mod_w, l0_mod_b), (l1_mod_w, l1_mod_b))]
    mod_ctx = [m[0:1] for m in mods]
    mod_lat = [m[1:1 + bl] for m in mods]

    (w_in0, wo0, ffn0_w1, ffn0_w3, ffn0_w2, w_in1, wo1, ffn1_w1, ffn1_w3, ffn1_w2) = _to_bf16(
        l0_w_in, l0_w_out, l0_ffn_w1, l0_ffn_w3, l0_ffn_w2, l1_w_in, l1_w_out, l1_ffn_w1, l1_ffn_w3, l1_ffn_w2)
    ffn0 = (ffn0_w1, ffn0_w3, ffn0_w2)
    ffn1 = (ffn1_w1, ffn1_w3, ffn1_w2)
    ssd_p = (l0_ssd_a_log, l0_ssd_dt_bias, l0_ssd_d, l0_ssd_norm_w)

    qc, kc, vc, zc, xsc, bcc, dtc = _inproj0(xc, l0_norm1_w, mod_ctx[0], w_in0, l0_conv_w, l0_conv_b, lc, F32)
    att_c, kc_heads, vc_heads = _na_ctx(qc, kc, vc, lc)
    ssd_c, sfin_c = _ssd(zc, xsc, bcc, dtc, *ssd_p, seq=lc)

    ql, kl, vl, zl, xsl, bcl, dtl = _inproj0(xl, l0_norm1_w, mod_lat[0], w_in0, l0_conv_w, l0_conv_b, ll, BF16)
    past = cache_l0_na_k.shape[1]
    att_l = _na_lat(ql, kl, vl, cache_l0_na_k.reshape(bl * past, NA_WIDTH),
                    cache_l0_na_v.reshape(bl * past, NA_WIDTH), _na_bias_table(l0_na_bias), ll)
    (ssd_l,) = _ssd(zl, xsl, bcl, dtl, *ssd_p, seq=ll, s0=state_l0_ssd)

    x1 = _out_ffn((xc, xl), [(att_c, att_l), (ssd_c, ssd_l)], wo0, mods[0], l0_norm2_w, *ffn0,
                  tc, tl, ll, split_out=False)

    ret_c, ret_state = _retention(x1, 0, tc, l1_norm1_w, mod_ctx[1], w_in1, l1_ret_decay, l1_ret_norm_w, lc)
    (ret_l,) = _retention(x1, tc, tl, l1_norm1_w, mod_lat[1], w_in1, l1_ret_decay, l1_ret_norm_w, ll,
                          s0=state_l1_ret, rope_tables=_rope_tables(ll))
    y_prompt, y_sample = _out_ffn(x1, [(ret_c, ret_l)], wo1, mods[1], l1_norm2_w, *ffn1,
                                  tc, tl, ll, split_out=True, final_norm_w=final_norm_w)

    return (y_prompt.reshape(bc, lc, d), y_sample.reshape(bl, ll, d),
            kc_heads.reshape(bc, lc, NA_HEADS, NA_HEAD_DIM), vc_heads.reshape(bc, lc, NA_HEADS, NA_HEAD_DIM),
            sfin_c, ret_state)
```

```python
import functools

import numpy as np
import jax
import jax.numpy as jnp
from jax import lax
from jax.experimental import pallas as pl
from jax.experimental.pallas import tpu as pltpu

F32 = jnp.float32
BF16 = jnp.bfloat16

D_MODEL = 1024
GRID_W = 64
NA_HEADS = 8
NA_HEAD_DIM = 64
NA_WIDTH = NA_HEADS * NA_HEAD_DIM
NA_WIN_ROWS = 8
NA_WIN_COLS = 16
SSD_HEADS = 16
SSD_HEAD_DIM = 64
SSD_D_INNER = SSD_HEADS * SSD_HEAD_DIM
SSD_GROUPS = 2
SSD_STATE = 128
SSD_CONV = 5
SSD_XBC = SSD_D_INNER + 2 * SSD_GROUPS * SSD_STATE
RET_HEADS = 4
RET_QK_DIM = 256
RET_V_DIM = 512
RET_QK_W = RET_HEADS * RET_QK_DIM
RET_V_W = RET_HEADS * RET_V_DIM
ROPE_BASE = 10000.0
EPS = 1e-6

LANES = 128
SUBLANES = 8
SEQ_BLOCK = 1024
ROW_TILE = 512
SSD_CHUNK = 128
CONV_ROWS = 256
INTRA_UNROLL = 4
RET_CHUNK = 256
FFN_CHUNK = 256
VMEM_LIMIT = 56 * 1024 * 1024

_NEG_INF = float("-inf")


def _params(**kw):
    return pltpu.CompilerParams(vmem_limit_bytes=VMEM_LIMIT, **kw)


def _silu(x):
    return x * (1.0 / (1.0 + jnp.exp(-x)))


def _softplus(x):
    return jnp.maximum(x, 0.0) + jnp.log(1.0 + jnp.exp(-jnp.abs(x)))


def _mm(a, b):
    return jnp.dot(a.astype(BF16), b.astype(BF16), preferred_element_type=F32)


def _mm_nt(a, b):
    return lax.dot_general(a.astype(BF16), b.astype(BF16), (((1,), (1,)), ((), ())),
                           preferred_element_type=F32)


def _const_spec(shape, **kw):
    nd = len(shape)
    return pl.BlockSpec(shape, lambda *_: (0,) * nd, **kw)


def _cast_kernel(*refs):
    n = len(refs) // 2
    for i_ref, o_ref in zip(refs[:n], refs[n:]):
        o_ref[...] = i_ref[...].astype(o_ref.dtype)


def _to_bf16(*ws):
    steps = 8
    specs = [pl.BlockSpec((w.shape[0] // steps, w.shape[1]), lambda i: (i, 0)) for w in ws]
    assert all(w.shape[0] % (steps * 2 * SUBLANES) == 0 for w in ws)
    return pl.pallas_call(
        _cast_kernel,
        grid=(steps,),
        in_specs=specs,
        out_specs=specs,
        out_shape=[jax.ShapeDtypeStruct(w.shape, BF16) for w in ws],
        compiler_params=_params(dimension_semantics=("arbitrary",)),
        name="weights_to_bf16",
    )(*ws)


def _adaln_kernel(c_ref, w_ref, b_ref, o_ref):
    o_ref[...] = _mm(_silu(c_ref[...]), w_ref[...]) + b_ref[...]


def _adaln(cond, mod_w, mod_b):
    r, d = cond.shape
    n = mod_w.shape[1]
    tn = 1536
    return pl.pallas_call(
        _adaln_kernel,
        grid=(n // tn,),
        in_specs=[_const_spec((r, d)),
                  pl.BlockSpec((d, tn), lambda j: (0, j)),
                  pl.BlockSpec((1, tn), lambda j: (0, j))],
        out_specs=pl.BlockSpec((r, tn), lambda j: (0, j)),
        out_shape=jax.ShapeDtypeStruct((r, n), F32),
        compiler_params=_params(dimension_semantics=("arbitrary",)),
        name="adaln_mod",
    )(cond, mod_w, mod_b.reshape(1, n))


def _norm_mod(x, nw, mod_ref, shift_idx, scale_idx):
    ms = jnp.mean(x * x, axis=-1, keepdims=True)
    h = x * lax.rsqrt(ms + EPS) * nw
    return h * (1.0 + mod_ref[0, scale_idx:scale_idx + 1, :]) + mod_ref[0, shift_idx:shift_idx + 1, :]


def _inproj0_kernel(*refs, seq, halo):
    if halo:
        (x_ref, xp_ref, xn_ref, nw_ref, mod_ref, w_ref, cw_ref, cb_ref,
         q_ref, k_ref, v_ref, z_ref, xs_ref, bc_ref, dt_ref, xpad) = refs
    else:
        (x_ref, nw_ref, mod_ref, w_ref, cw_ref, cb_ref,
         q_ref, k_ref, v_ref, z_ref, xs_ref, bc_ref, dt_ref, xpad) = refs
    tm = x_ref.shape[0]
    pad = SUBLANES
    sub = xpad.shape[1] - 2 * pad
    h = _norm_mod(x_ref[...], nw_ref[...], mod_ref, 0, 1)
    hb = h.astype(BF16)
    xbc_col = 3 * NA_WIDTH + SSD_D_INNER
    w_xbc = w_ref[:, xbc_col:xbc_col + SSD_XBC]
    if halo:
        tiles_per_seq = seq // tm
        p = pl.program_id(0) % tiles_per_seq
        h_prev = jnp.where(p > 0, _norm_mod(xp_ref[...], nw_ref[...], mod_ref, 0, 1), 0.0)
        h_next = jnp.where(p < tiles_per_seq - 1, _norm_mod(xn_ref[...], nw_ref[...], mod_ref, 0, 1), 0.0)
        ext = jnp.concatenate([h_prev, h, h_next], axis=0).astype(BF16)
        xpad[0] = jnp.dot(ext, w_xbc, preferred_element_type=F32)
    else:
        xbc = jnp.dot(hb, w_xbc, preferred_element_type=F32)
        for s in range(tm // sub):
            xpad[s, 0:pad, :] = jnp.zeros((pad, SSD_XBC), F32)
            xpad[s, pad:pad + sub, :] = xbc[s * sub:(s + 1) * sub]
            xpad[s, pad + sub:2 * pad + sub, :] = jnp.zeros((pad, SSD_XBC), F32)

    cr = CONV_ROWS

    def conv_chunk(r0):
        s, rs = divmod(r0, sub)
        nwin = cr + 2 * pad
        for lt in range(SSD_XBC // LANES):
            cols = slice(lt * LANES, (lt + 1) * LANES)
            win = xpad[s, rs:rs + nwin, cols]
            taps = [cw_ref[k:k + 1, cols] * win for k in range(SSD_CONV)]
            before = taps[1] + pltpu.roll(taps[0], 1, 0)
            after = taps[3] + pltpu.roll(taps[4], nwin - 1, 0)
            conv = taps[2] + pltpu.roll(before, 1, 0) + pltpu.roll(after, nwin - 1, 0)
            act = _silu(conv[pad:pad + cr, :] + cb_ref[:, cols])
            if lt < SSD_D_INNER // LANES:
                xs_ref[r0:r0 + cr, cols] = act
            else:
                bc_ref[r0:r0 + cr, lt * LANES - SSD_D_INNER:(lt + 1) * LANES - SSD_D_INNER] = act.astype(BF16)

    chunks = list(range(0, tm, cr))
    col = 0
    for o_ref in (q_ref, k_ref, v_ref, z_ref):
        if chunks:
            conv_chunk(chunks.pop(0))
        n = o_ref.shape[1]
        o_ref[...] = jnp.dot(hb, w_ref[:, col:col + n], preferred_element_type=F32).astype(o_ref.dtype)
        col += n
    col += SSD_XBC
    n_dt = w_ref.shape[1] - col
    dt_ref[...] = jnp.zeros(dt_ref.shape, F32)
    dt_ref[:, 0:n_dt] = jnp.dot(hb, w_ref[:, col:col + n_dt], preferred_element_type=F32)
    for r0 in chunks:
        conv_chunk(r0)


def _inproj0(x, norm_w, mod, w, conv_w, conv_b, seq, qkv_dtype):
    t, d = x.shape
    tm = ROW_TILE
    assert seq % tm == 0 or tm % seq == 0
    halo = seq > tm
    sub = min(seq, tm)
    widths = (NA_WIDTH, NA_WIDTH, NA_WIDTH, SSD_D_INNER, SSD_D_INNER, SSD_XBC - SSD_D_INNER, LANES)
    dtypes = (qkv_dtype,) * 3 + (F32, F32, BF16, F32)
    assert sum(widths) - LANES + 2 * SSD_HEADS == w.shape[1]
    per = max(seq // tm, 1) if mod.shape[0] > 1 else t // tm
    in_specs = [pl.BlockSpec((tm, d), lambda i: (i, 0))]
    args = [x]
    if halo:
        rb = tm // SUBLANES
        last = t // SUBLANES - 1
        in_specs += [pl.BlockSpec((SUBLANES, d), lambda i: (jnp.maximum(i * rb - 1, 0), 0)),
                     pl.BlockSpec((SUBLANES, d), lambda i: (jnp.minimum((i + 1) * rb, last), 0))]
        args += [x, x]
    in_specs += [_const_spec((1, d)),
                 pl.BlockSpec((1, 6, d), lambda i: (i // per, 0, 0)),
                 _const_spec(w.shape), _const_spec((SUBLANES, SSD_XBC)), _const_spec((1, SSD_XBC))]
    args += [norm_w.reshape(1, d), mod, w,
             jnp.pad(conv_w.astype(F32), ((0, SUBLANES - SSD_CONV), (0, 0))), conv_b.astype(F32).reshape(1, -1)]
    return pl.pallas_call(
        functools.partial(_inproj0_kernel, seq=seq, halo=halo),
        grid=(t // tm,),
        in_specs=in_specs,
        out_specs=[pl.BlockSpec((tm, n), lambda i: (i, 0)) for n in widths],
        out_shape=[jax.ShapeDtypeStruct((t, n), dt) for n, dt in zip(widths, dtypes)],
        scratch_shapes=[pltpu.VMEM((tm // sub, sub + 2 * SUBLANES, SSD_XBC), F32)],
        compiler_params=_params(dimension_semantics=("arbitrary",)),
        name="l0_in_proj",
    )(*args)


def _swap_lane_pairs(x):
    even = (lax.broadcasted_iota(jnp.int32, x.shape, 1) & 1) == 0
    return jnp.where(even, pltpu.roll(x, LANES - 1, 1), pltpu.roll(x, 1, 1))


def _lane_lo(shape):
    return lax.broadcasted_iota(jnp.int32, shape, 1) < NA_HEAD_DIM


def _head_lanes(x, half):
    lo = _lane_lo(x.shape)
    return jnp.where(lo if half == 0 else jnp.logical_not(lo), x, jnp.zeros_like(x))


def _softmax_pv(q2, keys, vals, biases):
    acc = None
    for half in (0, 1):
        qm = _head_lanes(q2, half)
        scores = []
        for kk, bb in zip(keys, biases):
            s = _mm_nt(qm, kk)
            if bb is not None:
                s = bb(half, s)
            scores.append(s)
        mx = functools.reduce(jnp.maximum, [jnp.max(s, axis=-1, keepdims=True) for s in scores])
        es = [jnp.exp(s - mx) for s in scores]
        den = functools.reduce(jnp.add, [jnp.sum(e, axis=-1, keepdims=True) for e in es])
        pv = functools.reduce(jnp.add, [_mm(e, _head_lanes(vv, half)) for e, vv in zip(es, vals)])
        out = pv * (1.0 / den)
        acc = out if acc is None else acc + out
    return acc


def _na_ctx_kernel(q_ref, k_ref, v_ref, o_ref, kh_ref, vh_ref, *, seq):
    scale = NA_HEAD_DIM ** -0.5
    nseq = q_ref.shape[0] // seq
    for s in range(nseq):
        r = slice(s * seq, (s + 1) * seq)
        for hp in range(NA_WIDTH // LANES):
            c = slice(hp * LANES, (hp + 1) * LANES)
            out = _softmax_pv(q_ref[r, c] * scale, [k_ref[r, c]], [v_ref[r, c]], [None])
            o_ref[r, c] = out.astype(o_ref.dtype)
    kh_ref[...] = k_ref[...].reshape(kh_ref.shape)
    vh_ref[...] = v_ref[...].reshape(vh_ref.shape)


def _na_ctx(q, k, v, seq):
    t, w = q.shape
    spec = pl.BlockSpec((SEQ_BLOCK, w), lambda i: (i, 0))
    hspec = pl.BlockSpec((SEQ_BLOCK, NA_HEADS, NA_HEAD_DIM), lambda i: (i, 0, 0))
    hshape = jax.ShapeDtypeStruct((t, NA_HEADS, NA_HEAD_DIM), k.dtype)
    return pl.pallas_call(
        functools.partial(_na_ctx_kernel, seq=seq),
        grid=(t // SEQ_BLOCK,),
        in_specs=[spec, spec, spec],
        out_specs=[spec, hspec, hspec],
        out_shape=[jax.ShapeDtypeStruct((t, w), BF16), hshape, hshape],
        compiler_params=_params(dimension_semantics=("arbitrary",)),
        name="na_context",
    )(q, k, v)


NA_QBLK = 256


def _na_lat_kernel(q_ref, k_ref, v_ref, kc_ref, vc_ref, tab_ref, o_ref):
    scale = NA_HEAD_DIM ** -0.5
    seq = q_ref.shape[0]
    nblk = seq // NA_QBLK
    rows_per_blk = NA_QBLK // GRID_W
    grid_rows = seq // GRID_W
    n_off = 2 * NA_WIN_ROWS

    qc = lax.broadcasted_iota(jnp.int32, (GRID_W, LANES), 0)
    lane = lax.broadcasted_iota(jnp.int32, (GRID_W, LANES), 1)
    kc = lane & (GRID_W - 1)
    c0 = jnp.clip(qc - NA_WIN_COLS // 2, 0, GRID_W - NA_WIN_COLS)
    col_ok = (kc >= c0) & (kc < c0 + NA_WIN_COLS)
    first = lane < GRID_W
    pair_bias = [[jnp.where(col_ok,
                            pltpu.roll(jnp.broadcast_to(tab_ref[half, e:e + 1, :], (GRID_W, LANES)), 0, 1,
                                       stride=1, stride_axis=0),
                            _NEG_INF)
                  for e in range(n_off)] for half in (0, 1)]
    neg = jnp.full((GRID_W, LANES), _NEG_INF, F32)

    def band_start(r):
        return min(max(r - NA_WIN_ROWS // 2, 0), grid_rows - NA_WIN_ROWS)

    for i in range(nblk):
        ws_row = band_start(i * rows_per_blk) // 2 * 2
        we_row = min((band_start((i + 1) * rows_per_blk - 1) + NA_WIN_ROWS + 1) // 2 * 2, grid_rows)
        ws, kwin = ws_row * GRID_W, (we_row - ws_row) * GRID_W

        def add_bias(half, s, i=i, ws_row=ws_row, kwin=kwin):
            out_rows = []
            for rq in range(rows_per_blk):
                r = i * rows_per_blk + rq
                r0 = band_start(r)
                tiles = []
                for kp in range(kwin // LANES):
                    kr = ws_row + 2 * kp
                    ok0 = r0 <= kr < r0 + NA_WIN_ROWS
                    ok1 = r0 <= kr + 1 < r0 + NA_WIN_ROWS
                    if ok0 or ok1:
                        tile = pair_bias[half][kr - r + NA_WIN_ROWS]
                        if not ok1:
                            tile = jnp.where(first, tile, _NEG_INF)
                        elif not ok0:
                            tile = jnp.where(first, _NEG_INF, tile)
                    else:
                        tile = neg
                    tiles.append(s[rq * GRID_W:(rq + 1) * GRID_W, kp * LANES:(kp + 1) * LANES] + tile)
                out_rows.append(jnp.concatenate(tiles, axis=1))
            return jnp.concatenate(out_rows, axis=0)

        rows = slice(i * NA_QBLK, (i + 1) * NA_QBLK)
        kw = k_ref[ws:ws + kwin, :]
        vw = v_ref[ws:ws + kwin, :]
        out = _softmax_pv(q_ref[rows, :] * scale, [kw, kc_ref[...]], [vw, vc_ref[...]], [add_bias, None])
        o_ref[rows, :] = out.astype(o_ref.dtype)


def _na_bias_table(rel_bias):
    h, n_dr, n_dc = rel_bias.shape
    half_dc = n_dc // 2
    tz = jnp.pad(rel_bias.astype(F32), ((0, 0), (1, 1), (0, 0)))
    lo, hi = tz[:, 0:n_dr + 1], tz[:, 1:n_dr + 2]
    gap = jnp.zeros((h, n_dr + 1, GRID_W - half_dc - 1 - half_dc), F32)
    return jnp.concatenate([lo[..., half_dc:], gap, hi, gap, lo[..., :half_dc]], axis=-1)


def _na_lat(q, k, v, k_ctx, v_ctx, bias_table, seq):
    t, w = q.shape
    b = t // seq
    npair = w // LANES
    past = k_ctx.shape[0] // b
    kspec = pl.BlockSpec((seq, LANES), lambda hp, bb: (bb, hp))
    cspec = pl.BlockSpec((past, LANES), lambda hp, bb: (bb, hp))
    tspec = pl.BlockSpec((2,) + bias_table.shape[1:], lambda hp, bb: (hp, 0, 0))
    return pl.pallas_call(
        _na_lat_kernel,
        grid=(npair, b),
        in_specs=[kspec, kspec, kspec, cspec, cspec, tspec],
        out_specs=kspec,
        out_shape=jax.ShapeDtypeStruct((t, w), BF16),
        compiler_params=_params(dimension_semantics=("arbitrary", "arbitrary")),
        name="na_latent",
    )(q, k, v, k_ctx, v_ctx, bias_table)


def _pack3(v, lane):
    vm = jnp.where(lane < 32, v, 0.0)
    hi = vm.astype(BF16).astype(F32)
    r1 = vm - hi
    mid = r1.astype(BF16).astype(F32)
    lo = r1 - mid
    return (hi + pltpu.roll(mid, 32, 1) + pltpu.roll(lo, 64, 1)).astype(BF16)


def _unpack3(res):
    return res + pltpu.roll(res, 96, 1) + pltpu.roll(res, 64, 1)


def _ssd_selectors():
    r = np.arange(LANES)
    out = []
    for width in (SSD_HEAD_DIM, SSD_CHUNK):
        l = np.arange(SSD_HEADS * width)
        for d in (0, 1):
            sel = (r[:, None] < 96) & ((r[:, None] % 32) == d * SSD_HEADS + l[None, :] // width)
            out.append(jnp.asarray(sel, BF16))
    return out


def _ssd_kernel(*refs, seq, has_state):
    if has_state:
        (z_ref, xs_ref, bc_ref, dtr_ref, dtb_ref, alog_ref, dsk_ref, nw_ref,
         s64f_ref, s64b_ref, s128f_ref, s128b_ref, s0_ref,
         y_ref, yacc, ep, wp, bts, st) = refs
        sfin_ref = None
    else:
        (z_ref, xs_ref, bc_ref, dtr_ref, dtb_ref, alog_ref, dsk_ref, nw_ref,
         s64f_ref, s64b_ref, s128f_ref, s128b_ref,
         y_ref, sfin_ref, yacc, ep, wp, bts, st) = refs
        s0_ref = None
    ch = SSD_CHUNK
    nseq = z_ref.shape[0] // seq
    nch = seq // ch
    assert nch % 2 == 0
    lane = lax.broadcasted_iota(jnp.int32, (ch, LANES), 1)
    ri = lax.broadcasted_iota(jnp.int32, (ch, ch), 0)
    ci = lax.broadcasted_iota(jnp.int32, (ch, ch), 1)
    keeps = (ci <= ri, ci >= ri)
    tril = jnp.where(keeps[0], 1.0, 0.0).astype(BF16)
    triu = jnp.where(keeps[1], 1.0, 0.0).astype(BF16)
    lane_lo = lane < SSD_HEAD_DIM
    fwd_lane = lane < SSD_HEADS
    neg_a = -jnp.exp(alog_ref[...]) * np.log2(np.e).astype(np.float32)
    gw = SSD_D_INNER // SSD_GROUPS
    sel64s = (s64f_ref, s64b_ref)
    sel128s = (s128f_ref, s128b_ref)
    edges = (ch - 1, 0)

    def intra_body(c, _):
        rows = pl.ds(pl.multiple_of(c * ch, ch), ch)
        x_c = xs_ref[rows, :]
        dt = _softplus(dtr_ref[rows, :] + dtb_ref[...])
        la = _pack3(dt * neg_a, lane)
        cs = jnp.where(fwd_lane,
                       _unpack3(jnp.dot(tril, la, preferred_element_type=F32)),
                       _unpack3(jnp.dot(triu, la, preferred_element_type=F32)))
        tot = jnp.where(fwd_lane, cs[ch - 1:ch, :], cs[0:1, :])
        ep[rows, :] = _pack3(jnp.exp2(cs), lane)
        wp[rows, :] = _pack3(dt * jnp.exp2(tot - cs), lane)
        csp = _pack3(cs, lane)
        col = [jnp.dot(csp, sel128s[d][...], preferred_element_type=F32) for d in (0, 1)]
        cs_t = cs.T
        dt_t = dt.T
        y_intra = []
        for g in range(SSD_GROUPS):
            b_g = bc_ref[rows, g * SSD_STATE:(g + 1) * SSD_STATE]
            cofs = SSD_GROUPS * SSD_STATE
            c_g = bc_ref[rows, cofs + g * SSD_STATE:cofs + (g + 1) * SSD_STATE]
            bts[c, g] = b_g.T
            gram = _mm_nt(c_g, b_g)
            for pp in range(gw // LANES):
                p = g * (gw // LANES) + pp
                ws = []
                for d in (0, 1):
                    for half in (0, 1):
                        h = 2 * p + half
                        hd = d * SSD_HEADS + h
                        diff = col[d][:, h * ch:(h + 1) * ch] - cs_t[hd:hd + 1, :]
                        wm = jnp.exp2(jnp.where(keeps[d], diff, _NEG_INF)) * gram * dt_t[hd:hd + 1, :]
                        ws.append(wm.astype(BF16))
                xp = x_c[:, p * LANES:(p + 1) * LANES]
                xcat = jnp.concatenate([jnp.where(lane_lo, xp, 0.0), jnp.where(lane_lo, 0.0, xp)],
                                       axis=0).astype(BF16)
                y_intra.append(jnp.dot(jnp.concatenate(ws, axis=1), jnp.concatenate([xcat, xcat], axis=0),
                                       preferred_element_type=F32))
        yacc[rows, :] = dsk_ref[...] * x_c + jnp.concatenate(y_intra, axis=1)
        return 0

    lax.fori_loop(0, nseq * nch, intra_body, 0, unroll=INTRA_UNROLL)

    def seq_body(s, _):
        base = pl.multiple_of(s * seq, seq)
        for d in (0, 1):
            if has_state:
                for h in range(SSD_HEADS):
                    st[d, :, h * SSD_HEAD_DIM:(h + 1) * SSD_HEAD_DIM] = s0_ref[0, d, h]
            else:
                st[d] = jnp.zeros((SSD_STATE, SSD_D_INNER), F32)

        def chunk(d, c):
            rows = pl.ds(pl.multiple_of(base + c * ch, ch), ch)
            e64 = jnp.dot(ep[rows, :], sel64s[d][...], preferred_element_type=F32)
            w64 = jnp.dot(wp[rows, :], sel64s[d][...], preferred_element_type=F32)
            state = st[d]
            state_b = state.astype(BF16)
            xw = (xs_ref[rows, :] * w64).astype(BF16)
            cofs = SSD_GROUPS * SSD_STATE
            y_state = [jnp.dot(bc_ref[rows, cofs + g * SSD_STATE:cofs + (g + 1) * SSD_STATE],
                               state_b[:, g * gw:(g + 1) * gw], preferred_element_type=F32)
                       for g in range(SSD_GROUPS)]
            upd = [jnp.dot(bts[s * nch + c, g], xw[:, g * gw:(g + 1) * gw], preferred_element_type=F32)
                   for g in range(SSD_GROUPS)]
            yacc[rows, :] = yacc[rows, :] + jnp.concatenate(y_state, axis=1) * e64
            st[d] = e64[edges[d]:edges[d] + 1, :] * state + jnp.concatenate(upd, axis=1)

        def pair_body(j, _):
            for step in (0, 1):
                chunk(0, 2 * j + step)
                chunk(1, nch - 1 - 2 * j - step)
            return 0

        lax.fori_loop(0, nch // 2, pair_body, 0)
        if sfin_ref is not None:
            for d in (0, 1):
                for h in range(SSD_HEADS):
                    sfin_ref[s, d, h] = st[d, :, h * SSD_HEAD_DIM:(h + 1) * SSD_HEAD_DIM]
        return 0

    lax.fori_loop(0, nseq, seq_body, 0)

    def out_body(c, _):
        rows = pl.ds(pl.multiple_of(c * ch, ch), ch)
        y = yacc[rows, :] * _silu(z_ref[rows, :])
        outs = []
        for g in range(SSD_GROUPS):
            yg = y[:, g * gw:(g + 1) * gw]
            outs.append(yg * lax.rsqrt(jnp.mean(yg * yg, axis=-1, keepdims=True) + EPS))
        y_ref[rows, :] = (jnp.concatenate(outs, axis=1) * nw_ref[...]).astype(y_ref.dtype)
        return 0

    lax.fori_loop(0, nseq * nch, out_body, 0)


def _ssd(z, xs, bc, dtr, a_log, dt_bias, d_skip, norm_w, seq, s0=None):
    t = z.shape[0]
    nseq = SEQ_BLOCK // seq
    nblk = t // SEQ_BLOCK
    nch = seq // SSD_CHUNK
    pad32 = lambda a: jnp.pad(a.astype(F32).reshape(1, -1), ((0, 0), (0, LANES - 2 * SSD_HEADS)))
    consts = [pad32(dt_bias), pad32(a_log),
              jnp.repeat(d_skip.astype(F32), SSD_HEAD_DIM).reshape(1, -1),
              norm_w.astype(F32).reshape(1, -1)] + _ssd_selectors()
    row = lambda n: pl.BlockSpec((SEQ_BLOCK, n), lambda i: (i, 0))
    in_specs = ([row(SSD_D_INNER), row(SSD_D_INNER), row(bc.shape[1]), row(LANES)]
                + [_const_spec(c.shape) for c in consts])
    args = [z, xs, bc, dtr] + consts
    out_specs = [row(SSD_D_INNER)]
    out_shape = [jax.ShapeDtypeStruct((t, SSD_D_INNER), BF16)]
    state_shape = (2, SSD_STATE, SSD_D_INNER)
    io_state = (2, SSD_HEADS, SSD_STATE, SSD_HEAD_DIM)
    if s0 is not None:
        in_specs.append(pl.BlockSpec((1,) + io_state, lambda i: (i, 0, 0, 0, 0)))
        args.append(s0)
    else:
        out_specs.append(pl.BlockSpec((nseq,) + io_state, lambda i: (i, 0, 0, 0, 0)))
        out_shape.append(jax.ShapeDtypeStruct((t // seq,) + io_state, F32))
    nblk_ch = SEQ_BLOCK // SSD_CHUNK
    scratch = [pltpu.VMEM((SEQ_BLOCK, SSD_D_INNER), F32),
               pltpu.VMEM((SEQ_BLOCK, LANES), BF16), pltpu.VMEM((SEQ_BLOCK, LANES), BF16),
               pltpu.VMEM((nblk_ch, SSD_GROUPS, SSD_STATE, SSD_CHUNK), BF16),
               pltpu.VMEM(state_shape, F32)]
    return pl.pallas_call(
        functools.partial(_ssd_kernel, seq=seq, has_state=s0 is not None),
        grid=(nblk,),
        in_specs=in_specs,
        out_specs=out_specs,
        out_shape=out_shape,
        scratch_shapes=scratch,
        compiler_params=_params(dimension_semantics=("arbitrary",)),
        name="ssd_mixer",
    )(*args)


def _log_sigmoid(x):
    return -_softplus(-x)


def _ret_kernel(*refs, seq, has_state, rope):
    refs = list(refs)
    x_ref, n1_ref, mod_ref, wq_ref, wk_ref, wv_ref, wg_ref = refs[:7]
    del refs[:7]
    cos_ref, sin_ref = (refs.pop(0), refs.pop(0)) if rope else (None, None)
    dec_ref, nw_ref = refs.pop(0), refs.pop(0)
    s0_ref = refs.pop(0) if has_state else None
    y_ref = refs.pop(0)
    sfin_ref = None if has_state else refs.pop(0)
    hb_s, q_ref, k_ref, v_ref, g_ref, yacc, st = refs

    @pl.when(pl.program_id(1) == 0)
    def _():
        hb_s[...] = _norm_mod(x_ref[...], n1_ref[...], mod_ref, 0, 1).astype(BF16)

    hb = hb_s[...]
    for o_ref, w_ref, scale in ((q_ref, wq_ref, 1.0), (k_ref, wk_ref, RET_QK_DIM ** -0.5)):
        y = jnp.dot(hb, w_ref[...], preferred_element_type=F32) * scale
        if rope:
            for j in range(RET_QK_DIM // LANES):
                lanes = slice(j * LANES, (j + 1) * LANES)
                yj = y[:, lanes]
                rot = yj * cos_ref[:, lanes] + _swap_lane_pairs(yj) * sin_ref[:, lanes]
                o_ref[:, lanes] = rot.astype(o_ref.dtype)
        else:
            o_ref[...] = y.astype(o_ref.dtype)
    v_ref[...] = jnp.dot(hb, wv_ref[...], preferred_element_type=F32).astype(v_ref.dtype)
    g_ref[...] = jnp.dot(hb, wg_ref[...], preferred_element_type=F32).astype(g_ref.dtype)

    ch = RET_CHUNK
    nseq = q_ref.shape[0] // seq
    nch = seq // ch
    gf = _log_sigmoid(dec_ref[0, 0:1, :])
    gb = _log_sigmoid(dec_ref[0, 1:2, :])
    ri = lax.broadcasted_iota(jnp.int32, (ch, ch), 0)
    ci = lax.broadcasted_iota(jnp.int32, (ch, ch), 1)
    dist = (ri - ci).astype(F32)
    gfk, gbk = gf[:, :ch], gb[:, :ch]
    decay = (jnp.where(ci <= ri, jnp.exp(jnp.where(ci <= ri, dist, 0.0) * gfk), 0.0)
             + jnp.where(ci >= ri, jnp.exp(jnp.where(ci >= ri, -dist, 0.0) * gbk), 0.0))
    pos = lax.broadcasted_iota(jnp.int32, (ch, RET_QK_DIM), 0).astype(F32)
    gfq, gbq = gf[:, :RET_QK_DIM], gb[:, :RET_QK_DIM]
    e_f = jnp.exp((pos + 1.0) * gfq)
    e_b = jnp.exp((ch - pos) * gbq)
    tail_f = jnp.exp((ch - 1.0 - pos) * gfq)
    tail_b = jnp.exp(pos * gbq)
    dec_f = jnp.exp(ch * gf)
    dec_b = jnp.exp(ch * gb)

    for s in range(nseq):
        for d in (0, 1):
            if has_state:
                st[d] = s0_ref[0, d, 0]
            else:
                st[d] = jnp.zeros((RET_QK_DIM, RET_V_DIM), F32)
        for c in range(nch):
            rows = slice(s * seq + c * ch, s * seq + (c + 1) * ch)
            q, k, v = q_ref[rows, :], k_ref[rows, :].astype(F32), v_ref[rows, :]
            y = _mm(_mm_nt(q, k) * decay, v)
            if has_state or c > 0:
                y = y + _mm(q.astype(F32) * e_f, st[0])
            yacc[rows, :] = y
            st[0] = dec_f * st[0] + _mm((k * tail_f).T, v)
        for c in reversed(range(nch)):
            rows = slice(s * seq + c * ch, s * seq + (c + 1) * ch)
            q, k, v = q_ref[rows, :], k_ref[rows, :].astype(F32), v_ref[rows, :]
            if has_state or c < nch - 1:
                yacc[rows, :] = yacc[rows, :] + _mm(q.astype(F32) * e_b, st[1])
            st[1] = dec_b * st[1] + _mm((k * tail_b).T, v)
        if sfin_ref is not None:
            for d in (0, 1):
                sfin_ref[s, d, 0] = st[d]
        for c in range(nch):
            rows = slice(s * seq + c * ch, s * seq + (c + 1) * ch)
            y = yacc[rows, :]
            y = y * lax.rsqrt(jnp.mean(y * y, axis=-1, keepdims=True) + EPS)
            y_ref[rows, :] = (y * nw_ref[...] * _silu(g_ref[rows, :].astype(F32))).astype(y_ref.dtype)


def _retention(x, row0, t, norm1_w, mod, w_in, ret_decay, norm_w, seq, s0=None, rope_tables=None):
    d = x.shape[1]
    nseq = SEQ_BLOCK // seq
    nblk = t // SEQ_BLOCK
    blk0 = row0 // SEQ_BLOCK
    per = 1 if mod.shape[0] > 1 else nblk
    dec = jnp.broadcast_to(jnp.pad(ret_decay.astype(F32).T, ((0, 0), (0, SUBLANES - 2)))[:, :, None],
                           (RET_HEADS, SUBLANES, RET_V_DIM))
    vspec = pl.BlockSpec((SEQ_BLOCK, RET_V_DIM), lambda i, h: (i, h))
    qk_blocks, v_blocks = RET_QK_W // RET_QK_DIM, RET_V_W // RET_V_DIM
    wq_spec = pl.BlockSpec((d, RET_QK_DIM), lambda i, h: (0, h))
    wk_spec = pl.BlockSpec((d, RET_QK_DIM), lambda i, h: (0, qk_blocks + h))
    wv_spec = pl.BlockSpec((d, RET_V_DIM), lambda i, h: (0, 2 * RET_QK_W // RET_V_DIM + h))
    wg_spec = pl.BlockSpec((d, RET_V_DIM), lambda i, h: (0, 2 * RET_QK_W // RET_V_DIM + v_blocks + h))
    in_specs = [pl.BlockSpec((SEQ_BLOCK, d), lambda i, h: (blk0 + i, 0)), _const_spec((1, d)),
                pl.BlockSpec((1, 6, d), lambda i, h: (i // per, 0, 0)),
                wq_spec, wk_spec, wv_spec, wg_spec]
    args = [x, norm1_w.reshape(1, d), mod, w_in, w_in, w_in, w_in]
    if rope_tables is not None:
        in_specs += [_const_spec(rope_tables[0].shape)] * 2
        args += list(rope_tables)
    in_specs += [pl.BlockSpec((1, SUBLANES, RET_V_DIM), lambda i, h: (h, 0, 0)),
                 pl.BlockSpec((1, RET_V_DIM), lambda i, h: (0, h))]
    args += [dec, norm_w.astype(F32).reshape(1, -1)]
    out_specs = [vspec]
    out_shape = [jax.ShapeDtypeStruct((t, RET_V_W), BF16)]
    if s0 is not None:
        in_specs.append(pl.BlockSpec((1, 2, 1, RET_QK_DIM, RET_V_DIM), lambda i, h: (i, 0, h, 0, 0)))
        args.append(s0)
    else:
        out_specs.append(pl.BlockSpec((nseq, 2, 1, RET_QK_DIM, RET_V_DIM), lambda i, h: (i, 0, h, 0, 0)))
        out_shape.append(jax.ShapeDtypeStruct((t // seq, 2, RET_HEADS, RET_QK_DIM, RET_V_DIM), F32))
    return pl.pallas_call(
        functools.partial(_ret_kernel, seq=seq, has_state=s0 is not None, rope=rope_tables is not None),
        grid=(nblk, RET_HEADS),
        in_specs=in_specs,
        out_specs=out_specs,
        out_shape=out_shape,
        scratch_shapes=[pltpu.VMEM((SEQ_BLOCK, d), BF16),
                        pltpu.VMEM((SEQ_BLOCK, RET_QK_DIM), BF16), pltpu.VMEM((SEQ_BLOCK, RET_QK_DIM), BF16),
                        pltpu.VMEM((SEQ_BLOCK, RET_V_DIM), BF16), pltpu.VMEM((SEQ_BLOCK, RET_V_DIM), BF16),
                        pltpu.VMEM((SEQ_BLOCK, RET_V_DIM), F32),
                        pltpu.VMEM((2, RET_QK_DIM, RET_V_DIM), F32)],
        compiler_params=_params(dimension_semantics=("arbitrary", "arbitrary")),
        name="retention_mixer",
    )(*args)


def _out_ffn_kernel(*refs, n_x, n_mix, n_out, n_ctx_steps, final_norm):
    refs = list(refs)
    x_refs = [refs.pop(0) for _ in range(n_x)]
    mix_refs = [(refs.pop(0), refs.pop(0)) for _ in range(n_mix)]
    wo_ref, mod_ref, n2_ref, w1_ref, w3_ref, w2_ref = (refs.pop(0) for _ in range(6))
    fn_ref = refs.pop(0) if final_norm else None
    o_refs = [refs.pop(0) for _ in range(n_out)]
    (act,) = refs
    is_ctx = pl.program_id(0) < n_ctx_steps

    def pick(pair):
        return pair[0][...] if len(pair) == 1 else jnp.where(is_ctx, pair[0][...], pair[1][...])

    mixed, row = None, 0
    for pair in mix_refs:
        k = pair[0].shape[1]
        term = jnp.dot(pick(pair), wo_ref[row:row + k, :], preferred_element_type=F32)
        mixed = term if mixed is None else mixed + term
        row += k
    x1 = pick(x_refs) + mod_ref[0, 2:3, :] * mixed
    hb = _norm_mod(x1, n2_ref[...], mod_ref, 3, 4).astype(BF16)
    for c in range(w1_ref.shape[1] // FFN_CHUNK):
        cols = slice(c * FFN_CHUNK, (c + 1) * FFN_CHUNK)
        h1 = jnp.dot(hb, w1_ref[:, cols], preferred_element_type=F32)
        h3 = jnp.dot(hb, w3_ref[:, cols], preferred_element_type=F32)
        act[:, cols] = (_silu(h1) * h3).astype(BF16)
    x2 = x1 + mod_ref[0, 5:6, :] * jnp.dot(act[...], w2_ref[...], preferred_element_type=F32)
    if final_norm:
        ms = jnp.mean(x2 * x2, axis=-1, keepdims=True)
        x2 = x2 * lax.rsqrt(ms + EPS) * fn_ref[...]
    if n_out == 1:
        o_refs[0][...] = x2
    else:
        @pl.when(is_ctx)
        def _():
            o_refs[0][...] = x2

        @pl.when(jnp.logical_not(is_ctx))
        def _():
            o_refs[1][...] = x2


def _out_ffn(x, mixes, wo, mod, norm2_w, w1, w3, w2, t_ctx, t_lat, lat_seq, split_out, final_norm_w=None):
    tm = ROW_TILE
    d = wo.shape[1]
    n0, n1 = t_ctx // tm, t_lat // tm
    per = lat_seq // tm
    ctx_rows = lambda n: pl.BlockSpec((tm, n), lambda i: (jnp.minimum(i, n0 - 1), 0))
    lat_rows = lambda n: pl.BlockSpec((tm, n), lambda i: (jnp.maximum(i - n0, 0), 0))
    all_rows = lambda n: pl.BlockSpec((tm, n), lambda i: (i, 0))
    single = dict(pipeline_mode=pl.Buffered(1))
    x = tuple(x) if isinstance(x, (tuple, list)) else (x,)
    in_specs = [all_rows(d)] if len(x) == 1 else [ctx_rows(d), lat_rows(d)]
    args = list(x)
    for m_ctx, m_lat in mixes:
        in_specs += [ctx_rows(m_ctx.shape[1]), lat_rows(m_lat.shape[1])]
        args += [m_ctx, m_lat]
    in_specs += [_const_spec(wo.shape, **single),
                 pl.BlockSpec((1, 6, d), lambda i: (jnp.where(i < n0, 0, 1 + (i - n0) // per), 0, 0)),
                 _const_spec((1, d))]
    in_specs += [_const_spec(w.shape, **single) for w in (w1, w3, w2)]
    args += [wo, mod, norm2_w.reshape(1, d), w1, w3, w2]
    if final_norm_w is not None:
        in_specs.append(_const_spec((1, d)))
        args.append(final_norm_w.reshape(1, d))
    if split_out:
        out_specs = [ctx_rows(d), lat_rows(d)]
        out_shape = [jax.ShapeDtypeStruct((t_ctx, d), F32), jax.ShapeDtypeStruct((t_lat, d), F32)]
    else:
        out_specs = [all_rows(d)]
        out_shape = [jax.ShapeDtypeStruct((t_ctx + t_lat, d), F32)]
    out = pl.pallas_call(
        functools.partial(_out_ffn_kernel, n_x=len(x), n_mix=len(mixes), n_out=len(out_specs), n_ctx_steps=n0,
                          final_norm=final_norm_w is not None),
        grid=(n0 + n1,),
        in_specs=in_specs,
        out_specs=out_specs,
        out_shape=out_shape,
        scratch_shapes=[pltpu.VMEM((tm, w1.shape[1]), BF16)],
        compiler_params=_params(dimension_semantics=("arbitrary",)),
        name="out_proj_ffn",
    )(*args)
    return out if split_out else out[0]


def _rope_tables(seq):
    half = RET_QK_DIM // 2
    t = jnp.arange(seq)
    row = (t // GRID_W).astype(F32)
    col = (t % GRID_W).astype(F32)
    freqs = ROPE_BASE ** (-jnp.arange(0, half, 2, dtype=F32) / half)
    ang = jnp.concatenate([row[:, None] * freqs, col[:, None] * freqs], axis=-1)
    cos = jnp.repeat(jnp.cos(ang), 2, axis=1)
    sin = jnp.stack([-jnp.sin(ang), jnp.sin(ang)], axis=-1).reshape(seq, RET_QK_DIM)
    return cos, sin


def kernel(x_prompt, x_sample, cache_l0_na_k, cache_l0_na_v, state_l0_ssd, state_l1_ret, c, c_ctx,
           l0_norm1_w, l0_norm2_w, l0_mod_w, l0_mod_b, l0_w_in, l0_w_out, l0_na_bias, l0_conv_w, l0_conv_b,
           l0_ssd_a_log, l0_ssd_dt_bias, l0_ssd_d, l0_ssd_norm_w, l0_ffn_w1, l0_ffn_w3, l0_ffn_w2,
           l1_norm1_w, l1_norm2_w, l1_mod_w, l1_mod_b, l1_w_in, l1_w_out, l1_ret_decay, l1_ret_norm_w,
           l1_ffn_w1, l1_ffn_w3, l1_ffn_w2, final_norm_w):
    bc, lc, d = x_prompt.shape
    bl, ll, _ = x_sample.shape
    assert d == D_MODEL and ll == SEQ_BLOCK and SEQ_BLOCK % lc == 0 and bc % (SEQ_BLOCK // lc) == 0
    tc, tl = bc * lc, bl * ll
    xc = x_prompt.reshape(tc, d)
    xl = x_sample.reshape(tl, d)

    nrow = SUBLANES * ((1 + bl + SUBLANES - 1) // SUBLANES)
    cond = jnp.concatenate([c_ctx[None], c, jnp.zeros((nrow - 1 - bl, d), F32)], axis=0)
    mods = [_adaln(cond, mod_w, mod_b).reshape(nrow, 6, d)
            for mod_w, mod_b in ((l0_mod_w, l0_mod_b), (l1_mod_w, l1_mod_b))]
    mod_ctx = [m[0:1] for m in mods]
    mod_lat = [m[1:1 + bl] for m in mods]

    (w_in0, wo0, ffn0_w1, ffn0_w3, ffn0_w2, w_in1, wo1, ffn1_w1, ffn1_w3, ffn1_w2) = _to_bf16(
        l0_w_in, l0_w_out, l0_ffn_w1, l0_ffn_w3, l0_ffn_w2, l1_w_in, l1_w_out, l1_ffn_w1, l1_ffn_w3, l1_ffn_w2)
    ffn0 = (ffn0_w1, ffn0_w3, ffn0_w2)
    ffn1 = (ffn1_w1, ffn1_w3, ffn1_w2)
    ssd_p = (l0_ssd_a_log, l0_ssd_dt_bias, l0_ssd_d, l0_ssd_norm_w)

    qc, kc, vc, zc, xsc, bcc, dtc = _inproj0(xc, l0_norm1_w, mod_ctx[0], w_in0, l0_conv_w, l0_conv_b, lc, F32)
    att_c, kc_heads, vc_heads = _na_ctx(qc, kc, vc, lc)
    ssd_c, sfin_c = _ssd(zc, xsc, bcc, dtc, *ssd_p, seq=lc)

    ql, kl, vl, zl, xsl, bcl, dtl = _inproj0(xl, l0_norm1_w, mod_lat[0], w_in0, l0_conv_w, l0_conv_b, ll, BF16)
    past = cache_l0_na_k.shape[1]
    att_l = _na_lat(ql, kl, vl, cache_l0_na_k.reshape(bl * past, NA_WIDTH),
                    cache_l0_na_v.reshape(bl * past, NA_WIDTH), _na_bias_table(l0_na_bias), ll)
    (ssd_l,) = _ssd(zl, xsl, bcl, dtl, *ssd_p, seq=ll, s0=state_l0_ssd)

    x1 = _out_ffn((xc, xl), [(att_c, att_l), (ssd_c, ssd_l)], wo0, mods[0], l0_norm2_w, *ffn0,
                  tc, tl, ll, split_out=False)

    ret_c, ret_state = _retention(x1, 0, tc, l1_norm1_w, mod_ctx[1], w_in1, l1_ret_decay, l1_ret_norm_w, lc)
    (ret_l,) = _retention(x1, tc, tl, l1_norm1_w, mod_lat[1], w_in1, l1_ret_decay, l1_ret_norm_w, ll,
                          s0=state_l1_ret, rope_tables=_rope_tables(ll))
    y_prompt, y_sample = _out_ffn(x1, [(ret_c, ret_l)], wo1, mods[1], l1_norm2_w, *ffn1,
                                  tc, tl, ll, split_out=True, final_norm_w=final_norm_w)

    return (y_prompt.reshape(bc, lc, d), y_sample.reshape(bl, ll, d),
            kc_heads.reshape(bc, lc, NA_HEADS, NA_HEAD_DIM), vc_heads.reshape(bc, lc, NA_HEADS, NA_HEAD_DIM),
            sfin_c, ret_state)
```

```python
import functools

import numpy as np
import jax
import jax.numpy as jnp
from jax import lax
from jax.experimental import pallas as pl
from jax.experimental.pallas import tpu as pltpu

F32 = jnp.float32
BF16 = jnp.bfloat16

D_MODEL = 1024
GRID_W = 64
NA_HEADS = 8
NA_HEAD_DIM = 64
NA_WIDTH = NA_HEADS * NA_HEAD_DIM
NA_WIN_ROWS = 8
NA_WIN_COLS = 16
SSD_HEADS = 16
SSD_HEAD_DIM = 64
SSD_D_INNER = SSD_HEADS * SSD_HEAD_DIM
SSD_GROUPS = 2
SSD_STATE = 128
SSD_CONV = 5
SSD_XBC = SSD_D_INNER + 2 * SSD_GROUPS * SSD_STATE
RET_HEADS = 4
RET_QK_DIM = 256
RET_V_DIM = 512
RET_QK_W = RET_HEADS * RET_QK_DIM
RET_V_W = RET_HEADS * RET_V_DIM
ROPE_BASE = 10000.0
EPS = 1e-6

LANES = 128
SUBLANES = 8
SEQ_BLOCK = 1024
ROW_TILE = 512
SSD_CHUNK = 128
CONV_ROWS = 256
INTRA_UNROLL = 4
PIECE_LANES = 2 * SSD_HEADS
assert 3 * PIECE_LANES <= LANES
CAST_STEPS = 8
ADALN_TILE = 1536
RET_CHUNK = 256
FFN_CHUNK = 256
VMEM_LIMIT = 56 * 1024 * 1024

_NEG_INF = float("-inf")


def _params(**kw):
    return pltpu.CompilerParams(vmem_limit_bytes=VMEM_LIMIT, **kw)


def _silu(x):
    return x * (1.0 / (1.0 + jnp.exp(-x)))


def _softplus(x):
    return jnp.maximum(x, 0.0) + jnp.log(1.0 + jnp.exp(-jnp.abs(x)))


def _mm(a, b):
    return jnp.dot(a.astype(BF16), b.astype(BF16), preferred_element_type=F32)


def _mm_nt(a, b):
    return lax.dot_general(a.astype(BF16), b.astype(BF16), (((1,), (1,)), ((), ())),
                           preferred_element_type=F32)


def _const_spec(shape, **kw):
    nd = len(shape)
    return pl.BlockSpec(shape, lambda *_: (0,) * nd, **kw)


def _cast_kernel(*refs):
    n = len(refs) // 2
    for i_ref, o_ref in zip(refs[:n], refs[n:]):
        o_ref[...] = i_ref[...].astype(o_ref.dtype)


def _to_bf16(*ws):
    steps = CAST_STEPS
    specs = [pl.BlockSpec((w.shape[0] // steps, w.shape[1]), lambda i: (i, 0)) for w in ws]
    assert all(w.shape[0] % (steps * 2 * SUBLANES) == 0 for w in ws)
    return pl.pallas_call(
        _cast_kernel,
        grid=(steps,),
        in_specs=specs,
        out_specs=specs,
        out_shape=[jax.ShapeDtypeStruct(w.shape, BF16) for w in ws],
        compiler_params=_params(dimension_semantics=("arbitrary",)),
        name="weights_to_bf16",
    )(*ws)


def _adaln_kernel(c_ref, w_ref, b_ref, o_ref):
    o_ref[...] = _mm(_silu(c_ref[...]), w_ref[...]) + b_ref[...]


def _adaln(cond, mod_w, mod_b):
    r, d = cond.shape
    n = mod_w.shape[1]
    tn = ADALN_TILE
    assert n % tn == 0
    return pl.pallas_call(
        _adaln_kernel,
        grid=(n // tn,),
        in_specs=[_const_spec((r, d)),
                  pl.BlockSpec((d, tn), lambda j: (0, j)),
                  pl.BlockSpec((1, tn), lambda j: (0, j))],
        out_specs=pl.BlockSpec((r, tn), lambda j: (0, j)),
        out_shape=jax.ShapeDtypeStruct((r, n), F32),
        compiler_params=_params(dimension_semantics=("arbitrary",)),
        name="adaln_mod",
    )(cond, mod_w, mod_b.reshape(1, n))


def _norm_mod(x, nw, mod_ref, shift_idx, scale_idx):
    ms = jnp.mean(x * x, axis=-1, keepdims=True)
    h = x * lax.rsqrt(ms + EPS) * nw
    return h * (1.0 + mod_ref[0, scale_idx:scale_idx + 1, :]) + mod_ref[0, shift_idx:shift_idx + 1, :]


def _inproj0_kernel(*refs, seq, halo):
    if halo:
        (x_ref, xp_ref, xn_ref, nw_ref, mod_ref, w_ref, cw_ref, cb_ref,
         q_ref, k_ref, v_ref, z_ref, xs_ref, bc_ref, dt_ref, xpad) = refs
    else:
        (x_ref, nw_ref, mod_ref, w_ref, cw_ref, cb_ref,
         q_ref, k_ref, v_ref, z_ref, xs_ref, bc_ref, dt_ref, xpad) = refs
    tm = x_ref.shape[0]
    pad = SUBLANES
    sub = xpad.shape[1] - 2 * pad
    h = _norm_mod(x_ref[...], nw_ref[...], mod_ref, 0, 1)
    hb = h.astype(BF16)
    xbc_col = 3 * NA_WIDTH + SSD_D_INNER
    w_xbc = w_ref[:, xbc_col:xbc_col + SSD_XBC]
    if halo:
        tiles_per_seq = seq // tm
        p = pl.program_id(0) % tiles_per_seq
        h_prev = jnp.where(p > 0, _norm_mod(xp_ref[...], nw_ref[...], mod_ref, 0, 1), 0.0)
        h_next = jnp.where(p < tiles_per_seq - 1, _norm_mod(xn_ref[...], nw_ref[...], mod_ref, 0, 1), 0.0)
        ext = jnp.concatenate([h_prev, h, h_next], axis=0).astype(BF16)
        xpad[0] = jnp.dot(ext, w_xbc, preferred_element_type=F32)
    else:
        xbc = jnp.dot(hb, w_xbc, preferred_element_type=F32)
        for s in range(tm // sub):
            xpad[s, 0:pad, :] = jnp.zeros((pad, SSD_XBC), F32)
            xpad[s, pad:pad + sub, :] = xbc[s * sub:(s + 1) * sub]
            xpad[s, pad + sub:2 * pad + sub, :] = jnp.zeros((pad, SSD_XBC), F32)

    cr = CONV_ROWS

    def conv_chunk(r0):
        s, rs = divmod(r0, sub)
        nwin = cr + 2 * pad
        for lt in range(SSD_XBC // LANES):
            cols = slice(lt * LANES, (lt + 1) * LANES)
            win = xpad[s, rs:rs + nwin, cols]
            taps = [cw_ref[k:k + 1, cols] * win for k in range(SSD_CONV)]
            before = taps[1] + pltpu.roll(taps[0], 1, 0)
            after = taps[3] + pltpu.roll(taps[4], nwin - 1, 0)
            conv = taps[2] + pltpu.roll(before, 1, 0) + pltpu.roll(after, nwin - 1, 0)
            act = _silu(conv[pad:pad + cr, :] + cb_ref[:, cols])
            if lt < SSD_D_INNER // LANES:
                xs_ref[r0:r0 + cr, cols] = act
            else:
                bc_ref[r0:r0 + cr, lt * LANES - SSD_D_INNER:(lt + 1) * LANES - SSD_D_INNER] = act.astype(BF16)

    chunks = list(range(0, tm, cr))
    col = 0
    for o_ref in (q_ref, k_ref, v_ref, z_ref):
        if chunks:
            conv_chunk(chunks.pop(0))
        n = o_ref.shape[1]
        o_ref[...] = jnp.dot(hb, w_ref[:, col:col + n], preferred_element_type=F32).astype(o_ref.dtype)
        col += n
    col += SSD_XBC
    n_dt = w_ref.shape[1] - col
    dt_ref[...] = jnp.zeros(dt_ref.shape, F32)
    dt_ref[:, 0:n_dt] = jnp.dot(hb, w_ref[:, col:col + n_dt], preferred_element_type=F32)
    for r0 in chunks:
        conv_chunk(r0)


def _inproj0(x, norm_w, mod, w, conv_w, conv_b, seq, qkv_dtype):
    t, d = x.shape
    tm = ROW_TILE
    assert seq % tm == 0 or tm % seq == 0
    halo = seq > tm
    sub = min(seq, tm)
    widths = (NA_WIDTH, NA_WIDTH, NA_WIDTH, SSD_D_INNER, SSD_D_INNER, SSD_XBC - SSD_D_INNER, LANES)
    dtypes = (qkv_dtype,) * 3 + (F32, F32, BF16, F32)
    assert sum(widths) - LANES + 2 * SSD_HEADS == w.shape[1]
    per = max(seq // tm, 1) if mod.shape[0] > 1 else t // tm
    in_specs = [pl.BlockSpec((tm, d), lambda i: (i, 0))]
    args = [x]
    if halo:
        rb = tm // SUBLANES
        last = t // SUBLANES - 1
        in_specs += [pl.BlockSpec((SUBLANES, d), lambda i: (jnp.maximum(i * rb - 1, 0), 0)),
                     pl.BlockSpec((SUBLANES, d), lambda i: (jnp.minimum((i + 1) * rb, last), 0))]
        args += [x, x]
    in_specs += [_const_spec((1, d)),
                 pl.BlockSpec((1, 6, d), lambda i: (i // per, 0, 0)),
                 _const_spec(w.shape), _const_spec((SUBLANES, SSD_XBC)), _const_spec((1, SSD_XBC))]
    args += [norm_w.reshape(1, d), mod, w,
             jnp.pad(conv_w.astype(F32), ((0, SUBLANES - SSD_CONV), (0, 0))), conv_b.astype(F32).reshape(1, -1)]
    return pl.pallas_call(
        functools.partial(_inproj0_kernel, seq=seq, halo=halo),
        grid=(t // tm,),
        in_specs=in_specs,
        out_specs=[pl.BlockSpec((tm, n), lambda i: (i, 0)) for n in widths],
        out_shape=[jax.ShapeDtypeStruct((t, n), dt) for n, dt in zip(widths, dtypes)],
        scratch_shapes=[pltpu.VMEM((tm // sub, sub + 2 * SUBLANES, SSD_XBC), F32)],
        compiler_params=_params(dimension_semantics=("arbitrary",)),
        name="l0_in_proj",
    )(*args)


def _swap_lane_pairs(x):
    even = (lax.broadcasted_iota(jnp.int32, x.shape, 1) & 1) == 0
    return jnp.where(even, pltpu.roll(x, LANES - 1, 1), pltpu.roll(x, 1, 1))


def _lane_lo(shape):
    return lax.broadcasted_iota(jnp.int32, shape, 1) < NA_HEAD_DIM


def _head_lanes(x, half):
    lo = _lane_lo(x.shape)
    return jnp.where(lo if half == 0 else jnp.logical_not(lo), x, jnp.zeros_like(x))


def _softmax_pv(q2, keys, vals, biases):
    acc = None
    for half in (0, 1):
        qm = _head_lanes(q2, half)
        scores = []
        for kk, bb in zip(keys, biases):
            s = _mm_nt(qm, kk)
            if bb is not None:
                s = bb(half, s)
            scores.append(s)
        mx = functools.reduce(jnp.maximum, [jnp.max(s, axis=-1, keepdims=True) for s in scores])
        es = [jnp.exp(s - mx) for s in scores]
        den = functools.reduce(jnp.add, [jnp.sum(e, axis=-1, keepdims=True) for e in es])
        pv = functools.reduce(jnp.add, [_mm(e, _head_lanes(vv, half)) for e, vv in zip(es, vals)])
        out = pv * (1.0 / den)
        acc = out if acc is None else acc + out
    return acc


def _na_ctx_kernel(q_ref, k_ref, v_ref, o_ref, kh_ref, vh_ref, *, seq):
    scale = NA_HEAD_DIM ** -0.5
    nseq = q_ref.shape[0] // seq
    for s in range(nseq):
        r = slice(s * seq, (s + 1) * seq)
        for hp in range(NA_WIDTH // LANES):
            c = slice(hp * LANES, (hp + 1) * LANES)
            out = _softmax_pv(q_ref[r, c] * scale, [k_ref[r, c]], [v_ref[r, c]], [None])
            o_ref[r, c] = out.astype(o_ref.dtype)
    kh_ref[...] = k_ref[...].reshape(kh_ref.shape)
    vh_ref[...] = v_ref[...].reshape(vh_ref.shape)


def _na_ctx(q, k, v, seq):
    t, w = q.shape
    spec = pl.BlockSpec((SEQ_BLOCK, w), lambda i: (i, 0))
    hspec = pl.BlockSpec((SEQ_BLOCK, NA_HEADS, NA_HEAD_DIM), lambda i: (i, 0, 0))
    hshape = jax.ShapeDtypeStruct((t, NA_HEADS, NA_HEAD_DIM), k.dtype)
    return pl.pallas_call(
        functools.partial(_na_ctx_kernel, seq=seq),
        grid=(t // SEQ_BLOCK,),
        in_specs=[spec, spec, spec],
        out_specs=[spec, hspec, hspec],
        out_shape=[jax.ShapeDtypeStruct((t, w), BF16), hshape, hshape],
        compiler_params=_params(dimension_semantics=("arbitrary",)),
        name="na_context",
    )(q, k, v)


NA_QBLK = 256


def _na_lat_kernel(q_ref, k_ref, v_ref, kc_ref, vc_ref, tab_ref, o_ref):
    scale = NA_HEAD_DIM ** -0.5
    seq = q_ref.shape[0]
    nblk = seq // NA_QBLK
    rows_per_blk = NA_QBLK // GRID_W
    grid_rows = seq // GRID_W
    n_off = 2 * NA_WIN_ROWS

    qc = lax.broadcasted_iota(jnp.int32, (GRID_W, LANES), 0)
    lane = lax.broadcasted_iota(jnp.int32, (GRID_W, LANES), 1)
    kc = lane & (GRID_W - 1)
    c0 = jnp.clip(qc - NA_WIN_COLS // 2, 0, GRID_W - NA_WIN_COLS)
    col_ok = (kc >= c0) & (kc < c0 + NA_WIN_COLS)
    first = lane < GRID_W
    pair_bias = [[jnp.where(col_ok,
                            pltpu.roll(jnp.broadcast_to(tab_ref[half, e:e + 1, :], (GRID_W, LANES)), 0, 1,
                                       stride=1, stride_axis=0),
                            _NEG_INF)
                  for e in range(n_off)] for half in (0, 1)]
    neg = jnp.full((GRID_W, LANES), _NEG_INF, F32)

    def band_start(r):
        return min(max(r - NA_WIN_ROWS // 2, 0), grid_rows - NA_WIN_ROWS)

    for i in range(nblk):
        ws_row = band_start(i * rows_per_blk) // 2 * 2
        we_row = min((band_start((i + 1) * rows_per_blk - 1) + NA_WIN_ROWS + 1) // 2 * 2, grid_rows)
        ws, kwin = ws_row * GRID_W, (we_row - ws_row) * GRID_W

        def add_bias(half, s, i=i, ws_row=ws_row, kwin=kwin):
            out_rows = []
            for rq in range(rows_per_blk):
                r = i * rows_per_blk + rq
                r0 = band_start(r)
                tiles = []
                for kp in range(kwin // LANES):
                    kr = ws_row + 2 * kp
                    ok0 = r0 <= kr < r0 + NA_WIN_ROWS
                    ok1 = r0 <= kr + 1 < r0 + NA_WIN_ROWS
                    if ok0 or ok1:
                        tile = pair_bias[half][kr - r + NA_WIN_ROWS]
                        if not ok1:
                            tile = jnp.where(first, tile, _NEG_INF)
                        elif not ok0:
                            tile = jnp.where(first, _NEG_INF, tile)
                    else:
                        tile = neg
                    tiles.append(s[rq * GRID_W:(rq + 1) * GRID_W, kp * LANES:(kp + 1) * LANES] + tile)
                out_rows.append(jnp.concatenate(tiles, axis=1))
            return jnp.concatenate(out_rows, axis=0)

        rows = slice(i * NA_QBLK, (i + 1) * NA_QBLK)
        kw = k_ref[ws:ws + kwin, :]
        vw = v_ref[ws:ws + kwin, :]
        out = _softmax_pv(q_ref[rows, :] * scale, [kw, kc_ref[...]], [vw, vc_ref[...]], [add_bias, None])
        o_ref[rows, :] = out.astype(o_ref.dtype)


def _na_bias_table(rel_bias):
    h, n_dr, n_dc = rel_bias.shape
    half_dc = n_dc // 2
    tz = jnp.pad(rel_bias.astype(F32), ((0, 0), (1, 1), (0, 0)))
    lo, hi = tz[:, 0:n_dr + 1], tz[:, 1:n_dr + 2]
    gap = jnp.zeros((h, n_dr + 1, GRID_W - half_dc - 1 - half_dc), F32)
    return jnp.concatenate([lo[..., half_dc:], gap, hi, gap, lo[..., :half_dc]], axis=-1)


def _na_lat(q, k, v, k_ctx, v_ctx, bias_table, seq):
    t, w = q.shape
    b = t // seq
    npair = w // LANES
    past = k_ctx.shape[0] // b
    kspec = pl.BlockSpec((seq, LANES), lambda hp, bb: (bb, hp))
    cspec = pl.BlockSpec((past, LANES), lambda hp, bb: (bb, hp))
    tspec = pl.BlockSpec((2,) + bias_table.shape[1:], lambda hp, bb: (hp, 0, 0))
    return pl.pallas_call(
        _na_lat_kernel,
        grid=(npair, b),
        in_specs=[kspec, kspec, kspec, cspec, cspec, tspec],
        out_specs=kspec,
        out_shape=jax.ShapeDtypeStruct((t, w), BF16),
        compiler_params=_params(dimension_semantics=("arbitrary", "arbitrary")),
        name="na_latent",
    )(q, k, v, k_ctx, v_ctx, bias_table)


def _pack3(v, lane):
    vm = jnp.where(lane < PIECE_LANES, v, 0.0)
    hi = vm.astype(BF16).astype(F32)
    r1 = vm - hi
    mid = r1.astype(BF16).astype(F32)
    lo = r1 - mid
    return (hi + pltpu.roll(mid, PIECE_LANES, 1) + pltpu.roll(lo, 2 * PIECE_LANES, 1)).astype(BF16)


def _unpack3(res):
    return res + pltpu.roll(res, LANES - PIECE_LANES, 1) + pltpu.roll(res, LANES - 2 * PIECE_LANES, 1)


def _ssd_selectors():
    r = np.arange(LANES)
    out = []
    for width in (SSD_HEAD_DIM, SSD_CHUNK):
        l = np.arange(SSD_HEADS * width)
        for d in (0, 1):
            sel = (r[:, None] < 3 * PIECE_LANES) & ((r[:, None] % PIECE_LANES) == d * SSD_HEADS + l[None, :] // width)
            out.append(jnp.asarray(sel, BF16))
    return out


def _ssd_kernel(*refs, seq, has_state):
    if has_state:
        (z_ref, xs_ref, bc_ref, dtr_ref, dtb_ref, alog_ref, dsk_ref, nw_ref,
         s64f_ref, s64b_ref, s128f_ref, s128b_ref, s0_ref,
         y_ref, yacc, ep, wp, bts, st) = refs
        sfin_ref = None
    else:
        (z_ref, xs_ref, bc_ref, dtr_ref, dtb_ref, alog_ref, dsk_ref, nw_ref,
         s64f_ref, s64b_ref, s128f_ref, s128b_ref,
         y_ref, sfin_ref, yacc, ep, wp, bts, st) = refs
        s0_ref = None
    ch = SSD_CHUNK
    nseq = z_ref.shape[0] // seq
    nch = seq // ch
    assert nch % 2 == 0
    lane = lax.broadcasted_iota(jnp.int32, (ch, LANES), 1)
    ri = lax.broadcasted_iota(jnp.int32, (ch, ch), 0)
    ci = lax.broadcasted_iota(jnp.int32, (ch, ch), 1)
    keeps = (ci <= ri, ci >= ri)
    tril = jnp.where(keeps[0], 1.0, 0.0).astype(BF16)
    triu = jnp.where(keeps[1], 1.0, 0.0).astype(BF16)
    lane_lo = lane < SSD_HEAD_DIM
    fwd_lane = lane < SSD_HEADS
    neg_a = -jnp.exp(alog_ref[...]) * np.log2(np.e).astype(np.float32)
    gw = SSD_D_INNER // SSD_GROUPS
    sel64s = (s64f_ref, s64b_ref)
    sel128s = (s128f_ref, s128b_ref)
    edges = (ch - 1, 0)

    def intra_body(c, _):
        rows = pl.ds(pl.multiple_of(c * ch, ch), ch)
        x_c = xs_ref[rows, :]
        dt = _softplus(dtr_ref[rows, :] + dtb_ref[...])
        la = _pack3(dt * neg_a, lane)
        cs = jnp.where(fwd_lane,
                       _unpack3(jnp.dot(tril, la, preferred_element_type=F32)),
                       _unpack3(jnp.dot(triu, la, preferred_element_type=F32)))
        tot = jnp.where(fwd_lane, cs[ch - 1:ch, :], cs[0:1, :])
        ep[rows, :] = _pack3(jnp.exp2(cs), lane)
        wp[rows, :] = _pack3(dt * jnp.exp2(tot - cs), lane)
        csp = _pack3(cs, lane)
        col = [jnp.dot(csp, sel128s[d][...], preferred_element_type=F32) for d in (0, 1)]
        cs_t = cs.T
        dt_t = dt.T
        y_intra = []
        for g in range(SSD_GROUPS):
            b_g = bc_ref[rows, g * SSD_STATE:(g + 1) * SSD_STATE]
            cofs = SSD_GROUPS * SSD_STATE
            c_g = bc_ref[rows, cofs + g * SSD_STATE:cofs + (g + 1) * SSD_STATE]
            bts[c, g] = b_g.T
            gram = _mm_nt(c_g, b_g)
            for pp in range(gw // LANES):
                p = g * (gw // LANES) + pp
                ws = []
                for d in (0, 1):
                    for half in (0, 1):
                        h = 2 * p + half
                        hd = d * SSD_HEADS + h
                        diff = col[d][:, h * ch:(h + 1) * ch] - cs_t[hd:hd + 1, :]
                        wm = jnp.exp2(jnp.where(keeps[d], diff, _NEG_INF)) * gram * dt_t[hd:hd + 1, :]
                        ws.append(wm.astype(BF16))
                xp = x_c[:, p * LANES:(p + 1) * LANES]
                xcat = jnp.concatenate([jnp.where(lane_lo, xp, 0.0), jnp.where(lane_lo, 0.0, xp)],
                                       axis=0).astype(BF16)
                y_intra.append(jnp.dot(jnp.concatenate(ws, axis=1), jnp.concatenate([xcat, xcat], axis=0),
                                       preferred_element_type=F32))
        yacc[rows, :] = dsk_ref[...] * x_c + jnp.concatenate(y_intra, axis=1)
        return 0

    lax.fori_loop(0, nseq * nch, intra_body, 0, unroll=INTRA_UNROLL)

    def seq_body(s, _):
        base = pl.multiple_of(s * seq, seq)
        for d in (0, 1):
            if has_state:
                for h in range(SSD_HEADS):
                    st[d, :, h * SSD_HEAD_DIM:(h + 1) * SSD_HEAD_DIM] = s0_ref[0, d, h]
            else:
                st[d] = jnp.zeros((SSD_STATE, SSD_D_INNER), F32)

        def chunk(d, c):
            rows = pl.ds(pl.multiple_of(base + c * ch, ch), ch)
            e64 = jnp.dot(ep[rows, :], sel64s[d][...], preferred_element_type=F32)
            w64 = jnp.dot(wp[rows, :], sel64s[d][...], preferred_element_type=F32)
            state = st[d]
            state_b = state.astype(BF16)
            xw = (xs_ref[rows, :] * w64).astype(BF16)
            cofs = SSD_GROUPS * SSD_STATE
            y_state = [jnp.dot(bc_ref[rows, cofs + g * SSD_STATE:cofs + (g + 1) * SSD_STATE],
                               state_b[:, g * gw:(g + 1) * gw], preferred_element_type=F32)
                       for g in range(SSD_GROUPS)]
            upd = [jnp.dot(bts[s * nch + c, g], xw[:, g * gw:(g + 1) * gw], preferred_element_type=F32)
                   for g in range(SSD_GROUPS)]
            yacc[rows, :] = yacc[rows, :] + jnp.concatenate(y_state, axis=1) * e64
            st[d] = e64[edges[d]:edges[d] + 1, :] * state + jnp.concatenate(upd, axis=1)

        def pair_body(j, _):
            for step in (0, 1):
                chunk(0, 2 * j + step)
                chunk(1, nch - 1 - 2 * j - step)
            return 0

        lax.fori_loop(0, nch // 2, pair_body, 0)
        if sfin_ref is not None:
            for d in (0, 1):
                for h in range(SSD_HEADS):
                    sfin_ref[s, d, h] = st[d, :, h * SSD_HEAD_DIM:(h + 1) * SSD_HEAD_DIM]
        return 0

    lax.fori_loop(0, nseq, seq_body, 0)

    def out_body(c, _):
        rows = pl.ds(pl.multiple_of(c * ch, ch), ch)
        y = yacc[rows, :] * _silu(z_ref[rows, :])
        outs = []
        for g in range(SSD_GROUPS):
            yg = y[:, g * gw:(g + 1) * gw]
            outs.append(yg * lax.rsqrt(jnp.mean(yg * yg, axis=-1, keepdims=True) + EPS))
        y_ref[rows, :] = (jnp.concatenate(outs, axis=1) * nw_ref[...]).astype(y_ref.dtype)
        return 0

    lax.fori_loop(0, nseq * nch, out_body, 0)


def _ssd(z, xs, bc, dtr, a_log, dt_bias, d_skip, norm_w, seq, s0=None):
    t = z.shape[0]
    nseq = SEQ_BLOCK // seq
    nblk = t // SEQ_BLOCK
    nch = seq // SSD_CHUNK
    pad32 = lambda a: jnp.pad(a.astype(F32).reshape(1, -1), ((0, 0), (0, LANES - 2 * SSD_HEADS)))
    consts = [pad32(dt_bias), pad32(a_log),
              jnp.repeat(d_skip.astype(F32), SSD_HEAD_DIM).reshape(1, -1),
              norm_w.astype(F32).reshape(1, -1)] + _ssd_selectors()
    row = lambda n: pl.BlockSpec((SEQ_BLOCK, n), lambda i: (i, 0))
    in_specs = ([row(SSD_D_INNER), row(SSD_D_INNER), row(bc.shape[1]), row(LANES)]
                + [_const_spec(c.shape) for c in consts])
    args = [z, xs, bc, dtr] + consts
    out_specs = [row(SSD_D_INNER)]
    out_shape = [jax.ShapeDtypeStruct((t, SSD_D_INNER), BF16)]
    state_shape = (2, SSD_STATE, SSD_D_INNER)
    io_state = (2, SSD_HEADS, SSD_STATE, SSD_HEAD_DIM)
    if s0 is not None:
        in_specs.append(pl.BlockSpec((1,) + io_state, lambda i: (i, 0, 0, 0, 0)))
        args.append(s0)
    else:
        out_specs.append(pl.BlockSpec((nseq,) + io_state, lambda i: (i, 0, 0, 0, 0)))
        out_shape.append(jax.ShapeDtypeStruct((t // seq,) + io_state, F32))
    nblk_ch = SEQ_BLOCK // SSD_CHUNK
    scratch = [pltpu.VMEM((SEQ_BLOCK, SSD_D_INNER), F32),
               pltpu.VMEM((SEQ_BLOCK, LANES), BF16), pltpu.VMEM((SEQ_BLOCK, LANES), BF16),
               pltpu.VMEM((nblk_ch, SSD_GROUPS, SSD_STATE, SSD_CHUNK), BF16),
               pltpu.VMEM(state_shape, F32)]
    return pl.pallas_call(
        functools.partial(_ssd_kernel, seq=seq, has_state=s0 is not None),
        grid=(nblk,),
        in_specs=in_specs,
        out_specs=out_specs,
        out_shape=out_shape,
        scratch_shapes=scratch,
        compiler_params=_params(dimension_semantics=("arbitrary",)),
        name="ssd_mixer",
    )(*args)


def _log_sigmoid(x):
    return -_softplus(-x)


def _ret_kernel(*refs, seq, has_state, rope):
    refs = list(refs)
    x_ref, n1_ref, mod_ref, wq_ref, wk_ref, wv_ref, wg_ref = refs[:7]
    del refs[:7]
    cos_ref, sin_ref = (refs.pop(0), refs.pop(0)) if rope else (None, None)
    dec_ref, nw_ref = refs.pop(0), refs.pop(0)
    s0_ref = refs.pop(0) if has_state else None
    y_ref = refs.pop(0)
    sfin_ref = None if has_state else refs.pop(0)
    hb_s, q_ref, k_ref, v_ref, g_ref, yacc, st = refs

    @pl.when(pl.program_id(1) == 0)
    def _():
        hb_s[...] = _norm_mod(x_ref[...], n1_ref[...], mod_ref, 0, 1).astype(BF16)

    hb = hb_s[...]
    for o_ref, w_ref, scale in ((q_ref, wq_ref, 1.0), (k_ref, wk_ref, RET_QK_DIM ** -0.5)):
        y = jnp.dot(hb, w_ref[...], preferred_element_type=F32) * scale
        if rope:
            for j in range(RET_QK_DIM // LANES):
                lanes = slice(j * LANES, (j + 1) * LANES)
                yj = y[:, lanes]
                rot = yj * cos_ref[:, lanes] + _swap_lane_pairs(yj) * sin_ref[:, lanes]
                o_ref[:, lanes] = rot.astype(o_ref.dtype)
        else:
            o_ref[...] = y.astype(o_ref.dtype)
    v_ref[...] = jnp.dot(hb, wv_ref[...], preferred_element_type=F32).astype(v_ref.dtype)
    g_ref[...] = jnp.dot(hb, wg_ref[...], preferred_element_type=F32).astype(g_ref.dtype)

    ch = RET_CHUNK
    nseq = q_ref.shape[0] // seq
    nch = seq // ch
    gf = _log_sigmoid(dec_ref[0, 0:1, :])
    gb = _log_sigmoid(dec_ref[0, 1:2, :])
    ri = lax.broadcasted_iota(jnp.int32, (ch, ch), 0)
    ci = lax.broadcasted_iota(jnp.int32, (ch, ch), 1)
    dist = (ri - ci).astype(F32)
    gfk, gbk = gf[:, :ch], gb[:, :ch]
    decay = (jnp.where(ci <= ri, jnp.exp(jnp.where(ci <= ri, dist, 0.0) * gfk), 0.0)
             + jnp.where(ci >= ri, jnp.exp(jnp.where(ci >= ri, -dist, 0.0) * gbk), 0.0))
    pos = lax.broadcasted_iota(jnp.int32, (ch, RET_QK_DIM), 0).astype(F32)
    gfq, gbq = gf[:, :RET_QK_DIM], gb[:, :RET_QK_DIM]
    e_f = jnp.exp((pos + 1.0) * gfq)
    e_b = jnp.exp((ch - pos) * gbq)
    tail_f = jnp.exp((ch - 1.0 - pos) * gfq)
    tail_b = jnp.exp(pos * gbq)
    dec_f = jnp.exp(ch * gf)
    dec_b = jnp.exp(ch * gb)

    for s in range(nseq):
        for d in (0, 1):
            if has_state:
                st[d] = s0_ref[0, d, 0]
            else:
                st[d] = jnp.zeros((RET_QK_DIM, RET_V_DIM), F32)
        for c in range(nch):
            rows = slice(s * seq + c * ch, s * seq + (c + 1) * ch)
            q, k, v = q_ref[rows, :], k_ref[rows, :].astype(F32), v_ref[rows, :]
            y = _mm(_mm_nt(q, k) * decay, v)
            if has_state or c > 0:
                y = y + _mm(q.astype(F32) * e_f, st[0])
            yacc[rows, :] = y
            st[0] = dec_f * st[0] + _mm((k * tail_f).T, v)
        for c in reversed(range(nch)):
            rows = slice(s * seq + c * ch, s * seq + (c + 1) * ch)
            q, k, v = q_ref[rows, :], k_ref[rows, :].astype(F32), v_ref[rows, :]
            if has_state or c < nch - 1:
                yacc[rows, :] = yacc[rows, :] + _mm(q.astype(F32) * e_b, st[1])
            st[1] = dec_b * st[1] + _mm((k * tail_b).T, v)
        if sfin_ref is not None:
            for d in (0, 1):
                sfin_ref[s, d, 0] = st[d]
        for c in range(nch):
            rows = slice(s * seq + c * ch, s * seq + (c + 1) * ch)
            y = yacc[rows, :]
            y = y * lax.rsqrt(jnp.mean(y * y, axis=-1, keepdims=True) + EPS)
            y_ref[rows, :] = (y * nw_ref[...] * _silu(g_ref[rows, :].astype(F32))).astype(y_ref.dtype)


def _retention(x, row0, t, norm1_w, mod, w_in, ret_decay, norm_w, seq, s0=None, rope_tables=None):
    d = x.shape[1]
    nseq = SEQ_BLOCK // seq
    nblk = t // SEQ_BLOCK
    blk0 = row0 // SEQ_BLOCK
    per = 1 if mod.shape[0] > 1 else nblk
    dec = jnp.broadcast_to(jnp.pad(ret_decay.astype(F32).T, ((0, 0), (0, SUBLANES - 2)))[:, :, None],
                           (RET_HEADS, SUBLANES, RET_V_DIM))
    vspec = pl.BlockSpec((SEQ_BLOCK, RET_V_DIM), lambda i, h: (i, h))
    qk_blocks, v_blocks = RET_QK_W // RET_QK_DIM, RET_V_W // RET_V_DIM
    wq_spec = pl.BlockSpec((d, RET_QK_DIM), lambda i, h: (0, h))
    wk_spec = pl.BlockSpec((d, RET_QK_DIM), lambda i, h: (0, qk_blocks + h))
    wv_spec = pl.BlockSpec((d, RET_V_DIM), lambda i, h: (0, 2 * RET_QK_W // RET_V_DIM + h))
    wg_spec = pl.BlockSpec((d, RET_V_DIM), lambda i, h: (0, 2 * RET_QK_W // RET_V_DIM + v_blocks + h))
    in_specs = [pl.BlockSpec((SEQ_BLOCK, d), lambda i, h: (blk0 + i, 0)), _const_spec((1, d)),
                pl.BlockSpec((1, 6, d), lambda i, h: (i // per, 0, 0)),
                wq_spec, wk_spec, wv_spec, wg_spec]
    args = [x, norm1_w.reshape(1, d), mod, w_in, w_in, w_in, w_in]
    if rope_tables is not None:
        in_specs += [_const_spec(rope_tables[0].shape)] * 2
        args += list(rope_tables)
    in_specs += [pl.BlockSpec((1, SUBLANES, RET_V_DIM), lambda i, h: (h, 0, 0)),
                 pl.BlockSpec((1, RET_V_DIM), lambda i, h: (0, h))]
    args += [dec, norm_w.astype(F32).reshape(1, -1)]
    out_specs = [vspec]
    out_shape = [jax.ShapeDtypeStruct((t, RET_V_W), BF16)]
    if s0 is not None:
        in_specs.append(pl.BlockSpec((1, 2, 1, RET_QK_DIM, RET_V_DIM), lambda i, h: (i, 0, h, 0, 0)))
        args.append(s0)
    else:
        out_specs.append(pl.BlockSpec((nseq, 2, 1, RET_QK_DIM, RET_V_DIM), lambda i, h: (i, 0, h, 0, 0)))
        out_shape.append(jax.ShapeDtypeStruct((t // seq, 2, RET_HEADS, RET_QK_DIM, RET_V_DIM), F32))
    return pl.pallas_call(
        functools.partial(_ret_kernel, seq=seq, has_state=s0 is not None, rope=rope_tables is not None),
        grid=(nblk, RET_HEADS),
        in_specs=in_specs,
        out_specs=out_specs,
        out_shape=out_shape,
        scratch_shapes=[pltpu.VMEM((SEQ_BLOCK, d), BF16),
                        pltpu.VMEM((SEQ_BLOCK, RET_QK_DIM), BF16), pltpu.VMEM((SEQ_BLOCK, RET_QK_DIM), BF16),
                        pltpu.VMEM((SEQ_BLOCK, RET_V_DIM), BF16), pltpu.VMEM((SEQ_BLOCK, RET_V_DIM), BF16),
                        pltpu.VMEM((SEQ_BLOCK, RET_V_DIM), F32),
                        pltpu.VMEM((2, RET_QK_DIM, RET_V_DIM), F32)],
        compiler_params=_params(dimension_semantics=("arbitrary", "arbitrary")),
        name="retention_mixer",
    )(*args)


def _out_ffn_kernel(*refs, n_x, n_mix, n_out, n_ctx_steps, final_norm):
    refs = list(refs)
    x_refs = [refs.pop(0) for _ in range(n_x)]
    mix_refs = [(refs.pop(0), refs.pop(0)) for _ in range(n_mix)]
    wo_ref, mod_ref, n2_ref, w1_ref, w3_ref, w2_ref = (refs.pop(0) for _ in range(6))
    fn_ref = refs.pop(0) if final_norm else None
    o_refs = [refs.pop(0) for _ in range(n_out)]
    (act,) = refs
    is_ctx = pl.program_id(0) < n_ctx_steps

    def pick(pair):
        return pair[0][...] if len(pair) == 1 else jnp.where(is_ctx, pair[0][...], pair[1][...])

    mixed, row = None, 0
    for pair in mix_refs:
        k = pair[0].shape[1]
        term = jnp.dot(pick(pair), wo_ref[row:row + k, :], preferred_element_type=F32)
        mixed = term if mixed is None else mixed + term
        row += k
    x1 = pick(x_refs) + mod_ref[0, 2:3, :] * mixed
    hb = _norm_mod(x1, n2_ref[...], mod_ref, 3, 4).astype(BF16)
    for c in range(w1_ref.shape[1] // FFN_CHUNK):
        cols = slice(c * FFN_CHUNK, (c + 1) * FFN_CHUNK)
        h1 = jnp.dot(hb, w1_ref[:, cols], preferred_element_type=F32)
        h3 = jnp.dot(hb, w3_ref[:, cols], preferred_element_type=F32)
        act[:, cols] = (_silu(h1) * h3).astype(BF16)
    x2 = x1 + mod_ref[0, 5:6, :] * jnp.dot(act[...], w2_ref[...], preferred_element_type=F32)
    if final_norm:
        ms = jnp.mean(x2 * x2, axis=-1, keepdims=True)
        x2 = x2 * lax.rsqrt(ms + EPS) * fn_ref[...]
    if n_out == 1:
        o_refs[0][...] = x2
    else:
        @pl.when(is_ctx)
        def _():
            o_refs[0][...] = x2

        @pl.when(jnp.logical_not(is_ctx))
        def _():
            o_refs[1][...] = x2


def _out_ffn(x, mixes, wo, mod, norm2_w, w1, w3, w2, t_ctx, t_lat, lat_seq, split_out, final_norm_w=None):
    tm = ROW_TILE
    d = wo.shape[1]
    n0, n1 = t_ctx // tm, t_lat // tm
    per = lat_seq // tm
    ctx_rows = lambda n: pl.BlockSpec((tm, n), lambda i: (jnp.minimum(i, n0 - 1), 0))
    lat_rows = lambda n: pl.BlockSpec((tm, n), lambda i: (jnp.maximum(i - n0, 0), 0))
    all_rows = lambda n: pl.BlockSpec((tm, n), lambda i: (i, 0))
    single = dict(pipeline_mode=pl.Buffered(1))
    x = tuple(x) if isinstance(x, (tuple, list)) else (x,)
    in_specs = [all_rows(d)] if len(x) == 1 else [ctx_rows(d), lat_rows(d)]
    args = list(x)
    for m_ctx, m_lat in mixes:
        in_specs += [ctx_rows(m_ctx.shape[1]), lat_rows(m_lat.shape[1])]
        args += [m_ctx, m_lat]
    in_specs += [_const_spec(wo.shape, **single),
                 pl.BlockSpec((1, 6, d), lambda i: (jnp.where(i < n0, 0, 1 + (i - n0) // per), 0, 0)),
                 _const_spec((1, d))]
    in_specs += [_const_spec(w.shape, **single) for w in (w1, w3, w2)]
    args += [wo, mod, norm2_w.reshape(1, d), w1, w3, w2]
    if final_norm_w is not None:
        in_specs.append(_const_spec((1, d)))
        args.append(final_norm_w.reshape(1, d))
    if split_out:
        out_specs = [ctx_rows(d), lat_rows(d)]
        out_shape = [jax.ShapeDtypeStruct((t_ctx, d), F32), jax.ShapeDtypeStruct((t_lat, d), F32)]
    else:
        out_specs = [all_rows(d)]
        out_shape = [jax.ShapeDtypeStruct((t_ctx + t_lat, d), F32)]
    out = pl.pallas_call(
        functools.partial(_out_ffn_kernel, n_x=len(x), n_mix=len(mixes), n_out=len(out_specs), n_ctx_steps=n0,
                          final_norm=final_norm_w is not None),
        grid=(n0 + n1,),
        in_specs=in_specs,
        out_specs=out_specs,
        out_shape=out_shape,
        scratch_shapes=[pltpu.VMEM((tm, w1.shape[1]), BF16)],
        compiler_params=_params(dimension_semantics=("arbitrary",)),
        name="out_proj_ffn",
    )(*args)
    return out if split_out else out[0]


def _rope_tables(seq):
    half = RET_QK_DIM // 2
    t = jnp.arange(seq)
    row = (t // GRID_W).astype(F32)
    col = (t % GRID_W).astype(F32)
    freqs = ROPE_BASE ** (-jnp.arange(0, half, 2, dtype=F32) / half)
    ang = jnp.concatenate([row[:, None] * freqs, col[:, None] * freqs], axis=-1)
    cos = jnp.repeat(jnp.cos(ang), 2, axis=1)
    sin = jnp.stack([-jnp.sin(ang), jnp.sin(ang)], axis=-1).reshape(seq, RET_QK_DIM)
    return cos, sin


def kernel(x_prompt, x_sample, cache_l0_na_k, cache_l0_na_v, state_l0_ssd, state_l1_ret, c, c_ctx,
           l0_norm1_w, l0_norm2_w, l0_mod_w, l0_mod_b, l0_w_in, l0_w_out, l0_na_bias, l0_conv_w, l0_conv_b,
           l0_ssd_a_log, l0_ssd_dt_bias, l0_ssd_d, l0_ssd_norm_w, l0_ffn_w1, l0_ffn_w3, l0_ffn_w2,
           l1_norm1_w, l1_norm2_w, l1_mod_w, l1_mod_b, l1_w_in, l1_w_out, l1_ret_decay, l1_ret_norm_w,
           l1_ffn_w1, l1_ffn_w3, l1_ffn_w2, final_norm_w):
    bc, lc, d = x_prompt.shape
    bl, ll, _ = x_sample.shape
    assert d == D_MODEL and ll == SEQ_BLOCK and SEQ_BLOCK % lc == 0 and bc % (SEQ_BLOCK // lc) == 0
    tc, tl = bc * lc, bl * ll
    xc = x_prompt.reshape(tc, d)
    xl = x_sample.reshape(tl, d)

    nrow = SUBLANES * ((1 + bl + SUBLANES - 1) // SUBLANES)
    cond = jnp.concatenate([c_ctx[None], c, jnp.zeros((nrow - 1 - bl, d), F32)], axis=0)
    mods = [_adaln(cond, mod_w, mod_b).reshape(nrow, 6, d)
            for mod_w, mod_b in ((l0_mod_w, l0_mod_b), (l1_mod_w, l1_mod_b))]
    mod_ctx = [m[0:1] for m in mods]
    mod_lat = [m[1:1 + bl] for m in mods]

    (w_in0, wo0, ffn0_w1, ffn0_w3, ffn0_w2, w_in1, wo1, ffn1_w1, ffn1_w3, ffn1_w2) = _to_bf16(
        l0_w_in, l0_w_out, l0_ffn_w1, l0_ffn_w3, l0_ffn_w2, l1_w_in, l1_w_out, l1_ffn_w1, l1_ffn_w3, l1_ffn_w2)
    ffn0 = (ffn0_w1, ffn0_w3, ffn0_w2)
    ffn1 = (ffn1_w1, ffn1_w3, ffn1_w2)
    ssd_p = (l0_ssd_a_log, l0_ssd_dt_bias, l0_ssd_d, l0_ssd_norm_w)

    qc, kc, vc, zc, xsc, bcc, dtc = _inproj0(xc, l0_norm1_w, mod_ctx[0], w_in0, l0_conv_w, l0_conv_b, lc, F32)
    att_c, kc_heads, vc_heads = _na_ctx(qc, kc, vc, lc)
    ssd_c, sfin_c = _ssd(zc, xsc, bcc, dtc, *ssd_p, seq=lc)

    ql, kl, vl, zl, xsl, bcl, dtl = _inproj0(xl, l0_norm1_w, mod_lat[0], w_in0, l0_conv_w, l0_conv_b, ll, BF16)
    past = cache_l0_na_k.shape[1]
    att_l = _na_lat(ql, kl, vl, cache_l0_na_k.reshape(bl * past, NA_WIDTH),
                    cache_l0_na_v.reshape(bl * past, NA_WIDTH), _na_bias_table(l0_na_bias), ll)
    (ssd_l,) = _ssd(zl, xsl, bcl, dtl, *ssd_p, seq=ll, s0=state_l0_ssd)

    x1 = _out_ffn((xc, xl), [(att_c, att_l), (ssd_c, ssd_l)], wo0, mods[0], l0_norm2_w, *ffn0,
                  tc, tl, ll, split_out=False)

    ret_c, ret_state = _retention(x1, 0, tc, l1_norm1_w, mod_ctx[1], w_in1, l1_ret_decay, l1_ret_norm_w, lc)
    (ret_l,) = _retention(x1, tc, tl, l1_norm1_w, mod_lat[1], w_in1, l1_ret_decay, l1_ret_norm_w, ll,
                          s0=state_l1_ret, rope_tables=_rope_tables(ll))
    y_prompt, y_sample = _out_ffn(x1, [(ret_c, ret_l)], wo1, mods[1], l1_norm2_w, *ffn1,
                                  tc, tl, ll, split_out=True, final_norm_w=final_norm_w)

    return (y_prompt.reshape(bc, lc, d), y_sample.reshape(bl, ll, d),
            kc_heads.reshape(bc, lc, NA_HEADS, NA_HEAD_DIM), vc_heads.reshape(bc, lc, NA_HEADS, NA_HEAD_DIM),
            sfin_c, ret_state)
```

```python
import functools

import numpy as np
import jax
import jax.numpy as jnp
from jax import lax
from jax.experimental import pallas as pl
from jax.experimental.pallas import tpu as pltpu

F32 = jnp.float32
BF16 = jnp.bfloat16

D_MODEL = 1024
GRID_W = 64
NA_HEADS = 8
NA_HEAD_DIM = 64
NA_WIDTH = NA_HEADS * NA_HEAD_DIM
NA_WIN_ROWS = 8
NA_WIN_COLS = 16
SSD_HEADS = 16
SSD_HEAD_DIM = 64
SSD_D_INNER = SSD_HEADS * SSD_HEAD_DIM
SSD_GROUPS = 2
SSD_STATE = 128
SSD_CONV = 5
SSD_XBC = SSD_D_INNER + 2 * SSD_GROUPS * SSD_STATE
RET_HEADS = 4
RET_QK_DIM = 256
RET_V_DIM = 512
RET_QK_W = RET_HEADS * RET_QK_DIM
RET_V_W = RET_HEADS * RET_V_DIM
ROPE_BASE = 10000.0
EPS = 1e-6

LANES = 128
SUBLANES = 8
SEQ_BLOCK = 1024
ROW_TILE = 512
SSD_CHUNK = 128
CONV_ROWS = 256
INTRA_UNROLL = 4
PIECE_LANES = 2 * SSD_HEADS
assert 3 * PIECE_LANES <= LANES
CAST_STEPS = 8
ADALN_TILE = 1536
RET_CHUNK = 256
FFN_CHUNK = 256
VMEM_LIMIT = 56 * 1024 * 1024

_NEG_INF = float("-inf")


def _params(**kw):
    return pltpu.CompilerParams(vmem_limit_bytes=VMEM_LIMIT, **kw)


def _silu(x):
    return x * (1.0 / (1.0 + jnp.exp(-x)))


def _softplus(x):
    return jnp.maximum(x, 0.0) + jnp.log(1.0 + jnp.exp(-jnp.abs(x)))


def _mm(a, b):
    return jnp.dot(a.astype(BF16), b.astype(BF16), preferred_element_type=F32)


def _mm_nt(a, b):
    return lax.dot_general(a.astype(BF16), b.astype(BF16), (((1,), (1,)), ((), ())),
                           preferred_element_type=F32)


def _const_spec(shape, **kw):
    nd = len(shape)
    return pl.BlockSpec(shape, lambda *_: (0,) * nd, **kw)


def _cast_kernel(*refs):
    n = len(refs) // 2
    (wt_ref, tail_ref, *plain_in), (w_ref, tail_out_ref, *plain_out) = refs[:n], refs[n:]
    for j in range(wt_ref.shape[0] // LANES):
        cols = slice(j * LANES, (j + 1) * LANES)
        w_ref[:, cols] = wt_ref[cols, :].T.astype(BF16)
    n_tail = tail_ref.shape[0]
    tail = jnp.concatenate([tail_ref[...], jnp.zeros((LANES - n_tail, LANES), F32)], axis=0)
    tail_out_ref[...] = tail.T.astype(BF16)
    for i_ref, o_ref in zip(plain_in, plain_out):
        o_ref[...] = i_ref[...].astype(o_ref.dtype)


def _to_bf16(wt, *ws):
    steps = CAST_STEPS
    n_all, k = wt.shape
    n_main = n_all // LANES * LANES
    n_tail = n_all - n_main
    assert k // steps == LANES and n_tail % SUBLANES == 0 and n_main % n_tail == 0
    specs = [pl.BlockSpec((w.shape[0] // steps, w.shape[1]), lambda i: (i, 0)) for w in ws]
    assert all(w.shape[0] % (steps * 2 * SUBLANES) == 0 for w in ws)
    return pl.pallas_call(
        _cast_kernel,
        grid=(steps,),
        in_specs=[pl.BlockSpec((n_main, LANES), lambda i: (0, i)),
                  pl.BlockSpec((n_tail, LANES), lambda i: (n_main // n_tail, i))] + specs,
        out_specs=[pl.BlockSpec((LANES, n_main), lambda i: (i, 0)),
                   pl.BlockSpec((LANES, LANES), lambda i: (i, 0))] + specs,
        out_shape=[jax.ShapeDtypeStruct((k, n_main), BF16), jax.ShapeDtypeStruct((k, LANES), BF16)]
        + [jax.ShapeDtypeStruct(w.shape, BF16) for w in ws],
        compiler_params=_params(dimension_semantics=("arbitrary",)),
        name="weights_to_bf16",
    )(wt, wt, *ws)


def _adaln_kernel(c_ref, w_ref, b_ref, o_ref):
    o_ref[...] = _mm(_silu(c_ref[...]), w_ref[...]) + b_ref[...]


def _adaln(cond, mod_w, mod_b):
    r, d = cond.shape
    n = mod_w.shape[1]
    tn = ADALN_TILE
    assert n % tn == 0
    return pl.pallas_call(
        _adaln_kernel,
        grid=(n // tn,),
        in_specs=[_const_spec((r, d)),
                  pl.BlockSpec((d, tn), lambda j: (0, j)),
                  pl.BlockSpec((1, tn), lambda j: (0, j))],
        out_specs=pl.BlockSpec((r, tn), lambda j: (0, j)),
        out_shape=jax.ShapeDtypeStruct((r, n), F32),
        compiler_params=_params(dimension_semantics=("arbitrary",)),
        name="adaln_mod",
    )(cond, mod_w, mod_b.reshape(1, n))


def _norm_mod(x, nw, mod_ref, shift_idx, scale_idx):
    ms = jnp.mean(x * x, axis=-1, keepdims=True)
    h = x * lax.rsqrt(ms + EPS) * nw
    return h * (1.0 + mod_ref[0, scale_idx:scale_idx + 1, :]) + mod_ref[0, shift_idx:shift_idx + 1, :]


def _inproj0_kernel(*refs, seq, halo):
    if halo:
        (x_ref, xp_ref, xn_ref, nw_ref, mod_ref, w_ref, wdt_ref, cw_ref, cb_ref,
         q_ref, k_ref, v_ref, z_ref, xs_ref, bc_ref, dt_ref, xpad) = refs
    else:
        (x_ref, nw_ref, mod_ref, w_ref, wdt_ref, cw_ref, cb_ref,
         q_ref, k_ref, v_ref, z_ref, xs_ref, bc_ref, dt_ref, xpad) = refs
    tm = x_ref.shape[0]
    pad = SUBLANES
    sub = xpad.shape[1] - 2 * pad
    h = _norm_mod(x_ref[...], nw_ref[...], mod_ref, 0, 1)
    hb = h.astype(BF16)
    xbc_col = 3 * NA_WIDTH + SSD_D_INNER
    w_xbc = w_ref[:, xbc_col:xbc_col + SSD_XBC]
    if halo:
        tiles_per_seq = seq // tm
        p = pl.program_id(0) % tiles_per_seq
        h_prev = jnp.where(p > 0, _norm_mod(xp_ref[...], nw_ref[...], mod_ref, 0, 1), 0.0)
        h_next = jnp.where(p < tiles_per_seq - 1, _norm_mod(xn_ref[...], nw_ref[...], mod_ref, 0, 1), 0.0)
        ext = jnp.concatenate([h_prev, h, h_next], axis=0).astype(BF16)
        xpad[0] = jnp.dot(ext, w_xbc, preferred_element_type=F32)
    else:
        xbc = jnp.dot(hb, w_xbc, preferred_element_type=F32)
        for s in range(tm // sub):
            xpad[s, 0:pad, :] = jnp.zeros((pad, SSD_XBC), F32)
            xpad[s, pad:pad + sub, :] = xbc[s * sub:(s + 1) * sub]
            xpad[s, pad + sub:2 * pad + sub, :] = jnp.zeros((pad, SSD_XBC), F32)

    cr = CONV_ROWS

    def conv_chunk(r0):
        s, rs = divmod(r0, sub)
        nwin = cr + 2 * pad
        for lt in range(SSD_XBC // LANES):
            cols = slice(lt * LANES, (lt + 1) * LANES)
            win = xpad[s, rs:rs + nwin, cols]
            taps = [cw_ref[k:k + 1, cols] * win for k in range(SSD_CONV)]
            before = taps[1] + pltpu.roll(taps[0], 1, 0)
            after = taps[3] + pltpu.roll(taps[4], nwin - 1, 0)
            conv = taps[2] + pltpu.roll(before, 1, 0) + pltpu.roll(after, nwin - 1, 0)
            act = _silu(conv[pad:pad + cr, :] + cb_ref[:, cols])
            if lt < SSD_D_INNER // LANES:
                xs_ref[r0:r0 + cr, cols] = act
            else:
                bc_ref[r0:r0 + cr, lt * LANES - SSD_D_INNER:(lt + 1) * LANES - SSD_D_INNER] = act.astype(BF16)

    chunks = list(range(0, tm, cr))
    col = 0
    for o_ref in (q_ref, k_ref, v_ref, z_ref):
        if chunks:
            conv_chunk(chunks.pop(0))
        n = o_ref.shape[1]
        o_ref[...] = jnp.dot(hb, w_ref[:, col:col + n], preferred_element_type=F32).astype(o_ref.dtype)
        col += n
    dt_ref[...] = jnp.dot(hb, wdt_ref[...], preferred_element_type=F32)
    for r0 in chunks:
        conv_chunk(r0)


def _inproj0(x, norm_w, mod, w, w_dt, conv_w, conv_b, seq, qkv_dtype):
    t, d = x.shape
    tm = ROW_TILE
    assert seq % tm == 0 or tm % seq == 0
    halo = seq > tm
    sub = min(seq, tm)
    widths = (NA_WIDTH, NA_WIDTH, NA_WIDTH, SSD_D_INNER, SSD_D_INNER, SSD_XBC - SSD_D_INNER, LANES)
    dtypes = (qkv_dtype,) * 3 + (F32, F32, BF16, F32)
    assert sum(widths) - LANES == w.shape[1] and w_dt.shape == (d, LANES)
    per = max(seq // tm, 1) if mod.shape[0] > 1 else t // tm
    in_specs = [pl.BlockSpec((tm, d), lambda i: (i, 0))]
    args = [x]
    if halo:
        rb = tm // SUBLANES
        last = t // SUBLANES - 1
        in_specs += [pl.BlockSpec((SUBLANES, d), lambda i: (jnp.maximum(i * rb - 1, 0), 0)),
                     pl.BlockSpec((SUBLANES, d), lambda i: (jnp.minimum((i + 1) * rb, last), 0))]
        args += [x, x]
    in_specs += [_const_spec((1, d)),
                 pl.BlockSpec((1, 6, d), lambda i: (i // per, 0, 0)),
                 _const_spec(w.shape), _const_spec(w_dt.shape),
                 _const_spec((SUBLANES, SSD_XBC)), _const_spec((1, SSD_XBC))]
    args += [norm_w.reshape(1, d), mod, w, w_dt,
             jnp.pad(conv_w.astype(F32), ((0, SUBLANES - SSD_CONV), (0, 0))), conv_b.astype(F32).reshape(1, -1)]
    return pl.pallas_call(
        functools.partial(_inproj0_kernel, seq=seq, halo=halo),
        grid=(t // tm,),
        in_specs=in_specs,
        out_specs=[pl.BlockSpec((tm, n), lambda i: (i, 0)) for n in widths],
        out_shape=[jax.ShapeDtypeStruct((t, n), dt) for n, dt in zip(widths, dtypes)],
        scratch_shapes=[pltpu.VMEM((tm // sub, sub + 2 * SUBLANES, SSD_XBC), F32)],
        compiler_params=_params(dimension_semantics=("arbitrary",)),
        name="l0_in_proj",
    )(*args)


def _swap_lane_pairs(x):
    even = (lax.broadcasted_iota(jnp.int32, x.shape, 1) & 1) == 0
    return jnp.where(even, pltpu.roll(x, LANES - 1, 1), pltpu.roll(x, 1, 1))


def _lane_lo(shape):
    return lax.broadcasted_iota(jnp.int32, shape, 1) < NA_HEAD_DIM


def _head_lanes(x, half):
    lo = _lane_lo(x.shape)
    return jnp.where(lo if half == 0 else jnp.logical_not(lo), x, jnp.zeros_like(x))


def _softmax_pv(q2, keys, vals, biases):
    acc = None
    for half in (0, 1):
        qm = _head_lanes(q2, half)
        scores = []
        for kk, bb in zip(keys, biases):
            s = _mm_nt(qm, kk)
            if bb is not None:
                s = bb(half, s)
            scores.append(s)
        mx = functools.reduce(jnp.maximum, [jnp.max(s, axis=-1, keepdims=True) for s in scores])
        es = [jnp.exp(s - mx) for s in scores]
        den = functools.reduce(jnp.add, [jnp.sum(e, axis=-1, keepdims=True) for e in es])
        pv = functools.reduce(jnp.add, [_mm(e, _head_lanes(vv, half)) for e, vv in zip(es, vals)])
        out = pv * (1.0 / den)
        acc = out if acc is None else acc + out
    return acc


def _na_ctx_kernel(q_ref, k_ref, v_ref, o_ref, kh_ref, vh_ref, *, seq):
    scale = NA_HEAD_DIM ** -0.5
    nseq = q_ref.shape[0] // seq
    for s in range(nseq):
        r = slice(s * seq, (s + 1) * seq)
        for hp in range(NA_WIDTH // LANES):
            c = slice(hp * LANES, (hp + 1) * LANES)
            out = _softmax_pv(q_ref[r, c] * scale, [k_ref[r, c]], [v_ref[r, c]], [None])
            o_ref[r, c] = out.astype(o_ref.dtype)
    kh_ref[...] = k_ref[...].reshape(kh_ref.shape)
    vh_ref[...] = v_ref[...].reshape(vh_ref.shape)


def _na_ctx(q, k, v, seq):
    t, w = q.shape
    spec = pl.BlockSpec((SEQ_BLOCK, w), lambda i: (i, 0))
    hspec = pl.BlockSpec((SEQ_BLOCK, NA_HEADS, NA_HEAD_DIM), lambda i: (i, 0, 0))
    hshape = jax.ShapeDtypeStruct((t, NA_HEADS, NA_HEAD_DIM), k.dtype)
    return pl.pallas_call(
        functools.partial(_na_ctx_kernel, seq=seq),
        grid=(t // SEQ_BLOCK,),
        in_specs=[spec, spec, spec],
        out_specs=[spec, hspec, hspec],
        out_shape=[jax.ShapeDtypeStruct((t, w), BF16), hshape, hshape],
        compiler_params=_params(dimension_semantics=("arbitrary",)),
        name="na_context",
    )(q, k, v)


NA_QBLK = 256


def _na_lat_kernel(q_ref, k_ref, v_ref, kc_ref, vc_ref, tab_ref, o_ref):
    scale = NA_HEAD_DIM ** -0.5
    seq = q_ref.shape[0]
    nblk = seq // NA_QBLK
    rows_per_blk = NA_QBLK // GRID_W
    grid_rows = seq // GRID_W
    n_off = 2 * NA_WIN_ROWS

    qc = lax.broadcasted_iota(jnp.int32, (GRID_W, LANES), 0)
    lane = lax.broadcasted_iota(jnp.int32, (GRID_W, LANES), 1)
    kc = lane & (GRID_W - 1)
    c0 = jnp.clip(qc - NA_WIN_COLS // 2, 0, GRID_W - NA_WIN_COLS)
    col_ok = (kc >= c0) & (kc < c0 + NA_WIN_COLS)
    first = lane < GRID_W
    pair_bias = [[jnp.where(col_ok,
                            pltpu.roll(jnp.broadcast_to(tab_ref[half, e:e + 1, :], (GRID_W, LANES)), 0, 1,
                                       stride=1, stride_axis=0),
                            _NEG_INF)
                  for e in range(n_off)] for half in (0, 1)]
    neg = jnp.full((GRID_W, LANES), _NEG_INF, F32)

    def band_start(r):
        return min(max(r - NA_WIN_ROWS // 2, 0), grid_rows - NA_WIN_ROWS)

    for i in range(nblk):
        ws_row = band_start(i * rows_per_blk) // 2 * 2
        we_row = min((band_start((i + 1) * rows_per_blk - 1) + NA_WIN_ROWS + 1) // 2 * 2, grid_rows)
        ws, kwin = ws_row * GRID_W, (we_row - ws_row) * GRID_W

        def add_bias(half, s, i=i, ws_row=ws_row, kwin=kwin):
            out_rows = []
            for rq in range(rows_per_blk):
                r = i * rows_per_blk + rq
                r0 = band_start(r)
                tiles = []
                for kp in range(kwin // LANES):
                    kr = ws_row + 2 * kp
                    ok0 = r0 <= kr < r0 + NA_WIN_ROWS
                    ok1 = r0 <= kr + 1 < r0 + NA_WIN_ROWS
                    if ok0 or ok1:
                        tile = pair_bias[half][kr - r + NA_WIN_ROWS]
                        if not ok1:
                            tile = jnp.where(first, tile, _NEG_INF)
                        elif not ok0:
                            tile = jnp.where(first, _NEG_INF, tile)
                    else:
                        tile = neg
                    tiles.append(s[rq * GRID_W:(rq + 1) * GRID_W, kp * LANES:(kp + 1) * LANES] + tile)
                out_rows.append(jnp.concatenate(tiles, axis=1))
            return jnp.concatenate(out_rows, axis=0)

        rows = slice(i * NA_QBLK, (i + 1) * NA_QBLK)
        kw = k_ref[ws:ws + kwin, :]
        vw = v_ref[ws:ws + kwin, :]
        out = _softmax_pv(q_ref[rows, :] * scale, [kw, kc_ref[...]], [vw, vc_ref[...]], [add_bias, None])
        o_ref[rows, :] = out.astype(o_ref.dtype)


def _na_bias_table(rel_bias):
    h, n_dr, n_dc = rel_bias.shape
    half_dc = n_dc // 2
    tz = jnp.pad(rel_bias.astype(F32), ((0, 0), (1, 1), (0, 0)))
    lo, hi = tz[:, 0:n_dr + 1], tz[:, 1:n_dr + 2]
    gap = jnp.zeros((h, n_dr + 1, GRID_W - half_dc - 1 - half_dc), F32)
    return jnp.concatenate([lo[..., half_dc:], gap, hi, gap, lo[..., :half_dc]], axis=-1)


def _na_lat(q, k, v, k_ctx, v_ctx, bias_table, seq):
    t, w = q.shape
    b = t // seq
    npair = w // LANES
    past = k_ctx.shape[0] // b
    kspec = pl.BlockSpec((seq, LANES), lambda hp, bb: (bb, hp))
    cspec = pl.BlockSpec((past, LANES), lambda hp, bb: (bb, hp))
    tspec = pl.BlockSpec((2,) + bias_table.shape[1:], lambda hp, bb: (hp, 0, 0))
    return pl.pallas_call(
        _na_lat_kernel,
        grid=(npair, b),
        in_specs=[kspec, kspec, kspec, cspec, cspec, tspec],
        out_specs=kspec,
        out_shape=jax.ShapeDtypeStruct((t, w), BF16),
        compiler_params=_params(dimension_semantics=("arbitrary", "arbitrary")),
        name="na_latent",
    )(q, k, v, k_ctx, v_ctx, bias_table)


def _pack3(v, lane):
    vm = jnp.where(lane < PIECE_LANES, v, 0.0)
    hi = vm.astype(BF16).astype(F32)
    r1 = vm - hi
    mid = r1.astype(BF16).astype(F32)
    lo = r1 - mid
    return (hi + pltpu.roll(mid, PIECE_LANES, 1) + pltpu.roll(lo, 2 * PIECE_LANES, 1)).astype(BF16)


def _unpack3(res):
    return res + pltpu.roll(res, LANES - PIECE_LANES, 1) + pltpu.roll(res, LANES - 2 * PIECE_LANES, 1)


def _ssd_selectors():
    r = np.arange(LANES)
    out = []
    for width in (SSD_HEAD_DIM, SSD_CHUNK):
        l = np.arange(SSD_HEADS * width)
        for d in (0, 1):
            sel = (r[:, None] < 3 * PIECE_LANES) & ((r[:, None] % PIECE_LANES) == d * SSD_HEADS + l[None, :] // width)
            out.append(jnp.asarray(sel, BF16))
    return out


def _ssd_kernel(*refs, seq, has_state):
    if has_state:
        (z_ref, xs_ref, bc_ref, dtr_ref, dtb_ref, alog_ref, dsk_ref, nw_ref,
         s64f_ref, s64b_ref, s128f_ref, s128b_ref, s0_ref,
         y_ref, yacc, ep, wp, bts, st) = refs
        sfin_ref = None
    else:
        (z_ref, xs_ref, bc_ref, dtr_ref, dtb_ref, alog_ref, dsk_ref, nw_ref,
         s64f_ref, s64b_ref, s128f_ref, s128b_ref,
         y_ref, sfin_ref, yacc, ep, wp, bts, st) = refs
        s0_ref = None
    ch = SSD_CHUNK
    nseq = z_ref.shape[0] // seq
    nch = seq // ch
    assert nch % 2 == 0
    lane = lax.broadcasted_iota(jnp.int32, (ch, LANES), 1)
    ri = lax.broadcasted_iota(jnp.int32, (ch, ch), 0)
    ci = lax.broadcasted_iota(jnp.int32, (ch, ch), 1)
    keeps = (ci <= ri, ci >= ri)
    tril = jnp.where(keeps[0], 1.0, 0.0).astype(BF16)
    triu = jnp.where(keeps[1], 1.0, 0.0).astype(BF16)
    lane_lo = lane < SSD_HEAD_DIM
    fwd_lane = lane < SSD_HEADS
    neg_a = -jnp.exp(alog_ref[...]) * np.log2(np.e).astype(np.float32)
    gw = SSD_D_INNER // SSD_GROUPS
    sel64s = (s64f_ref, s64b_ref)
    sel128s = (s128f_ref, s128b_ref)
    edges = (ch - 1, 0)

    def intra_body(c, _):
        rows = pl.ds(pl.multiple_of(c * ch, ch), ch)
        x_c = xs_ref[rows, :]
        dt = _softplus(dtr_ref[rows, :] + dtb_ref[...])
        la = _pack3(dt * neg_a, lane)
        cs = jnp.where(fwd_lane,
                       _unpack3(jnp.dot(tril, la, preferred_element_type=F32)),
                       _unpack3(jnp.dot(triu, la, preferred_element_type=F32)))
        tot = jnp.where(fwd_lane, cs[ch - 1:ch, :], cs[0:1, :])
        ep[rows, :] = _pack3(jnp.exp2(cs), lane)
        wp[rows, :] = _pack3(dt * jnp.exp2(tot - cs), lane)
        csp = _pack3(cs, lane)
        col = [jnp.dot(csp, sel128s[d][...], preferred_element_type=F32) for d in (0, 1)]
        cs_t = cs.T
        dt_t = dt.T
        y_intra = []
        for g in range(SSD_GROUPS):
            b_g = bc_ref[rows, g * SSD_STATE:(g + 1) * SSD_STATE]
            cofs = SSD_GROUPS * SSD_STATE
            c_g = bc_ref[rows, cofs + g * SSD_STATE:cofs + (g + 1) * SSD_STATE]
            bts[c, g] = b_g.T
            gram = _mm_nt(c_g, b_g)
            for pp in range(gw // LANES):
                p = g * (gw // LANES) + pp
                ws = []
                for d in (0, 1):
                    for half in (0, 1):
                        h = 2 * p + half
                        hd = d * SSD_HEADS + h
                        diff = col[d][:, h * ch:(h + 1) * ch] - cs_t[hd:hd + 1, :]
                        wm = jnp.exp2(jnp.where(keeps[d], diff, _NEG_INF)) * gram * dt_t[hd:hd + 1, :]
                        ws.append(wm.astype(BF16))
                xp = x_c[:, p * LANES:(p + 1) * LANES]
                xcat = jnp.concatenate([jnp.where(lane_lo, xp, 0.0), jnp.where(lane_lo, 0.0, xp)],
                                       axis=0).astype(BF16)
                y_intra.append(jnp.dot(jnp.concatenate(ws, axis=1), jnp.concatenate([xcat, xcat], axis=0),
                                       preferred_element_type=F32))
        yacc[rows, :] = dsk_ref[...] * x_c + jnp.concatenate(y_intra, axis=1)
        return 0

    lax.fori_loop(0, nseq * nch, intra_body, 0, unroll=INTRA_UNROLL)

    def seq_body(s, _):
        base = pl.multiple_of(s * seq, seq)
        for d in (0, 1):
            if has_state:
                for h in range(SSD_HEADS):
                    st[d, :, h * SSD_HEAD_DIM:(h + 1) * SSD_HEAD_DIM] = s0_ref[0, d, h]
            else:
                st[d] = jnp.zeros((SSD_STATE, SSD_D_INNER), F32)

        def chunk(d, c):
            rows = pl.ds(pl.multiple_of(base + c * ch, ch), ch)
            e64 = jnp.dot(ep[rows, :], sel64s[d][...], preferred_element_type=F32)
            w64 = jnp.dot(wp[rows, :], sel64s[d][...], preferred_element_type=F32)
            state = st[d]
            state_b = state.astype(BF16)
            xw = (xs_ref[rows, :] * w64).astype(BF16)
            cofs = SSD_GROUPS * SSD_STATE
            y_state = [jnp.dot(bc_ref[rows, cofs + g * SSD_STATE:cofs + (g + 1) * SSD_STATE],
                               state_b[:, g * gw:(g + 1) * gw], preferred_element_type=F32)
                       for g in range(SSD_GROUPS)]
            upd = [jnp.dot(bts[s * nch + c, g], xw[:, g * gw:(g + 1) * gw], preferred_element_type=F32)
                   for g in range(SSD_GROUPS)]
            yacc[rows, :] = yacc[rows, :] + jnp.concatenate(y_state, axis=1) * e64
            st[d] = e64[edges[d]:edges[d] + 1, :] * state + jnp.concatenate(upd, axis=1)

        def pair_body(j, _):
            for step in (0, 1):
                chunk(0, 2 * j + step)
                chunk(1, nch - 1 - 2 * j - step)
            return 0

        lax.fori_loop(0, nch // 2, pair_body, 0)
        if sfin_ref is not None:
            for d in (0, 1):
                for h in range(SSD_HEADS):
                    sfin_ref[s, d, h] = st[d, :, h * SSD_HEAD_DIM:(h + 1) * SSD_HEAD_DIM]
        return 0

    lax.fori_loop(0, nseq, seq_body, 0)

    def out_body(c, _):
        rows = pl.ds(pl.multiple_of(c * ch, ch), ch)
        y = yacc[rows, :] * _silu(z_ref[rows, :])
        outs = []
        for g in range(SSD_GROUPS):
            yg = y[:, g * gw:(g + 1) * gw]
            outs.append(yg * lax.rsqrt(jnp.mean(yg * yg, axis=-1, keepdims=True) + EPS))
        y_ref[rows, :] = (jnp.concatenate(outs, axis=1) * nw_ref[...]).astype(y_ref.dtype)
        return 0

    lax.fori_loop(0, nseq * nch, out_body, 0)


def _ssd(z, xs, bc, dtr, a_log, dt_bias, d_skip, norm_w, seq, s0=None):
    t = z.shape[0]
    nseq = SEQ_BLOCK // seq
    nblk = t // SEQ_BLOCK
    nch = seq // SSD_CHUNK
    pad32 = lambda a: jnp.pad(a.astype(F32).reshape(1, -1), ((0, 0), (0, LANES - 2 * SSD_HEADS)))
    consts = [pad32(dt_bias), pad32(a_log),
              jnp.repeat(d_skip.astype(F32), SSD_HEAD_DIM).reshape(1, -1),
              norm_w.astype(F32).reshape(1, -1)] + _ssd_selectors()
    row = lambda n: pl.BlockSpec((SEQ_BLOCK, n), lambda i: (i, 0))
    in_specs = ([row(SSD_D_INNER), row(SSD_D_INNER), row(bc.shape[1]), row(LANES)]
                + [_const_spec(c.shape) for c in consts])
    args = [z, xs, bc, dtr] + consts
    out_specs = [row(SSD_D_INNER)]
    out_shape = [jax.ShapeDtypeStruct((t, SSD_D_INNER), BF16)]
    state_shape = (2, SSD_STATE, SSD_D_INNER)
    io_state = (2, SSD_HEADS, SSD_STATE, SSD_HEAD_DIM)
    if s0 is not None:
        in_specs.append(pl.BlockSpec((1,) + io_state, lambda i: (i, 0, 0, 0, 0)))
        args.append(s0)
    else:
        out_specs.append(pl.BlockSpec((nseq,) + io_state, lambda i: (i, 0, 0, 0, 0)))
        out_shape.append(jax.ShapeDtypeStruct((t // seq,) + io_state, F32))
    nblk_ch = SEQ_BLOCK // SSD_CHUNK
    scratch = [pltpu.VMEM((SEQ_BLOCK, SSD_D_INNER), F32),
               pltpu.VMEM((SEQ_BLOCK, LANES), BF16), pltpu.VMEM((SEQ_BLOCK, LANES), BF16),
               pltpu.VMEM((nblk_ch, SSD_GROUPS, SSD_STATE, SSD_CHUNK), BF16),
               pltpu.VMEM(state_shape, F32)]
    return pl.pallas_call(
        functools.partial(_ssd_kernel, seq=seq, has_state=s0 is not None),
        grid=(nblk,),
        in_specs=in_specs,
        out_specs=out_specs,
        out_shape=out_shape,
        scratch_shapes=scratch,
        compiler_params=_params(dimension_semantics=("arbitrary",)),
        name="ssd_mixer",
    )(*args)


def _log_sigmoid(x):
    return -_softplus(-x)


def _ret_kernel(*refs, seq, has_state, rope):
    refs = list(refs)
    x_ref, n1_ref, mod_ref, wq_ref, wk_ref, wv_ref, wg_ref = refs[:7]
    del refs[:7]
    cos_ref, sin_ref = (refs.pop(0), refs.pop(0)) if rope else (None, None)
    dec_ref, nw_ref = refs.pop(0), refs.pop(0)
    s0_ref = refs.pop(0) if has_state else None
    y_ref = refs.pop(0)
    sfin_ref = None if has_state else refs.pop(0)
    hb_s, q_ref, k_ref, v_ref, g_ref, yacc, st = refs

    @pl.when(pl.program_id(1) == 0)
    def _():
        hb_s[...] = _norm_mod(x_ref[...], n1_ref[...], mod_ref, 0, 1).astype(BF16)

    hb = hb_s[...]
    for o_ref, w_ref, scale in ((q_ref, wq_ref, 1.0), (k_ref, wk_ref, RET_QK_DIM ** -0.5)):
        y = jnp.dot(hb, w_ref[...], preferred_element_type=F32) * scale
        if rope:
            for j in range(RET_QK_DIM // LANES):
                lanes = slice(j * LANES, (j + 1) * LANES)
                yj = y[:, lanes]
                rot = yj * cos_ref[:, lanes] + _swap_lane_pairs(yj) * sin_ref[:, lanes]
                o_ref[:, lanes] = rot.astype(o_ref.dtype)
        else:
            o_ref[...] = y.astype(o_ref.dtype)
    v_ref[...] = jnp.dot(hb, wv_ref[...], preferred_element_type=F32).astype(v_ref.dtype)
    g_ref[...] = jnp.dot(hb, wg_ref[...], preferred_element_type=F32).astype(g_ref.dtype)

    ch = RET_CHUNK
    nseq = q_ref.shape[0] // seq
    nch = seq // ch
    gf = _log_sigmoid(dec_ref[0, 0:1, :])
    gb = _log_sigmoid(dec_ref[0, 1:2, :])
    ri = lax.broadcasted_iota(jnp.int32, (ch, ch), 0)
    ci = lax.broadcasted_iota(jnp.int32, (ch, ch), 1)
    dist = (ri - ci).astype(F32)
    gfk, gbk = gf[:, :ch], gb[:, :ch]
    decay = (jnp.where(ci <= ri, jnp.exp(jnp.where(ci <= ri, dist, 0.0) * gfk), 0.0)
             + jnp.where(ci >= ri, jnp.exp(jnp.where(ci >= ri, -dist, 0.0) * gbk), 0.0))
    pos = lax.broadcasted_iota(jnp.int32, (ch, RET_QK_DIM), 0).astype(F32)
    gfq, gbq = gf[:, :RET_QK_DIM], gb[:, :RET_QK_DIM]
    e_f = jnp.exp((pos + 1.0) * gfq)
    e_b = jnp.exp((ch - pos) * gbq)
    tail_f = jnp.exp((ch - 1.0 - pos) * gfq)
    tail_b = jnp.exp(pos * gbq)
    dec_f = jnp.exp(ch * gf)
    dec_b = jnp.exp(ch * gb)

    for s in range(nseq):
        for d in (0, 1):
            if has_state:
                st[d] = s0_ref[0, d, 0]
            else:
                st[d] = jnp.zeros((RET_QK_DIM, RET_V_DIM), F32)
        for c in range(nch):
            rows = slice(s * seq + c * ch, s * seq + (c + 1) * ch)
            q, k, v = q_ref[rows, :], k_ref[rows, :].astype(F32), v_ref[rows, :]
            y = _mm(_mm_nt(q, k) * decay, v)
            if has_state or c > 0:
                y = y + _mm(q.astype(F32) * e_f, st[0])
            yacc[rows, :] = y
            st[0] = dec_f * st[0] + _mm((k * tail_f).T, v)
        for c in reversed(range(nch)):
            rows = slice(s * seq + c * ch, s * seq + (c + 1) * ch)
            q, k, v = q_ref[rows, :], k_ref[rows, :].astype(F32), v_ref[rows, :]
            if has_state or c < nch - 1:
                yacc[rows, :] = yacc[rows, :] + _mm(q.astype(F32) * e_b, st[1])
            st[1] = dec_b * st[1] + _mm((k * tail_b).T, v)
        if sfin_ref is not None:
            for d in (0, 1):
                sfin_ref[s, d, 0] = st[d]
        for c in range(nch):
            rows = slice(s * seq + c * ch, s * seq + (c + 1) * ch)
            y = yacc[rows, :]
            y = y * lax.rsqrt(jnp.mean(y * y, axis=-1, keepdims=True) + EPS)
            y_ref[rows, :] = (y * nw_ref[...] * _silu(g_ref[rows, :].astype(F32))).astype(y_ref.dtype)


def _retention(x, row0, t, norm1_w, mod, w_in, ret_decay, norm_w, seq, s0=None, rope_tables=None):
    d = x.shape[1]
    nseq = SEQ_BLOCK // seq
    nblk = t // SEQ_BLOCK
    blk0 = row0 // SEQ_BLOCK
    per = 1 if mod.shape[0] > 1 else nblk
    dec = jnp.broadcast_to(jnp.pad(ret_decay.astype(F32).T, ((0, 0), (0, SUBLANES - 2)))[:, :, None],
                           (RET_HEADS, SUBLANES, RET_V_DIM))
    vspec = pl.BlockSpec((SEQ_BLOCK, RET_V_DIM), lambda i, h: (i, h))
    qk_blocks, v_blocks = RET_QK_W // RET_QK_DIM, RET_V_W // RET_V_DIM
    wq_spec = pl.BlockSpec((d, RET_QK_DIM), lambda i, h: (0, h))
    wk_spec = pl.BlockSpec((d, RET_QK_DIM), lambda i, h: (0, qk_blocks + h))
    wv_spec = pl.BlockSpec((d, RET_V_DIM), lambda i, h: (0, 2 * RET_QK_W // RET_V_DIM + h))
    wg_spec = pl.BlockSpec((d, RET_V_DIM), lambda i, h: (0, 2 * RET_QK_W // RET_V_DIM + v_blocks + h))
    in_specs = [pl.BlockSpec((SEQ_BLOCK, d), lambda i, h: (blk0 + i, 0)), _const_spec((1, d)),
                pl.BlockSpec((1, 6, d), lambda i, h: (i // per, 0, 0)),
                wq_spec, wk_spec, wv_spec, wg_spec]
    args = [x, norm1_w.reshape(1, d), mod, w_in, w_in, w_in, w_in]
    if rope_tables is not None:
        in_specs += [_const_spec(rope_tables[0].shape)] * 2
        args += list(rope_tables)
    in_specs += [pl.BlockSpec((1, SUBLANES, RET_V_DIM), lambda i, h: (h, 0, 0)),
                 pl.BlockSpec((1, RET_V_DIM), lambda i, h: (0, h))]
    args += [dec, norm_w.astype(F32).reshape(1, -1)]
    out_specs = [vspec]
    out_shape = [jax.ShapeDtypeStruct((t, RET_V_W), BF16)]
    if s0 is not None:
        in_specs.append(pl.BlockSpec((1, 2, 1, RET_QK_DIM, RET_V_DIM), lambda i, h: (i, 0, h, 0, 0)))
        args.append(s0)
    else:
        out_specs.append(pl.BlockSpec((nseq, 2, 1, RET_QK_DIM, RET_V_DIM), lambda i, h: (i, 0, h, 0, 0)))
        out_shape.append(jax.ShapeDtypeStruct((t // seq, 2, RET_HEADS, RET_QK_DIM, RET_V_DIM), F32))
    return pl.pallas_call(
        functools.partial(_ret_kernel, seq=seq, has_state=s0 is not None, rope=rope_tables is not None),
        grid=(nblk, RET_HEADS),
        in_specs=in_specs,
        out_specs=out_specs,
        out_shape=out_shape,
        scratch_shapes=[pltpu.VMEM((SEQ_BLOCK, d), BF16),
                        pltpu.VMEM((SEQ_BLOCK, RET_QK_DIM), BF16), pltpu.VMEM((SEQ_BLOCK, RET_QK_DIM), BF16),
                        pltpu.VMEM((SEQ_BLOCK, RET_V_DIM), BF16), pltpu.VMEM((SEQ_BLOCK, RET_V_DIM), BF16),
                        pltpu.VMEM((SEQ_BLOCK, RET_V_DIM), F32),
                        pltpu.VMEM((2, RET_QK_DIM, RET_V_DIM), F32)],
        compiler_params=_params(dimension_semantics=("arbitrary", "arbitrary")),
        name="retention_mixer",
    )(*args)


def _out_ffn_kernel(*refs, n_x, n_mix, n_out, n_ctx_steps, final_norm):
    refs = list(refs)
    x_refs = [refs.pop(0) for _ in range(n_x)]
    mix_refs = [(refs.pop(0), refs.pop(0)) for _ in range(n_mix)]
    wo_ref, mod_ref, n2_ref, w1_ref, w3_ref, w2_ref = (refs.pop(0) for _ in range(6))
    fn_ref = refs.pop(0) if final_norm else None
    o_refs = [refs.pop(0) for _ in range(n_out)]
    (act,) = refs
    is_ctx = pl.program_id(0) < n_ctx_steps

    def pick(pair):
        return pair[0][...] if len(pair) == 1 else jnp.where(is_ctx, pair[0][...], pair[1][...])

    mixed, row = None, 0
    for pair in mix_refs:
        k = pair[0].shape[1]
        term = jnp.dot(pick(pair), wo_ref[row:row + k, :], preferred_element_type=F32)
        mixed = term if mixed is None else mixed + term
        row += k
    x1 = pick(x_refs) + mod_ref[0, 2:3, :] * mixed
    hb = _norm_mod(x1, n2_ref[...], mod_ref, 3, 4).astype(BF16)
    for c in range(w1_ref.shape[1] // FFN_CHUNK):
        cols = slice(c * FFN_CHUNK, (c + 1) * FFN_CHUNK)
        h1 = jnp.dot(hb, w1_ref[:, cols], preferred_element_type=F32)
        h3 = jnp.dot(hb, w3_ref[:, cols], preferred_element_type=F32)
        act[:, cols] = (_silu(h1) * h3).astype(BF16)
    x2 = x1 + mod_ref[0, 5:6, :] * jnp.dot(act[...], w2_ref[...], preferred_element_type=F32)
    if final_norm:
        ms = jnp.mean(x2 * x2, axis=-1, keepdims=True)
        x2 = x2 * lax.rsqrt(ms + EPS) * fn_ref[...]
    if n_out == 1:
        o_refs[0][...] = x2
    else:
        @pl.when(is_ctx)
        def _():
            o_refs[0][...] = x2

        @pl.when(jnp.logical_not(is_ctx))
        def _():
            o_refs[1][...] = x2


def _out_ffn(x, mixes, wo, mod, norm2_w, w1, w3, w2, t_ctx, t_lat, lat_seq, split_out, final_norm_w=None):
    tm = ROW_TILE
    d = wo.shape[1]
    n0, n1 = t_ctx // tm, t_lat // tm
    per = lat_seq // tm
    ctx_rows = lambda n: pl.BlockSpec((tm, n), lambda i: (jnp.minimum(i, n0 - 1), 0))
    lat_rows = lambda n: pl.BlockSpec((tm, n), lambda i: (jnp.maximum(i - n0, 0), 0))
    all_rows = lambda n: pl.BlockSpec((tm, n), lambda i: (i, 0))
    single = dict(pipeline_mode=pl.Buffered(1))
    x = tuple(x) if isinstance(x, (tuple, list)) else (x,)
    in_specs = [all_rows(d)] if len(x) == 1 else [ctx_rows(d), lat_rows(d)]
    args = list(x)
    for m_ctx, m_lat in mixes:
        in_specs += [ctx_rows(m_ctx.shape[1]), lat_rows(m_lat.shape[1])]
        args += [m_ctx, m_lat]
    in_specs += [_const_spec(wo.shape, **single),
                 pl.BlockSpec((1, 6, d), lambda i: (jnp.where(i < n0, 0, 1 + (i - n0) // per), 0, 0)),
                 _const_spec((1, d))]
    in_specs += [_const_spec(w.shape, **single) for w in (w1, w3, w2)]
    args += [wo, mod, norm2_w.reshape(1, d), w1, w3, w2]
    if final_norm_w is not None:
        in_specs.append(_const_spec((1, d)))
        args.append(final_norm_w.reshape(1, d))
    if split_out:
        out_specs = [ctx_rows(d), lat_rows(d)]
        out_shape = [jax.ShapeDtypeStruct((t_ctx, d), F32), jax.ShapeDtypeStruct((t_lat, d), F32)]
    else:
        out_specs = [all_rows(d)]
        out_shape = [jax.ShapeDtypeStruct((t_ctx + t_lat, d), F32)]
    out = pl.pallas_call(
        functools.partial(_out_ffn_kernel, n_x=len(x), n_mix=len(mixes), n_out=len(out_specs), n_ctx_steps=n0,
                          final_norm=final_norm_w is not None),
        grid=(n0 + n1,),
        in_specs=in_specs,
        out_specs=out_specs,
        out_shape=out_shape,
        scratch_shapes=[pltpu.VMEM((tm, w1.shape[1]), BF16)],
        compiler_params=_params(dimension_semantics=("arbitrary",)),
        name="out_proj_ffn",
    )(*args)
    return out if split_out else out[0]


def _rope_tables(seq):
    half = RET_QK_DIM // 2
    t = jnp.arange(seq)
    row = (t // GRID_W).astype(F32)
    col = (t % GRID_W).astype(F32)
    freqs = ROPE_BASE ** (-jnp.arange(0, half, 2, dtype=F32) / half)
    ang = jnp.concatenate([row[:, None] * freqs, col[:, None] * freqs], axis=-1)
    cos = jnp.repeat(jnp.cos(ang), 2, axis=1)
    sin = jnp.stack([-jnp.sin(ang), jnp.sin(ang)], axis=-1).reshape(seq, RET_QK_DIM)
    return cos, sin


def kernel(x_prompt, x_sample, cache_l0_na_k, cache_l0_na_v, state_l0_ssd, state_l1_ret, c, c_ctx,
           l0_norm1_w, l0_norm2_w, l0_mod_w, l0_mod_b, l0_w_in, l0_w_out, l0_na_bias, l0_conv_w, l0_conv_b,
           l0_ssd_a_log, l0_ssd_dt_bias, l0_ssd_d, l0_ssd_norm_w, l0_ffn_w1, l0_ffn_w3, l0_ffn_w2,
           l1_norm1_w, l1_norm2_w, l1_mod_w, l1_mod_b, l1_w_in, l1_w_out, l1_ret_decay, l1_ret_norm_w,
           l1_ffn_w1, l1_ffn_w3, l1_ffn_w2, final_norm_w):
    bc, lc, d = x_prompt.shape
    bl, ll, _ = x_sample.shape
    assert d == D_MODEL and ll == SEQ_BLOCK and SEQ_BLOCK % lc == 0 and bc % (SEQ_BLOCK // lc) == 0
    tc, tl = bc * lc, bl * ll
    xc = x_prompt.reshape(tc, d)
    xl = x_sample.reshape(tl, d)

    nrow = SUBLANES * ((1 + bl + SUBLANES - 1) // SUBLANES)
    cond = jnp.concatenate([c_ctx[None], c, jnp.zeros((nrow - 1 - bl, d), F32)], axis=0)
    mods = [_adaln(cond, mod_w, mod_b).reshape(nrow, 6, d)
            for mod_w, mod_b in ((l0_mod_w, l0_mod_b), (l1_mod_w, l1_mod_b))]
    mod_ctx = [m[0:1] for m in mods]
    mod_lat = [m[1:1 + bl] for m in mods]

    (w_in0, w_dt0, wo0, ffn0_w1, ffn0_w3, ffn0_w2, w_in1, wo1, ffn1_w1, ffn1_w3, ffn1_w2) = _to_bf16(
        l0_w_in.T, l0_w_out, l0_ffn_w1, l0_ffn_w3, l0_ffn_w2, l1_w_in, l1_w_out, l1_ffn_w1, l1_ffn_w3, l1_ffn_w2)
    ffn0 = (ffn0_w1, ffn0_w3, ffn0_w2)
    ffn1 = (ffn1_w1, ffn1_w3, ffn1_w2)
    ssd_p = (l0_ssd_a_log, l0_ssd_dt_bias, l0_ssd_d, l0_ssd_norm_w)

    qc, kc, vc, zc, xsc, bcc, dtc = _inproj0(xc, l0_norm1_w, mod_ctx[0], w_in0, w_dt0, l0_conv_w, l0_conv_b, lc, F32)
    att_c, kc_heads, vc_heads = _na_ctx(qc, kc, vc, lc)
    ssd_c, sfin_c = _ssd(zc, xsc, bcc, dtc, *ssd_p, seq=lc)

    ql, kl, vl, zl, xsl, bcl, dtl = _inproj0(xl, l0_norm1_w, mod_lat[0], w_in0, w_dt0, l0_conv_w, l0_conv_b, ll, BF16)
    past = cache_l0_na_k.shape[1]
    att_l = _na_lat(ql, kl, vl, cache_l0_na_k.reshape(bl * past, NA_WIDTH),
                    cache_l0_na_v.reshape(bl * past, NA_WIDTH), _na_bias_table(l0_na_bias), ll)
    (ssd_l,) = _ssd(zl, xsl, bcl, dtl, *ssd_p, seq=ll, s0=state_l0_ssd)

    x1 = _out_ffn((xc, xl), [(att_c, att_l), (ssd_c, ssd_l)], wo0, mods[0], l0_norm2_w, *ffn0,
                  tc, tl, ll, split_out=False)

    ret_c, ret_state = _retention(x1, 0, tc, l1_norm1_w, mod_ctx[1], w_in1, l1_ret_decay, l1_ret_norm_w, lc)
    (ret_l,) = _retention(x1, tc, tl, l1_norm1_w, mod_lat[1], w_in1, l1_ret_decay, l1_ret_norm_w, ll,
                          s0=state_l1_ret, rope_tables=_rope_tables(ll))
    y_prompt, y_sample = _out_ffn(x1, [(ret_c, ret_l)], wo1, mods[1], l1_norm2_w, *ffn1,
                                  tc, tl, ll, split_out=True, final_norm_w=final_norm_w)

    return (y_prompt.reshape(bc, lc, d), y_sample.reshape(bl, ll, d),
            kc_heads.reshape(bc, lc, NA_HEADS, NA_HEAD_DIM), vc_heads.reshape(bc, lc, NA_HEADS, NA_HEAD_DIM),
            sfin_c, ret_state)
```

```python
import functools

import numpy as np
import jax
import jax.numpy as jnp
from jax import lax
from jax.experimental import pallas as pl
from jax.experimental.pallas import tpu as pltpu

F32 = jnp.float32
BF16 = jnp.bfloat16

D_MODEL = 1024
GRID_W = 64
NA_HEADS = 8
NA_HEAD_DIM = 64
NA_WIDTH = NA_HEADS * NA_HEAD_DIM
NA_WIN_ROWS = 8
NA_WIN_COLS = 16
SSD_HEADS = 16
SSD_HEAD_DIM = 64
SSD_D_INNER = SSD_HEADS * SSD_HEAD_DIM
SSD_GROUPS = 2
SSD_STATE = 128
SSD_CONV = 5
SSD_XBC = SSD_D_INNER + 2 * SSD_GROUPS * SSD_STATE
RET_HEADS = 4
RET_QK_DIM = 256
RET_V_DIM = 512
RET_QK_W = RET_HEADS * RET_QK_DIM
RET_V_W = RET_HEADS * RET_V_DIM
ROPE_BASE = 10000.0
EPS = 1e-6

LANES = 128
SUBLANES = 8
SEQ_BLOCK = 1024
ROW_TILE = 512
SSD_CHUNK = 128
CONV_ROWS = 256
INTRA_UNROLL = 4
PIECE_LANES = 2 * SSD_HEADS
assert 3 * PIECE_LANES <= LANES
CAST_STEPS = 8
ADALN_TILE = 1536
RET_CHUNK = 256
FFN_CHUNK = 256
VMEM_LIMIT = 56 * 1024 * 1024

_NEG_INF = float("-inf")


def _params(**kw):
    return pltpu.CompilerParams(vmem_limit_bytes=VMEM_LIMIT, **kw)


def _silu(x):
    return x * (1.0 / (1.0 + jnp.exp(-x)))


def _softplus(x):
    return jnp.maximum(x, 0.0) + jnp.log(1.0 + jnp.exp(-jnp.abs(x)))


def _mm(a, b):
    return jnp.dot(a.astype(BF16), b.astype(BF16), preferred_element_type=F32)


def _mm_nt(a, b):
    return lax.dot_general(a.astype(BF16), b.astype(BF16), (((1,), (1,)), ((), ())),
                           preferred_element_type=F32)


def _const_spec(shape, **kw):
    nd = len(shape)
    return pl.BlockSpec(shape, lambda *_: (0,) * nd, **kw)


def _cast_kernel(*refs):
    n = len(refs) // 2
    (wt_ref, tail_ref, *plain_in), (w_ref, tail_out_ref, *plain_out) = refs[:n], refs[n:]
    for j in range(wt_ref.shape[0] // LANES):
        cols = slice(j * LANES, (j + 1) * LANES)
        w_ref[:, cols] = wt_ref[cols, :].T.astype(BF16)
    n_tail = tail_ref.shape[0]
    tail = jnp.concatenate([tail_ref[...], jnp.zeros((LANES - n_tail, LANES), F32)], axis=0)
    tail_out_ref[...] = tail.T.astype(BF16)
    for i_ref, o_ref in zip(plain_in, plain_out):
        o_ref[...] = i_ref[...].astype(o_ref.dtype)


def _to_bf16(wt, *ws):
    steps = CAST_STEPS
    n_all, k = wt.shape
    n_main = n_all // LANES * LANES
    n_tail = n_all - n_main
    assert k // steps == LANES and n_tail % SUBLANES == 0 and n_main % n_tail == 0
    specs = [pl.BlockSpec((w.shape[0] // steps, w.shape[1]), lambda i: (i, 0)) for w in ws]
    assert all(w.shape[0] % (steps * 2 * SUBLANES) == 0 for w in ws)
    return pl.pallas_call(
        _cast_kernel,
        grid=(steps,),
        in_specs=[pl.BlockSpec((n_main, LANES), lambda i: (0, i)),
                  pl.BlockSpec((n_tail, LANES), lambda i: (n_main // n_tail, i))] + specs,
        out_specs=[pl.BlockSpec((LANES, n_main), lambda i: (i, 0)),
                   pl.BlockSpec((LANES, LANES), lambda i: (i, 0))] + specs,
        out_shape=[jax.ShapeDtypeStruct((k, n_main), BF16), jax.ShapeDtypeStruct((k, LANES), BF16)]
        + [jax.ShapeDtypeStruct(w.shape, BF16) for w in ws],
        compiler_params=_params(dimension_semantics=("arbitrary",)),
        name="weights_to_bf16",
    )(wt, wt, *ws)


def _adaln_kernel(c_ref, w_ref, b_ref, o_ref):
    o_ref[...] = _mm(_silu(c_ref[...]), w_ref[...]) + b_ref[...]


def _adaln(cond, mod_w, mod_b):
    r, d = cond.shape
    n = mod_w.shape[1]
    tn = ADALN_TILE
    assert n % tn == 0
    return pl.pallas_call(
        _adaln_kernel,
        grid=(n // tn,),
        in_specs=[_const_spec((r, d)),
                  pl.BlockSpec((d, tn), lambda j: (0, j)),
                  pl.BlockSpec((1, tn), lambda j: (0, j))],
        out_specs=pl.BlockSpec((r, tn), lambda j: (0, j)),
        out_shape=jax.ShapeDtypeStruct((r, n), F32),
        compiler_params=_params(dimension_semantics=("arbitrary",)),
        name="adaln_mod",
    )(cond, mod_w, mod_b.reshape(1, n))


def _norm_mod(x, nw, mod_ref, shift_idx, scale_idx):
    ms = jnp.mean(x * x, axis=-1, keepdims=True)
    h = x * lax.rsqrt(ms + EPS) * nw
    return h * (1.0 + mod_ref[0, scale_idx:scale_idx + 1, :]) + mod_ref[0, shift_idx:shift_idx + 1, :]


def _inproj0_kernel(*refs, seq, halo):
    if halo:
        (x_ref, xp_ref, xn_ref, nw_ref, mod_ref, w_ref, wdt_ref, cw_ref, cb_ref,
         q_ref, k_ref, v_ref, z_ref, xs_ref, bc_ref, dt_ref, xpad) = refs
    else:
        (x_ref, nw_ref, mod_ref, w_ref, wdt_ref, cw_ref, cb_ref,
         q_ref, k_ref, v_ref, z_ref, xs_ref, bc_ref, dt_ref, xpad) = refs
    tm = x_ref.shape[0]
    pad = SUBLANES
    sub = xpad.shape[1] - 2 * pad
    h = _norm_mod(x_ref[...], nw_ref[...], mod_ref, 0, 1)
    hb = h.astype(BF16)
    xbc_col = 3 * NA_WIDTH + SSD_D_INNER
    w_xbc = w_ref[:, xbc_col:xbc_col + SSD_XBC]
    if halo:
        tiles_per_seq = seq // tm
        p = pl.program_id(0) % tiles_per_seq
        h_prev = jnp.where(p > 0, _norm_mod(xp_ref[...], nw_ref[...], mod_ref, 0, 1), 0.0)
        h_next = jnp.where(p < tiles_per_seq - 1, _norm_mod(xn_ref[...], nw_ref[...], mod_ref, 0, 1), 0.0)
        ext = jnp.concatenate([h_prev, h, h_next], axis=0).astype(BF16)
        xpad[0] = jnp.dot(ext, w_xbc, preferred_element_type=F32)
    else:
        xbc = jnp.dot(hb, w_xbc, preferred_element_type=F32)
        for s in range(tm // sub):
            xpad[s, 0:pad, :] = jnp.zeros((pad, SSD_XBC), F32)
            xpad[s, pad:pad + sub, :] = xbc[s * sub:(s + 1) * sub]
            xpad[s, pad + sub:2 * pad + sub, :] = jnp.zeros((pad, SSD_XBC), F32)

    cr = CONV_ROWS

    def conv_chunk(r0):
        s, rs = divmod(r0, sub)
        nwin = cr + 2 * pad
        for lt in range(SSD_XBC // LANES):
            cols = slice(lt * LANES, (lt + 1) * LANES)
            win = xpad[s, rs:rs + nwin, cols]
            taps = [cw_ref[k:k + 1, cols] * win for k in range(SSD_CONV)]
            before = taps[1] + pltpu.roll(taps[0], 1, 0)
            after = taps[3] + pltpu.roll(taps[4], nwin - 1, 0)
            conv = taps[2] + pltpu.roll(before, 1, 0) + pltpu.roll(after, nwin - 1, 0)
            act = _silu(conv[pad:pad + cr, :] + cb_ref[:, cols])
            if lt < SSD_D_INNER // LANES:
                xs_ref[r0:r0 + cr, cols] = act
            else:
                bc_ref[r0:r0 + cr, lt * LANES - SSD_D_INNER:(lt + 1) * LANES - SSD_D_INNER] = act.astype(BF16)

    chunks = list(range(0, tm, cr))
    col = 0
    for o_ref in (q_ref, k_ref, v_ref, z_ref):
        if chunks:
            conv_chunk(chunks.pop(0))
        n = o_ref.shape[1]
        o_ref[...] = jnp.dot(hb, w_ref[:, col:col + n], preferred_element_type=F32).astype(o_ref.dtype)
        col += n
    dt_ref[...] = jnp.dot(hb, wdt_ref[...], preferred_element_type=F32)
    for r0 in chunks:
        conv_chunk(r0)


def _inproj0(x, norm_w, mod, w, w_dt, conv_w, conv_b, seq, qkv_dtype):
    t, d = x.shape
    tm = ROW_TILE
    assert seq % tm == 0 or tm % seq == 0
    halo = seq > tm
    sub = min(seq, tm)
    widths = (NA_WIDTH, NA_WIDTH, NA_WIDTH, SSD_D_INNER, SSD_D_INNER, SSD_XBC - SSD_D_INNER, LANES)
    dtypes = (qkv_dtype,) * 3 + (F32, F32, BF16, F32)
    assert sum(widths) - LANES == w.shape[1] and w_dt.shape == (d, LANES)
    per = max(seq // tm, 1) if mod.shape[0] > 1 else t // tm
    in_specs = [pl.BlockSpec((tm, d), lambda i: (i, 0))]
    args = [x]
    if halo:
        rb = tm // SUBLANES
        last = t // SUBLANES - 1
        in_specs += [pl.BlockSpec((SUBLANES, d), lambda i: (jnp.maximum(i * rb - 1, 0), 0)),
                     pl.BlockSpec((SUBLANES, d), lambda i: (jnp.minimum((i + 1) * rb, last), 0))]
        args += [x, x]
    in_specs += [_const_spec((1, d)),
                 pl.BlockSpec((1, 6, d), lambda i: (i // per, 0, 0)),
                 _const_spec(w.shape), _const_spec(w_dt.shape),
                 _const_spec((SUBLANES, SSD_XBC)), _const_spec((1, SSD_XBC))]
    args += [norm_w.reshape(1, d), mod, w, w_dt,
             jnp.pad(conv_w.astype(F32), ((0, SUBLANES - SSD_CONV), (0, 0))), conv_b.astype(F32).reshape(1, -1)]
    return pl.pallas_call(
        functools.partial(_inproj0_kernel, seq=seq, halo=halo),
        grid=(t // tm,),
        in_specs=in_specs,
        out_specs=[pl.BlockSpec((tm, n), lambda i: (i, 0)) for n in widths],
        out_shape=[jax.ShapeDtypeStruct((t, n), dt) for n, dt in zip(widths, dtypes)],
        scratch_shapes=[pltpu.VMEM((tm // sub, sub + 2 * SUBLANES, SSD_XBC), F32)],
        compiler_params=_params(dimension_semantics=("arbitrary",)),
        name="l0_in_proj",
    )(*args)


def _swap_lane_pairs(x):
    even = (lax.broadcasted_iota(jnp.int32, x.shape, 1) & 1) == 0
    return jnp.where(even, pltpu.roll(x, LANES - 1, 1), pltpu.roll(x, 1, 1))


def _lane_lo(shape):
    return lax.broadcasted_iota(jnp.int32, shape, 1) < NA_HEAD_DIM


def _head_lanes(x, half):
    lo = _lane_lo(x.shape)
    return jnp.where(lo if half == 0 else jnp.logical_not(lo), x, jnp.zeros_like(x))


def _softmax_pv(q2, keys, vals, biases):
    acc = None
    for half in (0, 1):
        qm = _head_lanes(q2, half)
        scores = []
        for kk, bb in zip(keys, biases):
            s = _mm_nt(qm, kk)
            if bb is not None:
                s = bb(half, s)
            scores.append(s)
        mx = functools.reduce(jnp.maximum, [jnp.max(s, axis=-1, keepdims=True) for s in scores])
        es = [jnp.exp(s - mx) for s in scores]
        den = functools.reduce(jnp.add, [jnp.sum(e, axis=-1, keepdims=True) for e in es])
        pv = functools.reduce(jnp.add, [_mm(e, _head_lanes(vv, half)) for e, vv in zip(es, vals)])
        out = pv * (1.0 / den)
        acc = out if acc is None else acc + out
    return acc


def _na_ctx_kernel(q_ref, k_ref, v_ref, o_ref, kh_ref, vh_ref, *, seq):
    scale = NA_HEAD_DIM ** -0.5
    nseq = q_ref.shape[0] // seq
    for s in range(nseq):
        r = slice(s * seq, (s + 1) * seq)
        for hp in range(NA_WIDTH // LANES):
            c = slice(hp * LANES, (hp + 1) * LANES)
            out = _softmax_pv(q_ref[r, c] * scale, [k_ref[r, c]], [v_ref[r, c]], [None])
            o_ref[r, c] = out.astype(o_ref.dtype)
    kh_ref[...] = k_ref[...].reshape(kh_ref.shape)
    vh_ref[...] = v_ref[...].reshape(vh_ref.shape)


def _na_ctx(q, k, v, seq):
    t, w = q.shape
    spec = pl.BlockSpec((SEQ_BLOCK, w), lambda i: (i, 0))
    hspec = pl.BlockSpec((SEQ_BLOCK, NA_HEADS, NA_HEAD_DIM), lambda i: (i, 0, 0))
    hshape = jax.ShapeDtypeStruct((t, NA_HEADS, NA_HEAD_DIM), k.dtype)
    return pl.pallas_call(
        functools.partial(_na_ctx_kernel, seq=seq),
        grid=(t // SEQ_BLOCK,),
        in_specs=[spec, spec, spec],
        out_specs=[spec, hspec, hspec],
        out_shape=[jax.ShapeDtypeStruct((t, w), BF16), hshape, hshape],
        compiler_params=_params(dimension_semantics=("arbitrary",)),
        name="na_context",
    )(q, k, v)


NA_QBLK = 256


def _na_lat_kernel(q_ref, k_ref, v_ref, kc_ref, vc_ref, tab_ref, o_ref):
    scale = NA_HEAD_DIM ** -0.5
    seq = q_ref.shape[0]
    nblk = seq // NA_QBLK
    rows_per_blk = NA_QBLK // GRID_W
    grid_rows = seq // GRID_W
    n_off = 2 * NA_WIN_ROWS

    qc = lax.broadcasted_iota(jnp.int32, (GRID_W, LANES), 0)
    lane = lax.broadcasted_iota(jnp.int32, (GRID_W, LANES), 1)
    kc = lane & (GRID_W - 1)
    c0 = jnp.clip(qc - NA_WIN_COLS // 2, 0, GRID_W - NA_WIN_COLS)
    col_ok = (kc >= c0) & (kc < c0 + NA_WIN_COLS)
    first = lane < GRID_W
    pair_bias = [[jnp.where(col_ok,
                            pltpu.roll(jnp.broadcast_to(tab_ref[half, e:e + 1, :], (GRID_W, LANES)), 0, 1,
                                       stride=1, stride_axis=0),
                            _NEG_INF)
                  for e in range(n_off)] for half in (0, 1)]
    neg = jnp.full((GRID_W, LANES), _NEG_INF, F32)
    k_ctx, v_ctx = [jnp.concatenate([ref[:, j * LANES:(j + 1) * LANES].T for j in range(ref.shape[1] // LANES)],
                                    axis=0) for ref in (kc_ref, vc_ref)]

    def band_start(r):
        return min(max(r - NA_WIN_ROWS // 2, 0), grid_rows - NA_WIN_ROWS)

    for i in range(nblk):
        ws_row = band_start(i * rows_per_blk) // 2 * 2
        we_row = min((band_start((i + 1) * rows_per_blk - 1) + NA_WIN_ROWS + 1) // 2 * 2, grid_rows)
        ws, kwin = ws_row * GRID_W, (we_row - ws_row) * GRID_W

        def add_bias(half, s, i=i, ws_row=ws_row, kwin=kwin):
            out_rows = []
            for rq in range(rows_per_blk):
                r = i * rows_per_blk + rq
                r0 = band_start(r)
                tiles = []
                for kp in range(kwin // LANES):
                    kr = ws_row + 2 * kp
                    ok0 = r0 <= kr < r0 + NA_WIN_ROWS
                    ok1 = r0 <= kr + 1 < r0 + NA_WIN_ROWS
                    if ok0 or ok1:
                        tile = pair_bias[half][kr - r + NA_WIN_ROWS]
                        if not ok1:
                            tile = jnp.where(first, tile, _NEG_INF)
                        elif not ok0:
                            tile = jnp.where(first, _NEG_INF, tile)
                    else:
                        tile = neg
                    tiles.append(s[rq * GRID_W:(rq + 1) * GRID_W, kp * LANES:(kp + 1) * LANES] + tile)
                out_rows.append(jnp.concatenate(tiles, axis=1))
            return jnp.concatenate(out_rows, axis=0)

        rows = slice(i * NA_QBLK, (i + 1) * NA_QBLK)
        kw = k_ref[ws:ws + kwin, :]
        vw = v_ref[ws:ws + kwin, :]
        out = _softmax_pv(q_ref[rows, :] * scale, [kw, k_ctx], [vw, v_ctx], [add_bias, None])
        o_ref[rows, :] = out.astype(o_ref.dtype)


def _na_bias_table(rel_bias):
    h, n_dr, n_dc = rel_bias.shape
    half_dc = n_dc // 2
    tz = jnp.pad(rel_bias.astype(F32), ((0, 0), (1, 1), (0, 0)))
    lo, hi = tz[:, 0:n_dr + 1], tz[:, 1:n_dr + 2]
    gap = jnp.zeros((h, n_dr + 1, GRID_W - half_dc - 1 - half_dc), F32)
    return jnp.concatenate([lo[..., half_dc:], gap, hi, gap, lo[..., :half_dc]], axis=-1)


def _na_lat(q, k, v, k_ctx_t, v_ctx_t, bias_table, seq):
    t, w = q.shape
    b = t // seq
    npair = w // LANES
    past = k_ctx_t.shape[1]
    assert k_ctx_t.shape[0] == b * w and past % LANES == 0
    kspec = pl.BlockSpec((seq, LANES), lambda hp, bb: (bb, hp))
    cspec = pl.BlockSpec((LANES, past), lambda hp, bb: (bb * npair + hp, 0))
    tspec = pl.BlockSpec((2,) + bias_table.shape[1:], lambda hp, bb: (hp, 0, 0))
    return pl.pallas_call(
        _na_lat_kernel,
        grid=(npair, b),
        in_specs=[kspec, kspec, kspec, cspec, cspec, tspec],
        out_specs=kspec,
        out_shape=jax.ShapeDtypeStruct((t, w), BF16),
        compiler_params=_params(dimension_semantics=("arbitrary", "arbitrary")),
        name="na_latent",
    )(q, k, v, k_ctx_t, v_ctx_t, bias_table)


def _pack3(v, lane):
    vm = jnp.where(lane < PIECE_LANES, v, 0.0)
    hi = vm.astype(BF16).astype(F32)
    r1 = vm - hi
    mid = r1.astype(BF16).astype(F32)
    lo = r1 - mid
    return (hi + pltpu.roll(mid, PIECE_LANES, 1) + pltpu.roll(lo, 2 * PIECE_LANES, 1)).astype(BF16)


def _unpack3(res):
    return res + pltpu.roll(res, LANES - PIECE_LANES, 1) + pltpu.roll(res, LANES - 2 * PIECE_LANES, 1)


def _ssd_selectors():
    r = np.arange(LANES)
    out = []
    for width in (SSD_HEAD_DIM, SSD_CHUNK):
        l = np.arange(SSD_HEADS * width)
        for d in (0, 1):
            sel = (r[:, None] < 3 * PIECE_LANES) & ((r[:, None] % PIECE_LANES) == d * SSD_HEADS + l[None, :] // width)
            out.append(jnp.asarray(sel, BF16))
    return out


def _ssd_kernel(*refs, seq, has_state):
    if has_state:
        (z_ref, xs_ref, bc_ref, dtr_ref, dtb_ref, alog_ref, dsk_ref, nw_ref,
         s64f_ref, s64b_ref, s128f_ref, s128b_ref, s0_ref,
         y_ref, yacc, ep, wp, bts, st) = refs
        sfin_ref = None
    else:
        (z_ref, xs_ref, bc_ref, dtr_ref, dtb_ref, alog_ref, dsk_ref, nw_ref,
         s64f_ref, s64b_ref, s128f_ref, s128b_ref,
         y_ref, sfin_ref, yacc, ep, wp, bts, st) = refs
        s0_ref = None
    ch = SSD_CHUNK
    nseq = z_ref.shape[0] // seq
    nch = seq // ch
    assert nch % 2 == 0
    lane = lax.broadcasted_iota(jnp.int32, (ch, LANES), 1)
    ri = lax.broadcasted_iota(jnp.int32, (ch, ch), 0)
    ci = lax.broadcasted_iota(jnp.int32, (ch, ch), 1)
    keeps = (ci <= ri, ci >= ri)
    tril = jnp.where(keeps[0], 1.0, 0.0).astype(BF16)
    triu = jnp.where(keeps[1], 1.0, 0.0).astype(BF16)
    lane_lo = lane < SSD_HEAD_DIM
    fwd_lane = lane < SSD_HEADS
    neg_a = -jnp.exp(alog_ref[...]) * np.log2(np.e).astype(np.float32)
    gw = SSD_D_INNER // SSD_GROUPS
    sel64s = (s64f_ref, s64b_ref)
    sel128s = (s128f_ref, s128b_ref)
    edges = (ch - 1, 0)

    def intra_body(c, _):
        rows = pl.ds(pl.multiple_of(c * ch, ch), ch)
        x_c = xs_ref[rows, :]
        dt = _softplus(dtr_ref[rows, :] + dtb_ref[...])
        la = _pack3(dt * neg_a, lane)
        cs = jnp.where(fwd_lane,
                       _unpack3(jnp.dot(tril, la, preferred_element_type=F32)),
                       _unpack3(jnp.dot(triu, la, preferred_element_type=F32)))
        tot = jnp.where(fwd_lane, cs[ch - 1:ch, :], cs[0:1, :])
        ep[rows, :] = _pack3(jnp.exp2(cs), lane)
        wp[rows, :] = _pack3(dt * jnp.exp2(tot - cs), lane)
        csp = _pack3(cs, lane)
        col = [jnp.dot(csp, sel128s[d][...], preferred_element_type=F32) for d in (0, 1)]
        cs_t = cs.T
        dt_t = dt.T
        y_intra = []
        for g in range(SSD_GROUPS):
            b_g = bc_ref[rows, g * SSD_STATE:(g + 1) * SSD_STATE]
            cofs = SSD_GROUPS * SSD_STATE
            c_g = bc_ref[rows, cofs + g * SSD_STATE:cofs + (g + 1) * SSD_STATE]
            bts[c, g] = b_g.T
            gram = _mm_nt(c_g, b_g)
            for pp in range(gw // LANES):
                p = g * (gw // LANES) + pp
                ws = []
                for d in (0, 1):
                    for half in (0, 1):
                        h = 2 * p + half
                        hd = d * SSD_HEADS + h
                        diff = col[d][:, h * ch:(h + 1) * ch] - cs_t[hd:hd + 1, :]
                        wm = jnp.exp2(jnp.where(keeps[d], diff, _NEG_INF)) * gram * dt_t[hd:hd + 1, :]
                        ws.append(wm.astype(BF16))
                xp = x_c[:, p * LANES:(p + 1) * LANES]
                xcat = jnp.concatenate([jnp.where(lane_lo, xp, 0.0), jnp.where(lane_lo, 0.0, xp)],
                                       axis=0).astype(BF16)
                y_intra.append(jnp.dot(jnp.concatenate(ws, axis=1), jnp.concatenate([xcat, xcat], axis=0),
                                       preferred_element_type=F32))
        yacc[rows, :] = dsk_ref[...] * x_c + jnp.concatenate(y_intra, axis=1)
        return 0

    lax.fori_loop(0, nseq * nch, intra_body, 0, unroll=INTRA_UNROLL)

    def seq_body(s, _):
        base = pl.multiple_of(s * seq, seq)
        for d in (0, 1):
            if has_state:
                for h in range(0, SSD_HEADS, 2):
                    pair = jnp.concatenate([s0_ref[0, d, h], s0_ref[0, d, h + 1]], axis=0)
                    st[d, :, h * SSD_HEAD_DIM:(h + 2) * SSD_HEAD_DIM] = pair.T
            else:
                st[d] = jnp.zeros((SSD_STATE, SSD_D_INNER), F32)

        def chunk(d, c):
            rows = pl.ds(pl.multiple_of(base + c * ch, ch), ch)
            e64 = jnp.dot(ep[rows, :], sel64s[d][...], preferred_element_type=F32)
            w64 = jnp.dot(wp[rows, :], sel64s[d][...], preferred_element_type=F32)
            state = st[d]
            state_b = state.astype(BF16)
            xw = (xs_ref[rows, :] * w64).astype(BF16)
            cofs = SSD_GROUPS * SSD_STATE
            y_state = [jnp.dot(bc_ref[rows, cofs + g * SSD_STATE:cofs + (g + 1) * SSD_STATE],
                               state_b[:, g * gw:(g + 1) * gw], preferred_element_type=F32)
                       for g in range(SSD_GROUPS)]
            upd = [jnp.dot(bts[s * nch + c, g], xw[:, g * gw:(g + 1) * gw], preferred_element_type=F32)
                   for g in range(SSD_GROUPS)]
            yacc[rows, :] = yacc[rows, :] + jnp.concatenate(y_state, axis=1) * e64
            st[d] = e64[edges[d]:edges[d] + 1, :] * state + jnp.concatenate(upd, axis=1)

        def pair_body(j, _):
            for step in (0, 1):
                chunk(0, 2 * j + step)
                chunk(1, nch - 1 - 2 * j - step)
            return 0

        lax.fori_loop(0, nch // 2, pair_body, 0)
        if sfin_ref is not None:
            for d in (0, 1):
                for h in range(SSD_HEADS):
                    sfin_ref[s, d, h] = st[d, :, h * SSD_HEAD_DIM:(h + 1) * SSD_HEAD_DIM]
        return 0

    lax.fori_loop(0, nseq, seq_body, 0)

    def out_body(c, _):
        rows = pl.ds(pl.multiple_of(c * ch, ch), ch)
        y = yacc[rows, :] * _silu(z_ref[rows, :])
        outs = []
        for g in range(SSD_GROUPS):
            yg = y[:, g * gw:(g + 1) * gw]
            outs.append(yg * lax.rsqrt(jnp.mean(yg * yg, axis=-1, keepdims=True) + EPS))
        y_ref[rows, :] = (jnp.concatenate(outs, axis=1) * nw_ref[...]).astype(y_ref.dtype)
        return 0

    lax.fori_loop(0, nseq * nch, out_body, 0)


def _ssd(z, xs, bc, dtr, a_log, dt_bias, d_skip, norm_w, seq, s0=None):
    t = z.shape[0]
    nseq = SEQ_BLOCK // seq
    nblk = t // SEQ_BLOCK
    nch = seq // SSD_CHUNK
    pad32 = lambda a: jnp.pad(a.astype(F32).reshape(1, -1), ((0, 0), (0, LANES - 2 * SSD_HEADS)))
    consts = [pad32(dt_bias), pad32(a_log),
              jnp.repeat(d_skip.astype(F32), SSD_HEAD_DIM).reshape(1, -1),
              norm_w.astype(F32).reshape(1, -1)] + _ssd_selectors()
    row = lambda n: pl.BlockSpec((SEQ_BLOCK, n), lambda i: (i, 0))
    in_specs = ([row(SSD_D_INNER), row(SSD_D_INNER), row(bc.shape[1]), row(LANES)]
                + [_const_spec(c.shape) for c in consts])
    args = [z, xs, bc, dtr] + consts
    out_specs = [row(SSD_D_INNER)]
    out_shape = [jax.ShapeDtypeStruct((t, SSD_D_INNER), BF16)]
    state_shape = (2, SSD_STATE, SSD_D_INNER)
    io_state = (2, SSD_HEADS, SSD_STATE, SSD_HEAD_DIM)
    if s0 is not None:
        assert s0.shape[1:] == (2, SSD_HEADS, SSD_HEAD_DIM, SSD_STATE) and 2 * SSD_HEAD_DIM == SSD_STATE
        in_specs.append(pl.BlockSpec((1,) + s0.shape[1:], lambda i: (i, 0, 0, 0, 0)))
        args.append(s0)
    else:
        out_specs.append(pl.BlockSpec((nseq,) + io_state, lambda i: (i, 0, 0, 0, 0)))
        out_shape.append(jax.ShapeDtypeStruct((t // seq,) + io_state, F32))
    nblk_ch = SEQ_BLOCK // SSD_CHUNK
    scratch = [pltpu.VMEM((SEQ_BLOCK, SSD_D_INNER), F32),
               pltpu.VMEM((SEQ_BLOCK, LANES), BF16), pltpu.VMEM((SEQ_BLOCK, LANES), BF16),
               pltpu.VMEM((nblk_ch, SSD_GROUPS, SSD_STATE, SSD_CHUNK), BF16),
               pltpu.VMEM(state_shape, F32)]
    return pl.pallas_call(
        functools.partial(_ssd_kernel, seq=seq, has_state=s0 is not None),
        grid=(nblk,),
        in_specs=in_specs,
        out_specs=out_specs,
        out_shape=out_shape,
        scratch_shapes=scratch,
        compiler_params=_params(dimension_semantics=("arbitrary",)),
        name="ssd_mixer",
    )(*args)


def _log_sigmoid(x):
    return -_softplus(-x)


def _ret_kernel(*refs, seq, has_state, rope):
    refs = list(refs)
    x_ref, n1_ref, mod_ref, wq_ref, wk_ref, wv_ref, wg_ref = refs[:7]
    del refs[:7]
    cos_ref, sin_ref = (refs.pop(0), refs.pop(0)) if rope else (None, None)
    dec_ref, nw_ref = refs.pop(0), refs.pop(0)
    s0_ref = refs.pop(0) if has_state else None
    y_ref = refs.pop(0)
    sfin_ref = None if has_state else refs.pop(0)
    hb_s, q_ref, k_ref, v_ref, g_ref, yacc, st = refs

    @pl.when(pl.program_id(1) == 0)
    def _():
        hb_s[...] = _norm_mod(x_ref[...], n1_ref[...], mod_ref, 0, 1).astype(BF16)

    hb = hb_s[...]
    for o_ref, w_ref, scale in ((q_ref, wq_ref, 1.0), (k_ref, wk_ref, RET_QK_DIM ** -0.5)):
        y = jnp.dot(hb, w_ref[...], preferred_element_type=F32) * scale
        if rope:
            for j in range(RET_QK_DIM // LANES):
                lanes = slice(j * LANES, (j + 1) * LANES)
                yj = y[:, lanes]
                rot = yj * cos_ref[:, lanes] + _swap_lane_pairs(yj) * sin_ref[:, lanes]
                o_ref[:, lanes] = rot.astype(o_ref.dtype)
        else:
            o_ref[...] = y.astype(o_ref.dtype)
    v_ref[...] = jnp.dot(hb, wv_ref[...], preferred_element_type=F32).astype(v_ref.dtype)
    g_ref[...] = jnp.dot(hb, wg_ref[...], preferred_element_type=F32).astype(g_ref.dtype)

    ch = RET_CHUNK
    nseq = q_ref.shape[0] // seq
    nch = seq // ch
    gf = _log_sigmoid(dec_ref[0, 0:1, :])
    gb = _log_sigmoid(dec_ref[0, 1:2, :])
    ri = lax.broadcasted_iota(jnp.int32, (ch, ch), 0)
    ci = lax.broadcasted_iota(jnp.int32, (ch, ch), 1)
    dist = (ri - ci).astype(F32)
    gfk, gbk = gf[:, :ch], gb[:, :ch]
    decay = (jnp.where(ci <= ri, jnp.exp(jnp.where(ci <= ri, dist, 0.0) * gfk), 0.0)
             + jnp.where(ci >= ri, jnp.exp(jnp.where(ci >= ri, -dist, 0.0) * gbk), 0.0))
    pos = lax.broadcasted_iota(jnp.int32, (ch, RET_QK_DIM), 0).astype(F32)
    gfq, gbq = gf[:, :RET_QK_DIM], gb[:, :RET_QK_DIM]
    e_f = jnp.exp((pos + 1.0) * gfq)
    e_b = jnp.exp((ch - pos) * gbq)
    tail_f = jnp.exp((ch - 1.0 - pos) * gfq)
    tail_b = jnp.exp(pos * gbq)
    dec_f = jnp.exp(ch * gf)
    dec_b = jnp.exp(ch * gb)

    for s in range(nseq):
        for d in (0, 1):
            if has_state:
                st[d] = s0_ref[0, d, 0]
            else:
                st[d] = jnp.zeros((RET_QK_DIM, RET_V_DIM), F32)
        for c in range(nch):
            rows = slice(s * seq + c * ch, s * seq + (c + 1) * ch)
            q, k, v = q_ref[rows, :], k_ref[rows, :].astype(F32), v_ref[rows, :]
            y = _mm(_mm_nt(q, k) * decay, v)
            if has_state or c > 0:
                y = y + _mm(q.astype(F32) * e_f, st[0])
            yacc[rows, :] = y
            st[0] = dec_f * st[0] + _mm((k * tail_f).T, v)
        for c in reversed(range(nch)):
            rows = slice(s * seq + c * ch, s * seq + (c + 1) * ch)
            q, k, v = q_ref[rows, :], k_ref[rows, :].astype(F32), v_ref[rows, :]
            if has_state or c < nch - 1:
                yacc[rows, :] = yacc[rows, :] + _mm(q.astype(F32) * e_b, st[1])
            st[1] = dec_b * st[1] + _mm((k * tail_b).T, v)
        if sfin_ref is not None:
            for d in (0, 1):
                sfin_ref[s, d, 0] = st[d]
        for c in range(nch):
            rows = slice(s * seq + c * ch, s * seq + (c + 1) * ch)
            y = yacc[rows, :]
            y = y * lax.rsqrt(jnp.mean(y * y, axis=-1, keepdims=True) + EPS)
            y_ref[rows, :] = (y * nw_ref[...] * _silu(g_ref[rows, :].astype(F32))).astype(y_ref.dtype)


def _retention(x, row0, t, norm1_w, mod, w_in, ret_decay, norm_w, seq, s0=None, rope_tables=None):
    d = x.shape[1]
    nseq = SEQ_BLOCK // seq
    nblk = t // SEQ_BLOCK
    blk0 = row0 // SEQ_BLOCK
    per = 1 if mod.shape[0] > 1 else nblk
    dec = jnp.broadcast_to(jnp.pad(ret_decay.astype(F32).T, ((0, 0), (0, SUBLANES - 2)))[:, :, None],
                           (RET_HEADS, SUBLANES, RET_V_DIM))
    vspec = pl.BlockSpec((SEQ_BLOCK, RET_V_DIM), lambda i, h: (i, h))
    qk_blocks, v_blocks = RET_QK_W // RET_QK_DIM, RET_V_W // RET_V_DIM
    wq_spec = pl.BlockSpec((d, RET_QK_DIM), lambda i, h: (0, h))
    wk_spec = pl.BlockSpec((d, RET_QK_DIM), lambda i, h: (0, qk_blocks + h))
    wv_spec = pl.BlockSpec((d, RET_V_DIM), lambda i, h: (0, 2 * RET_QK_W // RET_V_DIM + h))
    wg_spec = pl.BlockSpec((d, RET_V_DIM), lambda i, h: (0, 2 * RET_QK_W // RET_V_DIM + v_blocks + h))
    in_specs = [pl.BlockSpec((SEQ_BLOCK, d), lambda i, h: (blk0 + i, 0)), _const_spec((1, d)),
                pl.BlockSpec((1, 6, d), lambda i, h: (i // per, 0, 0)),
                wq_spec, wk_spec, wv_spec, wg_spec]
    args = [x, norm1_w.reshape(1, d), mod, w_in, w_in, w_in, w_in]
    if rope_tables is not None:
        in_specs += [_const_spec(rope_tables[0].shape)] * 2
        args += list(rope_tables)
    in_specs += [pl.BlockSpec((1, SUBLANES, RET_V_DIM), lambda i, h: (h, 0, 0)),
                 pl.BlockSpec((1, RET_V_DIM), lambda i, h: (0, h))]
    args += [dec, norm_w.astype(F32).reshape(1, -1)]
    out_specs = [vspec]
    out_shape = [jax.ShapeDtypeStruct((t, RET_V_W), BF16)]
    if s0 is not None:
        in_specs.append(pl.BlockSpec((1, 2, 1, RET_QK_DIM, RET_V_DIM), lambda i, h: (i, 0, h, 0, 0)))
        args.append(s0)
    else:
        out_specs.append(pl.BlockSpec((nseq, 2, 1, RET_QK_DIM, RET_V_DIM), lambda i, h: (i, 0, h, 0, 0)))
        out_shape.append(jax.ShapeDtypeStruct((t // seq, 2, RET_HEADS, RET_QK_DIM, RET_V_DIM), F32))
    return pl.pallas_call(
        functools.partial(_ret_kernel, seq=seq, has_state=s0 is not None, rope=rope_tables is not None),
        grid=(nblk, RET_HEADS),
        in_specs=in_specs,
        out_specs=out_specs,
        out_shape=out_shape,
        scratch_shapes=[pltpu.VMEM((SEQ_BLOCK, d), BF16),
                        pltpu.VMEM((SEQ_BLOCK, RET_QK_DIM), BF16), pltpu.VMEM((SEQ_BLOCK, RET_QK_DIM), BF16),
                        pltpu.VMEM((SEQ_BLOCK, RET_V_DIM), BF16), pltpu.VMEM((SEQ_BLOCK, RET_V_DIM), BF16),
                        pltpu.VMEM((SEQ_BLOCK, RET_V_DIM), F32),
                        pltpu.VMEM((2, RET_QK_DIM, RET_V_DIM), F32)],
        compiler_params=_params(dimension_semantics=("arbitrary", "arbitrary")),
        name="retention_mixer",
    )(*args)


def _out_ffn_kernel(*refs, n_x, n_mix, n_out, n_ctx_steps, final_norm):
    refs = list(refs)
    x_refs = [refs.pop(0) for _ in range(n_x)]
    mix_refs = [(refs.pop(0), refs.pop(0)) for _ in range(n_mix)]
    wo_ref, mod_ref, n2_ref, w1_ref, w3_ref, w2_ref = (refs.pop(0) for _ in range(6))
    fn_ref = refs.pop(0) if final_norm else None
    o_refs = [refs.pop(0) for _ in range(n_out)]
    (act,) = refs
    is_ctx = pl.program_id(0) < n_ctx_steps

    def pick(pair):
        return pair[0][...] if len(pair) == 1 else jnp.where(is_ctx, pair[0][...], pair[1][...])

    mixed, row = None, 0
    for pair in mix_refs:
        k = pair[0].shape[1]
        term = jnp.dot(pick(pair), wo_ref[row:row + k, :], preferred_element_type=F32)
        mixed = term if mixed is None else mixed + term
        row += k
    x1 = pick(x_refs) + mod_ref[0, 2:3, :] * mixed
    hb = _norm_mod(x1, n2_ref[...], mod_ref, 3, 4).astype(BF16)
    for c in range(w1_ref.shape[1] // FFN_CHUNK):
        cols = slice(c * FFN_CHUNK, (c + 1) * FFN_CHUNK)
        h1 = jnp.dot(hb, w1_ref[:, cols], preferred_element_type=F32)
        h3 = jnp.dot(hb, w3_ref[:, cols], preferred_element_type=F32)
        act[:, cols] = (_silu(h1) * h3).astype(BF16)
    x2 = x1 + mod_ref[0, 5:6, :] * jnp.dot(act[...], w2_ref[...], preferred_element_type=F32)
    if final_norm:
        ms = jnp.mean(x2 * x2, axis=-1, keepdims=True)
        x2 = x2 * lax.rsqrt(ms + EPS) * fn_ref[...]
    if n_out == 1:
        o_refs[0][...] = x2
    else:
        @pl.when(is_ctx)
        def _():
            o_refs[0][...] = x2

        @pl.when(jnp.logical_not(is_ctx))
        def _():
            o_refs[1][...] = x2


def _out_ffn(x, mixes, wo, mod, norm2_w, w1, w3, w2, t_ctx, t_lat, lat_seq, split_out, final_norm_w=None):
    tm = ROW_TILE
    d = wo.shape[1]
    n0, n1 = t_ctx // tm, t_lat // tm
    per = lat_seq // tm
    ctx_rows = lambda n: pl.BlockSpec((tm, n), lambda i: (jnp.minimum(i, n0 - 1), 0))
    lat_rows = lambda n: pl.BlockSpec((tm, n), lambda i: (jnp.maximum(i - n0, 0), 0))
    all_rows = lambda n: pl.BlockSpec((tm, n), lambda i: (i, 0))
    single = dict(pipeline_mode=pl.Buffered(1))
    x = tuple(x) if isinstance(x, (tuple, list)) else (x,)
    in_specs = [all_rows(d)] if len(x) == 1 else [ctx_rows(d), lat_rows(d)]
    args = list(x)
    for m_ctx, m_lat in mixes:
        in_specs += [ctx_rows(m_ctx.shape[1]), lat_rows(m_lat.shape[1])]
        args += [m_ctx, m_lat]
    in_specs += [_const_spec(wo.shape, **single),
                 pl.BlockSpec((1, 6, d), lambda i: (jnp.where(i < n0, 0, 1 + (i - n0) // per), 0, 0)),
                 _const_spec((1, d))]
    in_specs += [_const_spec(w.shape, **single) for w in (w1, w3, w2)]
    args += [wo, mod, norm2_w.reshape(1, d), w1, w3, w2]
    if final_norm_w is not None:
        in_specs.append(_const_spec((1, d)))
        args.append(final_norm_w.reshape(1, d))
    if split_out:
        out_specs = [ctx_rows(d), lat_rows(d)]
        out_shape = [jax.ShapeDtypeStruct((t_ctx, d), F32), jax.ShapeDtypeStruct((t_lat, d), F32)]
    else:
        out_specs = [all_rows(d)]
        out_shape = [jax.ShapeDtypeStruct((t_ctx + t_lat, d), F32)]
    out = pl.pallas_call(
        functools.partial(_out_ffn_kernel, n_x=len(x), n_mix=len(mixes), n_out=len(out_specs), n_ctx_steps=n0,
                          final_norm=final_norm_w is not None),
        grid=(n0 + n1,),
        in_specs=in_specs,
        out_specs=out_specs,
        out_shape=out_shape,
        scratch_shapes=[pltpu.VMEM((tm, w1.shape[1]), BF16)],
        compiler_params=_params(dimension_semantics=("arbitrary",)),
        name="out_proj_ffn",
    )(*args)
    return out if split_out else out[0]


def _rope_tables(seq):
    half = RET_QK_DIM // 2
    t = jnp.arange(seq)
    row = (t // GRID_W).astype(F32)
    col = (t % GRID_W).astype(F32)
    freqs = ROPE_BASE ** (-jnp.arange(0, half, 2, dtype=F32) / half)
    ang = jnp.concatenate([row[:, None] * freqs, col[:, None] * freqs], axis=-1)
    cos = jnp.repeat(jnp.cos(ang), 2, axis=1)
    sin = jnp.stack([-jnp.sin(ang), jnp.sin(ang)], axis=-1).reshape(seq, RET_QK_DIM)
    return cos, sin


def kernel(x_prompt, x_sample, cache_l0_na_k, cache_l0_na_v, state_l0_ssd, state_l1_ret, c, c_ctx,
           l0_norm1_w, l0_norm2_w, l0_mod_w, l0_mod_b, l0_w_in, l0_w_out, l0_na_bias, l0_conv_w, l0_conv_b,
           l0_ssd_a_log, l0_ssd_dt_bias, l0_ssd_d, l0_ssd_norm_w, l0_ffn_w1, l0_ffn_w3, l0_ffn_w2,
           l1_norm1_w, l1_norm2_w, l1_mod_w, l1_mod_b, l1_w_in, l1_w_out, l1_ret_decay, l1_ret_norm_w,
           l1_ffn_w1, l1_ffn_w3, l1_ffn_w2, final_norm_w):
    bc, lc, d = x_prompt.shape
    bl, ll, _ = x_sample.shape
    assert d == D_MODEL and ll == SEQ_BLOCK and SEQ_BLOCK % lc == 0 and bc % (SEQ_BLOCK // lc) == 0
    tc, tl = bc * lc, bl * ll
    xc = x_prompt.reshape(tc, d)
    xl = x_sample.reshape(tl, d)

    nrow = SUBLANES * ((1 + bl + SUBLANES - 1) // SUBLANES)
    cond = jnp.concatenate([c_ctx[None], c, jnp.zeros((nrow - 1 - bl, d), F32)], axis=0)
    mods = [_adaln(cond, mod_w, mod_b).reshape(nrow, 6, d)
            for mod_w, mod_b in ((l0_mod_w, l0_mod_b), (l1_mod_w, l1_mod_b))]
    mod_ctx = [m[0:1] for m in mods]
    mod_lat = [m[1:1 + bl] for m in mods]

    (w_in0, w_dt0, wo0, ffn0_w1, ffn0_w3, ffn0_w2, w_in1, wo1, ffn1_w1, ffn1_w3, ffn1_w2) = _to_bf16(
        l0_w_in.T, l0_w_out, l0_ffn_w1, l0_ffn_w3, l0_ffn_w2, l1_w_in, l1_w_out, l1_ffn_w1, l1_ffn_w3, l1_ffn_w2)
    ffn0 = (ffn0_w1, ffn0_w3, ffn0_w2)
    ffn1 = (ffn1_w1, ffn1_w3, ffn1_w2)
    ssd_p = (l0_ssd_a_log, l0_ssd_dt_bias, l0_ssd_d, l0_ssd_norm_w)

    qc, kc, vc, zc, xsc, bcc, dtc = _inproj0(xc, l0_norm1_w, mod_ctx[0], w_in0, w_dt0, l0_conv_w, l0_conv_b, lc, F32)
    att_c, kc_heads, vc_heads = _na_ctx(qc, kc, vc, lc)
    ssd_c, sfin_c = _ssd(zc, xsc, bcc, dtc, *ssd_p, seq=lc)

    ql, kl, vl, zl, xsl, bcl, dtl = _inproj0(xl, l0_norm1_w, mod_lat[0], w_in0, w_dt0, l0_conv_w, l0_conv_b, ll, BF16)
    past = cache_l0_na_k.shape[1]
    cache_t = [jnp.transpose(a, (0, 2, 3, 1)).reshape(bl * NA_WIDTH, past) for a in (cache_l0_na_k, cache_l0_na_v)]
    att_l = _na_lat(ql, kl, vl, *cache_t, _na_bias_table(l0_na_bias), ll)
    (ssd_l,) = _ssd(zl, xsl, bcl, dtl, *ssd_p, seq=ll, s0=jnp.swapaxes(state_l0_ssd, 3, 4))

    x1 = _out_ffn((xc, xl), [(att_c, att_l), (ssd_c, ssd_l)], wo0, mods[0], l0_norm2_w, *ffn0,
                  tc, tl, ll, split_out=False)

    ret_c, ret_state = _retention(x1, 0, tc, l1_norm1_w, mod_ctx[1], w_in1, l1_ret_decay, l1_ret_norm_w, lc)
    (ret_l,) = _retention(x1, tc, tl, l1_norm1_w, mod_lat[1], w_in1, l1_ret_decay, l1_ret_norm_w, ll,
                          s0=state_l1_ret, rope_tables=_rope_tables(ll))
    y_prompt, y_sample = _out_ffn(x1, [(ret_c, ret_l)], wo1, mods[1], l1_norm2_w, *ffn1,
                                  tc, tl, ll, split_out=True, final_norm_w=final_norm_w)

    return (y_prompt.reshape(bc, lc, d), y_sample.reshape(bl, ll, d),
            kc_heads.reshape(bc, lc, NA_HEADS, NA_HEAD_DIM), vc_heads.reshape(bc, lc, NA_HEADS, NA_HEAD_DIM),
            sfin_c, ret_state)
```

```python
import functools

import numpy as np
import jax
import jax.numpy as jnp
from jax import lax
from jax.experimental import pallas as pl
from jax.experimental.pallas import tpu as pltpu

F32 = jnp.float32
BF16 = jnp.bfloat16

D_MODEL = 1024
GRID_W = 64
NA_HEADS = 8
NA_HEAD_DIM = 64
NA_WIDTH = NA_HEADS * NA_HEAD_DIM
NA_WIN_ROWS = 8
NA_WIN_COLS = 16
SSD_HEADS = 16
SSD_HEAD_DIM = 64
SSD_D_INNER = SSD_HEADS * SSD_HEAD_DIM
SSD_GROUPS = 2
SSD_STATE = 128
SSD_CONV = 5
SSD_XBC = SSD_D_INNER + 2 * SSD_GROUPS * SSD_STATE
RET_HEADS = 4
RET_QK_DIM = 256
RET_V_DIM = 512
RET_QK_W = RET_HEADS * RET_QK_DIM
RET_V_W = RET_HEADS * RET_V_DIM
ROPE_BASE = 10000.0
EPS = 1e-6

LANES = 128
SUBLANES = 8
SEQ_BLOCK = 1024
ROW_TILE = 512
SSD_CHUNK = 128
CONV_ROWS = 256
INTRA_UNROLL = 4
PIECE_LANES = 2 * SSD_HEADS
assert 3 * PIECE_LANES <= LANES
CAST_STEPS = 8
ADALN_TILE = 1536
RET_CHUNK = 256
FFN_CHUNK = 256
VMEM_LIMIT = 56 * 1024 * 1024

_NEG_INF = float("-inf")


def _params(**kw):
    return pltpu.CompilerParams(vmem_limit_bytes=VMEM_LIMIT, **kw)


def _silu(x):
    return x * (1.0 / (1.0 + jnp.exp(-x)))


def _softplus(x):
    return jnp.maximum(x, 0.0) + jnp.log(1.0 + jnp.exp(-jnp.abs(x)))


def _mm(a, b):
    return jnp.dot(a.astype(BF16), b.astype(BF16), preferred_element_type=F32)


def _mm_nt(a, b):
    return lax.dot_general(a.astype(BF16), b.astype(BF16), (((1,), (1,)), ((), ())),
                           preferred_element_type=F32)


def _const_spec(shape, **kw):
    nd = len(shape)
    return pl.BlockSpec(shape, lambda *_: (0,) * nd, **kw)


def _cast_kernel(*refs):
    n = len(refs) // 2
    (wt_ref, tail_ref, *plain_in), (w_ref, tail_out_ref, *plain_out) = refs[:n], refs[n:]
    for j in range(wt_ref.shape[0] // LANES):
        cols = slice(j * LANES, (j + 1) * LANES)
        w_ref[:, cols] = wt_ref[cols, :].T.astype(BF16)
    n_tail = tail_ref.shape[0]
    tail = jnp.concatenate([tail_ref[...], jnp.zeros((LANES - n_tail, LANES), F32)], axis=0)
    tail_out_ref[...] = tail.T.astype(BF16)
    for i_ref, o_ref in zip(plain_in, plain_out):
        o_ref[...] = i_ref[...].astype(o_ref.dtype)


def _to_bf16(wt, *ws):
    steps = CAST_STEPS
    n_all, k = wt.shape
    n_main = n_all // LANES * LANES
    n_tail = n_all - n_main
    assert k // steps == LANES and n_tail % SUBLANES == 0 and n_main % n_tail == 0
    specs = [pl.BlockSpec((w.shape[0] // steps, w.shape[1]), lambda i: (i, 0)) for w in ws]
    assert all(w.shape[0] % (steps * 2 * SUBLANES) == 0 for w in ws)
    return pl.pallas_call(
        _cast_kernel,
        grid=(steps,),
        in_specs=[pl.BlockSpec((n_main, LANES), lambda i: (0, i)),
                  pl.BlockSpec((n_tail, LANES), lambda i: (n_main // n_tail, i))] + specs,
        out_specs=[pl.BlockSpec((LANES, n_main), lambda i: (i, 0)),
                   pl.BlockSpec((LANES, LANES), lambda i: (i, 0))] + specs,
        out_shape=[jax.ShapeDtypeStruct((k, n_main), BF16), jax.ShapeDtypeStruct((k, LANES), BF16)]
        + [jax.ShapeDtypeStruct(w.shape, BF16) for w in ws],
        compiler_params=_params(dimension_semantics=("arbitrary",)),
        name="weights_to_bf16",
    )(wt, wt, *ws)


def _adaln_kernel(c_ref, w_ref, b_ref, o_ref):
    o_ref[...] = _mm(_silu(c_ref[...]), w_ref[...]) + b_ref[...]


def _adaln(cond, mod_w, mod_b):
    r, d = cond.shape
    n = mod_w.shape[1]
    tn = ADALN_TILE
    assert n % tn == 0
    return pl.pallas_call(
        _adaln_kernel,
        grid=(n // tn,),
        in_specs=[_const_spec((r, d)),
                  pl.BlockSpec((d, tn), lambda j: (0, j)),
                  pl.BlockSpec((1, tn), lambda j: (0, j))],
        out_specs=pl.BlockSpec((r, tn), lambda j: (0, j)),
        out_shape=jax.ShapeDtypeStruct((r, n), F32),
        compiler_params=_params(dimension_semantics=("arbitrary",)),
        name="adaln_mod",
    )(cond, mod_w, mod_b.reshape(1, n))


def _norm_mod(x, nw, mod_ref, shift_idx, scale_idx):
    ms = jnp.mean(x * x, axis=-1, keepdims=True)
    h = x * lax.rsqrt(ms + EPS) * nw
    return h * (1.0 + mod_ref[0, scale_idx:scale_idx + 1, :]) + mod_ref[0, shift_idx:shift_idx + 1, :]


def _inproj0_kernel(*refs, seq, halo):
    if halo:
        (x_ref, xp_ref, xn_ref, nw_ref, mod_ref, w_ref, wdt_ref, cw_ref, cb_ref,
         q_ref, k_ref, v_ref, z_ref, xs_ref, bc_ref, dt_ref, xpad) = refs
    else:
        (x_ref, nw_ref, mod_ref, w_ref, wdt_ref, cw_ref, cb_ref,
         q_ref, k_ref, v_ref, z_ref, xs_ref, bc_ref, dt_ref, xpad) = refs
    tm = x_ref.shape[0]
    pad = SUBLANES
    sub = xpad.shape[1] - 2 * pad
    h = _norm_mod(x_ref[...], nw_ref[...], mod_ref, 0, 1)
    hb = h.astype(BF16)
    xbc_col = 3 * NA_WIDTH + SSD_D_INNER
    w_xbc = w_ref[:, xbc_col:xbc_col + SSD_XBC]
    if halo:
        tiles_per_seq = seq // tm
        p = pl.program_id(0) % tiles_per_seq
        h_prev = jnp.where(p > 0, _norm_mod(xp_ref[...], nw_ref[...], mod_ref, 0, 1), 0.0)
        h_next = jnp.where(p < tiles_per_seq - 1, _norm_mod(xn_ref[...], nw_ref[...], mod_ref, 0, 1), 0.0)
        ext = jnp.concatenate([h_prev, h, h_next], axis=0).astype(BF16)
        xpad[0] = jnp.dot(ext, w_xbc, preferred_element_type=F32)
    else:
        xbc = jnp.dot(hb, w_xbc, preferred_element_type=F32)
        for s in range(tm // sub):
            xpad[s, 0:pad, :] = jnp.zeros((pad, SSD_XBC), F32)
            xpad[s, pad:pad + sub, :] = xbc[s * sub:(s + 1) * sub]
            xpad[s, pad + sub:2 * pad + sub, :] = jnp.zeros((pad, SSD_XBC), F32)

    cr = CONV_ROWS

    def conv_chunk(r0):
        s, rs = divmod(r0, sub)
        nwin = cr + 2 * pad
        for lt in range(SSD_XBC // LANES):
            cols = slice(lt * LANES, (lt + 1) * LANES)
            win = xpad[s, rs:rs + nwin, cols]
            taps = [cw_ref[k:k + 1, cols] * win for k in range(SSD_CONV)]
            before = taps[1] + pltpu.roll(taps[0], 1, 0)
            after = taps[3] + pltpu.roll(taps[4], nwin - 1, 0)
            conv = taps[2] + pltpu.roll(before, 1, 0) + pltpu.roll(after, nwin - 1, 0)
            act = _silu(conv[pad:pad + cr, :] + cb_ref[:, cols])
            if lt < SSD_D_INNER // LANES:
                xs_ref[r0:r0 + cr, cols] = act
            else:
                bc_ref[r0:r0 + cr, lt * LANES - SSD_D_INNER:(lt + 1) * LANES - SSD_D_INNER] = act.astype(BF16)

    chunks = list(range(0, tm, cr))
    col = 0
    for o_ref in (q_ref, k_ref, v_ref, z_ref):
        if chunks:
            conv_chunk(chunks.pop(0))
        n = o_ref.shape[1]
        o_ref[...] = jnp.dot(hb, w_ref[:, col:col + n], preferred_element_type=F32).astype(o_ref.dtype)
        col += n
    dt_ref[...] = jnp.dot(hb, wdt_ref[...], preferred_element_type=F32)
    for r0 in chunks:
        conv_chunk(r0)


def _inproj0(x, norm_w, mod, w, w_dt, conv_w, conv_b, seq, qkv_dtype):
    t, d = x.shape
    tm = ROW_TILE
    assert seq % tm == 0 or tm % seq == 0
    halo = seq > tm
    sub = min(seq, tm)
    widths = (NA_WIDTH, NA_WIDTH, NA_WIDTH, SSD_D_INNER, SSD_D_INNER, SSD_XBC - SSD_D_INNER, LANES)
    dtypes = (qkv_dtype,) * 3 + (F32, F32, BF16, F32)
    assert sum(widths) - LANES == w.shape[1] and w_dt.shape == (d, LANES)
    per = max(seq // tm, 1) if mod.shape[0] > 1 else t // tm
    in_specs = [pl.BlockSpec((tm, d), lambda i: (i, 0))]
    args = [x]
    if halo:
        rb = tm // SUBLANES
        last = t // SUBLANES - 1
        in_specs += [pl.BlockSpec((SUBLANES, d), lambda i: (jnp.maximum(i * rb - 1, 0), 0)),
                     pl.BlockSpec((SUBLANES, d), lambda i: (jnp.minimum((i + 1) * rb, last), 0))]
        args += [x, x]
    in_specs += [_const_spec((1, d)),
                 pl.BlockSpec((1, 6, d), lambda i: (i // per, 0, 0)),
                 _const_spec(w.shape), _const_spec(w_dt.shape),
                 _const_spec((SUBLANES, SSD_XBC)), _const_spec((1, SSD_XBC))]
    args += [norm_w.reshape(1, d), mod, w, w_dt,
             jnp.pad(conv_w.astype(F32), ((0, SUBLANES - SSD_CONV), (0, 0))), conv_b.astype(F32).reshape(1, -1)]
    return pl.pallas_call(
        functools.partial(_inproj0_kernel, seq=seq, halo=halo),
        grid=(t // tm,),
        in_specs=in_specs,
        out_specs=[pl.BlockSpec((tm, n), lambda i: (i, 0)) for n in widths],
        out_shape=[jax.ShapeDtypeStruct((t, n), dt) for n, dt in zip(widths, dtypes)],
        scratch_shapes=[pltpu.VMEM((tm // sub, sub + 2 * SUBLANES, SSD_XBC), F32)],
        compiler_params=_params(dimension_semantics=("arbitrary",)),
        name="l0_in_proj",
    )(*args)


def _swap_lane_pairs(x):
    even = (lax.broadcasted_iota(jnp.int32, x.shape, 1) & 1) == 0
    return jnp.where(even, pltpu.roll(x, LANES - 1, 1), pltpu.roll(x, 1, 1))


def _lane_lo(shape):
    return lax.broadcasted_iota(jnp.int32, shape, 1) < NA_HEAD_DIM


def _head_lanes(x, half):
    lo = _lane_lo(x.shape)
    return jnp.where(lo if half == 0 else jnp.logical_not(lo), x, jnp.zeros_like(x))


def _softmax_pv(q2, keys, vals, biases):
    acc = None
    for half in (0, 1):
        qm = _head_lanes(q2, half)
        scores = []
        for kk, bb in zip(keys, biases):
            s = _mm_nt(qm, kk)
            if bb is not None:
                s = bb(half, s)
            scores.append(s)
        mx = functools.reduce(jnp.maximum, [jnp.max(s, axis=-1, keepdims=True) for s in scores])
        es = [jnp.exp(s - mx) for s in scores]
        den = functools.reduce(jnp.add, [jnp.sum(e, axis=-1, keepdims=True) for e in es])
        pv = functools.reduce(jnp.add, [_mm(e, _head_lanes(vv, half)) for e, vv in zip(es, vals)])
        out = pv * (1.0 / den)
        acc = out if acc is None else acc + out
    return acc


def _na_ctx_kernel(q_ref, k_ref, v_ref, o_ref, kh_ref, vh_ref, *, seq):
    scale = NA_HEAD_DIM ** -0.5
    nseq = q_ref.shape[0] // seq
    for s in range(nseq):
        r = slice(s * seq, (s + 1) * seq)
        for hp in range(NA_WIDTH // LANES):
            c = slice(hp * LANES, (hp + 1) * LANES)
            out = _softmax_pv(q_ref[r, c] * scale, [k_ref[r, c]], [v_ref[r, c]], [None])
            o_ref[r, c] = out.astype(o_ref.dtype)
    w = k_ref.shape[1]
    for src, dst in ((k_ref, kh_ref), (v_ref, vh_ref)):
        for s in range(nseq):
            for c in range(w // LANES):
                for r in range(seq // LANES):
                    dst[s * w + c * LANES:s * w + (c + 1) * LANES, r * LANES:(r + 1) * LANES] = (
                        src[s * seq + r * LANES:s * seq + (r + 1) * LANES, c * LANES:(c + 1) * LANES].T)


def _na_ctx(q, k, v, seq):
    t, w = q.shape
    assert seq % LANES == 0
    spec = pl.BlockSpec((SEQ_BLOCK, w), lambda i: (i, 0))
    hspec = pl.BlockSpec((SEQ_BLOCK // seq * w, seq), lambda i: (i, 0))
    hshape = jax.ShapeDtypeStruct((t // seq * w, seq), k.dtype)
    return pl.pallas_call(
        functools.partial(_na_ctx_kernel, seq=seq),
        grid=(t // SEQ_BLOCK,),
        in_specs=[spec, spec, spec],
        out_specs=[spec, hspec, hspec],
        out_shape=[jax.ShapeDtypeStruct((t, w), BF16), hshape, hshape],
        compiler_params=_params(dimension_semantics=("arbitrary",)),
        name="na_context",
    )(q, k, v)


NA_QBLK = 256


def _na_lat_kernel(q_ref, k_ref, v_ref, kc_ref, vc_ref, tab_ref, o_ref):
    scale = NA_HEAD_DIM ** -0.5
    seq = q_ref.shape[0]
    nblk = seq // NA_QBLK
    rows_per_blk = NA_QBLK // GRID_W
    grid_rows = seq // GRID_W
    n_off = 2 * NA_WIN_ROWS

    qc = lax.broadcasted_iota(jnp.int32, (GRID_W, LANES), 0)
    lane = lax.broadcasted_iota(jnp.int32, (GRID_W, LANES), 1)
    kc = lane & (GRID_W - 1)
    c0 = jnp.clip(qc - NA_WIN_COLS // 2, 0, GRID_W - NA_WIN_COLS)
    col_ok = (kc >= c0) & (kc < c0 + NA_WIN_COLS)
    first = lane < GRID_W
    pair_bias = [[jnp.where(col_ok,
                            pltpu.roll(jnp.broadcast_to(tab_ref[half, e:e + 1, :], (GRID_W, LANES)), 0, 1,
                                       stride=1, stride_axis=0),
                            _NEG_INF)
                  for e in range(n_off)] for half in (0, 1)]
    neg = jnp.full((GRID_W, LANES), _NEG_INF, F32)
    k_ctx, v_ctx = [jnp.concatenate([ref[:, j * LANES:(j + 1) * LANES].T for j in range(ref.shape[1] // LANES)],
                                    axis=0) for ref in (kc_ref, vc_ref)]

    def band_start(r):
        return min(max(r - NA_WIN_ROWS // 2, 0), grid_rows - NA_WIN_ROWS)

    for i in range(nblk):
        ws_row = band_start(i * rows_per_blk) // 2 * 2
        we_row = min((band_start((i + 1) * rows_per_blk - 1) + NA_WIN_ROWS + 1) // 2 * 2, grid_rows)
        ws, kwin = ws_row * GRID_W, (we_row - ws_row) * GRID_W

        def add_bias(half, s, i=i, ws_row=ws_row, kwin=kwin):
            out_rows = []
            for rq in range(rows_per_blk):
                r = i * rows_per_blk + rq
                r0 = band_start(r)
                tiles = []
                for kp in range(kwin // LANES):
                    kr = ws_row + 2 * kp
                    ok0 = r0 <= kr < r0 + NA_WIN_ROWS
                    ok1 = r0 <= kr + 1 < r0 + NA_WIN_ROWS
                    if ok0 or ok1:
                        tile = pair_bias[half][kr - r + NA_WIN_ROWS]
                        if not ok1:
                            tile = jnp.where(first, tile, _NEG_INF)
                        elif not ok0:
                            tile = jnp.where(first, _NEG_INF, tile)
                    else:
                        tile = neg
                    tiles.append(s[rq * GRID_W:(rq + 1) * GRID_W, kp * LANES:(kp + 1) * LANES] + tile)
                out_rows.append(jnp.concatenate(tiles, axis=1))
            return jnp.concatenate(out_rows, axis=0)

        rows = slice(i * NA_QBLK, (i + 1) * NA_QBLK)
        kw = k_ref[ws:ws + kwin, :]
        vw = v_ref[ws:ws + kwin, :]
        out = _softmax_pv(q_ref[rows, :] * scale, [kw, k_ctx], [vw, v_ctx], [add_bias, None])
        o_ref[rows, :] = out.astype(o_ref.dtype)


def _na_bias_table(rel_bias):
    h, n_dr, n_dc = rel_bias.shape
    half_dc = n_dc // 2
    tz = jnp.pad(rel_bias.astype(F32), ((0, 0), (1, 1), (0, 0)))
    lo, hi = tz[:, 0:n_dr + 1], tz[:, 1:n_dr + 2]
    gap = jnp.zeros((h, n_dr + 1, GRID_W - half_dc - 1 - half_dc), F32)
    return jnp.concatenate([lo[..., half_dc:], gap, hi, gap, lo[..., :half_dc]], axis=-1)


def _na_lat(q, k, v, k_ctx_t, v_ctx_t, bias_table, seq):
    t, w = q.shape
    b = t // seq
    npair = w // LANES
    past = k_ctx_t.shape[1]
    assert k_ctx_t.shape[0] == b * w and past % LANES == 0
    kspec = pl.BlockSpec((seq, LANES), lambda hp, bb: (bb, hp))
    cspec = pl.BlockSpec((LANES, past), lambda hp, bb: (bb * npair + hp, 0))
    tspec = pl.BlockSpec((2,) + bias_table.shape[1:], lambda hp, bb: (hp, 0, 0))
    return pl.pallas_call(
        _na_lat_kernel,
        grid=(npair, b),
        in_specs=[kspec, kspec, kspec, cspec, cspec, tspec],
        out_specs=kspec,
        out_shape=jax.ShapeDtypeStruct((t, w), BF16),
        compiler_params=_params(dimension_semantics=("arbitrary", "arbitrary")),
        name="na_latent",
    )(q, k, v, k_ctx_t, v_ctx_t, bias_table)


def _pack3(v, lane):
    vm = jnp.where(lane < PIECE_LANES, v, 0.0)
    hi = vm.astype(BF16).astype(F32)
    r1 = vm - hi
    mid = r1.astype(BF16).astype(F32)
    lo = r1 - mid
    return (hi + pltpu.roll(mid, PIECE_LANES, 1) + pltpu.roll(lo, 2 * PIECE_LANES, 1)).astype(BF16)


def _unpack3(res):
    return res + pltpu.roll(res, LANES - PIECE_LANES, 1) + pltpu.roll(res, LANES - 2 * PIECE_LANES, 1)


def _ssd_selectors():
    r = np.arange(LANES)
    out = []
    for width in (SSD_HEAD_DIM, SSD_CHUNK):
        l = np.arange(SSD_HEADS * width)
        for d in (0, 1):
            sel = (r[:, None] < 3 * PIECE_LANES) & ((r[:, None] % PIECE_LANES) == d * SSD_HEADS + l[None, :] // width)
            out.append(jnp.asarray(sel, BF16))
    return out


def _ssd_kernel(*refs, seq, has_state):
    if has_state:
        (z_ref, xs_ref, bc_ref, dtr_ref, dtb_ref, alog_ref, dsk_ref, nw_ref,
         s64f_ref, s64b_ref, s128f_ref, s128b_ref, s0_ref,
         y_ref, yacc, ep, wp, bts, st) = refs
        sfin_ref = None
    else:
        (z_ref, xs_ref, bc_ref, dtr_ref, dtb_ref, alog_ref, dsk_ref, nw_ref,
         s64f_ref, s64b_ref, s128f_ref, s128b_ref,
         y_ref, sfin_ref, yacc, ep, wp, bts, st) = refs
        s0_ref = None
    ch = SSD_CHUNK
    nseq = z_ref.shape[0] // seq
    nch = seq // ch
    assert nch % 2 == 0
    lane = lax.broadcasted_iota(jnp.int32, (ch, LANES), 1)
    ri = lax.broadcasted_iota(jnp.int32, (ch, ch), 0)
    ci = lax.broadcasted_iota(jnp.int32, (ch, ch), 1)
    keeps = (ci <= ri, ci >= ri)
    tril = jnp.where(keeps[0], 1.0, 0.0).astype(BF16)
    triu = jnp.where(keeps[1], 1.0, 0.0).astype(BF16)
    lane_lo = lane < SSD_HEAD_DIM
    fwd_lane = lane < SSD_HEADS
    neg_a = -jnp.exp(alog_ref[...]) * np.log2(np.e).astype(np.float32)
    gw = SSD_D_INNER // SSD_GROUPS
    sel64s = (s64f_ref, s64b_ref)
    sel128s = (s128f_ref, s128b_ref)
    edges = (ch - 1, 0)

    def intra_body(c, _):
        rows = pl.ds(pl.multiple_of(c * ch, ch), ch)
        x_c = xs_ref[rows, :]
        dt = _softplus(dtr_ref[rows, :] + dtb_ref[...])
        la = _pack3(dt * neg_a, lane)
        cs = jnp.where(fwd_lane,
                       _unpack3(jnp.dot(tril, la, preferred_element_type=F32)),
                       _unpack3(jnp.dot(triu, la, preferred_element_type=F32)))
        tot = jnp.where(fwd_lane, cs[ch - 1:ch, :], cs[0:1, :])
        ep[rows, :] = _pack3(jnp.exp2(cs), lane)
        wp[rows, :] = _pack3(dt * jnp.exp2(tot - cs), lane)
        csp = _pack3(cs, lane)
        col = [jnp.dot(csp, sel128s[d][...], preferred_element_type=F32) for d in (0, 1)]
        cs_t = cs.T
        dt_t = dt.T
        y_intra = []
        for g in range(SSD_GROUPS):
            b_g = bc_ref[rows, g * SSD_STATE:(g + 1) * SSD_STATE]
            cofs = SSD_GROUPS * SSD_STATE
            c_g = bc_ref[rows, cofs + g * SSD_STATE:cofs + (g + 1) * SSD_STATE]
            bts[c, g] = b_g.T
            gram = _mm_nt(c_g, b_g)
            for pp in range(gw // LANES):
                p = g * (gw // LANES) + pp
                ws = []
                for d in (0, 1):
                    for half in (0, 1):
                        h = 2 * p + half
                        hd = d * SSD_HEADS + h
                        diff = col[d][:, h * ch:(h + 1) * ch] - cs_t[hd:hd + 1, :]
                        wm = jnp.exp2(jnp.where(keeps[d], diff, _NEG_INF)) * gram * dt_t[hd:hd + 1, :]
                        ws.append(wm.astype(BF16))
                xp = x_c[:, p * LANES:(p + 1) * LANES]
                xcat = jnp.concatenate([jnp.where(lane_lo, xp, 0.0), jnp.where(lane_lo, 0.0, xp)],
                                       axis=0).astype(BF16)
                y_intra.append(jnp.dot(jnp.concatenate(ws, axis=1), jnp.concatenate([xcat, xcat], axis=0),
                                       preferred_element_type=F32))
        yacc[rows, :] = dsk_ref[...] * x_c + jnp.concatenate(y_intra, axis=1)
        return 0

    lax.fori_loop(0, nseq * nch, intra_body, 0, unroll=INTRA_UNROLL)

    def seq_body(s, _):
        base = pl.multiple_of(s * seq, seq)
        for d in (0, 1):
            if has_state:
                for h in range(0, SSD_HEADS, 2):
                    pair = jnp.concatenate([s0_ref[0, d, h], s0_ref[0, d, h + 1]], axis=0)
                    st[d, :, h * SSD_HEAD_DIM:(h + 2) * SSD_HEAD_DIM] = pair.T
            else:
                st[d] = jnp.zeros((SSD_STATE, SSD_D_INNER), F32)

        def chunk(d, c):
            rows = pl.ds(pl.multiple_of(base + c * ch, ch), ch)
            e64 = jnp.dot(ep[rows, :], sel64s[d][...], preferred_element_type=F32)
            w64 = jnp.dot(wp[rows, :], sel64s[d][...], preferred_element_type=F32)
            state = st[d]
            state_b = state.astype(BF16)
            xw = (xs_ref[rows, :] * w64).astype(BF16)
            cofs = SSD_GROUPS * SSD_STATE
            y_state = [jnp.dot(bc_ref[rows, cofs + g * SSD_STATE:cofs + (g + 1) * SSD_STATE],
                               state_b[:, g * gw:(g + 1) * gw], preferred_element_type=F32)
                       for g in range(SSD_GROUPS)]
            upd = [jnp.dot(bts[s * nch + c, g], xw[:, g * gw:(g + 1) * gw], preferred_element_type=F32)
                   for g in range(SSD_GROUPS)]
            yacc[rows, :] = yacc[rows, :] + jnp.concatenate(y_state, axis=1) * e64
            st[d] = e64[edges[d]:edges[d] + 1, :] * state + jnp.concatenate(upd, axis=1)

        def pair_body(j, _):
            for step in (0, 1):
                chunk(0, 2 * j + step)
                chunk(1, nch - 1 - 2 * j - step)
            return 0

        lax.fori_loop(0, nch // 2, pair_body, 0)
        if sfin_ref is not None:
            for d in (0, 1):
                for h in range(0, SSD_HEADS, 2):
                    pair = st[d, :, h * SSD_HEAD_DIM:(h + 2) * SSD_HEAD_DIM].T
                    sfin_ref[s, d, h] = pair[0:SSD_HEAD_DIM]
                    sfin_ref[s, d, h + 1] = pair[SSD_HEAD_DIM:2 * SSD_HEAD_DIM]
        return 0

    lax.fori_loop(0, nseq, seq_body, 0)

    def out_body(c, _):
        rows = pl.ds(pl.multiple_of(c * ch, ch), ch)
        y = yacc[rows, :] * _silu(z_ref[rows, :])
        outs = []
        for g in range(SSD_GROUPS):
            yg = y[:, g * gw:(g + 1) * gw]
            outs.append(yg * lax.rsqrt(jnp.mean(yg * yg, axis=-1, keepdims=True) + EPS))
        y_ref[rows, :] = (jnp.concatenate(outs, axis=1) * nw_ref[...]).astype(y_ref.dtype)
        return 0

    lax.fori_loop(0, nseq * nch, out_body, 0)


def _ssd(z, xs, bc, dtr, a_log, dt_bias, d_skip, norm_w, seq, s0=None):
    t = z.shape[0]
    nseq = SEQ_BLOCK // seq
    nblk = t // SEQ_BLOCK
    nch = seq // SSD_CHUNK
    pad32 = lambda a: jnp.pad(a.astype(F32).reshape(1, -1), ((0, 0), (0, LANES - 2 * SSD_HEADS)))
    consts = [pad32(dt_bias), pad32(a_log),
              jnp.repeat(d_skip.astype(F32), SSD_HEAD_DIM).reshape(1, -1),
              norm_w.astype(F32).reshape(1, -1)] + _ssd_selectors()
    row = lambda n: pl.BlockSpec((SEQ_BLOCK, n), lambda i: (i, 0))
    in_specs = ([row(SSD_D_INNER), row(SSD_D_INNER), row(bc.shape[1]), row(LANES)]
                + [_const_spec(c.shape) for c in consts])
    args = [z, xs, bc, dtr] + consts
    out_specs = [row(SSD_D_INNER)]
    out_shape = [jax.ShapeDtypeStruct((t, SSD_D_INNER), BF16)]
    state_shape = (2, SSD_STATE, SSD_D_INNER)
    io_state = (2, SSD_HEADS, SSD_HEAD_DIM, SSD_STATE)
    assert 2 * SSD_HEAD_DIM == SSD_STATE
    if s0 is not None:
        assert s0.shape[1:] == io_state
        in_specs.append(pl.BlockSpec((1,) + io_state, lambda i: (i, 0, 0, 0, 0)))
        args.append(s0)
    else:
        out_specs.append(pl.BlockSpec((nseq,) + io_state, lambda i: (i, 0, 0, 0, 0)))
        out_shape.append(jax.ShapeDtypeStruct((t // seq,) + io_state, F32))
    nblk_ch = SEQ_BLOCK // SSD_CHUNK
    scratch = [pltpu.VMEM((SEQ_BLOCK, SSD_D_INNER), F32),
               pltpu.VMEM((SEQ_BLOCK, LANES), BF16), pltpu.VMEM((SEQ_BLOCK, LANES), BF16),
               pltpu.VMEM((nblk_ch, SSD_GROUPS, SSD_STATE, SSD_CHUNK), BF16),
               pltpu.VMEM(state_shape, F32)]
    return pl.pallas_call(
        functools.partial(_ssd_kernel, seq=seq, has_state=s0 is not None),
        grid=(nblk,),
        in_specs=in_specs,
        out_specs=out_specs,
        out_shape=out_shape,
        scratch_shapes=scratch,
        compiler_params=_params(dimension_semantics=("arbitrary",)),
        name="ssd_mixer",
    )(*args)


def _log_sigmoid(x):
    return -_softplus(-x)


def _ret_kernel(*refs, seq, has_state, rope):
    refs = list(refs)
    x_ref, n1_ref, mod_ref, wq_ref, wk_ref, wv_ref, wg_ref = refs[:7]
    del refs[:7]
    cos_ref, sin_ref = (refs.pop(0), refs.pop(0)) if rope else (None, None)
    dec_ref, nw_ref = refs.pop(0), refs.pop(0)
    s0_ref = refs.pop(0) if has_state else None
    y_ref = refs.pop(0)
    sfin_ref = None if has_state else refs.pop(0)
    hb_s, q_ref, k_ref, v_ref, g_ref, yacc, st = refs

    @pl.when(pl.program_id(1) == 0)
    def _():
        hb_s[...] = _norm_mod(x_ref[...], n1_ref[...], mod_ref, 0, 1).astype(BF16)

    hb = hb_s[...]
    for o_ref, w_ref, scale in ((q_ref, wq_ref, 1.0), (k_ref, wk_ref, RET_QK_DIM ** -0.5)):
        y = jnp.dot(hb, w_ref[...], preferred_element_type=F32) * scale
        if rope:
            for j in range(RET_QK_DIM // LANES):
                lanes = slice(j * LANES, (j + 1) * LANES)
                yj = y[:, lanes]
                rot = yj * cos_ref[:, lanes] + _swap_lane_pairs(yj) * sin_ref[:, lanes]
                o_ref[:, lanes] = rot.astype(o_ref.dtype)
        else:
            o_ref[...] = y.astype(o_ref.dtype)
    v_ref[...] = jnp.dot(hb, wv_ref[...], preferred_element_type=F32).astype(v_ref.dtype)
    g_ref[...] = jnp.dot(hb, wg_ref[...], preferred_element_type=F32).astype(g_ref.dtype)

    ch = RET_CHUNK
    nseq = q_ref.shape[0] // seq
    nch = seq // ch
    gf = _log_sigmoid(dec_ref[0, 0:1, :])
    gb = _log_sigmoid(dec_ref[0, 1:2, :])
    ri = lax.broadcasted_iota(jnp.int32, (ch, ch), 0)
    ci = lax.broadcasted_iota(jnp.int32, (ch, ch), 1)
    dist = (ri - ci).astype(F32)
    gfk, gbk = gf[:, :ch], gb[:, :ch]
    decay = (jnp.where(ci <= ri, jnp.exp(jnp.where(ci <= ri, dist, 0.0) * gfk), 0.0)
             + jnp.where(ci >= ri, jnp.exp(jnp.where(ci >= ri, -dist, 0.0) * gbk), 0.0))
    pos = lax.broadcasted_iota(jnp.int32, (ch, RET_QK_DIM), 0).astype(F32)
    gfq, gbq = gf[:, :RET_QK_DIM], gb[:, :RET_QK_DIM]
    e_f = jnp.exp((pos + 1.0) * gfq)
    e_b = jnp.exp((ch - pos) * gbq)
    tail_f = jnp.exp((ch - 1.0 - pos) * gfq)
    tail_b = jnp.exp(pos * gbq)
    dec_f = jnp.exp(ch * gf)
    dec_b = jnp.exp(ch * gb)

    for s in range(nseq):
        for d in (0, 1):
            if has_state:
                st[d] = s0_ref[0, d, 0]
            else:
                st[d] = jnp.zeros((RET_QK_DIM, RET_V_DIM), F32)
        for c in range(nch):
            rows = slice(s * seq + c * ch, s * seq + (c + 1) * ch)
            q, k, v = q_ref[rows, :], k_ref[rows, :].astype(F32), v_ref[rows, :]
            y = _mm(_mm_nt(q, k) * decay, v)
            if has_state or c > 0:
                y = y + _mm(q.astype(F32) * e_f, st[0])
            yacc[rows, :] = y
            st[0] = dec_f * st[0] + _mm((k * tail_f).T, v)
        for c in reversed(range(nch)):
            rows = slice(s * seq + c * ch, s * seq + (c + 1) * ch)
            q, k, v = q_ref[rows, :], k_ref[rows, :].astype(F32), v_ref[rows, :]
            if has_state or c < nch - 1:
                yacc[rows, :] = yacc[rows, :] + _mm(q.astype(F32) * e_b, st[1])
            st[1] = dec_b * st[1] + _mm((k * tail_b).T, v)
        if sfin_ref is not None:
            for d in (0, 1):
                sfin_ref[s, d, 0] = st[d]
        for c in range(nch):
            rows = slice(s * seq + c * ch, s * seq + (c + 1) * ch)
            y = yacc[rows, :]
            y = y * lax.rsqrt(jnp.mean(y * y, axis=-1, keepdims=True) + EPS)
            y_ref[rows, :] = (y * nw_ref[...] * _silu(g_ref[rows, :].astype(F32))).astype(y_ref.dtype)


def _retention(x, row0, t, norm1_w, mod, w_in, ret_decay, norm_w, seq, s0=None, rope_tables=None):
    d = x.shape[1]
    nseq = SEQ_BLOCK // seq
    nblk = t // SEQ_BLOCK
    blk0 = row0 // SEQ_BLOCK
    per = 1 if mod.shape[0] > 1 else nblk
    dec = jnp.broadcast_to(jnp.pad(ret_decay.astype(F32).T, ((0, 0), (0, SUBLANES - 2)))[:, :, None],
                           (RET_HEADS, SUBLANES, RET_V_DIM))
    vspec = pl.BlockSpec((SEQ_BLOCK, RET_V_DIM), lambda i, h: (i, h))
    qk_blocks, v_blocks = RET_QK_W // RET_QK_DIM, RET_V_W // RET_V_DIM
    wq_spec = pl.BlockSpec((d, RET_QK_DIM), lambda i, h: (0, h))
    wk_spec = pl.BlockSpec((d, RET_QK_DIM), lambda i, h: (0, qk_blocks + h))
    wv_spec = pl.BlockSpec((d, RET_V_DIM), lambda i, h: (0, 2 * RET_QK_W // RET_V_DIM + h))
    wg_spec = pl.BlockSpec((d, RET_V_DIM), lambda i, h: (0, 2 * RET_QK_W // RET_V_DIM + v_blocks + h))
    in_specs = [pl.BlockSpec((SEQ_BLOCK, d), lambda i, h: (blk0 + i, 0)), _const_spec((1, d)),
                pl.BlockSpec((1, 6, d), lambda i, h: (i // per, 0, 0)),
                wq_spec, wk_spec, wv_spec, wg_spec]
    args = [x, norm1_w.reshape(1, d), mod, w_in, w_in, w_in, w_in]
    if rope_tables is not None:
        in_specs += [_const_spec(rope_tables[0].shape)] * 2
        args += list(rope_tables)
    in_specs += [pl.BlockSpec((1, SUBLANES, RET_V_DIM), lambda i, h: (h, 0, 0)),
                 pl.BlockSpec((1, RET_V_DIM), lambda i, h: (0, h))]
    args += [dec, norm_w.astype(F32).reshape(1, -1)]
    out_specs = [vspec]
    out_shape = [jax.ShapeDtypeStruct((t, RET_V_W), BF16)]
    if s0 is not None:
        in_specs.append(pl.BlockSpec((1, 2, 1, RET_QK_DIM, RET_V_DIM), lambda i, h: (i, 0, h, 0, 0)))
        args.append(s0)
    else:
        out_specs.append(pl.BlockSpec((nseq, 2, 1, RET_QK_DIM, RET_V_DIM), lambda i, h: (i, 0, h, 0, 0)))
        out_shape.append(jax.ShapeDtypeStruct((t // seq, 2, RET_HEADS, RET_QK_DIM, RET_V_DIM), F32))
    return pl.pallas_call(
        functools.partial(_ret_kernel, seq=seq, has_state=s0 is not None, rope=rope_tables is not None),
        grid=(nblk, RET_HEADS),
        in_specs=in_specs,
        out_specs=out_specs,
        out_shape=out_shape,
        scratch_shapes=[pltpu.VMEM((SEQ_BLOCK, d), BF16),
                        pltpu.VMEM((SEQ_BLOCK, RET_QK_DIM), BF16), pltpu.VMEM((SEQ_BLOCK, RET_QK_DIM), BF16),
                        pltpu.VMEM((SEQ_BLOCK, RET_V_DIM), BF16), pltpu.VMEM((SEQ_BLOCK, RET_V_DIM), BF16),
                        pltpu.VMEM((SEQ_BLOCK, RET_V_DIM), F32),
                        pltpu.VMEM((2, RET_QK_DIM, RET_V_DIM), F32)],
        compiler_params=_params(dimension_semantics=("arbitrary", "arbitrary")),
        name="retention_mixer",
    )(*args)


def _out_ffn_kernel(*refs, n_x, n_mix, n_out, n_ctx_steps, final_norm):
    refs = list(refs)
    x_refs = [refs.pop(0) for _ in range(n_x)]
    mix_refs = [(refs.pop(0), refs.pop(0)) for _ in range(n_mix)]
    wo_ref, mod_ref, n2_ref, w1_ref, w3_ref, w2_ref = (refs.pop(0) for _ in range(6))
    fn_ref = refs.pop(0) if final_norm else None
    o_refs = [refs.pop(0) for _ in range(n_out)]
    (act,) = refs
    is_ctx = pl.program_id(0) < n_ctx_steps

    def pick(pair):
        return pair[0][...] if len(pair) == 1 else jnp.where(is_ctx, pair[0][...], pair[1][...])

    mixed, row = None, 0
    for pair in mix_refs:
        k = pair[0].shape[1]
        term = jnp.dot(pick(pair), wo_ref[row:row + k, :], preferred_element_type=F32)
        mixed = term if mixed is None else mixed + term
        row += k
    x1 = pick(x_refs) + mod_ref[0, 2:3, :] * mixed
    hb = _norm_mod(x1, n2_ref[...], mod_ref, 3, 4).astype(BF16)
    for c in range(w1_ref.shape[1] // FFN_CHUNK):
        cols = slice(c * FFN_CHUNK, (c + 1) * FFN_CHUNK)
        h1 = jnp.dot(hb, w1_ref[:, cols], preferred_element_type=F32)
        h3 = jnp.dot(hb, w3_ref[:, cols], preferred_element_type=F32)
        act[:, cols] = (_silu(h1) * h3).astype(BF16)
    x2 = x1 + mod_ref[0, 5:6, :] * jnp.dot(act[...], w2_ref[...], preferred_element_type=F32)
    if final_norm:
        ms = jnp.mean(x2 * x2, axis=-1, keepdims=True)
        x2 = x2 * lax.rsqrt(ms + EPS) * fn_ref[...]
    if n_out == 1:
        o_refs[0][...] = x2
    else:
        @pl.when(is_ctx)
        def _():
            o_refs[0][...] = x2

        @pl.when(jnp.logical_not(is_ctx))
        def _():
            o_refs[1][...] = x2


def _out_ffn(x, mixes, wo, mod, norm2_w, w1, w3, w2, t_ctx, t_lat, lat_seq, split_out, final_norm_w=None):
    tm = ROW_TILE
    d = wo.shape[1]
    n0, n1 = t_ctx // tm, t_lat // tm
    per = lat_seq // tm
    ctx_rows = lambda n: pl.BlockSpec((tm, n), lambda i: (jnp.minimum(i, n0 - 1), 0))
    lat_rows = lambda n: pl.BlockSpec((tm, n), lambda i: (jnp.maximum(i - n0, 0), 0))
    all_rows = lambda n: pl.BlockSpec((tm, n), lambda i: (i, 0))
    single = dict(pipeline_mode=pl.Buffered(1))
    x = tuple(x) if isinstance(x, (tuple, list)) else (x,)
    in_specs = [all_rows(d)] if len(x) == 1 else [ctx_rows(d), lat_rows(d)]
    args = list(x)
    for m_ctx, m_lat in mixes:
        in_specs += [ctx_rows(m_ctx.shape[1]), lat_rows(m_lat.shape[1])]
        args += [m_ctx, m_lat]
    in_specs += [_const_spec(wo.shape, **single),
                 pl.BlockSpec((1, 6, d), lambda i: (jnp.where(i < n0, 0, 1 + (i - n0) // per), 0, 0)),
                 _const_spec((1, d))]
    in_specs += [_const_spec(w.shape, **single) for w in (w1, w3, w2)]
    args += [wo, mod, norm2_w.reshape(1, d), w1, w3, w2]
    if final_norm_w is not None:
        in_specs.append(_const_spec((1, d)))
        args.append(final_norm_w.reshape(1, d))
    if split_out:
        out_specs = [ctx_rows(d), lat_rows(d)]
        out_shape = [jax.ShapeDtypeStruct((t_ctx, d), F32), jax.ShapeDtypeStruct((t_lat, d), F32)]
    else:
        out_specs = [all_rows(d)]
        out_shape = [jax.ShapeDtypeStruct((t_ctx + t_lat, d), F32)]
    out = pl.pallas_call(
        functools.partial(_out_ffn_kernel, n_x=len(x), n_mix=len(mixes), n_out=len(out_specs), n_ctx_steps=n0,
                          final_norm=final_norm_w is not None),
        grid=(n0 + n1,),
        in_specs=in_specs,
        out_specs=out_specs,
        out_shape=out_shape,
        scratch_shapes=[pltpu.VMEM((tm, w1.shape[1]), BF16)],
        compiler_params=_params(dimension_semantics=("arbitrary",)),
        name="out_proj_ffn",
    )(*args)
    return out if split_out else out[0]


def _rope_tables(seq):
    half = RET_QK_DIM // 2
    t = jnp.arange(seq)
    row = (t // GRID_W).astype(F32)
    col = (t % GRID_W).astype(F32)
    freqs = ROPE_BASE ** (-jnp.arange(0, half, 2, dtype=F32) / half)
    ang = jnp.concatenate([row[:, None] * freqs, col[:, None] * freqs], axis=-1)
    cos = jnp.repeat(jnp.cos(ang), 2, axis=1)
    sin = jnp.stack([-jnp.sin(ang), jnp.sin(ang)], axis=-1).reshape(seq, RET_QK_DIM)
    return cos, sin


def kernel(x_prompt, x_sample, cache_l0_na_k, cache_l0_na_v, state_l0_ssd, state_l1_ret, c, c_ctx,
           l0_norm1_w, l0_norm2_w, l0_mod_w, l0_mod_b, l0_w_in, l0_w_out, l0_na_bias, l0_conv_w, l0_conv_b,
           l0_ssd_a_log, l0_ssd_dt_bias, l0_ssd_d, l0_ssd_norm_w, l0_ffn_w1, l0_ffn_w3, l0_ffn_w2,
           l1_norm1_w, l1_norm2_w, l1_mod_w, l1_mod_b, l1_w_in, l1_w_out, l1_ret_decay, l1_ret_norm_w,
           l1_ffn_w1, l1_ffn_w3, l1_ffn_w2, final_norm_w):
    bc, lc, d = x_prompt.shape
    bl, ll, _ = x_sample.shape
    assert d == D_MODEL and ll == SEQ_BLOCK and SEQ_BLOCK % lc == 0 and bc % (SEQ_BLOCK // lc) == 0
    tc, tl = bc * lc, bl * ll
    xc = x_prompt.reshape(tc, d)
    xl = x_sample.reshape(tl, d)

    nrow = SUBLANES * ((1 + bl + SUBLANES - 1) // SUBLANES)
    cond = jnp.concatenate([c_ctx[None], c, jnp.zeros((nrow - 1 - bl, d), F32)], axis=0)
    mods = [_adaln(cond, mod_w, mod_b).reshape(nrow, 6, d)
            for mod_w, mod_b in ((l0_mod_w, l0_mod_b), (l1_mod_w, l1_mod_b))]
    mod_ctx = [m[0:1] for m in mods]
    mod_lat = [m[1:1 + bl] for m in mods]

    (w_in0, w_dt0, wo0, ffn0_w1, ffn0_w3, ffn0_w2, w_in1, wo1, ffn1_w1, ffn1_w3, ffn1_w2) = _to_bf16(
        l0_w_in.T, l0_w_out, l0_ffn_w1, l0_ffn_w3, l0_ffn_w2, l1_w_in, l1_w_out, l1_ffn_w1, l1_ffn_w3, l1_ffn_w2)
    ffn0 = (ffn0_w1, ffn0_w3, ffn0_w2)
    ffn1 = (ffn1_w1, ffn1_w3, ffn1_w2)
    ssd_p = (l0_ssd_a_log, l0_ssd_dt_bias, l0_ssd_d, l0_ssd_norm_w)

    qc, kc, vc, zc, xsc, bcc, dtc = _inproj0(xc, l0_norm1_w, mod_ctx[0], w_in0, w_dt0, l0_conv_w, l0_conv_b, lc, F32)
    att_c, kc_heads, vc_heads = _na_ctx(qc, kc, vc, lc)
    ssd_c, sfin_c = _ssd(zc, xsc, bcc, dtc, *ssd_p, seq=lc)

    ql, kl, vl, zl, xsl, bcl, dtl = _inproj0(xl, l0_norm1_w, mod_lat[0], w_in0, w_dt0, l0_conv_w, l0_conv_b, ll, BF16)
    past = cache_l0_na_k.shape[1]
    cache_t = [jnp.transpose(a, (0, 2, 3, 1)).reshape(bl * NA_WIDTH, past) for a in (cache_l0_na_k, cache_l0_na_v)]
    att_l = _na_lat(ql, kl, vl, *cache_t, _na_bias_table(l0_na_bias), ll)
    (ssd_l,) = _ssd(zl, xsl, bcl, dtl, *ssd_p, seq=ll, s0=jnp.swapaxes(state_l0_ssd, 3, 4))

    x1 = _out_ffn((xc, xl), [(att_c, att_l), (ssd_c, ssd_l)], wo0, mods[0], l0_norm2_w, *ffn0,
                  tc, tl, ll, split_out=False)

    ret_c, ret_state = _retention(x1, 0, tc, l1_norm1_w, mod_ctx[1], w_in1, l1_ret_decay, l1_ret_norm_w, lc)
    (ret_l,) = _retention(x1, tc, tl, l1_norm1_w, mod_lat[1], w_in1, l1_ret_decay, l1_ret_norm_w, ll,
                          s0=state_l1_ret, rope_tables=_rope_tables(ll))
    y_prompt, y_sample = _out_ffn(x1, [(ret_c, ret_l)], wo1, mods[1], l1_norm2_w, *ffn1,
                                  tc, tl, ll, split_out=True, final_norm_w=final_norm_w)

    k_out, v_out = [a.reshape(bc, NA_HEADS, NA_HEAD_DIM, lc).transpose(0, 3, 1, 2) for a in (kc_heads, vc_heads)]
    return (y_prompt.reshape(bc, lc, d), y_sample.reshape(bl, ll, d),
            k_out, v_out, jnp.swapaxes(sfin_c, 3, 4), ret_state)
```

```python
import functools

import numpy as np
import jax
import jax.numpy as jnp
from jax import lax
from jax.experimental import pallas as pl
from jax.experimental.pallas import tpu as pltpu

F32 = jnp.float32
BF16 = jnp.bfloat16

D_MODEL = 1024
GRID_W = 64
NA_HEADS = 8
NA_HEAD_DIM = 64
NA_WIDTH = NA_HEADS * NA_HEAD_DIM
NA_WIN_ROWS = 8
NA_WIN_COLS = 16
SSD_HEADS = 16
SSD_HEAD_DIM = 64
SSD_D_INNER = SSD_HEADS * SSD_HEAD_DIM
SSD_GROUPS = 2
SSD_STATE = 128
SSD_CONV = 5
SSD_XBC = SSD_D_INNER + 2 * SSD_GROUPS * SSD_STATE
RET_HEADS = 4
RET_QK_DIM = 256
RET_V_DIM = 512
RET_QK_W = RET_HEADS * RET_QK_DIM
RET_V_W = RET_HEADS * RET_V_DIM
ROPE_BASE = 10000.0
EPS = 1e-6

LANES = 128
SUBLANES = 8
SEQ_BLOCK = 1024
ROW_TILE = 512
SSD_CHUNK = 128
CONV_ROWS = 256
INTRA_UNROLL = 4
PIECE_LANES = 2 * SSD_HEADS
assert 3 * PIECE_LANES <= LANES
CAST_STEPS = 8
ADALN_TILE = 1536
RET_CHUNK = 256
FFN_CHUNK = 256
VMEM_LIMIT = 56 * 1024 * 1024

_NEG_INF = float("-inf")


def _params(**kw):
    return pltpu.CompilerParams(vmem_limit_bytes=VMEM_LIMIT, **kw)


def _silu(x):
    return x * (1.0 / (1.0 + jnp.exp(-x)))


def _softplus(x):
    return jnp.maximum(x, 0.0) + jnp.log(1.0 + jnp.exp(-jnp.abs(x)))


def _mm(a, b):
    return jnp.dot(a.astype(BF16), b.astype(BF16), preferred_element_type=F32)


def _mm_nt(a, b):
    return lax.dot_general(a.astype(BF16), b.astype(BF16), (((1,), (1,)), ((), ())),
                           preferred_element_type=F32)


def _const_spec(shape, **kw):
    nd = len(shape)
    return pl.BlockSpec(shape, lambda *_: (0,) * nd, **kw)


def _cast_kernel(*refs):
    n = len(refs) // 2
    (wt_ref, tail_ref, *plain_in), (w_ref, tail_out_ref, *plain_out) = refs[:n], refs[n:]
    for j in range(wt_ref.shape[0] // LANES):
        cols = slice(j * LANES, (j + 1) * LANES)
        w_ref[:, cols] = wt_ref[cols, :].T.astype(BF16)
    n_tail = tail_ref.shape[0]
    tail = jnp.concatenate([tail_ref[...], jnp.zeros((LANES - n_tail, LANES), F32)], axis=0)
    tail_out_ref[...] = tail.T.astype(BF16)
    for i_ref, o_ref in zip(plain_in, plain_out):
        o_ref[...] = i_ref[...].astype(o_ref.dtype)


def _to_bf16(wt, *ws):
    steps = CAST_STEPS
    n_all, k = wt.shape
    n_main = n_all // LANES * LANES
    n_tail = n_all - n_main
    assert k // steps == LANES and n_tail % SUBLANES == 0 and n_main % n_tail == 0
    specs = [pl.BlockSpec((w.shape[0] // steps, w.shape[1]), lambda i: (i, 0)) for w in ws]
    assert all(w.shape[0] % (steps * 2 * SUBLANES) == 0 for w in ws)
    return pl.pallas_call(
        _cast_kernel,
        grid=(steps,),
        in_specs=[pl.BlockSpec((n_main, LANES), lambda i: (0, i)),
                  pl.BlockSpec((n_tail, LANES), lambda i: (n_main // n_tail, i))] + specs,
        out_specs=[pl.BlockSpec((LANES, n_main), lambda i: (i, 0)),
                   pl.BlockSpec((LANES, LANES), lambda i: (i, 0))] + specs,
        out_shape=[jax.ShapeDtypeStruct((k, n_main), BF16), jax.ShapeDtypeStruct((k, LANES), BF16)]
        + [jax.ShapeDtypeStruct(w.shape, BF16) for w in ws],
        compiler_params=_params(dimension_semantics=("arbitrary",)),
        name="weights_to_bf16",
    )(wt, wt, *ws)


def _adaln_kernel(c_ref, w0_ref, b0_ref, w1_ref, b1_ref, o0_ref, o1_ref):
    j = pl.program_id(0)
    steps = pl.num_programs(0) // 2
    hc = _silu(c_ref[...])

    @pl.when(j < steps)
    def _():
        o0_ref[...] = _mm(hc, w0_ref[...]) + b0_ref[...]

    @pl.when(j >= steps)
    def _():
        o1_ref[...] = _mm(hc, w1_ref[...]) + b1_ref[...]


def _adaln(cond, mod_w0, mod_b0, mod_w1, mod_b1):
    r, d = cond.shape
    n = mod_w0.shape[1]
    tn = ADALN_TILE
    assert n % tn == 0 and mod_w1.shape == mod_w0.shape
    nb = n // tn
    first = lambda j: (0, jnp.minimum(j, nb - 1))
    second = lambda j: (0, jnp.maximum(j - nb, 0))
    return pl.pallas_call(
        _adaln_kernel,
        grid=(2 * nb,),
        in_specs=[_const_spec((r, d)),
                  pl.BlockSpec((d, tn), first), pl.BlockSpec((1, tn), first),
                  pl.BlockSpec((d, tn), second), pl.BlockSpec((1, tn), second)],
        out_specs=[pl.BlockSpec((r, tn), first), pl.BlockSpec((r, tn), second)],
        out_shape=[jax.ShapeDtypeStruct((r, n), F32)] * 2,
        compiler_params=_params(dimension_semantics=("arbitrary",)),
        name="adaln_mod",
    )(cond, mod_w0, mod_b0.reshape(1, n), mod_w1, mod_b1.reshape(1, n))


def _norm_mod(x, nw, mod_ref, shift_idx, scale_idx):
    ms = jnp.mean(x * x, axis=-1, keepdims=True)
    h = x * lax.rsqrt(ms + EPS) * nw
    return h * (1.0 + mod_ref[0, scale_idx:scale_idx + 1, :]) + mod_ref[0, shift_idx:shift_idx + 1, :]


def _inproj0_kernel(*refs, seq, halo):
    if halo:
        (x_ref, xp_ref, xn_ref, nw_ref, mod_ref, w_ref, wdt_ref, cw_ref, cb_ref,
         q_ref, k_ref, v_ref, z_ref, xs_ref, bc_ref, dt_ref, xpad) = refs
    else:
        (x_ref, nw_ref, mod_ref, w_ref, wdt_ref, cw_ref, cb_ref,
         q_ref, k_ref, v_ref, z_ref, xs_ref, bc_ref, dt_ref, xpad) = refs
    tm = x_ref.shape[0]
    pad = SUBLANES
    sub = xpad.shape[1] - 2 * pad
    h = _norm_mod(x_ref[...], nw_ref[...], mod_ref, 0, 1)
    hb = h.astype(BF16)
    xbc_col = 3 * NA_WIDTH + SSD_D_INNER
    w_xbc = w_ref[:, xbc_col:xbc_col + SSD_XBC]
    if halo:
        tiles_per_seq = seq // tm
        p = pl.program_id(0) % tiles_per_seq
        h_prev = jnp.where(p > 0, _norm_mod(xp_ref[...], nw_ref[...], mod_ref, 0, 1), 0.0)
        h_next = jnp.where(p < tiles_per_seq - 1, _norm_mod(xn_ref[...], nw_ref[...], mod_ref, 0, 1), 0.0)
        ext = jnp.concatenate([h_prev, h, h_next], axis=0).astype(BF16)
        xpad[0] = jnp.dot(ext, w_xbc, preferred_element_type=F32)
    else:
        xbc = jnp.dot(hb, w_xbc, preferred_element_type=F32)
        for s in range(tm // sub):
            xpad[s, 0:pad, :] = jnp.zeros((pad, SSD_XBC), F32)
            xpad[s, pad:pad + sub, :] = xbc[s * sub:(s + 1) * sub]
            xpad[s, pad + sub:2 * pad + sub, :] = jnp.zeros((pad, SSD_XBC), F32)

    cr = CONV_ROWS

    def conv_chunk(r0):
        s, rs = divmod(r0, sub)
        nwin = cr + 2 * pad
        for lt in range(SSD_XBC // LANES):
            cols = slice(lt * LANES, (lt + 1) * LANES)
            win = xpad[s, rs:rs + nwin, cols]
            taps = [cw_ref[k:k + 1, cols] * win for k in range(SSD_CONV)]
            before = taps[1] + pltpu.roll(taps[0], 1, 0)
            after = taps[3] + pltpu.roll(taps[4], nwin - 1, 0)
            conv = taps[2] + pltpu.roll(before, 1, 0) + pltpu.roll(after, nwin - 1, 0)
            act = _silu(conv[pad:pad + cr, :] + cb_ref[:, cols])
            if lt < SSD_D_INNER // LANES:
                xs_ref[r0:r0 + cr, cols] = act
            else:
                bc_ref[r0:r0 + cr, lt * LANES - SSD_D_INNER:(lt + 1) * LANES - SSD_D_INNER] = act.astype(BF16)

    chunks = list(range(0, tm, cr))
    col = 0
    for o_ref in (q_ref, k_ref, v_ref, z_ref):
        if chunks:
            conv_chunk(chunks.pop(0))
        n = o_ref.shape[1]
        o_ref[...] = jnp.dot(hb, w_ref[:, col:col + n], preferred_element_type=F32).astype(o_ref.dtype)
        col += n
    dt_ref[...] = jnp.dot(hb, wdt_ref[...], preferred_element_type=F32)
    for r0 in chunks:
        conv_chunk(r0)


def _inproj0(x, norm_w, mod, w, w_dt, conv_w, conv_b, seq, qkv_dtype):
    t, d = x.shape
    tm = ROW_TILE
    assert seq % tm == 0 or tm % seq == 0
    halo = seq > tm
    sub = min(seq, tm)
    widths = (NA_WIDTH, NA_WIDTH, NA_WIDTH, SSD_D_INNER, SSD_D_INNER, SSD_XBC - SSD_D_INNER, LANES)
    dtypes = (qkv_dtype,) * 3 + (F32, F32, BF16, F32)
    assert sum(widths) - LANES == w.shape[1] and w_dt.shape == (d, LANES)
    per = max(seq // tm, 1) if mod.shape[0] > 1 else t // tm
    in_specs = [pl.BlockSpec((tm, d), lambda i: (i, 0))]
    args = [x]
    if halo:
        rb = tm // SUBLANES
        last = t // SUBLANES - 1
        in_specs += [pl.BlockSpec((SUBLANES, d), lambda i: (jnp.maximum(i * rb - 1, 0), 0)),
                     pl.BlockSpec((SUBLANES, d), lambda i: (jnp.minimum((i + 1) * rb, last), 0))]
        args += [x, x]
    in_specs += [_const_spec((1, d)),
                 pl.BlockSpec((1, 6, d), lambda i: (i // per, 0, 0)),
                 _const_spec(w.shape), _const_spec(w_dt.shape),
                 _const_spec((SUBLANES, SSD_XBC)), _const_spec((1, SSD_XBC))]
    args += [norm_w.reshape(1, d), mod, w, w_dt,
             jnp.pad(conv_w.astype(F32), ((0, SUBLANES - SSD_CONV), (0, 0))), conv_b.astype(F32).reshape(1, -1)]
    return pl.pallas_call(
        functools.partial(_inproj0_kernel, seq=seq, halo=halo),
        grid=(t // tm,),
        in_specs=in_specs,
        out_specs=[pl.BlockSpec((tm, n), lambda i: (i, 0)) for n in widths],
        out_shape=[jax.ShapeDtypeStruct((t, n), dt) for n, dt in zip(widths, dtypes)],
        scratch_shapes=[pltpu.VMEM((tm // sub, sub + 2 * SUBLANES, SSD_XBC), F32)],
        compiler_params=_params(dimension_semantics=("arbitrary",)),
        name="l0_in_proj",
    )(*args)


def _swap_lane_pairs(x):
    even = (lax.broadcasted_iota(jnp.int32, x.shape, 1) & 1) == 0
    return jnp.where(even, pltpu.roll(x, LANES - 1, 1), pltpu.roll(x, 1, 1))


def _lane_lo(shape):
    return lax.broadcasted_iota(jnp.int32, shape, 1) < NA_HEAD_DIM


def _head_lanes(x, half):
    lo = _lane_lo(x.shape)
    return jnp.where(lo if half == 0 else jnp.logical_not(lo), x, jnp.zeros_like(x))


def _softmax_pv(q2, keys, vals, biases):
    acc = None
    for half in (0, 1):
        qm = _head_lanes(q2, half)
        scores = []
        for kk, bb in zip(keys, biases):
            s = _mm_nt(qm, kk)
            if bb is not None:
                s = bb(half, s)
            scores.append(s)
        mx = functools.reduce(jnp.maximum, [jnp.max(s, axis=-1, keepdims=True) for s in scores])
        es = [jnp.exp(s - mx) for s in scores]
        den = functools.reduce(jnp.add, [jnp.sum(e, axis=-1, keepdims=True) for e in es])
        pv = functools.reduce(jnp.add, [_mm(e, _head_lanes(vv, half)) for e, vv in zip(es, vals)])
        out = pv * (1.0 / den)
        acc = out if acc is None else acc + out
    return acc


def _na_ctx_kernel(q_ref, k_ref, v_ref, o_ref, kh_ref, vh_ref, *, seq):
    scale = NA_HEAD_DIM ** -0.5
    nseq = q_ref.shape[0] // seq
    for s in range(nseq):
        r = slice(s * seq, (s + 1) * seq)
        for hp in range(NA_WIDTH // LANES):
            c = slice(hp * LANES, (hp + 1) * LANES)
            out = _softmax_pv(q_ref[r, c] * scale, [k_ref[r, c]], [v_ref[r, c]], [None])
            o_ref[r, c] = out.astype(o_ref.dtype)
    w = k_ref.shape[1]
    for src, dst in ((k_ref, kh_ref), (v_ref, vh_ref)):
        for s in range(nseq):
            for c in range(w // LANES):
                for r in range(seq // LANES):
                    dst[s * w + c * LANES:s * w + (c + 1) * LANES, r * LANES:(r + 1) * LANES] = (
                        src[s * seq + r * LANES:s * seq + (r + 1) * LANES, c * LANES:(c + 1) * LANES].T)


def _na_ctx(q, k, v, seq):
    t, w = q.shape
    assert seq % LANES == 0
    spec = pl.BlockSpec((SEQ_BLOCK, w), lambda i: (i, 0))
    hspec = pl.BlockSpec((SEQ_BLOCK // seq * w, seq), lambda i: (i, 0))
    hshape = jax.ShapeDtypeStruct((t // seq * w, seq), k.dtype)
    return pl.pallas_call(
        functools.partial(_na_ctx_kernel, seq=seq),
        grid=(t // SEQ_BLOCK,),
        in_specs=[spec, spec, spec],
        out_specs=[spec, hspec, hspec],
        out_shape=[jax.ShapeDtypeStruct((t, w), BF16), hshape, hshape],
        compiler_params=_params(dimension_semantics=("arbitrary",)),
        name="na_context",
    )(q, k, v)


NA_QBLK = 256


def _na_lat_kernel(q_ref, k_ref, v_ref, kc_ref, vc_ref, tab_ref, o_ref):
    scale = NA_HEAD_DIM ** -0.5
    seq = q_ref.shape[0]
    nblk = seq // NA_QBLK
    rows_per_blk = NA_QBLK // GRID_W
    grid_rows = seq // GRID_W
    n_off = 2 * NA_WIN_ROWS

    qc = lax.broadcasted_iota(jnp.int32, (GRID_W, LANES), 0)
    lane = lax.broadcasted_iota(jnp.int32, (GRID_W, LANES), 1)
    kc = lane & (GRID_W - 1)
    c0 = jnp.clip(qc - NA_WIN_COLS // 2, 0, GRID_W - NA_WIN_COLS)
    col_ok = (kc >= c0) & (kc < c0 + NA_WIN_COLS)
    first = lane < GRID_W
    pair_bias = [[jnp.where(col_ok,
                            pltpu.roll(jnp.broadcast_to(tab_ref[half, e:e + 1, :], (GRID_W, LANES)), 0, 1,
                                       stride=1, stride_axis=0),
                            _NEG_INF)
                  for e in range(n_off)] for half in (0, 1)]
    neg = jnp.full((GRID_W, LANES), _NEG_INF, F32)
    k_ctx, v_ctx = [jnp.concatenate([ref[:, j * LANES:(j + 1) * LANES].T for j in range(ref.shape[1] // LANES)],
                                    axis=0) for ref in (kc_ref, vc_ref)]

    def band_start(r):
        return min(max(r - NA_WIN_ROWS // 2, 0), grid_rows - NA_WIN_ROWS)

    for i in range(nblk):
        ws_row = band_start(i * rows_per_blk) // 2 * 2
        we_row = min((band_start((i + 1) * rows_per_blk - 1) + NA_WIN_ROWS + 1) // 2 * 2, grid_rows)
        ws, kwin = ws_row * GRID_W, (we_row - ws_row) * GRID_W

        def add_bias(half, s, i=i, ws_row=ws_row, kwin=kwin):
            out_rows = []
            for rq in range(rows_per_blk):
                r = i * rows_per_blk + rq
                r0 = band_start(r)
                tiles = []
                for kp in range(kwin // LANES):
                    kr = ws_row + 2 * kp
                    ok0 = r0 <= kr < r0 + NA_WIN_ROWS
                    ok1 = r0 <= kr + 1 < r0 + NA_WIN_ROWS
                    if ok0 or ok1:
                        tile = pair_bias[half][kr - r + NA_WIN_ROWS]
                        if not ok1:
                            tile = jnp.where(first, tile, _NEG_INF)
                        elif not ok0:
                            tile = jnp.where(first, _NEG_INF, tile)
                    else:
                        tile = neg
                    tiles.append(s[rq * GRID_W:(rq + 1) * GRID_W, kp * LANES:(kp + 1) * LANES] + tile)
                out_rows.append(jnp.concatenate(tiles, axis=1))
            return jnp.concatenate(out_rows, axis=0)

        rows = slice(i * NA_QBLK, (i + 1) * NA_QBLK)
        kw = k_ref[ws:ws + kwin, :]
        vw = v_ref[ws:ws + kwin, :]
        out = _softmax_pv(q_ref[rows, :] * scale, [kw, k_ctx], [vw, v_ctx], [add_bias, None])
        o_ref[rows, :] = out.astype(o_ref.dtype)


def _na_bias_table(rel_bias):
    h, n_dr, n_dc = rel_bias.shape
    half_dc = n_dc // 2
    tz = jnp.pad(rel_bias.astype(F32), ((0, 0), (1, 1), (0, 0)))
    lo, hi = tz[:, 0:n_dr + 1], tz[:, 1:n_dr + 2]
    gap = jnp.zeros((h, n_dr + 1, GRID_W - half_dc - 1 - half_dc), F32)
    return jnp.concatenate([lo[..., half_dc:], gap, hi, gap, lo[..., :half_dc]], axis=-1)


def _na_lat(q, k, v, k_ctx_t, v_ctx_t, bias_table, seq):
    t, w = q.shape
    b = t // seq
    npair = w // LANES
    past = k_ctx_t.shape[1]
    assert k_ctx_t.shape[0] == b * w and past % LANES == 0
    kspec = pl.BlockSpec((seq, LANES), lambda hp, bb: (bb, hp))
    cspec = pl.BlockSpec((LANES, past), lambda hp, bb: (bb * npair + hp, 0))
    tspec = pl.BlockSpec((2,) + bias_table.shape[1:], lambda hp, bb: (hp, 0, 0))
    return pl.pallas_call(
        _na_lat_kernel,
        grid=(npair, b),
        in_specs=[kspec, kspec, kspec, cspec, cspec, tspec],
        out_specs=kspec,
        out_shape=jax.ShapeDtypeStruct((t, w), BF16),
        compiler_params=_params(dimension_semantics=("arbitrary", "arbitrary")),
        name="na_latent",
    )(q, k, v, k_ctx_t, v_ctx_t, bias_table)


def _pack3(v, lane):
    vm = jnp.where(lane < PIECE_LANES, v, 0.0)
    hi = vm.astype(BF16).astype(F32)
    r1 = vm - hi
    mid = r1.astype(BF16).astype(F32)
    lo = r1 - mid
    return (hi + pltpu.roll(mid, PIECE_LANES, 1) + pltpu.roll(lo, 2 * PIECE_LANES, 1)).astype(BF16)


def _unpack3(res):
    return res + pltpu.roll(res, LANES - PIECE_LANES, 1) + pltpu.roll(res, LANES - 2 * PIECE_LANES, 1)


def _ssd_selectors():
    r = np.arange(LANES)
    out = []
    for width in (SSD_HEAD_DIM, SSD_CHUNK):
        l = np.arange(SSD_HEADS * width)
        for d in (0, 1):
            sel = (r[:, None] < 3 * PIECE_LANES) & ((r[:, None] % PIECE_LANES) == d * SSD_HEADS + l[None, :] // width)
            out.append(jnp.asarray(sel, BF16))
    return out


def _ssd_kernel(*refs, seq, has_state):
    if has_state:
        (z_ref, xs_ref, bc_ref, dtr_ref, dtb_ref, alog_ref, dsk_ref, nw_ref,
         s64f_ref, s64b_ref, s128f_ref, s128b_ref, s0_ref,
         y_ref, yacc, ep, wp, bts, st) = refs
        sfin_ref = None
    else:
        (z_ref, xs_ref, bc_ref, dtr_ref, dtb_ref, alog_ref, dsk_ref, nw_ref,
         s64f_ref, s64b_ref, s128f_ref, s128b_ref,
         y_ref, sfin_ref, yacc, ep, wp, bts, st) = refs
        s0_ref = None
    ch = SSD_CHUNK
    nseq = z_ref.shape[0] // seq
    nch = seq // ch
    assert nch % 2 == 0
    lane = lax.broadcasted_iota(jnp.int32, (ch, LANES), 1)
    ri = lax.broadcasted_iota(jnp.int32, (ch, ch), 0)
    ci = lax.broadcasted_iota(jnp.int32, (ch, ch), 1)
    keeps = (ci <= ri, ci >= ri)
    tril = jnp.where(keeps[0], 1.0, 0.0).astype(BF16)
    triu = jnp.where(keeps[1], 1.0, 0.0).astype(BF16)
    lane_lo = lane < SSD_HEAD_DIM
    fwd_lane = lane < SSD_HEADS
    neg_a = -jnp.exp(alog_ref[...]) * np.log2(np.e).astype(np.float32)
    gw = SSD_D_INNER // SSD_GROUPS
    sel64s = (s64f_ref, s64b_ref)
    sel128s = (s128f_ref, s128b_ref)
    edges = (ch - 1, 0)

    def intra_body(c, _):
        rows = pl.ds(pl.multiple_of(c * ch, ch), ch)
        x_c = xs_ref[rows, :]
        dt = _softplus(dtr_ref[rows, :] + dtb_ref[...])
        la = _pack3(dt * neg_a, lane)
        cs = jnp.where(fwd_lane,
                       _unpack3(jnp.dot(tril, la, preferred_element_type=F32)),
                       _unpack3(jnp.dot(triu, la, preferred_element_type=F32)))
        tot = jnp.where(fwd_lane, cs[ch - 1:ch, :], cs[0:1, :])
        ep[rows, :] = _pack3(jnp.exp2(cs), lane)
        wp[rows, :] = _pack3(dt * jnp.exp2(tot - cs), lane)
        csp = _pack3(cs, lane)
        col = [jnp.dot(csp, sel128s[d][...], preferred_element_type=F32) for d in (0, 1)]
        cs_t = cs.T
        dt_t = dt.T
        y_intra = []
        for g in range(SSD_GROUPS):
            b_g = bc_ref[rows, g * SSD_STATE:(g + 1) * SSD_STATE]
            cofs = SSD_GROUPS * SSD_STATE
            c_g = bc_ref[rows, cofs + g * SSD_STATE:cofs + (g + 1) * SSD_STATE]
            bts[c, g] = b_g.T
            gram = _mm_nt(c_g, b_g)
            for pp in range(gw // LANES):
                p = g * (gw // LANES) + pp
                ws = []
                for d in (0, 1):
                    for half in (0, 1):
                        h = 2 * p + half
                        hd = d * SSD_HEADS + h
                        diff = col[d][:, h * ch:(h + 1) * ch] - cs_t[hd:hd + 1, :]
                        wm = jnp.exp2(jnp.where(keeps[d], diff, _NEG_INF)) * gram * dt_t[hd:hd + 1, :]
                        ws.append(wm.astype(BF16))
                xp = x_c[:, p * LANES:(p + 1) * LANES]
                xcat = jnp.concatenate([jnp.where(lane_lo, xp, 0.0), jnp.where(lane_lo, 0.0, xp)],
                                       axis=0).astype(BF16)
                y_intra.append(jnp.dot(jnp.concatenate(ws, axis=1), jnp.concatenate([xcat, xcat], axis=0),
                                       preferred_element_type=F32))
        yacc[rows, :] = dsk_ref[...] * x_c + jnp.concatenate(y_intra, axis=1)
        return 0

    lax.fori_loop(0, nseq * nch, intra_body, 0, unroll=INTRA_UNROLL)

    def seq_body(s, _):
        base = pl.multiple_of(s * seq, seq)
        for d in (0, 1):
            if has_state:
                for h in range(0, SSD_HEADS, 2):
                    pair = jnp.concatenate([s0_ref[0, d, h], s0_ref[0, d, h + 1]], axis=0)
                    st[d, :, h * SSD_HEAD_DIM:(h + 2) * SSD_HEAD_DIM] = pair.T
            else:
                st[d] = jnp.zeros((SSD_STATE, SSD_D_INNER), F32)

        def chunk(d, c):
            rows = pl.ds(pl.multiple_of(base + c * ch, ch), ch)
            e64 = jnp.dot(ep[rows, :], sel64s[d][...], preferred_element_type=F32)
            w64 = jnp.dot(wp[rows, :], sel64s[d][...], preferred_element_type=F32)
            state = st[d]
            state_b = state.astype(BF16)
            xw = (xs_ref[rows, :] * w64).astype(BF16)
            cofs = SSD_GROUPS * SSD_STATE
            y_state = [jnp.dot(bc_ref[rows, cofs + g * SSD_STATE:cofs + (g + 1) * SSD_STATE],
                               state_b[:, g * gw:(g + 1) * gw], preferred_element_type=F32)
                       for g in range(SSD_GROUPS)]
            upd = [jnp.dot(bts[s * nch + c, g], xw[:, g * gw:(g + 1) * gw], preferred_element_type=F32)
                   for g in range(SSD_GROUPS)]
            yacc[rows, :] = yacc[rows, :] + jnp.concatenate(y_state, axis=1) * e64
            st[d] = e64[edges[d]:edges[d] + 1, :] * state + jnp.concatenate(upd, axis=1)

        def pair_body(j, _):
            for step in (0, 1):
                chunk(0, 2 * j + step)
                chunk(1, nch - 1 - 2 * j - step)
            return 0

        lax.fori_loop(0, nch // 2, pair_body, 0)
        if sfin_ref is not None:
            for d in (0, 1):
                for h in range(0, SSD_HEADS, 2):
                    pair = st[d, :, h * SSD_HEAD_DIM:(h + 2) * SSD_HEAD_DIM].T
                    sfin_ref[s, d, h] = pair[0:SSD_HEAD_DIM]
                    sfin_ref[s, d, h + 1] = pair[SSD_HEAD_DIM:2 * SSD_HEAD_DIM]
        return 0

    lax.fori_loop(0, nseq, seq_body, 0)

    def out_body(c, _):
        rows = pl.ds(pl.multiple_of(c * ch, ch), ch)
        y = yacc[rows, :] * _silu(z_ref[rows, :])
        outs = []
        for g in range(SSD_GROUPS):
            yg = y[:, g * gw:(g + 1) * gw]
            outs.append(yg * lax.rsqrt(jnp.mean(yg * yg, axis=-1, keepdims=True) + EPS))
        y_ref[rows, :] = (jnp.concatenate(outs, axis=1) * nw_ref[...]).astype(y_ref.dtype)
        return 0

    lax.fori_loop(0, nseq * nch, out_body, 0)


def _ssd(z, xs, bc, dtr, a_log, dt_bias, d_skip, norm_w, seq, s0=None):
    t = z.shape[0]
    nseq = SEQ_BLOCK // seq
    nblk = t // SEQ_BLOCK
    nch = seq // SSD_CHUNK
    pad32 = lambda a: jnp.pad(a.astype(F32).reshape(1, -1), ((0, 0), (0, LANES - 2 * SSD_HEADS)))
    consts = [pad32(dt_bias), pad32(a_log),
              jnp.repeat(d_skip.astype(F32), SSD_HEAD_DIM).reshape(1, -1),
              norm_w.astype(F32).reshape(1, -1)] + _ssd_selectors()
    row = lambda n: pl.BlockSpec((SEQ_BLOCK, n), lambda i: (i, 0))
    in_specs = ([row(SSD_D_INNER), row(SSD_D_INNER), row(bc.shape[1]), row(LANES)]
                + [_const_spec(c.shape) for c in consts])
    args = [z, xs, bc, dtr] + consts
    out_specs = [row(SSD_D_INNER)]
    out_shape = [jax.ShapeDtypeStruct((t, SSD_D_INNER), BF16)]
    state_shape = (2, SSD_STATE, SSD_D_INNER)
    io_state = (2, SSD_HEADS, SSD_HEAD_DIM, SSD_STATE)
    assert 2 * SSD_HEAD_DIM == SSD_STATE
    if s0 is not None:
        assert s0.shape[1:] == io_state
        in_specs.append(pl.BlockSpec((1,) + io_state, lambda i: (i, 0, 0, 0, 0)))
        args.append(s0)
    else:
        out_specs.append(pl.BlockSpec((nseq,) + io_state, lambda i: (i, 0, 0, 0, 0)))
        out_shape.append(jax.ShapeDtypeStruct((t // seq,) + io_state, F32))
    nblk_ch = SEQ_BLOCK // SSD_CHUNK
    scratch = [pltpu.VMEM((SEQ_BLOCK, SSD_D_INNER), F32),
               pltpu.VMEM((SEQ_BLOCK, LANES), BF16), pltpu.VMEM((SEQ_BLOCK, LANES), BF16),
               pltpu.VMEM((nblk_ch, SSD_GROUPS, SSD_STATE, SSD_CHUNK), BF16),
               pltpu.VMEM(state_shape, F32)]
    return pl.pallas_call(
        functools.partial(_ssd_kernel, seq=seq, has_state=s0 is not None),
        grid=(nblk,),
        in_specs=in_specs,
        out_specs=out_specs,
        out_shape=out_shape,
        scratch_shapes=scratch,
        compiler_params=_params(dimension_semantics=("arbitrary",)),
        name="ssd_mixer",
    )(*args)


def _log_sigmoid(x):
    return -_softplus(-x)


def _ret_kernel(*refs, seq, has_state, rope):
    refs = list(refs)
    x_ref, n1_ref, mod_ref, wq_ref, wk_ref, wv_ref, wg_ref = refs[:7]
    del refs[:7]
    cos_ref, sin_ref = (refs.pop(0), refs.pop(0)) if rope else (None, None)
    dec_ref, nw_ref = refs.pop(0), refs.pop(0)
    s0_ref = refs.pop(0) if has_state else None
    y_ref = refs.pop(0)
    sfin_ref = None if has_state else refs.pop(0)
    hb_s, q_ref, k_ref, v_ref, g_ref, yacc, st = refs

    @pl.when(pl.program_id(1) == 0)
    def _():
        hb_s[...] = _norm_mod(x_ref[...], n1_ref[...], mod_ref, 0, 1).astype(BF16)

    hb = hb_s[...]
    for o_ref, w_ref, scale in ((q_ref, wq_ref, 1.0), (k_ref, wk_ref, RET_QK_DIM ** -0.5)):
        y = jnp.dot(hb, w_ref[...], preferred_element_type=F32) * scale
        if rope:
            for j in range(RET_QK_DIM // LANES):
                lanes = slice(j * LANES, (j + 1) * LANES)
                yj = y[:, lanes]
                rot = yj * cos_ref[:, lanes] + _swap_lane_pairs(yj) * sin_ref[:, lanes]
                o_ref[:, lanes] = rot.astype(o_ref.dtype)
        else:
            o_ref[...] = y.astype(o_ref.dtype)
    v_ref[...] = jnp.dot(hb, wv_ref[...], preferred_element_type=F32).astype(v_ref.dtype)
    g_ref[...] = jnp.dot(hb, wg_ref[...], preferred_element_type=F32).astype(g_ref.dtype)

    ch = RET_CHUNK
    nseq = q_ref.shape[0] // seq
    nch = seq // ch
    gf = _log_sigmoid(dec_ref[0, 0:1, :])
    gb = _log_sigmoid(dec_ref[0, 1:2, :])
    ri = lax.broadcasted_iota(jnp.int32, (ch, ch), 0)
    ci = lax.broadcasted_iota(jnp.int32, (ch, ch), 1)
    dist = (ri - ci).astype(F32)
    gfk, gbk = gf[:, :ch], gb[:, :ch]
    decay = (jnp.where(ci <= ri, jnp.exp(jnp.where(ci <= ri, dist, 0.0) * gfk), 0.0)
             + jnp.where(ci >= ri, jnp.exp(jnp.where(ci >= ri, -dist, 0.0) * gbk), 0.0))
    pos = lax.broadcasted_iota(jnp.int32, (ch, RET_QK_DIM), 0).astype(F32)
    gfq, gbq = gf[:, :RET_QK_DIM], gb[:, :RET_QK_DIM]
    e_f = jnp.exp((pos + 1.0) * gfq)
    e_b = jnp.exp((ch - pos) * gbq)
    tail_f = jnp.exp((ch - 1.0 - pos) * gfq)
    tail_b = jnp.exp(pos * gbq)
    dec_f = jnp.exp(ch * gf)
    dec_b = jnp.exp(ch * gb)

    for s in range(nseq):
        for d in (0, 1):
            if has_state:
                st[d] = s0_ref[0, d, 0]
            else:
                st[d] = jnp.zeros((RET_QK_DIM, RET_V_DIM), F32)
        for c in range(nch):
            rows = slice(s * seq + c * ch, s * seq + (c + 1) * ch)
            q, k, v = q_ref[rows, :], k_ref[rows, :].astype(F32), v_ref[rows, :]
            y = _mm(_mm_nt(q, k) * decay, v)
            if has_state or c > 0:
                y = y + _mm(q.astype(F32) * e_f, st[0])
            yacc[rows, :] = y
            st[0] = dec_f * st[0] + _mm((k * tail_f).T, v)
        for c in reversed(range(nch)):
            rows = slice(s * seq + c * ch, s * seq + (c + 1) * ch)
            q, k, v = q_ref[rows, :], k_ref[rows, :].astype(F32), v_ref[rows, :]
            if has_state or c < nch - 1:
                yacc[rows, :] = yacc[rows, :] + _mm(q.astype(F32) * e_b, st[1])
            st[1] = dec_b * st[1] + _mm((k * tail_b).T, v)
        if sfin_ref is not None:
            for d in (0, 1):
                sfin_ref[s, d, 0] = st[d]
        for c in range(nch):
            rows = slice(s * seq + c * ch, s * seq + (c + 1) * ch)
            y = yacc[rows, :]
            y = y * lax.rsqrt(jnp.mean(y * y, axis=-1, keepdims=True) + EPS)
            y_ref[rows, :] = (y * nw_ref[...] * _silu(g_ref[rows, :].astype(F32))).astype(y_ref.dtype)


def _retention(x, row0, t, norm1_w, mod, w_in, ret_decay, norm_w, seq, s0=None, rope_tables=None):
    d = x.shape[1]
    nseq = SEQ_BLOCK // seq
    nblk = t // SEQ_BLOCK
    blk0 = row0 // SEQ_BLOCK
    per = 1 if mod.shape[0] > 1 else nblk
    dec = jnp.broadcast_to(jnp.pad(ret_decay.astype(F32).T, ((0, 0), (0, SUBLANES - 2)))[:, :, None],
                           (RET_HEADS, SUBLANES, RET_V_DIM))
    vspec = pl.BlockSpec((SEQ_BLOCK, RET_V_DIM), lambda i, h: (i, h))
    qk_blocks, v_blocks = RET_QK_W // RET_QK_DIM, RET_V_W // RET_V_DIM
    wq_spec = pl.BlockSpec((d, RET_QK_DIM), lambda i, h: (0, h))
    wk_spec = pl.BlockSpec((d, RET_QK_DIM), lambda i, h: (0, qk_blocks + h))
    wv_spec = pl.BlockSpec((d, RET_V_DIM), lambda i, h: (0, 2 * RET_QK_W // RET_V_DIM + h))
    wg_spec = pl.BlockSpec((d, RET_V_DIM), lambda i, h: (0, 2 * RET_QK_W // RET_V_DIM + v_blocks + h))
    in_specs = [pl.BlockSpec((SEQ_BLOCK, d), lambda i, h: (blk0 + i, 0)), _const_spec((1, d)),
                pl.BlockSpec((1, 6, d), lambda i, h: (i // per, 0, 0)),
                wq_spec, wk_spec, wv_spec, wg_spec]
    args = [x, norm1_w.reshape(1, d), mod, w_in, w_in, w_in, w_in]
    if rope_tables is not None:
        in_specs += [_const_spec(rope_tables[0].shape)] * 2
        args += list(rope_tables)
    in_specs += [pl.BlockSpec((1, SUBLANES, RET_V_DIM), lambda i, h: (h, 0, 0)),
                 pl.BlockSpec((1, RET_V_DIM), lambda i, h: (0, h))]
    args += [dec, norm_w.astype(F32).reshape(1, -1)]
    out_specs = [vspec]
    out_shape = [jax.ShapeDtypeStruct((t, RET_V_W), BF16)]
    if s0 is not None:
        in_specs.append(pl.BlockSpec((1, 2, 1, RET_QK_DIM, RET_V_DIM), lambda i, h: (i, 0, h, 0, 0)))
        args.append(s0)
    else:
        out_specs.append(pl.BlockSpec((nseq, 2, 1, RET_QK_DIM, RET_V_DIM), lambda i, h: (i, 0, h, 0, 0)))
        out_shape.append(jax.ShapeDtypeStruct((t // seq, 2, RET_HEADS, RET_QK_DIM, RET_V_DIM), F32))
    return pl.pallas_call(
        functools.partial(_ret_kernel, seq=seq, has_state=s0 is not None, rope=rope_tables is not None),
        grid=(nblk, RET_HEADS),
        in_specs=in_specs,
        out_specs=out_specs,
        out_shape=out_shape,
        scratch_shapes=[pltpu.VMEM((SEQ_BLOCK, d), BF16),
                        pltpu.VMEM((SEQ_BLOCK, RET_QK_DIM), BF16), pltpu.VMEM((SEQ_BLOCK, RET_QK_DIM), BF16),
                        pltpu.VMEM((SEQ_BLOCK, RET_V_DIM), BF16), pltpu.VMEM((SEQ_BLOCK, RET_V_DIM), BF16),
                        pltpu.VMEM((SEQ_BLOCK, RET_V_DIM), F32),
                        pltpu.VMEM((2, RET_QK_DIM, RET_V_DIM), F32)],
        compiler_params=_params(dimension_semantics=("arbitrary", "arbitrary")),
        name="retention_mixer",
    )(*args)


def _out_ffn_kernel(*refs, n_x, n_mix, n_out, n_ctx_steps, final_norm):
    refs = list(refs)
    x_refs = [refs.pop(0) for _ in range(n_x)]
    mix_refs = [(refs.pop(0), refs.pop(0)) for _ in range(n_mix)]
    wo_ref, mod_ref, n2_ref, w1_ref, w3_ref, w2_ref = (refs.pop(0) for _ in range(6))
    fn_ref = refs.pop(0) if final_norm else None
    o_refs = [refs.pop(0) for _ in range(n_out)]
    (act,) = refs
    is_ctx = pl.program_id(0) < n_ctx_steps

    def pick(pair):
        return pair[0][...] if len(pair) == 1 else jnp.where(is_ctx, pair[0][...], pair[1][...])

    mixed, row = None, 0
    for pair in mix_refs:
        k = pair[0].shape[1]
        term = jnp.dot(pick(pair), wo_ref[row:row + k, :], preferred_element_type=F32)
        mixed = term if mixed is None else mixed + term
        row += k
    x1 = pick(x_refs) + mod_ref[0, 2:3, :] * mixed
    hb = _norm_mod(x1, n2_ref[...], mod_ref, 3, 4).astype(BF16)
    for c in range(w1_ref.shape[1] // FFN_CHUNK):
        cols = slice(c * FFN_CHUNK, (c + 1) * FFN_CHUNK)
        h1 = jnp.dot(hb, w1_ref[:, cols], preferred_element_type=F32)
        h3 = jnp.dot(hb, w3_ref[:, cols], preferred_element_type=F32)
        act[:, cols] = (_silu(h1) * h3).astype(BF16)
    x2 = x1 + mod_ref[0, 5:6, :] * jnp.dot(act[...], w2_ref[...], preferred_element_type=F32)
    if final_norm:
        ms = jnp.mean(x2 * x2, axis=-1, keepdims=True)
        x2 = x2 * lax.rsqrt(ms + EPS) * fn_ref[...]
    if n_out == 1:
        o_refs[0][...] = x2
    else:
        @pl.when(is_ctx)
        def _():
            o_refs[0][...] = x2

        @pl.when(jnp.logical_not(is_ctx))
        def _():
            o_refs[1][...] = x2


def _out_ffn(x, mixes, wo, mod, norm2_w, w1, w3, w2, t_ctx, t_lat, lat_seq, split_out, final_norm_w=None):
    tm = ROW_TILE
    d = wo.shape[1]
    n0, n1 = t_ctx // tm, t_lat // tm
    per = lat_seq // tm
    ctx_rows = lambda n: pl.BlockSpec((tm, n), lambda i: (jnp.minimum(i, n0 - 1), 0))
    lat_rows = lambda n: pl.BlockSpec((tm, n), lambda i: (jnp.maximum(i - n0, 0), 0))
    all_rows = lambda n: pl.BlockSpec((tm, n), lambda i: (i, 0))
    single = dict(pipeline_mode=pl.Buffered(1))
    x = tuple(x) if isinstance(x, (tuple, list)) else (x,)
    in_specs = [all_rows(d)] if len(x) == 1 else [ctx_rows(d), lat_rows(d)]
    args = list(x)
    for m_ctx, m_lat in mixes:
        in_specs += [ctx_rows(m_ctx.shape[1]), lat_rows(m_lat.shape[1])]
        args += [m_ctx, m_lat]
    in_specs += [_const_spec(wo.shape, **single),
                 pl.BlockSpec((1, 6, d), lambda i: (jnp.where(i < n0, 0, 1 + (i - n0) // per), 0, 0)),
                 _const_spec((1, d))]
    in_specs += [_const_spec(w.shape, **single) for w in (w1, w3, w2)]
    args += [wo, mod, norm2_w.reshape(1, d), w1, w3, w2]
    if final_norm_w is not None:
        in_specs.append(_const_spec((1, d)))
        args.append(final_norm_w.reshape(1, d))
    if split_out:
        out_specs = [ctx_rows(d), lat_rows(d)]
        out_shape = [jax.ShapeDtypeStruct((t_ctx, d), F32), jax.ShapeDtypeStruct((t_lat, d), F32)]
    else:
        out_specs = [all_rows(d)]
        out_shape = [jax.ShapeDtypeStruct((t_ctx + t_lat, d), F32)]
    out = pl.pallas_call(
        functools.partial(_out_ffn_kernel, n_x=len(x), n_mix=len(mixes), n_out=len(out_specs), n_ctx_steps=n0,
                          final_norm=final_norm_w is not None),
        grid=(n0 + n1,),
        in_specs=in_specs,
        out_specs=out_specs,
        out_shape=out_shape,
        scratch_shapes=[pltpu.VMEM((tm, w1.shape[1]), BF16)],
        compiler_params=_params(dimension_semantics=("arbitrary",)),
        name="out_proj_ffn",
    )(*args)
    return out if split_out else out[0]


def _rope_tables(seq):
    half = RET_QK_DIM // 2
    t = jnp.arange(seq)
    row = (t // GRID_W).astype(F32)
    col = (t % GRID_W).astype(F32)
    freqs = ROPE_BASE ** (-jnp.arange(0, half, 2, dtype=F32) / half)
    ang = jnp.concatenate([row[:, None] * freqs, col[:, None] * freqs], axis=-1)
    cos = jnp.repeat(jnp.cos(ang), 2, axis=1)
    sin = jnp.stack([-jnp.sin(ang), jnp.sin(ang)], axis=-1).reshape(seq, RET_QK_DIM)
    return cos, sin


def kernel(x_prompt, x_sample, cache_l0_na_k, cache_l0_na_v, state_l0_ssd, state_l1_ret, c, c_ctx,
           l0_norm1_w, l0_norm2_w, l0_mod_w, l0_mod_b, l0_w_in, l0_w_out, l0_na_bias, l0_conv_w, l0_conv_b,
           l0_ssd_a_log, l0_ssd_dt_bias, l0_ssd_d, l0_ssd_norm_w, l0_ffn_w1, l0_ffn_w3, l0_ffn_w2,
           l1_norm1_w, l1_norm2_w, l1_mod_w, l1_mod_b, l1_w_in, l1_w_out, l1_ret_decay, l1_ret_norm_w,
           l1_ffn_w1, l1_ffn_w3, l1_ffn_w2, final_norm_w):
    bc, lc, d = x_prompt.shape
    bl, ll, _ = x_sample.shape
    assert d == D_MODEL and ll == SEQ_BLOCK and SEQ_BLOCK % lc == 0 and bc % (SEQ_BLOCK // lc) == 0
    tc, tl = bc * lc, bl * ll
    xc = x_prompt.reshape(tc, d)
    xl = x_sample.reshape(tl, d)

    nrow = SUBLANES * ((1 + bl + SUBLANES - 1) // SUBLANES)
    cond = jnp.concatenate([c_ctx[None], c, jnp.zeros((nrow - 1 - bl, d), F32)], axis=0)
    mods = [m.reshape(nrow, 6, d) for m in _adaln(cond, l0_mod_w, l0_mod_b, l1_mod_w, l1_mod_b)]
    mod_ctx = [m[0:1] for m in mods]
    mod_lat = [m[1:1 + bl] for m in mods]

    (w_in0, w_dt0, wo0, ffn0_w1, ffn0_w3, ffn0_w2, w_in1, wo1, ffn1_w1, ffn1_w3, ffn1_w2) = _to_bf16(
        l0_w_in.T, l0_w_out, l0_ffn_w1, l0_ffn_w3, l0_ffn_w2, l1_w_in, l1_w_out, l1_ffn_w1, l1_ffn_w3, l1_ffn_w2)
    ffn0 = (ffn0_w1, ffn0_w3, ffn0_w2)
    ffn1 = (ffn1_w1, ffn1_w3, ffn1_w2)
    ssd_p = (l0_ssd_a_log, l0_ssd_dt_bias, l0_ssd_d, l0_ssd_norm_w)

    qc, kc, vc, zc, xsc, bcc, dtc = _inproj0(xc, l0_norm1_w, mod_ctx[0], w_in0, w_dt0, l0_conv_w, l0_conv_b, lc, F32)
    att_c, kc_heads, vc_heads = _na_ctx(qc, kc, vc, lc)
    ssd_c, sfin_c = _ssd(zc, xsc, bcc, dtc, *ssd_p, seq=lc)

    ql, kl, vl, zl, xsl, bcl, dtl = _inproj0(xl, l0_norm1_w, mod_lat[0], w_in0, w_dt0, l0_conv_w, l0_conv_b, ll, BF16)
    past = cache_l0_na_k.shape[1]
    cache_t = [jnp.transpose(a, (0, 2, 3, 1)).reshape(bl * NA_WIDTH, past) for a in (cache_l0_na_k, cache_l0_na_v)]
    att_l = _na_lat(ql, kl, vl, *cache_t, _na_bias_table(l0_na_bias), ll)
    (ssd_l,) = _ssd(zl, xsl, bcl, dtl, *ssd_p, seq=ll, s0=jnp.swapaxes(state_l0_ssd, 3, 4))

    x1 = _out_ffn((xc, xl), [(att_c, att_l), (ssd_c, ssd_l)], wo0, mods[0], l0_norm2_w, *ffn0,
                  tc, tl, ll, split_out=False)

    ret_c, ret_state = _retention(x1, 0, tc, l1_norm1_w, mod_ctx[1], w_in1, l1_ret_decay, l1_ret_norm_w, lc)
    (ret_l,) = _retention(x1, tc, tl, l1_norm1_w, mod_lat[1], w_in1, l1_ret_decay, l1_ret_norm_w, ll,
                          s0=state_l1_ret, rope_tables=_rope_tables(ll))
    y_prompt, y_sample = _out_ffn(x1, [(ret_c, ret_l)], wo1, mods[1], l1_norm2_w, *ffn1,
                                  tc, tl, ll, split_out=True, final_norm_w=final_norm_w)

    k_out, v_out = [a.reshape(bc, NA_HEADS, NA_HEAD_DIM, lc).transpose(0, 3, 1, 2) for a in (kc_heads, vc_heads)]
    return (y_prompt.reshape(bc, lc, d), y_sample.reshape(bl, ll, d),
            k_out, v_out, jnp.swapaxes(sfin_c, 3, 4), ret_state)
```
